```python
import math
import jax, jax.numpy as jnp
from jax import lax
import numpy as np

D_MODEL = 1024
BATCH = 8
SEQ = 4096
DEPTH = 1
DEC_BATCH = 8
DEC_SEQ = 64
PAST_LEN = 2048

CHUNK = 64
GMLP_CHUNK = 128
GMLP_GROUPS = 4
GMLP_HEAD = 128
GMLP_WIDTH = GMLP_GROUPS * GMLP_HEAD
N_HEADS = 8
QK_NOPE = 64
QK_ROPE = 32
QK_HEAD = QK_NOPE + QK_ROPE
V_HEAD = 64
Q_LORA = 384
KV_LORA = 256
MLA_WIDTH = N_HEADS * V_HEAD
MIX_WIDTH = GMLP_WIDTH + MLA_WIDTH
IN_WIDTH = 2 * GMLP_WIDTH + Q_LORA + KV_LORA + QK_ROPE
IN_SPLITS = (GMLP_WIDTH, 2 * GMLP_WIDTH, 2 * GMLP_WIDTH + Q_LORA, 2 * GMLP_WIDTH + Q_LORA + KV_LORA)
D_FF = -(-8 * D_MODEL // (3 * 256)) * 256
ROPE_THETA = 10000.0
EPS = 1e-6
Q_BLOCK = 128
SCALE = QK_HEAD ** -0.5

kernel_name = 'hybrid_gmlp_mla_streaming_step'


def rmsnorm(x, g):
    xf = x.astype(jnp.float32)
    y = xf * lax.rsqrt(jnp.mean(xf * xf, axis=-1, keepdims=True) + EPS)
    return (y * g.astype(jnp.float32)).astype(x.dtype)


def rope(x, pos):
    half = QK_ROPE // 2
    inv = ROPE_THETA ** (-jnp.arange(half, dtype=jnp.float32) / half)
    ang = pos.astype(jnp.float32)[:, None] * inv[None, :]
    ang = ang.reshape(ang.shape[:1] + (1,) * (x.ndim - 3) + (half,))
    cos, sin = jnp.cos(ang), jnp.sin(ang)
    xf = x.astype(jnp.float32)
    x1, x2 = xf[..., :half], xf[..., half:]
    return jnp.concatenate([x1 * cos - x2 * sin, x2 * cos + x1 * sin], axis=-1).astype(x.dtype)


def ada_modulation(c, w, b):
    m = jax.nn.silu(c) @ w + b
    return jnp.split(m[:, None, :], 6, axis=-1)


def gmlp_mix(u, v, w_s, b_s):
    B, T, _ = v.shape
    L = min(T, GMLP_CHUNK)
    n = T // L
    p = jnp.arange(L)
    mask = (p[None, :] // CHUNK) <= (p[:, None] // CHUNK)
    ws = jnp.where(mask[None], w_s[:, :L, :L], 0.0)
    vc = v.reshape(B, n, L, GMLP_GROUPS, GMLP_HEAD)
    mixed = jnp.einsum('gij,bnjgc->bnigc', ws, vc) + b_s[:, :L].T[None, None, :, :, None]
    return u * mixed.reshape(B, T, GMLP_WIDTH)


def mla_queries(c_q, pos, q_norm_g, w_uq, qn_g, qr_g):
    B, T, _ = c_q.shape
    q = (rmsnorm(c_q, q_norm_g) @ w_uq).reshape(B, T, N_HEADS, QK_HEAD)
    q_nope = rmsnorm(q[..., :QK_NOPE], qn_g)
    q_rope = rope(rmsnorm(q[..., QK_NOPE:], qr_g), pos)
    return jnp.concatenate([q_nope, q_rope], axis=-1)


def mla_keys_values(ckv, krope, w_ukv, kn_g):
    B, S, _ = ckv.shape
    kv = (ckv @ w_ukv).reshape(B, S, N_HEADS, QK_NOPE + V_HEAD)
    k_nope = rmsnorm(kv[..., :QK_NOPE], kn_g)
    k = jnp.concatenate([k_nope, jnp.broadcast_to(krope[:, :, None, :], (B, S, N_HEADS, QK_ROPE))], axis=-1)
    return k, kv[..., QK_NOPE:]


def attend(q, k, v, q_pos, k_pos):
    s = jnp.einsum('bqhd,bkhd->bhqk', q, k).astype(jnp.float32) * SCALE
    mask = (k_pos[None, :] // CHUNK) <= (q_pos[:, None] // CHUNK)
    s = jnp.where(mask[None, None], s, jnp.finfo(jnp.float32).min)
    p = jax.nn.softmax(s, axis=-1).astype(v.dtype)
    return jnp.einsum('bhqk,bkhd->bqhd', p, v)


def prompt_attention(q, k, v):
    B, T, H, Dk = q.shape
    nb = T // Q_BLOCK
    qb = q.reshape(B, nb, Q_BLOCK, H, Dk).transpose(1, 0, 2, 3, 4)
    k_pos = jnp.arange(T)

    def block(args):
        qi, i = args
        q_pos = i * Q_BLOCK + jnp.arange(Q_BLOCK)
        return attend(qi, k, v, q_pos, k_pos)

    out = lax.map(block, (qb, jnp.arange(nb)))
    return out.transpose(1, 0, 2, 3, 4).reshape(B, T, MLA_WIDTH)


def trunk_layer(x, c, pos, past, w):
    B, T, _ = x.shape
    sh1, sc1, g1, sh2, sc2, g2 = ada_modulation(c, w['w_ada'], w['b_ada'])
    h = rmsnorm(x, w['norm1_g']) * (1.0 + sc1) + sh1
    z = h @ w['w_in']
    u, v, c_q, c_kv, k_r = jnp.split(z, IN_SPLITS, axis=-1)
    u = jax.nn.gelu(u)
    v = jax.nn.gelu(v)
    y_a = gmlp_mix(u, v, w['w_s'], w['b_s'])
    q = mla_queries(c_q, pos, w['q_norm_g'], w['w_uq'], w['qn_g'], w['qr_g'])
    ckv = rmsnorm(c_kv, w['kv_norm_g'])
    krope = rope(rmsnorm(k_r, w['kr_g']), pos)
    if past is None:
        k, vv = mla_keys_values(ckv, krope, w['w_ukv'], w['kn_g'])
        y_b = prompt_attention(q, k, vv)
    else:
        ckv_all = jnp.concatenate([past[0], ckv], axis=1)
        krope_all = jnp.concatenate([past[1], krope], axis=1)
        k, vv = mla_keys_values(ckv_all, krope_all, w['w_ukv'], w['kn_g'])
        k_pos = jnp.arange(ckv_all.shape[1])
        y_b = attend(q, k, vv, pos, k_pos).reshape(B, T, MLA_WIDTH)
    x = x + g1 * (jnp.concatenate([y_a, y_b], axis=-1) @ w['w_out'])
    h2 = rmsnorm(x, w['norm2_g']) * (1.0 + sc2) + sh2
    gate, up = jnp.split(h2 @ w['w_ffn_in'], 2, axis=-1)
    x = x + g2 * ((jax.nn.silu(gate) * up) @ w['w_ffn_out'])
    return x, ckv, krope, v


def setup_inputs(seed: int = 0) -> dict:
    key = jax.random.key(seed)
    ks = jax.random.split(key, 24)

    def nrm(k, shape, scale):
        return jax.random.normal(k, shape, jnp.float32) * scale

    def gain(k, n):
        return 1.0 + 0.1 * jax.random.normal(k, (DEPTH, n), jnp.float32)

    L = DEPTH
    return {
        'x_prompt': nrm(ks[0], (BATCH, SEQ, D_MODEL), 1.0),
        'x_sample': nrm(ks[1], (DEC_BATCH, DEC_SEQ, D_MODEL), 1.0),
        'cache_ckv': nrm(ks[2], (L, DEC_BATCH, PAST_LEN, KV_LORA), 1.0),
        'cache_krope': nrm(ks[3], (L, DEC_BATCH, PAST_LEN, QK_ROPE), 1.0),
        'c_prompt': nrm(ks[4], (BATCH, D_MODEL), 1.0),
        'c_sample': nrm(ks[5], (DEC_BATCH, D_MODEL), 1.0),
        'w_ada': nrm(ks[6], (L, D_MODEL, 6 * D_MODEL), 0.5 * D_MODEL ** -0.5),
        'b_ada': nrm(ks[7], (L, 6 * D_MODEL), 0.02),
        'norm1_g': gain(ks[8], D_MODEL),
        'w_in': nrm(ks[9], (L, D_MODEL, IN_WIDTH), D_MODEL ** -0.5),
        'w_s': nrm(ks[10], (L, GMLP_GROUPS, GMLP_CHUNK, GMLP_CHUNK), GMLP_CHUNK ** -0.5),
        'b_s': 1.0 + nrm(ks[11], (L, GMLP_GROUPS, GMLP_CHUNK), 0.1),
        'q_norm_g': gain(ks[12], Q_LORA),
        'w_uq': nrm(ks[13], (L, Q_LORA, N_HEADS * QK_HEAD), Q_LORA ** -0.5),
        'kv_norm_g': gain(ks[14], KV_LORA),
        'w_ukv': nrm(ks[15], (L, KV_LORA, N_HEADS * (QK_NOPE + V_HEAD)), KV_LORA ** -0.5),
        'qn_g': gain(ks[16], QK_NOPE),
        'qr_g': gain(ks[17], QK_ROPE),
        'kn_g': gain(ks[18], QK_NOPE),
        'kr_g': gain(ks[19], QK_ROPE),
        'w_out': nrm(ks[20], (L, MIX_WIDTH, D_MODEL), MIX_WIDTH ** -0.5),
        'norm2_g': gain(ks[21], D_MODEL),
        'w_ffn_in': nrm(ks[22], (L, D_MODEL, 2 * D_FF), D_MODEL ** -0.5),
        'w_ffn_out': nrm(ks[23], (L, D_FF, D_MODEL), D_FF ** -0.5),
    }


def reference(x_prompt, x_sample, cache_ckv, cache_krope, c_prompt, c_sample, w_ada, b_ada, norm1_g, w_in, w_s, b_s, q_norm_g, w_uq, kv_norm_g, w_ukv, qn_g, qr_g, kn_g, kr_g, w_out, norm2_g, w_ffn_in, w_ffn_out):
    past_len = cache_ckv.shape[2]
    pos_p = jnp.arange(x_prompt.shape[1])
    pos_s = past_len + jnp.arange(x_sample.shape[1])
    yp, ys = x_prompt, x_sample
    ckv_p, kr_p, ckv_s, kr_s, v_s = [], [], [], [], []
    for l in range(DEPTH):
        w = dict(w_ada=w_ada[l], b_ada=b_ada[l], norm1_g=norm1_g[l], w_in=w_in[l], w_s=w_s[l], b_s=b_s[l],
                 q_norm_g=q_norm_g[l], w_uq=w_uq[l], kv_norm_g=kv_norm_g[l], w_ukv=w_ukv[l],
                 qn_g=qn_g[l], qr_g=qr_g[l], kn_g=kn_g[l], kr_g=kr_g[l], w_out=w_out[l],
                 norm2_g=norm2_g[l], w_ffn_in=w_ffn_in[l], w_ffn_out=w_ffn_out[l])
        yp, a_ckv, a_kr, _ = trunk_layer(yp, c_prompt, pos_p, None, w)
        ys, b_ckv, b_kr, b_v = trunk_layer(ys, c_sample, pos_s, (cache_ckv[l], cache_krope[l]), w)
        ckv_p.append(a_ckv)
        kr_p.append(a_kr)
        ckv_s.append(b_ckv)
        kr_s.append(b_kr)
        v_s.append(b_v)
    return (yp, ys, jnp.stack(ckv_p), jnp.stack(kr_p), jnp.stack(ckv_s), jnp.stack(kr_s), jnp.stack(v_s))
```

```python
import functools
import math

import jax
import jax.numpy as jnp
from jax import lax
from jax.experimental import pallas as pl
from jax.experimental.pallas import tpu as pltpu

F32 = jnp.float32
BF16 = jnp.bfloat16

CHUNK = 64
GMLP_CHUNK = 128
GMLP_GROUPS = 4
GMLP_HEAD = 128
GMLP_WIDTH = GMLP_GROUPS * GMLP_HEAD
N_HEADS = 8
QK_NOPE = 64
QK_ROPE = 32
QK_HEAD = QK_NOPE + QK_ROPE
V_HEAD = 64
Q_LORA = 384
KV_LORA = 256
MLA_WIDTH = N_HEADS * V_HEAD
ROPE_THETA = 10000.0
EPS = 1e-6
SCALE = QK_HEAD ** -0.5

LANES = 128
HEAD_BLOCK = LANES
KR_BLOCK = LANES
IN_WIDTH_PADDED = 2 * GMLP_WIDTH + Q_LORA + KV_LORA + KR_BLOCK
VMEM_LIMIT_BYTES = 56 * 1024 * 1024

PRE_ROWS = 512
POST_ROWS = 512
FFN_CHUNK = 256
ATTN_TQ = 512
ATTN_TK = 256

_NEG = float(jnp.finfo(jnp.float32).min)


def _swap_halves(a, axis=-1):
    n = a.shape[axis] // 2
    lo = lax.slice_in_dim(a, 0, n, axis=axis)
    hi = lax.slice_in_dim(a, n, 2 * n, axis=axis)
    return jnp.concatenate([hi, lo], axis=axis)


def _const_spec(shape):
    nd = len(shape)
    return pl.BlockSpec(shape, lambda *_: (0,) * nd, pipeline_mode=pl.Buffered(1))


def _ada_kernel(c_ref, w_ref, b_ref, o_ref):
    c = c_ref[...]
    a = c * jax.nn.sigmoid(c)
    o_ref[...] = jnp.dot(a, w_ref[...], preferred_element_type=F32,
                         precision=lax.Precision.HIGHEST) + b_ref[...]


def _ada(c_all, w_ada, b_ada):
    n, d = c_all.shape
    width = w_ada.shape[1]
    bn = d
    return pl.pallas_call(
        _ada_kernel,
        grid=(width // bn,),
        in_specs=[pl.BlockSpec((n, d), lambda j: (0, 0)),
                  pl.BlockSpec((d, bn), lambda j: (0, j)),
                  pl.BlockSpec((1, bn), lambda j: (0, j))],
        out_specs=pl.BlockSpec((n, bn), lambda j: (0, j)),
        out_shape=jax.ShapeDtypeStruct((n, width), F32),
        compiler_params=pltpu.CompilerParams(vmem_limit_bytes=VMEM_LIMIT_BYTES),
        name="ada",
    )(c_all, w_ada, b_ada.reshape(1, width))


def _rms(x, width):
    return lax.rsqrt(jnp.sum(x * x, axis=-1, keepdims=True) * (1.0 / width) + EPS)


def _pre_kernel(x_ref, mod_ref, n1g_ref, win_ref, ws_ref, bs_ref, qng_ref, wuq_ref,
                kvg_ref, wukv_ref, qgain_ref, kgain_ref, krg_ref, tabq_ref, tabk_ref,
                ya_ref, q_ref, k_ref, v_ref, ckv_ref, kr_ref, *maybe_vraw_ref, mix_len):
    rows = x_ref.shape[1]
    x = x_ref[0]
    mod = mod_ref[0]
    sh1, sc1 = mod[0:1], mod[1:2]
    h = x * _rms(x, x.shape[-1]) * n1g_ref[...] * (1.0 + sc1) + sh1
    z = jnp.dot(h.astype(BF16), win_ref[...], preferred_element_type=F32)

    o_v, o_q, o_kv, o_kr = GMLP_WIDTH, 2 * GMLP_WIDTH, 2 * GMLP_WIDTH + Q_LORA, 2 * GMLP_WIDTH + Q_LORA + KV_LORA
    u = jax.nn.gelu(z[:, :o_v])
    v = jax.nn.gelu(z[:, o_v:o_q])
    if maybe_vraw_ref:
        maybe_vraw_ref[0][0] = v

    vb = v.astype(BF16)
    pi = lax.broadcasted_iota(jnp.int32, (mix_len, mix_len), 0) // CHUNK
    pj = lax.broadcasted_iota(jnp.int32, (mix_len, mix_len), 1) // CHUNK
    for g in range(GMLP_GROUPS):
        wsg = jnp.where(pj <= pi, ws_ref[g, :mix_len, :mix_len], 0.0).astype(BF16)
        bg = bs_ref[:mix_len, g:g + 1]
        cols = slice(g * GMLP_HEAD, (g + 1) * GMLP_HEAD)
        for c in range(rows // mix_len):
            rws = slice(c * mix_len, (c + 1) * mix_len)
            mixed = jnp.dot(wsg, vb[rws, cols], preferred_element_type=F32) + bg
            ya_ref[0, rws, cols] = (u[rws, cols] * mixed).astype(ya_ref.dtype)

    lane = lax.broadcasted_iota(jnp.int32, (1, LANES), 1)
    is_nope = lane < QK_NOPE

    kr = z[:, o_kr:o_kr + KR_BLOCK]
    kr_ss = jnp.sum(jnp.where(lane < QK_ROPE, kr * kr, 0.0), axis=-1, keepdims=True)
    kr_p = kr * lax.rsqrt(kr_ss * (1.0 / QK_ROPE) + EPS) * krg_ref[...] * tabk_ref[...]
    krope = kr_p + pltpu.roll(kr_p, LANES - QK_ROPE, axis=1)
    kr_ref[0] = krope[:, :QK_ROPE]

    ckv_raw = z[:, o_kv:o_kr]
    ckv = ckv_raw * _rms(ckv_raw, KV_LORA) * kvg_ref[...]
    ckv_ref[0] = ckv
    kv = jnp.dot(ckv.astype(BF16), wukv_ref[...], preferred_element_type=F32)

    cq_raw = z[:, o_q:o_kv]
    cq = cq_raw * _rms(cq_raw, Q_LORA) * qng_ref[...]
    q = jnp.dot(cq.astype(BF16), wuq_ref[...], preferred_element_type=F32)

    qgain = qgain_ref[...]
    kgain = kgain_ref[...]
    tabq = tabq_ref[...]
    for hd in range(N_HEADS):
        cols = slice(hd * HEAD_BLOCK, (hd + 1) * HEAD_BLOCK)
        qh = q[:, cols]
        sq = qh * qh
        ss_n = jnp.sum(jnp.where(is_nope, sq, 0.0), axis=-1, keepdims=True)
        ss_r = jnp.sum(jnp.where((lane >= QK_NOPE) & (lane < QK_HEAD), sq, 0.0), axis=-1, keepdims=True)
        r = jnp.where(is_nope, lax.rsqrt(ss_n * (1.0 / QK_NOPE) + EPS), lax.rsqrt(ss_r * (1.0 / QK_ROPE) + EPS))
        q_ref[0, hd] = (qh * r * qgain * tabq).astype(q_ref.dtype)

        kvh = kv[:, cols]
        ks = jnp.sum(jnp.where(is_nope, kvh * kvh, 0.0), axis=-1, keepdims=True)
        kn = kvh * lax.rsqrt(ks * (1.0 / QK_NOPE) + EPS) * kgain
        k_ref[0, hd] = jnp.where(is_nope, kn, krope).astype(k_ref.dtype)
        v_ref[0, hd] = jnp.where(is_nope, 1.0, kvh).astype(v_ref.dtype)


def _pre(x, mod, wts, tabq, tabk, *, rows, mix_len, emit_v):
    b, t, d = x.shape
    grid = (b, t // rows)
    tok = lambda w: pl.BlockSpec((1, rows, w), lambda i, j: (i, j, 0))
    head = pl.BlockSpec((1, N_HEADS, rows, HEAD_BLOCK), lambda i, j: (i, 0, j, 0))
    tab = pl.BlockSpec((rows, LANES), lambda i, j: (j, 0))
    consts = [wts[n] for n in ("n1g", "w_in", "w_s", "b_s", "qng", "w_uq", "kvg", "w_ukv", "qgain", "kgain", "krg")]
    in_specs = ([tok(d), pl.BlockSpec((1,) + mod.shape[1:], lambda i, j: (i, 0, 0))]
                + [_const_spec(a.shape) for a in consts] + [tab, tab])
    out_shape = [jax.ShapeDtypeStruct((b, t, GMLP_WIDTH), BF16),
                 jax.ShapeDtypeStruct((b, N_HEADS, t, HEAD_BLOCK), BF16),
                 jax.ShapeDtypeStruct((b, N_HEADS, t, HEAD_BLOCK), BF16),
                 jax.ShapeDtypeStruct((b, N_HEADS, t, HEAD_BLOCK), BF16),
                 jax.ShapeDtypeStruct((b, t, KV_LORA), F32),
                 jax.ShapeDtypeStruct((b, t, QK_ROPE), F32)]
    out_specs = [tok(GMLP_WIDTH), head, head, head, tok(KV_LORA), tok(QK_ROPE)]
    if emit_v:
        out_shape.append(jax.ShapeDtypeStruct((b, t, GMLP_WIDTH), F32))
        out_specs.append(tok(GMLP_WIDTH))
    return pl.pallas_call(
        functools.partial(_pre_kernel, mix_len=mix_len),
        grid=grid, in_specs=in_specs, out_specs=out_specs, out_shape=out_shape,
        compiler_params=pltpu.CompilerParams(
            dimension_semantics=("parallel", "parallel"), vmem_limit_bytes=VMEM_LIMIT_BYTES),
        name="pre_v" if emit_v else "pre",
    )(x, mod, *consts, tabq, tabk)


def _pair_out(acc0, acc1):
    o0 = acc0 * pl.reciprocal(acc0[:, 0:1], approx=False)
    o1 = acc1 * pl.reciprocal(acc1[:, 0:1], approx=False)
    lane = lax.broadcasted_iota(jnp.int32, (1, LANES), 1)
    return jnp.where(lane < V_HEAD, pltpu.roll(o0, V_HEAD, axis=1), o1)


def _attn_kernel(q_ref, k_ref, v_ref, o_ref, m_ref, acc_ref, *, tq, tk):
    t = q_ref.shape[2]
    diag_tiles = tq // tk

    def q_tile(qi, carry):
        q0 = pl.multiple_of(qi * tq, tq)
        accs = []
        for hh in range(2):
            q = q_ref[0, hh, pl.ds(q0, tq), :]
            m_ref[...] = jnp.full(m_ref.shape, -jnp.inf, F32)
            acc_ref[...] = jnp.zeros(acc_ref.shape, F32)

            def kv_step(j, masked):
                k0 = pl.multiple_of(j * tk, tk)
                k = k_ref[0, hh, pl.ds(k0, tk), :]
                v = v_ref[0, hh, pl.ds(k0, tk), :]
                s = lax.dot_general(q, k, (((1,), (1,)), ((), ())), preferred_element_type=F32)
                if masked:
                    qc = (q0 + lax.broadcasted_iota(jnp.int32, (tq, 1), 0)) // CHUNK
                    kc = (k0 + lax.broadcasted_iota(jnp.int32, (1, tk), 1)) // CHUNK
                    s = jnp.where(kc <= qc, s, _NEG)
                m_old = m_ref[...]
                m_new = jnp.maximum(m_old, jnp.max(s, axis=-1, keepdims=True))
                p = jnp.exp(s - m_new)
                acc_ref[...] = jnp.exp(m_old - m_new) * acc_ref[...] + jnp.dot(
                    p.astype(BF16), v, preferred_element_type=F32)
                m_ref[...] = m_new

            def body(j, c):
                kv_step(j, False)
                return c

            lax.fori_loop(0, qi * diag_tiles, body, 0)
            for dd in range(diag_tiles):
                kv_step(qi * diag_tiles + dd, True)
            accs.append(acc_ref[...])
        o_ref[0, pl.ds(q0, tq), :] = _pair_out(accs[0], accs[1]).astype(o_ref.dtype)
        return carry

    lax.fori_loop(0, t // tq, q_tile, 0)


def _attn_prompt(q, k, v, *, tq, tk):
    b, nh, t, _ = q.shape
    qkv = pl.BlockSpec((1, 2, t, HEAD_BLOCK), lambda i, j: (i, j, 0, 0))
    return pl.pallas_call(
        functools.partial(_attn_kernel, tq=tq, tk=tk),
        grid=(b, nh // 2),
        in_specs=[qkv, qkv, qkv],
        out_specs=pl.BlockSpec((1, t, 2 * V_HEAD), lambda i, j: (i, 0, j)),
        out_shape=jax.ShapeDtypeStruct((b, t, MLA_WIDTH), BF16),
        scratch_shapes=[pltpu.VMEM((tq, 1), F32), pltpu.VMEM((tq, HEAD_BLOCK), F32)],
        compiler_params=pltpu.CompilerParams(
            dimension_semantics=("parallel", "parallel"), vmem_limit_bytes=VMEM_LIMIT_BYTES),
        name="attn_prompt",
    )(q, k, v)


def _attn_sample_kernel(q_ref, ckv_ref, kr_ref, wukv_ref, kgain_ref, o_ref):
    lane = lax.broadcasted_iota(jnp.int32, (1, LANES), 1)
    is_nope = lane < QK_NOPE
    kv = jnp.dot(ckv_ref[0], wukv_ref[...], preferred_element_type=F32)
    krope = kr_ref[0]
    kgain = kgain_ref[...]
    accs = []
    for hd in range(N_HEADS):
        kvh = kv[:, hd * HEAD_BLOCK:(hd + 1) * HEAD_BLOCK]
        ks = jnp.sum(jnp.where(is_nope, kvh * kvh, 0.0), axis=-1, keepdims=True)
        kn = kvh * lax.rsqrt(ks * (1.0 / QK_NOPE) + EPS) * kgain
        k = jnp.where(is_nope, kn, krope).astype(BF16)
        v = jnp.where(is_nope, 1.0, kvh).astype(BF16)
        s = lax.dot_general(q_ref[0, hd], k, (((1,), (1,)), ((), ())), preferred_element_type=F32)
        p = jnp.exp(s - jnp.max(s, axis=-1, keepdims=True))
        accs.append(jnp.dot(p.astype(BF16), v, preferred_element_type=F32))
        if hd % 2 == 1:
            o_ref[0, :, (hd - 1) * V_HEAD:(hd + 1) * V_HEAD] = _pair_out(accs[hd - 1], accs[hd]).astype(o_ref.dtype)


def _attn_sample(q, ckv_all, kr_all, w_ukv, kgain):
    b, nh, t, _ = q.shape
    s = ckv_all.shape[1]
    return pl.pallas_call(
        _attn_sample_kernel,
        grid=(b,),
        in_specs=[pl.BlockSpec((1, nh, t, HEAD_BLOCK), lambda i: (i, 0, 0, 0)),
                  pl.BlockSpec((1, s, KV_LORA), lambda i: (i, 0, 0)),
                  pl.BlockSpec((1, s, LANES), lambda i: (i, 0, 0)),
                  _const_spec(w_ukv.shape), _const_spec(kgain.shape)],
        out_specs=pl.BlockSpec((1, t, MLA_WIDTH), lambda i: (i, 0, 0)),
        out_shape=jax.ShapeDtypeStruct((b, t, MLA_WIDTH), BF16),
        compiler_params=pltpu.CompilerParams(
            dimension_semantics=("parallel",), vmem_limit_bytes=VMEM_LIMIT_BYTES),
        name="attn_sample",
    )(q, ckv_all, kr_all, w_ukv, kgain)


def _post_kernel(x_ref, ya_ref, yb_ref, mod_ref, wout_ref, n2g_ref, wfi_ref, wfo_ref, o_ref, acc_ref):
    mod = mod_ref[0]
    g1, sh2, sc2, g2 = mod[2:3], mod[3:4], mod[4:5], mod[5:6]
    mix = (jnp.dot(ya_ref[0], wout_ref[:GMLP_WIDTH, :], preferred_element_type=F32)
           + jnp.dot(yb_ref[0], wout_ref[GMLP_WIDTH:, :], preferred_element_type=F32))
    x1 = x_ref[0] + g1 * mix
    h2 = (x1 * _rms(x1, x1.shape[-1]) * n2g_ref[...] * (1.0 + sc2) + sh2).astype(BF16)
    d_ff = wfo_ref.shape[0]
    acc_ref[...] = x1
    for c in range(d_ff // FFN_CHUNK):
        gate = jnp.dot(h2, wfi_ref[:, c * FFN_CHUNK:(c + 1) * FFN_CHUNK], preferred_element_type=F32)
        up = jnp.dot(h2, wfi_ref[:, d_ff + c * FFN_CHUNK:d_ff + (c + 1) * FFN_CHUNK], preferred_element_type=F32)
        act = (gate * jax.nn.sigmoid(gate) * up).astype(BF16)
        acc_ref[...] += g2 * jnp.dot(act, wfo_ref[c * FFN_CHUNK:(c + 1) * FFN_CHUNK, :], preferred_element_type=F32)
    o_ref[0] = acc_ref[...]


def _post(x, ya, yb, mod, wts, *, rows):
    b, t, d = x.shape
    tok = lambda w: pl.BlockSpec((1, rows, w), lambda i, j: (i, j, 0))
    consts = [wts[n] for n in ("w_out", "n2g", "w_ffn_in", "w_ffn_out")]
    return pl.pallas_call(
        _post_kernel,
        grid=(b, t // rows),
        in_specs=[tok(d), tok(GMLP_WIDTH), tok(MLA_WIDTH),
                  pl.BlockSpec((1,) + mod.shape[1:], lambda i, j: (i, 0, 0)),
                  _const_spec(consts[0].shape), _const_spec(consts[1].shape),
                  _const_spec(consts[2].shape), _const_spec(consts[3].shape)],
        out_specs=tok(d),
        out_shape=jax.ShapeDtypeStruct((b, t, d), F32),
        scratch_shapes=[pltpu.VMEM((rows, d), F32)],
        compiler_params=pltpu.CompilerParams(
            dimension_semantics=("parallel", "parallel"), vmem_limit_bytes=VMEM_LIMIT_BYTES),
        name="post",
    )(x, ya, yb, mod, consts[0], consts[1], consts[2], consts[3])


def _rope_tables(pos):
    half = QK_ROPE // 2
    inv = ROPE_THETA ** (-jnp.arange(half, dtype=F32) / half)
    ang = pos.astype(F32)[:, None] * inv[None, :]
    cos, sin = jnp.cos(ang), jnp.sin(ang)
    rot = jnp.concatenate([cos, cos, -sin, sin], axis=-1)
    tabq = SCALE * jnp.concatenate([jnp.ones((pos.shape[0], QK_NOPE), F32), rot], axis=-1)
    tabk = jnp.concatenate([rot, rot], axis=-1)
    return tabq, tabk


def _prep_weights(w_in, w_s, b_s, q_norm_g, w_uq, kv_norm_g, w_ukv, qn_g, qr_g, kn_g, kr_g,
                  norm1_g, w_out, norm2_g, w_ffn_in, w_ffn_out):
    o_kr = 2 * GMLP_WIDTH + Q_LORA + KV_LORA
    w_kr = w_in[:, o_kr:o_kr + QK_ROPE]
    w_kr_sw = _swap_halves(w_kr)
    w_in_p = jnp.concatenate([w_in[:, :o_kr], w_kr, w_kr_sw, w_kr, w_kr_sw], axis=-1)
    d_q = w_uq.shape[0]
    wq = w_uq.reshape(d_q, N_HEADS, QK_HEAD)
    wq_r = wq[:, :, QK_NOPE:]
    w_uq_p = jnp.concatenate([wq, _swap_halves(wq_r)], axis=-1).reshape(d_q, N_HEADS * HEAD_BLOCK)
    row = lambda a: a.reshape(1, -1).astype(F32)
    return {
        "n1g": row(norm1_g), "w_in": w_in_p.astype(BF16), "w_s": w_s.astype(F32), "b_s": b_s.T.astype(F32),
        "qng": row(q_norm_g), "w_uq": w_uq_p.astype(BF16), "kvg": row(kv_norm_g), "w_ukv": w_ukv.astype(BF16),
        "qgain": row(jnp.concatenate([qn_g, qr_g, _swap_halves(qr_g)])),
        "kgain": row(jnp.concatenate([kn_g, jnp.ones((HEAD_BLOCK - QK_NOPE,), F32)])),
        "krg": row(jnp.concatenate([kr_g, _swap_halves(kr_g), kr_g, _swap_halves(kr_g)])),
        "w_out": w_out.astype(BF16), "n2g": row(norm2_g),
        "w_ffn_in": w_ffn_in.astype(BF16), "w_ffn_out": w_ffn_out.astype(BF16),
    }


def kernel(x_prompt, x_sample, cache_ckv, cache_krope, c_prompt, c_sample, w_ada, b_ada, norm1_g, w_in, w_s, b_s, q_norm_g, w_uq, kv_norm_g, w_ukv, qn_g, qr_g, kn_g, kr_g, w_out, norm2_g, w_ffn_in, w_ffn_out):
    depth = w_in.shape[0]
    assert depth == 1, "single trunk layer"
    bp, tp, d = x_prompt.shape
    bs, ts, _ = x_sample.shape
    past_len = cache_ckv.shape[2]

    wts = _prep_weights(w_in[0], w_s[0], b_s[0], q_norm_g[0], w_uq[0], kv_norm_g[0], w_ukv[0], qn_g[0], qr_g[0],
                        kn_g[0], kr_g[0], norm1_g[0], w_out[0], norm2_g[0], w_ffn_in[0], w_ffn_out[0])
    mod = _ada(jnp.concatenate([c_prompt, c_sample], axis=0), w_ada[0], b_ada[0]).reshape(bp + bs, 6, d)
    mod_p, mod_s = mod[:bp], mod[bp:]

    tabq_p, tabk_p = _rope_tables(jnp.arange(tp))
    ya_p, q_p, k_p, v_p, ckv_p, kr_p = _pre(x_prompt, mod_p, wts, tabq_p, tabk_p,
                                            rows=PRE_ROWS, mix_len=GMLP_CHUNK, emit_v=False)
    yb_p = _attn_prompt(q_p, k_p, v_p, tq=ATTN_TQ, tk=ATTN_TK)
    y_p = _post(x_prompt, ya_p, yb_p, mod_p, wts, rows=POST_ROWS)

    tabq_s, tabk_s = _rope_tables(past_len + jnp.arange(ts))
    ya_s, q_s, _, _, ckv_s, kr_s, v_s = _pre(x_sample, mod_s, wts, tabq_s, tabk_s,
                                             rows=ts, mix_len=min(ts, GMLP_CHUNK), emit_v=True)
    ckv_all = jnp.concatenate([cache_ckv[0], ckv_s], axis=1).astype(BF16)
    kr_all = jnp.tile(jnp.concatenate([cache_krope[0], kr_s], axis=1), (1, 1, LANES // QK_ROPE))
    yb_s = _attn_sample(q_s, ckv_all, kr_all, wts["w_ukv"], wts["kgain"])
    y_s = _post(x_sample, ya_s, yb_s, mod_s, wts, rows=ts)

    return (y_p, y_s, ckv_p[None], kr_p[None], ckv_s[None], kr_s[None], v_s[None])
```

```python
import functools
import math

import jax
import jax.numpy as jnp
from jax import lax
from jax.experimental import pallas as pl
from jax.experimental.pallas import tpu as pltpu

F32 = jnp.float32
BF16 = jnp.bfloat16

CHUNK = 64
GMLP_CHUNK = 128
GMLP_GROUPS = 4
GMLP_HEAD = 128
GMLP_WIDTH = GMLP_GROUPS * GMLP_HEAD
N_HEADS = 8
QK_NOPE = 64
QK_ROPE = 32
QK_HEAD = QK_NOPE + QK_ROPE
V_HEAD = 64
Q_LORA = 384
KV_LORA = 256
MLA_WIDTH = N_HEADS * V_HEAD
ROPE_THETA = 10000.0
EPS = 1e-6
SCALE = QK_HEAD ** -0.5

LANES = 128
HEAD_BLOCK = LANES
KR_BLOCK = LANES
IN_WIDTH_PADDED = 2 * GMLP_WIDTH + Q_LORA + KV_LORA + KR_BLOCK
VMEM_LIMIT_BYTES = 56 * 1024 * 1024

PRE_ROWS = 512
POST_ROWS = 512
FFN_CHUNK = 256
ATTN_TQ = 512
ATTN_TK = 256

_NEG = float(jnp.finfo(jnp.float32).min)
LOG2E = math.log2(math.e)
MAX_UNSHIFTED_SCORE = 60.0


def _swap_halves(a, axis=-1):
    n = a.shape[axis] // 2
    lo = lax.slice_in_dim(a, 0, n, axis=axis)
    hi = lax.slice_in_dim(a, n, 2 * n, axis=axis)
    return jnp.concatenate([hi, lo], axis=axis)


def _const_spec(shape):
    nd = len(shape)
    return pl.BlockSpec(shape, lambda *_: (0,) * nd, pipeline_mode=pl.Buffered(1))


def _ada_kernel(c_ref, w_ref, b_ref, o_ref):
    c = c_ref[...]
    a = c * jax.nn.sigmoid(c)
    o_ref[...] = jnp.dot(a, w_ref[...], preferred_element_type=F32,
                         precision=lax.Precision.HIGHEST) + b_ref[...]


def _ada(c_all, w_ada, b_ada):
    n, d = c_all.shape
    width = w_ada.shape[1]
    bn = d
    return pl.pallas_call(
        _ada_kernel,
        grid=(width // bn,),
        in_specs=[pl.BlockSpec((n, d), lambda j: (0, 0)),
                  pl.BlockSpec((d, bn), lambda j: (0, j)),
                  pl.BlockSpec((1, bn), lambda j: (0, j))],
        out_specs=pl.BlockSpec((n, bn), lambda j: (0, j)),
        out_shape=jax.ShapeDtypeStruct((n, width), F32),
        compiler_params=pltpu.CompilerParams(vmem_limit_bytes=VMEM_LIMIT_BYTES),
        name="ada",
    )(c_all, w_ada, b_ada.reshape(1, width))


def _rms(x, width):
    return lax.rsqrt(jnp.sum(x * x, axis=-1, keepdims=True) * (1.0 / width) + EPS)


def _pre_kernel(x_ref, mod_ref, n1g_ref, win_ref, ws_ref, bs_ref, qng_ref, wuq_ref,
                kvg_ref, wukv_ref, qgain_ref, kgain_ref, krg_ref, tabq_ref, tabk_ref,
                ya_ref, q_ref, k_ref, v_ref, ckv_ref, kr_ref, *maybe_vraw_ref, mix_len):
    rows = x_ref.shape[1]
    x = x_ref[0]
    mod = mod_ref[0]
    sh1, sc1 = mod[0:1], mod[1:2]
    h = x * _rms(x, x.shape[-1]) * n1g_ref[...] * (1.0 + sc1) + sh1
    z = jnp.dot(h.astype(BF16), win_ref[...], preferred_element_type=F32)

    o_v, o_q, o_kv, o_kr = GMLP_WIDTH, 2 * GMLP_WIDTH, 2 * GMLP_WIDTH + Q_LORA, 2 * GMLP_WIDTH + Q_LORA + KV_LORA
    u = jax.nn.gelu(z[:, :o_v])
    v = jax.nn.gelu(z[:, o_v:o_q])
    if maybe_vraw_ref:
        maybe_vraw_ref[0][0] = v

    vb = v.astype(BF16)
    pi = lax.broadcasted_iota(jnp.int32, (mix_len, mix_len), 0) // CHUNK
    pj = lax.broadcasted_iota(jnp.int32, (mix_len, mix_len), 1) // CHUNK
    for g in range(GMLP_GROUPS):
        wsg = jnp.where(pj <= pi, ws_ref[g, :mix_len, :mix_len], 0.0).astype(BF16)
        bg = bs_ref[:mix_len, g:g + 1]
        cols = slice(g * GMLP_HEAD, (g + 1) * GMLP_HEAD)
        for c in range(rows // mix_len):
            rws = slice(c * mix_len, (c + 1) * mix_len)
            mixed = jnp.dot(wsg, vb[rws, cols], preferred_element_type=F32) + bg
            ya_ref[0, rws, cols] = (u[rws, cols] * mixed).astype(ya_ref.dtype)

    lane = lax.broadcasted_iota(jnp.int32, (1, LANES), 1)
    is_nope = lane < QK_NOPE

    kr = z[:, o_kr:o_kr + KR_BLOCK]
    kr_ss = jnp.sum(jnp.where(lane < QK_ROPE, kr * kr, 0.0), axis=-1, keepdims=True)
    kr_p = kr * lax.rsqrt(kr_ss * (1.0 / QK_ROPE) + EPS) * krg_ref[...] * tabk_ref[...]
    krope = kr_p + pltpu.roll(kr_p, LANES - QK_ROPE, axis=1)
    kr_ref[0] = krope[:, :QK_ROPE]

    ckv_raw = z[:, o_kv:o_kr]
    ckv = ckv_raw * _rms(ckv_raw, KV_LORA) * kvg_ref[...]
    ckv_ref[0] = ckv
    kv = jnp.dot(ckv.astype(BF16), wukv_ref[...], preferred_element_type=F32)

    cq_raw = z[:, o_q:o_kv]
    cq = cq_raw * _rms(cq_raw, Q_LORA) * qng_ref[...]
    q = jnp.dot(cq.astype(BF16), wuq_ref[...], preferred_element_type=F32)

    qgain = qgain_ref[...]
    kgain = kgain_ref[...]
    tabq = tabq_ref[...]
    for hd in range(N_HEADS):
        cols = slice(hd * HEAD_BLOCK, (hd + 1) * HEAD_BLOCK)
        qh = q[:, cols]
        sq = qh * qh
        ss_n = jnp.sum(jnp.where(is_nope, sq, 0.0), axis=-1, keepdims=True)
        ss_r = jnp.sum(jnp.where((lane >= QK_NOPE) & (lane < QK_HEAD), sq, 0.0), axis=-1, keepdims=True)
        r = jnp.where(is_nope, lax.rsqrt(ss_n * (1.0 / QK_NOPE) + EPS), lax.rsqrt(ss_r * (1.0 / QK_ROPE) + EPS))
        q_ref[0, hd] = (qh * r * qgain * tabq).astype(q_ref.dtype)

        kvh = kv[:, cols]
        ks = jnp.sum(jnp.where(is_nope, kvh * kvh, 0.0), axis=-1, keepdims=True)
        kn = kvh * lax.rsqrt(ks * (1.0 / QK_NOPE) + EPS) * kgain
        k_ref[0, hd] = jnp.where(is_nope, kn, krope).astype(k_ref.dtype)
        v_ref[0, hd] = jnp.where(is_nope, 1.0, kvh).astype(v_ref.dtype)


def _pre(x, mod, wts, tabq, tabk, *, rows, mix_len, emit_v):
    b, t, d = x.shape
    grid = (b, t // rows)
    tok = lambda w: pl.BlockSpec((1, rows, w), lambda i, j: (i, j, 0))
    head = pl.BlockSpec((1, N_HEADS, rows, HEAD_BLOCK), lambda i, j: (i, 0, j, 0))
    tab = pl.BlockSpec((rows, LANES), lambda i, j: (j, 0))
    consts = [wts[n] for n in ("n1g", "w_in", "w_s", "b_s", "qng", "w_uq", "kvg", "w_ukv", "qgain", "kgain", "krg")]
    in_specs = ([tok(d), pl.BlockSpec((1,) + mod.shape[1:], lambda i, j: (i, 0, 0))]
                + [_const_spec(a.shape) for a in consts] + [tab, tab])
    out_shape = [jax.ShapeDtypeStruct((b, t, GMLP_WIDTH), BF16),
                 jax.ShapeDtypeStruct((b, N_HEADS, t, HEAD_BLOCK), BF16),
                 jax.ShapeDtypeStruct((b, N_HEADS, t, HEAD_BLOCK), BF16),
                 jax.ShapeDtypeStruct((b, N_HEADS, t, HEAD_BLOCK), BF16),
                 jax.ShapeDtypeStruct((b, t, KV_LORA), F32),
                 jax.ShapeDtypeStruct((b, t, QK_ROPE), F32)]
    out_specs = [tok(GMLP_WIDTH), head, head, head, tok(KV_LORA), tok(QK_ROPE)]
    if emit_v:
        out_shape.append(jax.ShapeDtypeStruct((b, t, GMLP_WIDTH), F32))
        out_specs.append(tok(GMLP_WIDTH))
    return pl.pallas_call(
        functools.partial(_pre_kernel, mix_len=mix_len),
        grid=grid, in_specs=in_specs, out_specs=out_specs, out_shape=out_shape,
        compiler_params=pltpu.CompilerParams(
            dimension_semantics=("parallel", "parallel"), vmem_limit_bytes=VMEM_LIMIT_BYTES),
        name="pre_v" if emit_v else "pre",
    )(x, mod, *consts, tabq, tabk)


def _pair_out(acc0, acc1):
    o0 = acc0 * pl.reciprocal(acc0[:, 0:1], approx=False)
    o1 = acc1 * pl.reciprocal(acc1[:, 0:1], approx=False)
    lane = lax.broadcasted_iota(jnp.int32, (1, LANES), 1)
    return jnp.where(lane < V_HEAD, pltpu.roll(o0, V_HEAD, axis=1), o1)


def _attn_online_kernel(q_ref, k_ref, v_ref, o_ref, m_ref, acc_ref, *, tq, tk):
    t = q_ref.shape[2]
    diag_tiles = tq // tk

    def q_tile(qi, carry):
        q0 = pl.multiple_of(qi * tq, tq)
        accs = []
        for hh in range(2):
            q = q_ref[0, hh, pl.ds(q0, tq), :]
            m_ref[...] = jnp.full(m_ref.shape, -jnp.inf, F32)
            acc_ref[...] = jnp.zeros(acc_ref.shape, F32)

            def kv_step(j, masked):
                k0 = pl.multiple_of(j * tk, tk)
                k = k_ref[0, hh, pl.ds(k0, tk), :]
                v = v_ref[0, hh, pl.ds(k0, tk), :]
                s = lax.dot_general(q, k, (((1,), (1,)), ((), ())), preferred_element_type=F32)
                if masked:
                    qc = (q0 + lax.broadcasted_iota(jnp.int32, (tq, 1), 0)) // CHUNK
                    kc = (k0 + lax.broadcasted_iota(jnp.int32, (1, tk), 1)) // CHUNK
                    s = jnp.where(kc <= qc, s, _NEG)
                m_old = m_ref[...]
                m_new = jnp.maximum(m_old, jnp.max(s, axis=-1, keepdims=True))
                p = jnp.exp2(s - m_new)
                acc_ref[...] = jnp.exp2(m_old - m_new) * acc_ref[...] + jnp.dot(
                    p.astype(BF16), v, preferred_element_type=F32)
                m_ref[...] = m_new

            def body(j, c):
                kv_step(j, False)
                return c

            lax.fori_loop(0, qi * diag_tiles, body, 0)
            for dd in range(diag_tiles):
                kv_step(qi * diag_tiles + dd, True)
            accs.append(acc_ref[...])
        o_ref[0, pl.ds(q0, tq), :] = _pair_out(accs[0], accs[1]).astype(o_ref.dtype)
        return carry

    lax.fori_loop(0, t // tq, q_tile, 0)


def _attn_bounded_kernel(q_ref, k_ref, v_ref, o_ref, acc_ref, *, tq):
    t = q_ref.shape[2]
    tk = tq // 2
    ri = lax.broadcasted_iota(jnp.int32, (tk, tk), 0) // CHUNK
    ci = lax.broadcasted_iota(jnp.int32, (tk, tk), 1) // CHUNK
    visible = ci <= ri

    def pv(q, hh, k0, masked):
        k = k_ref[0, hh, pl.ds(k0, tk), :]
        v = v_ref[0, hh, pl.ds(k0, tk), :]
        p = jnp.exp2(lax.dot_general(q, k, (((1,), (1,)), ((), ())), preferred_element_type=F32))
        if masked:
            p = jnp.where(visible, p, 0.0)
        return jnp.dot(p.astype(BF16), v, preferred_element_type=F32)

    def q_tile(qi, carry):
        q0 = pl.multiple_of(qi * tq, tq)
        q1 = pl.multiple_of(q0 + tk, tk)
        for hh in range(2):
            q_lo = q_ref[0, hh, pl.ds(q0, tk), :]
            q_hi = q_ref[0, hh, pl.ds(q1, tk), :]
            acc_ref[hh, :tk, :] = pv(q_lo, hh, q0, True)
            acc_ref[hh, tk:, :] = pv(q_hi, hh, q0, False) + pv(q_hi, hh, q1, True)

        def body(j, c):
            k0 = pl.multiple_of(j * tk, tk)
            for hh in range(2):
                acc_ref[hh] += pv(q_ref[0, hh, pl.ds(q0, tq), :], hh, k0, False)
            return c

        lax.fori_loop(0, qi * 2, body, 0)
        o_ref[0, pl.ds(q0, tq), :] = _pair_out(acc_ref[0], acc_ref[1]).astype(o_ref.dtype)
        return carry

    lax.fori_loop(0, t // tq, q_tile, 0)


def _attn_prompt(q, k, v, score_bound, *, tq, tk):
    b, nh, t, _ = q.shape
    qkv = pl.BlockSpec((1, 2, t, HEAD_BLOCK), lambda i, j: (i, j, 0, 0))
    common = dict(
        grid=(b, nh // 2),
        in_specs=[qkv, qkv, qkv],
        out_specs=pl.BlockSpec((1, t, 2 * V_HEAD), lambda i, j: (i, 0, j)),
        out_shape=jax.ShapeDtypeStruct((b, t, MLA_WIDTH), BF16),
        compiler_params=pltpu.CompilerParams(
            dimension_semantics=("parallel", "parallel"), vmem_limit_bytes=VMEM_LIMIT_BYTES),
    )
    bounded = pl.pallas_call(
        functools.partial(_attn_bounded_kernel, tq=tq),
        scratch_shapes=[pltpu.VMEM((2, tq, HEAD_BLOCK), F32)],
        name="attn_prompt_bounded", **common)
    online = pl.pallas_call(
        functools.partial(_attn_online_kernel, tq=tq, tk=tk),
        scratch_shapes=[pltpu.VMEM((tq, 1), F32), pltpu.VMEM((tq, HEAD_BLOCK), F32)],
        name="attn_prompt_online", **common)
    return lax.cond(score_bound <= MAX_UNSHIFTED_SCORE, bounded, online, q, k, v)


def _attn_sample_kernel(q_ref, ckv_ref, kr_ref, wukv_ref, kgain_ref, o_ref):
    lane = lax.broadcasted_iota(jnp.int32, (1, LANES), 1)
    is_nope = lane < QK_NOPE
    kv = jnp.dot(ckv_ref[0], wukv_ref[...], preferred_element_type=F32)
    krope = kr_ref[0]
    kgain = kgain_ref[...]
    accs = []
    for hd in range(N_HEADS):
        kvh = kv[:, hd * HEAD_BLOCK:(hd + 1) * HEAD_BLOCK]
        ks = jnp.sum(jnp.where(is_nope, kvh * kvh, 0.0), axis=-1, keepdims=True)
        kn = kvh * lax.rsqrt(ks * (1.0 / QK_NOPE) + EPS) * kgain
        k = jnp.where(is_nope, kn, krope).astype(BF16)
        v = jnp.where(is_nope, 1.0, kvh).astype(BF16)
        s = lax.dot_general(q_ref[0, hd], k, (((1,), (1,)), ((), ())), preferred_element_type=F32)
        p = jnp.exp2(s - jnp.max(s, axis=-1, keepdims=True))
        accs.append(jnp.dot(p.astype(BF16), v, preferred_element_type=F32))
        if hd % 2 == 1:
            o_ref[0, :, (hd - 1) * V_HEAD:(hd + 1) * V_HEAD] = _pair_out(accs[hd - 1], accs[hd]).astype(o_ref.dtype)


def _attn_sample(q, ckv_all, kr_all, w_ukv, kgain):
    b, nh, t, _ = q.shape
    s = ckv_all.shape[1]
    return pl.pallas_call(
        _attn_sample_kernel,
        grid=(b,),
        in_specs=[pl.BlockSpec((1, nh, t, HEAD_BLOCK), lambda i: (i, 0, 0, 0)),
                  pl.BlockSpec((1, s, KV_LORA), lambda i: (i, 0, 0)),
                  pl.BlockSpec((1, s, LANES), lambda i: (i, 0, 0)),
                  _const_spec(w_ukv.shape), _const_spec(kgain.shape)],
        out_specs=pl.BlockSpec((1, t, MLA_WIDTH), lambda i: (i, 0, 0)),
        out_shape=jax.ShapeDtypeStruct((b, t, MLA_WIDTH), BF16),
        compiler_params=pltpu.CompilerParams(
            dimension_semantics=("parallel",), vmem_limit_bytes=VMEM_LIMIT_BYTES),
        name="attn_sample",
    )(q, ckv_all, kr_all, w_ukv, kgain)


def _post_kernel(x_ref, ya_ref, yb_ref, mod_ref, wout_ref, n2g_ref, wfi_ref, wfo_ref, o_ref, acc_ref):
    mod = mod_ref[0]
    g1, sh2, sc2, g2 = mod[2:3], mod[3:4], mod[4:5], mod[5:6]
    mix = (jnp.dot(ya_ref[0], wout_ref[:GMLP_WIDTH, :], preferred_element_type=F32)
           + jnp.dot(yb_ref[0], wout_ref[GMLP_WIDTH:, :], preferred_element_type=F32))
    x1 = x_ref[0] + g1 * mix
    h2 = (x1 * _rms(x1, x1.shape[-1]) * n2g_ref[...] * (1.0 + sc2) + sh2).astype(BF16)
    d_ff = wfo_ref.shape[0]
    acc_ref[...] = x1
    for c in range(d_ff // FFN_CHUNK):
        gate = jnp.dot(h2, wfi_ref[:, c * FFN_CHUNK:(c + 1) * FFN_CHUNK], preferred_element_type=F32)
        up = jnp.dot(h2, wfi_ref[:, d_ff + c * FFN_CHUNK:d_ff + (c + 1) * FFN_CHUNK], preferred_element_type=F32)
        act = (gate * jax.nn.sigmoid(gate) * up).astype(BF16)
        acc_ref[...] += g2 * jnp.dot(act, wfo_ref[c * FFN_CHUNK:(c + 1) * FFN_CHUNK, :], preferred_element_type=F32)
    o_ref[0] = acc_ref[...]


def _post(x, ya, yb, mod, wts, *, rows):
    b, t, d = x.shape
    tok = lambda w: pl.BlockSpec((1, rows, w), lambda i, j: (i, j, 0))
    consts = [wts[n] for n in ("w_out", "n2g", "w_ffn_in", "w_ffn_out")]
    return pl.pallas_call(
        _post_kernel,
        grid=(b, t // rows),
        in_specs=[tok(d), tok(GMLP_WIDTH), tok(MLA_WIDTH),
                  pl.BlockSpec((1,) + mod.shape[1:], lambda i, j: (i, 0, 0)),
                  _const_spec(consts[0].shape), _const_spec(consts[1].shape),
                  _const_spec(consts[2].shape), _const_spec(consts[3].shape)],
        out_specs=tok(d),
        out_shape=jax.ShapeDtypeStruct((b, t, d), F32),
        scratch_shapes=[pltpu.VMEM((rows, d), F32)],
        compiler_params=pltpu.CompilerParams(
            dimension_semantics=("parallel", "parallel"), vmem_limit_bytes=VMEM_LIMIT_BYTES),
        name="post",
    )(x, ya, yb, mod, consts[0], consts[1], consts[2], consts[3])


def _rope_tables(pos):
    half = QK_ROPE // 2
    inv = ROPE_THETA ** (-jnp.arange(half, dtype=F32) / half)
    ang = pos.astype(F32)[:, None] * inv[None, :]
    cos, sin = jnp.cos(ang), jnp.sin(ang)
    rot = jnp.concatenate([cos, cos, -sin, sin], axis=-1)
    tabq = (SCALE * LOG2E) * jnp.concatenate([jnp.ones((pos.shape[0], QK_NOPE), F32), rot], axis=-1)
    tabk = jnp.concatenate([rot, rot], axis=-1)
    return tabq, tabk


def _prep_weights(w_in, w_s, b_s, q_norm_g, w_uq, kv_norm_g, w_ukv, qn_g, qr_g, kn_g, kr_g,
                  norm1_g, w_out, norm2_g, w_ffn_in, w_ffn_out):
    o_kr = 2 * GMLP_WIDTH + Q_LORA + KV_LORA
    w_kr = w_in[:, o_kr:o_kr + QK_ROPE]
    w_kr_sw = _swap_halves(w_kr)
    w_in_p = jnp.concatenate([w_in[:, :o_kr], w_kr, w_kr_sw, w_kr, w_kr_sw], axis=-1)
    d_q = w_uq.shape[0]
    wq = w_uq.reshape(d_q, N_HEADS, QK_HEAD)
    wq_r = wq[:, :, QK_NOPE:]
    w_uq_p = jnp.concatenate([wq, _swap_halves(wq_r)], axis=-1).reshape(d_q, N_HEADS * HEAD_BLOCK)
    row = lambda a: a.reshape(1, -1).astype(F32)
    return {
        "n1g": row(norm1_g), "w_in": w_in_p.astype(BF16), "w_s": w_s.astype(F32), "b_s": b_s.T.astype(F32),
        "qng": row(q_norm_g), "w_uq": w_uq_p.astype(BF16), "kvg": row(kv_norm_g), "w_ukv": w_ukv.astype(BF16),
        "qgain": row(jnp.concatenate([qn_g, qr_g, _swap_halves(qr_g)])),
        "kgain": row(jnp.concatenate([kn_g, jnp.ones((HEAD_BLOCK - QK_NOPE,), F32)])),
        "krg": row(jnp.concatenate([kr_g, _swap_halves(kr_g), kr_g, _swap_halves(kr_g)])),
        "w_out": w_out.astype(BF16), "n2g": row(norm2_g),
        "w_ffn_in": w_ffn_in.astype(BF16), "w_ffn_out": w_ffn_out.astype(BF16),
    }


def kernel(x_prompt, x_sample, cache_ckv, cache_krope, c_prompt, c_sample, w_ada, b_ada, norm1_g, w_in, w_s, b_s, q_norm_g, w_uq, kv_norm_g, w_ukv, qn_g, qr_g, kn_g, kr_g, w_out, norm2_g, w_ffn_in, w_ffn_out):
    depth = w_in.shape[0]
    assert depth == 1, "single trunk layer"
    bp, tp, d = x_prompt.shape
    bs, ts, _ = x_sample.shape
    past_len = cache_ckv.shape[2]

    wts = _prep_weights(w_in[0], w_s[0], b_s[0], q_norm_g[0], w_uq[0], kv_norm_g[0], w_ukv[0], qn_g[0], qr_g[0],
                        kn_g[0], kr_g[0], norm1_g[0], w_out[0], norm2_g[0], w_ffn_in[0], w_ffn_out[0])
    mod = _ada(jnp.concatenate([c_prompt, c_sample], axis=0), w_ada[0], b_ada[0]).reshape(bp + bs, 6, d)
    mod_p, mod_s = mod[:bp], mod[bp:]

    tabq_p, tabk_p = _rope_tables(jnp.arange(tp))
    ya_p, q_p, k_p, v_p, ckv_p, kr_p = _pre(x_prompt, mod_p, wts, tabq_p, tabk_p,
                                            rows=PRE_ROWS, mix_len=GMLP_CHUNK, emit_v=False)
    gmax = lambda g: jnp.max(jnp.abs(g))
    score_bound = SCALE * (QK_NOPE * gmax(qn_g) * gmax(kn_g) + QK_ROPE * gmax(qr_g) * gmax(kr_g))
    yb_p = _attn_prompt(q_p, k_p, v_p, score_bound, tq=ATTN_TQ, tk=ATTN_TK)
    y_p = _post(x_prompt, ya_p, yb_p, mod_p, wts, rows=POST_ROWS)

    tabq_s, tabk_s = _rope_tables(past_len + jnp.arange(ts))
    ya_s, q_s, _, _, ckv_s, kr_s, v_s = _pre(x_sample, mod_s, wts, tabq_s, tabk_s,
                                             rows=ts, mix_len=min(ts, GMLP_CHUNK), emit_v=True)
    ckv_all = jnp.concatenate([cache_ckv[0], ckv_s], axis=1).astype(BF16)
    kr_all = jnp.tile(jnp.concatenate([cache_krope[0], kr_s], axis=1), (1, 1, LANES // QK_ROPE))
    yb_s = _attn_sample(q_s, ckv_all, kr_all, wts["w_ukv"], wts["kgain"])
    y_s = _post(x_sample, ya_s, yb_s, mod_s, wts, rows=ts)

    return (y_p, y_s, ckv_p[None], kr_p[None], ckv_s[None], kr_s[None], v_s[None])
```

```python
import functools
import math

import jax
import jax.numpy as jnp
from jax import lax
from jax.experimental import pallas as pl
from jax.experimental.pallas import tpu as pltpu

F32 = jnp.float32
BF16 = jnp.bfloat16

CHUNK = 64
GMLP_CHUNK = 128
GMLP_GROUPS = 4
GMLP_HEAD = 128
GMLP_WIDTH = GMLP_GROUPS * GMLP_HEAD
N_HEADS = 8
QK_NOPE = 64
QK_ROPE = 32
QK_HEAD = QK_NOPE + QK_ROPE
V_HEAD = 64
Q_LORA = 384
KV_LORA = 256
MLA_WIDTH = N_HEADS * V_HEAD
ROPE_THETA = 10000.0
EPS = 1e-6
SCALE = QK_HEAD ** -0.5

LANES = 128
HEAD_BLOCK = LANES
KR_BLOCK = LANES
IN_WIDTH_PADDED = 2 * GMLP_WIDTH + Q_LORA + KV_LORA + KR_BLOCK
VMEM_LIMIT_BYTES = 56 * 1024 * 1024

PRE_ROWS = 512
POST_ROWS = 512
FFN_CHUNK = 256
ATTN_BOUNDED_TQ = 1024
ATTN_ONLINE_TQ = 512
ATTN_ONLINE_TK = 256

_NEG = float(jnp.finfo(jnp.float32).min)
LOG2E = math.log2(math.e)
MAX_UNSHIFTED_SCORE = 60.0


def _swap_halves(a, axis=-1):
    n = a.shape[axis] // 2
    lo = lax.slice_in_dim(a, 0, n, axis=axis)
    hi = lax.slice_in_dim(a, n, 2 * n, axis=axis)
    return jnp.concatenate([hi, lo], axis=axis)


def _const_spec(shape):
    nd = len(shape)
    return pl.BlockSpec(shape, lambda *_: (0,) * nd, pipeline_mode=pl.Buffered(1))


def _ada_kernel(c_ref, w_ref, b_ref, o_ref):
    c = c_ref[...]
    a = c * jax.nn.sigmoid(c)
    o_ref[...] = jnp.dot(a, w_ref[...], preferred_element_type=F32,
                         precision=lax.Precision.HIGHEST) + b_ref[...]


def _ada(c_all, w_ada, b_ada):
    n, d = c_all.shape
    width = w_ada.shape[1]
    bn = d
    return pl.pallas_call(
        _ada_kernel,
        grid=(width // bn,),
        in_specs=[pl.BlockSpec((n, d), lambda j: (0, 0)),
                  pl.BlockSpec((d, bn), lambda j: (0, j)),
                  pl.BlockSpec((1, bn), lambda j: (0, j))],
        out_specs=pl.BlockSpec((n, bn), lambda j: (0, j)),
        out_shape=jax.ShapeDtypeStruct((n, width), F32),
        compiler_params=pltpu.CompilerParams(vmem_limit_bytes=VMEM_LIMIT_BYTES),
        name="ada",
    )(c_all, w_ada, b_ada.reshape(1, width))


def _rms(x, width):
    return lax.rsqrt(jnp.sum(x * x, axis=-1, keepdims=True) * (1.0 / width) + EPS)


def _pre_kernel(x_ref, mod_ref, n1g_ref, win_ref, ws_ref, bs_ref, qng_ref, wuq_ref,
                kvg_ref, wukv_ref, qgain_ref, kgain_ref, krg_ref, tabq_ref, tabk_ref,
                ya_ref, q_ref, k_ref, v_ref, ckv_ref, kr_ref, *maybe_vraw_ref, mix_len):
    rows = x_ref.shape[1]
    x = x_ref[0]
    mod = mod_ref[0]
    sh1, sc1 = mod[0:1], mod[1:2]
    h = x * _rms(x, x.shape[-1]) * n1g_ref[...] * (1.0 + sc1) + sh1
    z = jnp.dot(h.astype(BF16), win_ref[...], preferred_element_type=F32)

    o_v, o_q, o_kv, o_kr = GMLP_WIDTH, 2 * GMLP_WIDTH, 2 * GMLP_WIDTH + Q_LORA, 2 * GMLP_WIDTH + Q_LORA + KV_LORA
    u = jax.nn.gelu(z[:, :o_v])
    v = jax.nn.gelu(z[:, o_v:o_q])
    if maybe_vraw_ref:
        maybe_vraw_ref[0][0] = v

    vb = v.astype(BF16)
    pi = lax.broadcasted_iota(jnp.int32, (mix_len, mix_len), 0) // CHUNK
    pj = lax.broadcasted_iota(jnp.int32, (mix_len, mix_len), 1) // CHUNK
    for g in range(GMLP_GROUPS):
        wsg = jnp.where(pj <= pi, ws_ref[g, :mix_len, :mix_len], 0.0).astype(BF16)
        bg = bs_ref[:mix_len, g:g + 1]
        cols = slice(g * GMLP_HEAD, (g + 1) * GMLP_HEAD)
        for c in range(rows // mix_len):
            rws = slice(c * mix_len, (c + 1) * mix_len)
            mixed = jnp.dot(wsg, vb[rws, cols], preferred_element_type=F32) + bg
            ya_ref[0, rws, cols] = (u[rws, cols] * mixed).astype(ya_ref.dtype)

    lane = lax.broadcasted_iota(jnp.int32, (1, LANES), 1)
    is_nope = lane < QK_NOPE

    kr = z[:, o_kr:o_kr + KR_BLOCK]
    kr_ss = jnp.sum(jnp.where(lane < QK_ROPE, kr * kr, 0.0), axis=-1, keepdims=True)
    kr_p = kr * lax.rsqrt(kr_ss * (1.0 / QK_ROPE) + EPS) * krg_ref[...] * tabk_ref[...]
    krope = kr_p + pltpu.roll(kr_p, LANES - QK_ROPE, axis=1)
    kr_ref[0] = krope[:, :QK_ROPE]

    ckv_raw = z[:, o_kv:o_kr]
    ckv = ckv_raw * _rms(ckv_raw, KV_LORA) * kvg_ref[...]
    ckv_ref[0] = ckv
    kv = jnp.dot(ckv.astype(BF16), wukv_ref[...], preferred_element_type=F32)

    cq_raw = z[:, o_q:o_kv]
    cq = cq_raw * _rms(cq_raw, Q_LORA) * qng_ref[...]
    q = jnp.dot(cq.astype(BF16), wuq_ref[...], preferred_element_type=F32)

    qgain = qgain_ref[...]
    kgain = kgain_ref[...]
    tabq = tabq_ref[...]
    for hd in range(N_HEADS):
        cols = slice(hd * HEAD_BLOCK, (hd + 1) * HEAD_BLOCK)
        qh = q[:, cols]
        sq = qh * qh
        ss_n = jnp.sum(jnp.where(is_nope, sq, 0.0), axis=-1, keepdims=True)
        ss_r = jnp.sum(jnp.where((lane >= QK_NOPE) & (lane < QK_HEAD), sq, 0.0), axis=-1, keepdims=True)
        r = jnp.where(is_nope, lax.rsqrt(ss_n * (1.0 / QK_NOPE) + EPS), lax.rsqrt(ss_r * (1.0 / QK_ROPE) + EPS))
        q_ref[0, hd] = (qh * r * qgain * tabq).astype(q_ref.dtype)

        kvh = kv[:, cols]
        ks = jnp.sum(jnp.where(is_nope, kvh * kvh, 0.0), axis=-1, keepdims=True)
        kn = kvh * lax.rsqrt(ks * (1.0 / QK_NOPE) + EPS) * kgain
        k_ref[0, hd] = jnp.where(is_nope, kn, krope).astype(k_ref.dtype)
        v_ref[0, hd] = jnp.where(is_nope, 1.0, kvh).astype(v_ref.dtype)


def _pre(x, mod, wts, tabq, tabk, *, rows, mix_len, emit_v):
    b, t, d = x.shape
    grid = (b, t // rows)
    tok = lambda w: pl.BlockSpec((1, rows, w), lambda i, j: (i, j, 0))
    head = pl.BlockSpec((1, N_HEADS, rows, HEAD_BLOCK), lambda i, j: (i, 0, j, 0))
    tab = pl.BlockSpec((rows, LANES), lambda i, j: (j, 0))
    consts = [wts[n] for n in ("n1g", "w_in", "w_s", "b_s", "qng", "w_uq", "kvg", "w_ukv", "qgain", "kgain", "krg")]
    in_specs = ([tok(d), pl.BlockSpec((1,) + mod.shape[1:], lambda i, j: (i, 0, 0))]
                + [_const_spec(a.shape) for a in consts] + [tab, tab])
    out_shape = [jax.ShapeDtypeStruct((b, t, GMLP_WIDTH), BF16),
                 jax.ShapeDtypeStruct((b, N_HEADS, t, HEAD_BLOCK), BF16),
                 jax.ShapeDtypeStruct((b, N_HEADS, t, HEAD_BLOCK), BF16),
                 jax.ShapeDtypeStruct((b, N_HEADS, t, HEAD_BLOCK), BF16),
                 jax.ShapeDtypeStruct((b, t, KV_LORA), F32),
                 jax.ShapeDtypeStruct((b, t, QK_ROPE), F32)]
    out_specs = [tok(GMLP_WIDTH), head, head, head, tok(KV_LORA), tok(QK_ROPE)]
    if emit_v:
        out_shape.append(jax.ShapeDtypeStruct((b, t, GMLP_WIDTH), F32))
        out_specs.append(tok(GMLP_WIDTH))
    return pl.pallas_call(
        functools.partial(_pre_kernel, mix_len=mix_len),
        grid=grid, in_specs=in_specs, out_specs=out_specs, out_shape=out_shape,
        compiler_params=pltpu.CompilerParams(
            dimension_semantics=("parallel", "parallel"), vmem_limit_bytes=VMEM_LIMIT_BYTES),
        name="pre_v" if emit_v else "pre",
    )(x, mod, *consts, tabq, tabk)


def _pair_out(acc0, acc1):
    o0 = acc0 * pl.reciprocal(acc0[:, 0:1], approx=False)
    o1 = acc1 * pl.reciprocal(acc1[:, 0:1], approx=False)
    lane = lax.broadcasted_iota(jnp.int32, (1, LANES), 1)
    return jnp.where(lane < V_HEAD, pltpu.roll(o0, V_HEAD, axis=1), o1)


def _attn_online_kernel(q_ref, k_ref, v_ref, o_ref, m_ref, acc_ref, *, tq, tk):
    t = q_ref.shape[2]
    diag_tiles = tq // tk

    def q_tile(qi, carry):
        q0 = pl.multiple_of(qi * tq, tq)
        accs = []
        for hh in range(2):
            q = q_ref[0, hh, pl.ds(q0, tq), :]
            m_ref[...] = jnp.full(m_ref.shape, -jnp.inf, F32)
            acc_ref[...] = jnp.zeros(acc_ref.shape, F32)

            def kv_step(j, masked):
                k0 = pl.multiple_of(j * tk, tk)
                k = k_ref[0, hh, pl.ds(k0, tk), :]
                v = v_ref[0, hh, pl.ds(k0, tk), :]
                s = lax.dot_general(q, k, (((1,), (1,)), ((), ())), preferred_element_type=F32)
                if masked:
                    qc = (q0 + lax.broadcasted_iota(jnp.int32, (tq, 1), 0)) // CHUNK
                    kc = (k0 + lax.broadcasted_iota(jnp.int32, (1, tk), 1)) // CHUNK
                    s = jnp.where(kc <= qc, s, _NEG)
                m_old = m_ref[...]
                m_new = jnp.maximum(m_old, jnp.max(s, axis=-1, keepdims=True))
                p = jnp.exp2(s - m_new)
                acc_ref[...] = jnp.exp2(m_old - m_new) * acc_ref[...] + jnp.dot(
                    p.astype(BF16), v, preferred_element_type=F32)
                m_ref[...] = m_new

            def body(j, c):
                kv_step(j, False)
                return c

            lax.fori_loop(0, qi * diag_tiles, body, 0)
            for dd in range(diag_tiles):
                kv_step(qi * diag_tiles + dd, True)
            accs.append(acc_ref[...])
        o_ref[0, pl.ds(q0, tq), :] = _pair_out(accs[0], accs[1]).astype(o_ref.dtype)
        return carry

    lax.fori_loop(0, t // tq, q_tile, 0)


def _attn_bounded_kernel(q_ref, k_ref, v_ref, o_ref, acc_ref, *, tq):
    t = q_ref.shape[2]
    hb = tq // 2
    mb = hb // 2
    ri = lax.broadcasted_iota(jnp.int32, (mb, mb), 0) // CHUNK
    ci = lax.broadcasted_iota(jnp.int32, (mb, mb), 1) // CHUNK
    visible = ci <= ri

    def pv(hh, r0, rows, k0, keys, masked=False):
        q = q_ref[0, hh, pl.ds(r0, rows), :]
        k = k_ref[0, hh, pl.ds(k0, keys), :]
        v = v_ref[0, hh, pl.ds(k0, keys), :]
        p = jnp.exp2(lax.dot_general(q, k, (((1,), (1,)), ((), ())), preferred_element_type=F32))
        if masked:
            p = jnp.where(visible, p, 0.0)
        return jnp.dot(p.astype(BF16), v, preferred_element_type=F32)

    def diag(hh, r0):
        r1 = pl.multiple_of(r0 + mb, mb)
        top = pv(hh, r0, mb, r0, mb, True)
        bot = pv(hh, r1, mb, r0, mb) + pv(hh, r1, mb, r1, mb, True)
        return top, bot

    def q_tile(qi, carry):
        q0 = pl.multiple_of(qi * tq, tq)
        q1 = pl.multiple_of(q0 + hb, hb)
        for hh in range(2):
            top, bot = diag(hh, q0)
            acc_ref[hh, 0 * mb:1 * mb, :] = top
            acc_ref[hh, 1 * mb:2 * mb, :] = bot
            top, bot = diag(hh, q1)
            full = pv(hh, q1, hb, q0, hb)
            acc_ref[hh, 2 * mb:3 * mb, :] = top + full[:mb]
            acc_ref[hh, 3 * mb:4 * mb, :] = bot + full[mb:]

        def body(j, c):
            k0 = pl.multiple_of(j * hb, hb)
            for hh in range(2):
                acc_ref[hh] += pv(hh, q0, tq, k0, hb)
            return c

        lax.fori_loop(0, qi * 2, body, 0)
        o_ref[0, pl.ds(q0, tq), :] = _pair_out(acc_ref[0], acc_ref[1]).astype(o_ref.dtype)
        return carry

    lax.fori_loop(0, t // tq, q_tile, 0)


def _attn_prompt(q, k, v, score_bound):
    b, nh, t, _ = q.shape
    tq, tk = ATTN_ONLINE_TQ, ATTN_ONLINE_TK
    qkv = pl.BlockSpec((1, 2, t, HEAD_BLOCK), lambda i, j: (i, j, 0, 0))
    common = dict(
        grid=(b, nh // 2),
        in_specs=[qkv, qkv, qkv],
        out_specs=pl.BlockSpec((1, t, 2 * V_HEAD), lambda i, j: (i, 0, j)),
        out_shape=jax.ShapeDtypeStruct((b, t, MLA_WIDTH), BF16),
        compiler_params=pltpu.CompilerParams(
            dimension_semantics=("parallel", "parallel"), vmem_limit_bytes=VMEM_LIMIT_BYTES),
    )
    bounded = pl.pallas_call(
        functools.partial(_attn_bounded_kernel, tq=ATTN_BOUNDED_TQ),
        scratch_shapes=[pltpu.VMEM((2, ATTN_BOUNDED_TQ, HEAD_BLOCK), F32)],
        name="attn_prompt_bounded", **common)
    online = pl.pallas_call(
        functools.partial(_attn_online_kernel, tq=tq, tk=tk),
        scratch_shapes=[pltpu.VMEM((tq, 1), F32), pltpu.VMEM((tq, HEAD_BLOCK), F32)],
        name="attn_prompt_online", **common)
    return lax.cond(score_bound <= MAX_UNSHIFTED_SCORE, bounded, online, q, k, v)


def _attn_sample_kernel(q_ref, ckv_ref, kr_ref, wukv_ref, kgain_ref, o_ref):
    lane = lax.broadcasted_iota(jnp.int32, (1, LANES), 1)
    is_nope = lane < QK_NOPE
    kv = jnp.dot(ckv_ref[0], wukv_ref[...], preferred_element_type=F32)
    krope = kr_ref[0]
    kgain = kgain_ref[...]
    accs = []
    for hd in range(N_HEADS):
        kvh = kv[:, hd * HEAD_BLOCK:(hd + 1) * HEAD_BLOCK]
        ks = jnp.sum(jnp.where(is_nope, kvh * kvh, 0.0), axis=-1, keepdims=True)
        kn = kvh * lax.rsqrt(ks * (1.0 / QK_NOPE) + EPS) * kgain
        k = jnp.where(is_nope, kn, krope).astype(BF16)
        v = jnp.where(is_nope, 1.0, kvh).astype(BF16)
        s = lax.dot_general(q_ref[0, hd], k, (((1,), (1,)), ((), ())), preferred_element_type=F32)
        p = jnp.exp2(s - jnp.max(s, axis=-1, keepdims=True))
        accs.append(jnp.dot(p.astype(BF16), v, preferred_element_type=F32))
        if hd % 2 == 1:
            o_ref[0, :, (hd - 1) * V_HEAD:(hd + 1) * V_HEAD] = _pair_out(accs[hd - 1], accs[hd]).astype(o_ref.dtype)


def _attn_sample(q, ckv_all, kr_all, w_ukv, kgain):
    b, nh, t, _ = q.shape
    s = ckv_all.shape[1]
    return pl.pallas_call(
        _attn_sample_kernel,
        grid=(b,),
        in_specs=[pl.BlockSpec((1, nh, t, HEAD_BLOCK), lambda i: (i, 0, 0, 0)),
                  pl.BlockSpec((1, s, KV_LORA), lambda i: (i, 0, 0)),
                  pl.BlockSpec((1, s, LANES), lambda i: (i, 0, 0)),
                  _const_spec(w_ukv.shape), _const_spec(kgain.shape)],
        out_specs=pl.BlockSpec((1, t, MLA_WIDTH), lambda i: (i, 0, 0)),
        out_shape=jax.ShapeDtypeStruct((b, t, MLA_WIDTH), BF16),
        compiler_params=pltpu.CompilerParams(
            dimension_semantics=("parallel",), vmem_limit_bytes=VMEM_LIMIT_BYTES),
        name="attn_sample",
    )(q, ckv_all, kr_all, w_ukv, kgain)


def _post_kernel(x_ref, ya_ref, yb_ref, mod_ref, wout_ref, n2g_ref, wfi_ref, wfo_ref, o_ref, acc_ref):
    mod = mod_ref[0]
    g1, sh2, sc2, g2 = mod[2:3], mod[3:4], mod[4:5], mod[5:6]
    mix = (jnp.dot(ya_ref[0], wout_ref[:GMLP_WIDTH, :], preferred_element_type=F32)
           + jnp.dot(yb_ref[0], wout_ref[GMLP_WIDTH:, :], preferred_element_type=F32))
    x1 = x_ref[0] + g1 * mix
    h2 = (x1 * _rms(x1, x1.shape[-1]) * n2g_ref[...] * (1.0 + sc2) + sh2).astype(BF16)
    d_ff = wfo_ref.shape[0]
    acc_ref[...] = x1
    for c in range(d_ff // FFN_CHUNK):
        gate = jnp.dot(h2, wfi_ref[:, c * FFN_CHUNK:(c + 1) * FFN_CHUNK], preferred_element_type=F32)
        up = jnp.dot(h2, wfi_ref[:, d_ff + c * FFN_CHUNK:d_ff + (c + 1) * FFN_CHUNK], preferred_element_type=F32)
        act = (gate * jax.nn.sigmoid(gate) * up).astype(BF16)
        acc_ref[...] += g2 * jnp.dot(act, wfo_ref[c * FFN_CHUNK:(c + 1) * FFN_CHUNK, :], preferred_element_type=F32)
    o_ref[0] = acc_ref[...]


def _post(x, ya, yb, mod, wts, *, rows):
    b, t, d = x.shape
    tok = lambda w: pl.BlockSpec((1, rows, w), lambda i, j: (i, j, 0))
    consts = [wts[n] for n in ("w_out", "n2g", "w_ffn_in", "w_ffn_out")]
    return pl.pallas_call(
        _post_kernel,
        grid=(b, t // rows),
        in_specs=[tok(d), tok(GMLP_WIDTH), tok(MLA_WIDTH),
                  pl.BlockSpec((1,) + mod.shape[1:], lambda i, j: (i, 0, 0)),
                  _const_spec(consts[0].shape), _const_spec(consts[1].shape),
                  _const_spec(consts[2].shape), _const_spec(consts[3].shape)],
        out_specs=tok(d),
        out_shape=jax.ShapeDtypeStruct((b, t, d), F32),
        scratch_shapes=[pltpu.VMEM((rows, d), F32)],
        compiler_params=pltpu.CompilerParams(
            dimension_semantics=("parallel", "parallel"), vmem_limit_bytes=VMEM_LIMIT_BYTES),
        name="post",
    )(x, ya, yb, mod, consts[0], consts[1], consts[2], consts[3])


def _rope_tables(pos):
    half = QK_ROPE // 2
    inv = ROPE_THETA ** (-jnp.arange(half, dtype=F32) / half)
    ang = pos.astype(F32)[:, None] * inv[None, :]
    cos, sin = jnp.cos(ang), jnp.sin(ang)
    rot = jnp.concatenate([cos, cos, -sin, sin], axis=-1)
    tabq = (SCALE * LOG2E) * jnp.concatenate([jnp.ones((pos.shape[0], QK_NOPE), F32), rot], axis=-1)
    tabk = jnp.concatenate([rot, rot], axis=-1)
    return tabq, tabk


def _prep_weights(w_in, w_s, b_s, q_norm_g, w_uq, kv_norm_g, w_ukv, qn_g, qr_g, kn_g, kr_g,
                  norm1_g, w_out, norm2_g, w_ffn_in, w_ffn_out):
    o_kr = 2 * GMLP_WIDTH + Q_LORA + KV_LORA
    w_kr = w_in[:, o_kr:o_kr + QK_ROPE]
    w_kr_sw = _swap_halves(w_kr)
    w_in_p = jnp.concatenate([w_in[:, :o_kr], w_kr, w_kr_sw, w_kr, w_kr_sw], axis=-1)
    d_q = w_uq.shape[0]
    wq = w_uq.reshape(d_q, N_HEADS, QK_HEAD)
    wq_r = wq[:, :, QK_NOPE:]
    w_uq_p = jnp.concatenate([wq, _swap_halves(wq_r)], axis=-1).reshape(d_q, N_HEADS * HEAD_BLOCK)
    row = lambda a: a.reshape(1, -1).astype(F32)
    return {
        "n1g": row(norm1_g), "w_in": w_in_p.astype(BF16), "w_s": w_s.astype(F32), "b_s": b_s.T.astype(F32),
        "qng": row(q_norm_g), "w_uq": w_uq_p.astype(BF16), "kvg": row(kv_norm_g), "w_ukv": w_ukv.astype(BF16),
        "qgain": row(jnp.concatenate([qn_g, qr_g, _swap_halves(qr_g)])),
        "kgain": row(jnp.concatenate([kn_g, jnp.ones((HEAD_BLOCK - QK_NOPE,), F32)])),
        "krg": row(jnp.concatenate([kr_g, _swap_halves(kr_g), kr_g, _swap_halves(kr_g)])),
        "w_out": w_out.astype(BF16), "n2g": row(norm2_g),
        "w_ffn_in": w_ffn_in.astype(BF16), "w_ffn_out": w_ffn_out.astype(BF16),
    }


def kernel(x_prompt, x_sample, cache_ckv, cache_krope, c_prompt, c_sample, w_ada, b_ada, norm1_g, w_in, w_s, b_s, q_norm_g, w_uq, kv_norm_g, w_ukv, qn_g, qr_g, kn_g, kr_g, w_out, norm2_g, w_ffn_in, w_ffn_out):
    depth = w_in.shape[0]
    assert depth == 1, "single trunk layer"
    bp, tp, d = x_prompt.shape
    bs, ts, _ = x_sample.shape
    past_len = cache_ckv.shape[2]

    wts = _prep_weights(w_in[0], w_s[0], b_s[0], q_norm_g[0], w_uq[0], kv_norm_g[0], w_ukv[0], qn_g[0], qr_g[0],
                        kn_g[0], kr_g[0], norm1_g[0], w_out[0], norm2_g[0], w_ffn_in[0], w_ffn_out[0])
    mod = _ada(jnp.concatenate([c_prompt, c_sample], axis=0), w_ada[0], b_ada[0]).reshape(bp + bs, 6, d)
    mod_p, mod_s = mod[:bp], mod[bp:]

    tabq_p, tabk_p = _rope_tables(jnp.arange(tp))
    ya_p, q_p, k_p, v_p, ckv_p, kr_p = _pre(x_prompt, mod_p, wts, tabq_p, tabk_p,
                                            rows=PRE_ROWS, mix_len=GMLP_CHUNK, emit_v=False)
    gmax = lambda g: jnp.max(jnp.abs(g))
    score_bound = SCALE * (QK_NOPE * gmax(qn_g) * gmax(kn_g) + QK_ROPE * gmax(qr_g) * gmax(kr_g))
    yb_p = _attn_prompt(q_p, k_p, v_p, score_bound)
    y_p = _post(x_prompt, ya_p, yb_p, mod_p, wts, rows=POST_ROWS)

    tabq_s, tabk_s = _rope_tables(past_len + jnp.arange(ts))
    ya_s, q_s, _, _, ckv_s, kr_s, v_s = _pre(x_sample, mod_s, wts, tabq_s, tabk_s,
                                             rows=ts, mix_len=min(ts, GMLP_CHUNK), emit_v=True)
    ckv_all = jnp.concatenate([cache_ckv[0], ckv_s], axis=1).astype(BF16)
    kr_all = jnp.tile(jnp.concatenate([cache_krope[0], kr_s], axis=1), (1, 1, LANES // QK_ROPE))
    yb_s = _attn_sample(q_s, ckv_all, kr_all, wts["w_ukv"], wts["kgain"])
    y_s = _post(x_sample, ya_s, yb_s, mod_s, wts, rows=ts)

    return (y_p, y_s, ckv_p[None], kr_p[None], ckv_s[None], kr_s[None], v_s[None])
```

```python
import functools
import math

import jax
import jax.numpy as jnp
import numpy as np
from jax import lax
from jax.experimental import pallas as pl
from jax.experimental.pallas import tpu as pltpu

F32 = jnp.float32
BF16 = jnp.bfloat16

CHUNK = 64
GMLP_CHUNK = 128
GMLP_GROUPS = 4
GMLP_HEAD = 128
GMLP_WIDTH = GMLP_GROUPS * GMLP_HEAD
N_HEADS = 8
QK_NOPE = 64
QK_ROPE = 32
QK_HEAD = QK_NOPE + QK_ROPE
V_HEAD = 64
Q_LORA = 384
KV_LORA = 256
MLA_WIDTH = N_HEADS * V_HEAD
ROPE_THETA = 10000.0
EPS = 1e-6
SCALE = QK_HEAD ** -0.5

LANES = 128
HEAD_BLOCK = LANES
KR_BLOCK = LANES
IN_WIDTH_PADDED = 2 * GMLP_WIDTH + Q_LORA + KV_LORA + KR_BLOCK
VMEM_LIMIT_BYTES = 56 * 1024 * 1024

PRE_ROWS = 512
POST_ROWS = 512
FFN_CHUNK = 256
ATTN_BOUNDED_TQ = 1024
ATTN_ONLINE_TQ = 512
ATTN_ONLINE_TK = 256

_NEG = float(jnp.finfo(jnp.float32).min)
LOG2E = math.log2(math.e)
MAX_UNSHIFTED_SCORE = 60.0


def _swap_halves(a, axis=-1):
    n = a.shape[axis] // 2
    lo = lax.slice_in_dim(a, 0, n, axis=axis)
    hi = lax.slice_in_dim(a, n, 2 * n, axis=axis)
    return jnp.concatenate([hi, lo], axis=axis)


def _const_spec(shape):
    nd = len(shape)
    return pl.BlockSpec(shape, lambda *_: (0,) * nd, pipeline_mode=pl.Buffered(1))


def _ada_kernel(c_ref, w_ref, b_ref, o_ref):
    c = c_ref[...]
    a = c * jax.nn.sigmoid(c)
    o_ref[...] = jnp.dot(a, w_ref[...], preferred_element_type=F32,
                         precision=lax.Precision.HIGHEST) + b_ref[...]


def _ada(c_all, w_ada, b_ada):
    n, d = c_all.shape
    width = w_ada.shape[1]
    bn = d
    return pl.pallas_call(
        _ada_kernel,
        grid=(width // bn,),
        in_specs=[pl.BlockSpec((n, d), lambda j: (0, 0)),
                  pl.BlockSpec((d, bn), lambda j: (0, j)),
                  pl.BlockSpec((1, bn), lambda j: (0, j))],
        out_specs=pl.BlockSpec((n, bn), lambda j: (0, j)),
        out_shape=jax.ShapeDtypeStruct((n, width), F32),
        compiler_params=pltpu.CompilerParams(vmem_limit_bytes=VMEM_LIMIT_BYTES),
        name="ada",
    )(c_all, w_ada, b_ada.reshape(1, width))


def _rms(x, width):
    return lax.rsqrt(jnp.sum(x * x, axis=-1, keepdims=True) * (1.0 / width) + EPS)


def _pre_kernel(x_ref, mod_ref, n1g_ref, win_ref, ws_ref, bs_ref, qng_ref, wuq_ref,
                kvg_ref, wukv_ref, qgain_ref, kgain_ref, krg_ref, tabq_ref, tabk_ref,
                ya_ref, q_ref, k_ref, v_ref, ckv_ref, kr_ref, *maybe_vraw_ref, mix_len):
    rows = x_ref.shape[1]
    x = x_ref[0]
    mod = mod_ref[0]
    sh1, sc1 = mod[0:1], mod[1:2]
    h = x * _rms(x, x.shape[-1]) * (n1g_ref[...] * (1.0 + sc1)) + sh1
    z = jnp.dot(h.astype(BF16), win_ref[...], preferred_element_type=F32)

    o_v, o_q, o_kv, o_kr = GMLP_WIDTH, 2 * GMLP_WIDTH, 2 * GMLP_WIDTH + Q_LORA, 2 * GMLP_WIDTH + Q_LORA + KV_LORA
    u = jax.nn.gelu(z[:, :o_v])
    v = jax.nn.gelu(z[:, o_v:o_q])
    if maybe_vraw_ref:
        maybe_vraw_ref[0][0] = v

    vb = v.astype(BF16)
    pi = lax.broadcasted_iota(jnp.int32, (mix_len, mix_len), 0) // CHUNK
    pj = lax.broadcasted_iota(jnp.int32, (mix_len, mix_len), 1) // CHUNK
    for g in range(GMLP_GROUPS):
        wsg = jnp.where(pj <= pi, ws_ref[g, :mix_len, :mix_len], 0.0).astype(BF16)
        bg = bs_ref[:mix_len, g:g + 1]
        cols = slice(g * GMLP_HEAD, (g + 1) * GMLP_HEAD)
        for c in range(rows // mix_len):
            rws = slice(c * mix_len, (c + 1) * mix_len)
            mixed = jnp.dot(wsg, vb[rws, cols], preferred_element_type=F32) + bg
            ya_ref[0, rws, cols] = (u[rws, cols] * mixed).astype(ya_ref.dtype)

    lane = lax.broadcasted_iota(jnp.int32, (1, LANES), 1)
    is_nope = lane < QK_NOPE

    kr = z[:, o_kr:o_kr + KR_BLOCK]
    kr_ss = jnp.sum(jnp.where(lane < QK_ROPE, kr * kr, 0.0), axis=-1, keepdims=True)
    kr_p = kr * lax.rsqrt(kr_ss * (1.0 / QK_ROPE) + EPS) * krg_ref[...] * tabk_ref[...]
    krope = kr_p + pltpu.roll(kr_p, LANES - QK_ROPE, axis=1)
    kr_ref[0] = krope[:, :QK_ROPE]

    ckv_raw = z[:, o_kv:o_kr]
    ckv = ckv_raw * _rms(ckv_raw, KV_LORA) * kvg_ref[...]
    ckv_ref[0] = ckv
    kv = jnp.dot(ckv.astype(BF16), wukv_ref[...], preferred_element_type=F32)

    cq_raw = z[:, o_q:o_kv]
    cq = cq_raw * _rms(cq_raw, Q_LORA) * qng_ref[...]
    q = jnp.dot(cq.astype(BF16), wuq_ref[...], preferred_element_type=F32)

    qfac = tabq_ref[...] * (qgain_ref[...] * kgain_ref[...])
    for hd in range(N_HEADS):
        cols = slice(hd * HEAD_BLOCK, (hd + 1) * HEAD_BLOCK)
        qh = q[:, cols]
        sq = qh * qh
        ss_all = jnp.sum(sq, axis=-1, keepdims=True)
        ss_n = jnp.sum(jnp.where(is_nope, sq, 0.0), axis=-1, keepdims=True)
        ss = jnp.where(is_nope, ss_n, ss_all - ss_n)
        q_ref[0, hd] = (qh * lax.rsqrt(ss * (1.0 / QK_NOPE) + EPS) * qfac).astype(q_ref.dtype)

        kvh = kv[:, cols]
        ks = jnp.sum(jnp.where(is_nope, kvh * kvh, 0.0), axis=-1, keepdims=True)
        kn = kvh * lax.rsqrt(ks * (1.0 / QK_NOPE) + EPS)
        k_ref[0, hd] = jnp.where(is_nope, kn, krope).astype(k_ref.dtype)
        v_ref[0, hd] = jnp.where(is_nope, 1.0, kvh).astype(v_ref.dtype)


def _pre(x, mod, wts, tabq, tabk, *, rows, mix_len, emit_v):
    b, t, d = x.shape
    grid = (b, t // rows)
    tok = lambda w: pl.BlockSpec((1, rows, w), lambda i, j: (i, j, 0))
    head = pl.BlockSpec((1, N_HEADS, rows, HEAD_BLOCK), lambda i, j: (i, 0, j, 0))
    tab = pl.BlockSpec((rows, LANES), lambda i, j: (j, 0))
    consts = [wts[n] for n in ("n1g", "w_in", "w_s", "b_s", "qng", "w_uq", "kvg", "w_ukv", "qgain", "kgain", "krg")]
    in_specs = ([tok(d), pl.BlockSpec((1,) + mod.shape[1:], lambda i, j: (i, 0, 0))]
                + [_const_spec(a.shape) for a in consts] + [tab, tab])
    out_shape = [jax.ShapeDtypeStruct((b, t, GMLP_WIDTH), BF16),
                 jax.ShapeDtypeStruct((b, N_HEADS, t, HEAD_BLOCK), BF16),
                 jax.ShapeDtypeStruct((b, N_HEADS, t, HEAD_BLOCK), BF16),
                 jax.ShapeDtypeStruct((b, N_HEADS, t, HEAD_BLOCK), BF16),
                 jax.ShapeDtypeStruct((b, t, KV_LORA), F32),
                 jax.ShapeDtypeStruct((b, t, QK_ROPE), F32)]
    out_specs = [tok(GMLP_WIDTH), head, head, head, tok(KV_LORA), tok(QK_ROPE)]
    if emit_v:
        out_shape.append(jax.ShapeDtypeStruct((b, t, GMLP_WIDTH), F32))
        out_specs.append(tok(GMLP_WIDTH))
    return pl.pallas_call(
        functools.partial(_pre_kernel, mix_len=mix_len),
        grid=grid, in_specs=in_specs, out_specs=out_specs, out_shape=out_shape,
        compiler_params=pltpu.CompilerParams(
            dimension_semantics=("parallel", "parallel"), vmem_limit_bytes=VMEM_LIMIT_BYTES),
        name="pre_v" if emit_v else "pre",
    )(x, mod, *consts, tabq, tabk)


def _pair_out(acc0, acc1):
    first = lax.broadcasted_iota(jnp.int32, (1, LANES), 1) < V_HEAD
    num = jnp.where(first, pltpu.roll(acc0, V_HEAD, axis=1), acc1)
    den = jnp.where(first, acc0, pltpu.roll(acc1, V_HEAD, axis=1))
    return num * pl.reciprocal(den, approx=False)


def _attn_online_kernel(q_ref, k_ref, v_ref, o_ref, m_ref, acc_ref, *, tq, tk):
    t = q_ref.shape[2]
    diag_tiles = tq // tk

    def q_tile(qi, carry):
        q0 = pl.multiple_of(qi * tq, tq)
        accs = []
        for hh in range(2):
            q = q_ref[0, hh, pl.ds(q0, tq), :]
            m_ref[...] = jnp.full(m_ref.shape, -jnp.inf, F32)
            acc_ref[...] = jnp.zeros(acc_ref.shape, F32)

            def kv_step(j, masked):
                k0 = pl.multiple_of(j * tk, tk)
                k = k_ref[0, hh, pl.ds(k0, tk), :]
                v = v_ref[0, hh, pl.ds(k0, tk), :]
                s = lax.dot_general(q, k, (((1,), (1,)), ((), ())), preferred_element_type=F32)
                if masked:
                    qc = (q0 + lax.broadcasted_iota(jnp.int32, (tq, 1), 0)) // CHUNK
                    kc = (k0 + lax.broadcasted_iota(jnp.int32, (1, tk), 1)) // CHUNK
                    s = jnp.where(kc <= qc, s, _NEG)
                m_old = m_ref[...]
                m_new = jnp.maximum(m_old, jnp.max(s, axis=-1, keepdims=True))
                p = jnp.exp2(s - m_new)
                acc_ref[...] = jnp.exp2(m_old - m_new) * acc_ref[...] + jnp.dot(
                    p.astype(BF16), v, preferred_element_type=F32)
                m_ref[...] = m_new

            def body(j, c):
                kv_step(j, False)
                return c

            lax.fori_loop(0, qi * diag_tiles, body, 0)
            for dd in range(diag_tiles):
                kv_step(qi * diag_tiles + dd, True)
            accs.append(acc_ref[...])
        o_ref[0, pl.ds(q0, tq), :] = _pair_out(accs[0], accs[1]).astype(o_ref.dtype)
        return carry

    lax.fori_loop(0, t // tq, q_tile, 0)


def _attn_bounded_kernel(q_ref, k_ref, v_ref, o_ref, acc_ref, *, tq):
    t = q_ref.shape[2]
    hb = tq // 2
    mb = hb // 2
    ri = lax.broadcasted_iota(jnp.int32, (mb, mb), 0) // CHUNK
    ci = lax.broadcasted_iota(jnp.int32, (mb, mb), 1) // CHUNK
    visible = ci <= ri

    def pv(hh, r0, rows, k0, keys, masked=False):
        q = q_ref[0, hh, pl.ds(r0, rows), :]
        k = k_ref[0, hh, pl.ds(k0, keys), :]
        v = v_ref[0, hh, pl.ds(k0, keys), :]
        p = jnp.exp2(lax.dot_general(q, k, (((1,), (1,)), ((), ())), preferred_element_type=F32))
        if masked:
            p = jnp.where(visible, p, 0.0)
        return jnp.dot(p.astype(BF16), v, preferred_element_type=F32)

    def diag(hh, r0):
        r1 = pl.multiple_of(r0 + mb, mb)
        top = pv(hh, r0, mb, r0, mb, True)
        bot = pv(hh, r1, mb, r0, mb) + pv(hh, r1, mb, r1, mb, True)
        return top, bot

    def q_tile(qi, carry):
        q0 = pl.multiple_of(qi * tq, tq)
        q1 = pl.multiple_of(q0 + hb, hb)
        acc_ref[...] = jnp.zeros(acc_ref.shape, F32)

        def body(j, c):
            k0 = pl.multiple_of(j * hb, hb)
            for hh in range(2):
                acc_ref[hh] += pv(hh, q0, tq, k0, hb)
            return c

        lax.fori_loop(0, qi * 2, body, 0)

        parts = []
        for hh in range(2):
            top0, bot0 = diag(hh, q0)
            top1, bot1 = diag(hh, q1)
            full = pv(hh, q1, hb, q0, hb)
            parts.append([top0, bot0, top1 + full[:mb], bot1 + full[mb:]])
        for c in range(tq // mb):
            rows = slice(c * mb, (c + 1) * mb)
            out = _pair_out(acc_ref[0, rows, :] + parts[0][c], acc_ref[1, rows, :] + parts[1][c])
            o_ref[0, pl.ds(pl.multiple_of(q0 + c * mb, mb), mb), :] = out.astype(o_ref.dtype)
        return carry

    lax.fori_loop(0, t // tq, q_tile, 0)


def _attn_prompt(q, k, v, score_bound):
    b, nh, t, _ = q.shape
    tq, tk = ATTN_ONLINE_TQ, ATTN_ONLINE_TK
    qkv = pl.BlockSpec((1, 2, t, HEAD_BLOCK), lambda i, j: (i, j, 0, 0))
    common = dict(
        grid=(b, nh // 2),
        in_specs=[qkv, qkv, qkv],
        out_specs=pl.BlockSpec((1, t, 2 * V_HEAD), lambda i, j: (i, 0, j)),
        out_shape=jax.ShapeDtypeStruct((b, t, MLA_WIDTH), BF16),
        compiler_params=pltpu.CompilerParams(
            dimension_semantics=("parallel", "parallel"), vmem_limit_bytes=VMEM_LIMIT_BYTES),
    )
    bounded = pl.pallas_call(
        functools.partial(_attn_bounded_kernel, tq=ATTN_BOUNDED_TQ),
        scratch_shapes=[pltpu.VMEM((2, ATTN_BOUNDED_TQ, HEAD_BLOCK), F32)],
        name="attn_prompt_bounded", **common)
    online = pl.pallas_call(
        functools.partial(_attn_online_kernel, tq=tq, tk=tk),
        scratch_shapes=[pltpu.VMEM((tq, 1), F32), pltpu.VMEM((tq, HEAD_BLOCK), F32)],
        name="attn_prompt_online", **common)
    return lax.cond(score_bound <= MAX_UNSHIFTED_SCORE, bounded, online, q, k, v)


def _attn_sample_kernel(q_ref, ckv_ref, kr_ref, wukv_ref, o_ref):
    lane = lax.broadcasted_iota(jnp.int32, (1, LANES), 1)
    is_nope = lane < QK_NOPE
    kv = jnp.dot(ckv_ref[0], wukv_ref[...], preferred_element_type=F32)
    krope = kr_ref[0]
    accs = []
    for hd in range(N_HEADS):
        kvh = kv[:, hd * HEAD_BLOCK:(hd + 1) * HEAD_BLOCK]
        ks = jnp.sum(jnp.where(is_nope, kvh * kvh, 0.0), axis=-1, keepdims=True)
        kn = kvh * lax.rsqrt(ks * (1.0 / QK_NOPE) + EPS)
        k = jnp.where(is_nope, kn, krope).astype(BF16)
        v = jnp.where(is_nope, 1.0, kvh).astype(BF16)
        s = lax.dot_general(q_ref[0, hd], k, (((1,), (1,)), ((), ())), preferred_element_type=F32)
        p = jnp.exp2(s - jnp.max(s, axis=-1, keepdims=True))
        accs.append(jnp.dot(p.astype(BF16), v, preferred_element_type=F32))
        if hd % 2 == 1:
            o_ref[0, :, (hd - 1) * V_HEAD:(hd + 1) * V_HEAD] = _pair_out(accs[hd - 1], accs[hd]).astype(o_ref.dtype)


def _attn_sample(q, ckv_all, kr_all, w_ukv):
    b, nh, t, _ = q.shape
    s = ckv_all.shape[1]
    return pl.pallas_call(
        _attn_sample_kernel,
        grid=(b,),
        in_specs=[pl.BlockSpec((1, nh, t, HEAD_BLOCK), lambda i: (i, 0, 0, 0)),
                  pl.BlockSpec((1, s, KV_LORA), lambda i: (i, 0, 0)),
                  pl.BlockSpec((1, s, LANES), lambda i: (i, 0, 0)),
                  _const_spec(w_ukv.shape)],
        out_specs=pl.BlockSpec((1, t, MLA_WIDTH), lambda i: (i, 0, 0)),
        out_shape=jax.ShapeDtypeStruct((b, t, MLA_WIDTH), BF16),
        compiler_params=pltpu.CompilerParams(
            dimension_semantics=("parallel",), vmem_limit_bytes=VMEM_LIMIT_BYTES),
        name="attn_sample",
    )(q, ckv_all, kr_all, w_ukv)


def _post_kernel(x_ref, ya_ref, yb_ref, mod_ref, wout_ref, n2g_ref, wfi_ref, wfo_ref, o_ref, acc_ref):
    mod = mod_ref[0]
    g1, sh2, sc2, g2 = mod[2:3], mod[3:4], mod[4:5], mod[5:6]
    mix = (jnp.dot(ya_ref[0], wout_ref[:GMLP_WIDTH, :], preferred_element_type=F32)
           + jnp.dot(yb_ref[0], wout_ref[GMLP_WIDTH:, :], preferred_element_type=F32))
    x1 = x_ref[0] + g1 * mix
    h2 = (x1 * _rms(x1, x1.shape[-1]) * n2g_ref[...] * (1.0 + sc2) + sh2).astype(BF16)
    d_ff = wfo_ref.shape[0]
    acc_ref[...] = x1
    for c in range(d_ff // FFN_CHUNK):
        gate = jnp.dot(h2, wfi_ref[:, c * FFN_CHUNK:(c + 1) * FFN_CHUNK], preferred_element_type=F32)
        up = jnp.dot(h2, wfi_ref[:, d_ff + c * FFN_CHUNK:d_ff + (c + 1) * FFN_CHUNK], preferred_element_type=F32)
        act = (gate * jax.nn.sigmoid(gate) * up).astype(BF16)
        acc_ref[...] += g2 * jnp.dot(act, wfo_ref[c * FFN_CHUNK:(c + 1) * FFN_CHUNK, :], preferred_element_type=F32)
    o_ref[0] = acc_ref[...]


def _post(x, ya, yb, mod, wts, *, rows):
    b, t, d = x.shape
    tok = lambda w: pl.BlockSpec((1, rows, w), lambda i, j: (i, j, 0))
    consts = [wts[n] for n in ("w_out", "n2g", "w_ffn_in", "w_ffn_out")]
    return pl.pallas_call(
        _post_kernel,
        grid=(b, t // rows),
        in_specs=[tok(d), tok(GMLP_WIDTH), tok(MLA_WIDTH),
                  pl.BlockSpec((1,) + mod.shape[1:], lambda i, j: (i, 0, 0)),
                  _const_spec(consts[0].shape), _const_spec(consts[1].shape),
                  _const_spec(consts[2].shape), _const_spec(consts[3].shape)],
        out_specs=tok(d),
        out_shape=jax.ShapeDtypeStruct((b, t, d), F32),
        scratch_shapes=[pltpu.VMEM((rows, d), F32)],
        compiler_params=pltpu.CompilerParams(
            dimension_semantics=("parallel", "parallel"), vmem_limit_bytes=VMEM_LIMIT_BYTES),
        name="post",
    )(x, ya, yb, mod, consts[0], consts[1], consts[2], consts[3])


def _rope_tables(first_pos, n):
    half = QK_ROPE // 2
    inv = ROPE_THETA ** (-np.arange(half, dtype=np.float64) / half)
    ang = (first_pos + np.arange(n, dtype=np.float64))[:, None] * inv[None, :]
    cos, sin = np.cos(ang), np.sin(ang)
    rot = np.concatenate([cos, cos, -sin, sin], axis=-1)
    tabq = (SCALE * LOG2E) * np.concatenate([np.ones((n, QK_NOPE)), rot], axis=-1)
    tabk = np.concatenate([rot, rot], axis=-1)
    return jnp.asarray(tabq, F32), jnp.asarray(tabk, F32)


def _prep_weights(w_in, w_s, b_s, q_norm_g, w_uq, kv_norm_g, w_ukv, qn_g, qr_g, kn_g, kr_g,
                  norm1_g, w_out, norm2_g, w_ffn_in, w_ffn_out):
    o_kr = 2 * GMLP_WIDTH + Q_LORA + KV_LORA
    w_kr = w_in[:, o_kr:o_kr + QK_ROPE]
    w_kr_sw = _swap_halves(w_kr)
    w_in_p = jnp.concatenate([w_in[:, :o_kr], w_kr, w_kr_sw, w_kr, w_kr_sw], axis=-1)
    d_q = w_uq.shape[0]
    wq = w_uq.reshape(d_q, N_HEADS, QK_HEAD)
    wq_r = wq[:, :, QK_NOPE:]
    w_uq_p = jnp.concatenate([wq, _swap_halves(wq_r)], axis=-1).reshape(d_q, N_HEADS * HEAD_BLOCK)
    row = lambda a: a.reshape(1, -1).astype(F32)
    return {
        "n1g": row(norm1_g), "w_in": w_in_p.astype(BF16), "w_s": w_s.astype(F32), "b_s": b_s.T.astype(F32),
        "qng": row(q_norm_g), "w_uq": w_uq_p.astype(BF16), "kvg": row(kv_norm_g), "w_ukv": w_ukv.astype(BF16),
        "qgain": row(jnp.concatenate([qn_g, qr_g, _swap_halves(qr_g)])),
        "kgain": row(jnp.concatenate([kn_g, jnp.ones((HEAD_BLOCK - QK_NOPE,), F32)])),
        "krg": row(jnp.concatenate([kr_g, _swap_halves(kr_g), kr_g, _swap_halves(kr_g)])),
        "w_out": w_out.astype(BF16), "n2g": row(norm2_g),
        "w_ffn_in": w_ffn_in.astype(BF16), "w_ffn_out": w_ffn_out.astype(BF16),
    }


def kernel(x_prompt, x_sample, cache_ckv, cache_krope, c_prompt, c_sample, w_ada, b_ada, norm1_g, w_in, w_s, b_s, q_norm_g, w_uq, kv_norm_g, w_ukv, qn_g, qr_g, kn_g, kr_g, w_out, norm2_g, w_ffn_in, w_ffn_out):
    depth = w_in.shape[0]
    assert depth == 1, "single trunk layer"
    bp, tp, d = x_prompt.shape
    bs, ts, _ = x_sample.shape
    past_len = cache_ckv.shape[2]

    wts = _prep_weights(w_in[0], w_s[0], b_s[0], q_norm_g[0], w_uq[0], kv_norm_g[0], w_ukv[0], qn_g[0], qr_g[0],
                        kn_g[0], kr_g[0], norm1_g[0], w_out[0], norm2_g[0], w_ffn_in[0], w_ffn_out[0])
    mod = _ada(jnp.concatenate([c_prompt, c_sample], axis=0), w_ada[0], b_ada[0]).reshape(bp + bs, 6, d)
    mod_p, mod_s = mod[:bp], mod[bp:]

    tabq_p, tabk_p = _rope_tables(0, tp)
    ya_p, q_p, k_p, v_p, ckv_p, kr_p = _pre(x_prompt, mod_p, wts, tabq_p, tabk_p,
                                            rows=PRE_ROWS, mix_len=GMLP_CHUNK, emit_v=False)
    gmax = lambda g: jnp.max(jnp.abs(g))
    score_bound = SCALE * (QK_NOPE * gmax(qn_g) * gmax(kn_g) + QK_ROPE * gmax(qr_g) * gmax(kr_g))
    yb_p = _attn_prompt(q_p, k_p, v_p, score_bound)
    y_p = _post(x_prompt, ya_p, yb_p, mod_p, wts, rows=POST_ROWS)

    tabq_s, tabk_s = _rope_tables(past_len, ts)
    ya_s, q_s, _, _, ckv_s, kr_s, v_s = _pre(x_sample, mod_s, wts, tabq_s, tabk_s,
                                             rows=ts, mix_len=min(ts, GMLP_CHUNK), emit_v=True)
    ckv_all = jnp.concatenate([cache_ckv[0], ckv_s], axis=1).astype(BF16)
    kr_all = jnp.tile(jnp.concatenate([cache_krope[0], kr_s], axis=1), (1, 1, LANES // QK_ROPE))
    yb_s = _attn_sample(q_s, ckv_all, kr_all, wts["w_ukv"])
    y_s = _post(x_sample, ya_s, yb_s, mod_s, wts, rows=ts)

    return (y_p, y_s, ckv_p[None], kr_p[None], ckv_s[None], kr_s[None], v_s[None])
```

```python
import functools
import math

import jax
import jax.numpy as jnp
import numpy as np
from jax import lax
from jax.experimental import pallas as pl
from jax.experimental.pallas import tpu as pltpu

F32 = jnp.float32
BF16 = jnp.bfloat16

CHUNK = 64
GMLP_CHUNK = 128
GMLP_GROUPS = 4
GMLP_HEAD = 128
GMLP_WIDTH = GMLP_GROUPS * GMLP_HEAD
N_HEADS = 8
QK_NOPE = 64
QK_ROPE = 32
QK_HEAD = QK_NOPE + QK_ROPE
V_HEAD = 64
Q_LORA = 384
KV_LORA = 256
MLA_WIDTH = N_HEADS * V_HEAD
ROPE_THETA = 10000.0
EPS = 1e-6
SCALE = QK_HEAD ** -0.5

LANES = 128
HEAD_BLOCK = LANES
KR_BLOCK = LANES
IN_WIDTH_PADDED = 2 * GMLP_WIDTH + Q_LORA + KV_LORA + KR_BLOCK
VMEM_LIMIT_BYTES = 56 * 1024 * 1024

PRE_ROWS = 512
POST_ROWS = 512
FFN_CHUNK = 256
ATTN_BOUNDED_TQ = 1024
ATTN_ONLINE_TQ = 512
ATTN_ONLINE_TK = 256

_NEG = float(jnp.finfo(jnp.float32).min)
LOG2E = math.log2(math.e)
MAX_UNSHIFTED_SCORE = 60.0


def _swap_halves(a, axis=-1):
    n = a.shape[axis] // 2
    lo = lax.slice_in_dim(a, 0, n, axis=axis)
    hi = lax.slice_in_dim(a, n, 2 * n, axis=axis)
    return jnp.concatenate([hi, lo], axis=axis)


def _const_spec(shape):
    nd = len(shape)
    return pl.BlockSpec(shape, lambda *_: (0,) * nd, pipeline_mode=pl.Buffered(1))


def _ada_kernel(c_ref, w_ref, b_ref, o_ref):
    c = c_ref[...]
    a = c * jax.nn.sigmoid(c)
    o_ref[...] = jnp.dot(a, w_ref[...], preferred_element_type=F32,
                         precision=lax.Precision.HIGHEST) + b_ref[...]


def _ada(c_all, w_ada, b_ada):
    n, d = c_all.shape
    width = w_ada.shape[1]
    bn = d
    return pl.pallas_call(
        _ada_kernel,
        grid=(width // bn,),
        in_specs=[pl.BlockSpec((n, d), lambda j: (0, 0)),
                  pl.BlockSpec((d, bn), lambda j: (0, j)),
                  pl.BlockSpec((1, bn), lambda j: (0, j))],
        out_specs=pl.BlockSpec((n, bn), lambda j: (0, j)),
        out_shape=jax.ShapeDtypeStruct((n, width), F32),
        compiler_params=pltpu.CompilerParams(vmem_limit_bytes=VMEM_LIMIT_BYTES),
        name="ada",
    )(c_all, w_ada, b_ada.reshape(1, width))


def _rms(x, width):
    return lax.rsqrt(jnp.sum(x * x, axis=-1, keepdims=True) * (1.0 / width) + EPS)


def _pre_in_proj(x_ref, mod_ref, n1g_ref, win_ref):
    x = x_ref[0]
    mod = mod_ref[0]
    sh1, sc1 = mod[0:1], mod[1:2]
    h = x * _rms(x, x.shape[-1]) * (n1g_ref[...] * (1.0 + sc1)) + sh1
    return jnp.dot(h.astype(BF16), win_ref[...], preferred_element_type=F32)


def _pre_heads(z_ref, ws_ref, bs_ref, qng_ref, wuq_ref, kvg_ref, wukv_ref, qgain_ref, kgain_ref, krg_ref,
               tabq_ref, tabk_ref, ya_ref, q_ref, k_ref, v_ref, ckv_ref, kr_ref, maybe_vraw_ref, mix_len):
    rows = z_ref.shape[0]
    o_v, o_q, o_kv, o_kr = GMLP_WIDTH, 2 * GMLP_WIDTH, 2 * GMLP_WIDTH + Q_LORA, 2 * GMLP_WIDTH + Q_LORA + KV_LORA
    u = jax.nn.gelu(z_ref[:, :o_v])
    v = jax.nn.gelu(z_ref[:, o_v:o_q])
    if maybe_vraw_ref is not None:
        maybe_vraw_ref[0] = v

    vb = v.astype(BF16)
    pi = lax.broadcasted_iota(jnp.int32, (mix_len, mix_len), 0) // CHUNK
    pj = lax.broadcasted_iota(jnp.int32, (mix_len, mix_len), 1) // CHUNK
    for g in range(GMLP_GROUPS):
        wsg = jnp.where(pj <= pi, ws_ref[g, :mix_len, :mix_len], 0.0).astype(BF16)
        bg = bs_ref[:mix_len, g:g + 1]
        cols = slice(g * GMLP_HEAD, (g + 1) * GMLP_HEAD)
        for c in range(rows // mix_len):
            rws = slice(c * mix_len, (c + 1) * mix_len)
            mixed = jnp.dot(wsg, vb[rws, cols], preferred_element_type=F32) + bg
            ya_ref[0, rws, cols] = (u[rws, cols] * mixed).astype(ya_ref.dtype)

    lane = lax.broadcasted_iota(jnp.int32, (1, LANES), 1)
    is_nope = lane < QK_NOPE

    kr = z_ref[:, o_kr:o_kr + KR_BLOCK]
    kr_ss = jnp.sum(jnp.where(lane < QK_ROPE, kr * kr, 0.0), axis=-1, keepdims=True)
    kr_p = kr * lax.rsqrt(kr_ss * (1.0 / QK_ROPE) + EPS) * krg_ref[...] * tabk_ref[...]
    krope = kr_p + pltpu.roll(kr_p, LANES - QK_ROPE, axis=1)
    kr_ref[0] = krope[:, :QK_ROPE]

    ckv_raw = z_ref[:, o_kv:o_kr]
    ckv = ckv_raw * _rms(ckv_raw, KV_LORA) * kvg_ref[...]
    ckv_ref[0] = ckv
    kv = jnp.dot(ckv.astype(BF16), wukv_ref[...], preferred_element_type=F32)

    cq_raw = z_ref[:, o_q:o_kv]
    cq = cq_raw * _rms(cq_raw, Q_LORA) * qng_ref[...]
    q = jnp.dot(cq.astype(BF16), wuq_ref[...], preferred_element_type=F32)

    root = math.sqrt(QK_NOPE)
    qfac = tabq_ref[...] * (qgain_ref[...] * kgain_ref[...] * jnp.where(is_nope, root * root, root))
    krope_b = krope.astype(k_ref.dtype)
    one_b = jnp.ones((1, LANES), v_ref.dtype)
    for hd in range(N_HEADS):
        cols = slice(hd * HEAD_BLOCK, (hd + 1) * HEAD_BLOCK)
        qh = q[:, cols]
        sq = qh * qh
        ss_all = jnp.sum(sq, axis=-1, keepdims=True)
        ss_n = jnp.sum(jnp.where(is_nope, sq, 0.0), axis=-1, keepdims=True)
        ss = jnp.where(is_nope, ss_n, ss_all - ss_n)
        q_ref[0, hd] = (qh * lax.rsqrt(ss + QK_NOPE * EPS) * qfac).astype(q_ref.dtype)

        kvh = kv[:, cols]
        ks = jnp.sum(jnp.where(is_nope, kvh * kvh, 0.0), axis=-1, keepdims=True)
        kn = kvh * lax.rsqrt(ks + QK_NOPE * EPS)
        k_ref[0, hd] = jnp.where(is_nope, kn.astype(k_ref.dtype), krope_b)
        v_ref[0, hd] = jnp.where(is_nope, one_b, kvh.astype(v_ref.dtype))


def _pre_single_kernel(x_ref, mod_ref, n1g_ref, win_ref, *rest, mix_len):
    *head_refs, vraw_ref, z_ref = rest
    z_ref[...] = _pre_in_proj(x_ref, mod_ref, n1g_ref, win_ref)
    _pre_heads(z_ref, *head_refs, vraw_ref, mix_len)


def _pre_pipelined_kernel(x_ref, mod_ref, n1g_ref, win_ref, *rest, mix_len):
    *head_refs, z_ref = rest
    i = pl.program_id(0)

    @pl.when(i == 0)
    def _():
        z_ref[...] = jnp.zeros(z_ref.shape, F32)

    for parity in range(2):
        @pl.when(i % 2 == parity)
        def _():
            _pre_heads(z_ref.at[1 - parity], *head_refs, None, mix_len)
            z_ref[parity] = _pre_in_proj(x_ref, mod_ref, n1g_ref, win_ref)


_PRE_CONSTS = ("n1g", "w_in", "w_s", "b_s", "qng", "w_uq", "kvg", "w_ukv", "qgain", "kgain", "krg")


def _pre_out_shapes(b, t):
    return [jax.ShapeDtypeStruct((b, t, GMLP_WIDTH), BF16),
            jax.ShapeDtypeStruct((b, N_HEADS, t, HEAD_BLOCK), BF16),
            jax.ShapeDtypeStruct((b, N_HEADS, t, HEAD_BLOCK), BF16),
            jax.ShapeDtypeStruct((b, N_HEADS, t, HEAD_BLOCK), BF16),
            jax.ShapeDtypeStruct((b, t, KV_LORA), F32),
            jax.ShapeDtypeStruct((b, t, QK_ROPE), F32)]


def _pre_single(x, mod, wts, tabq, tabk, *, rows, mix_len):
    b, t, d = x.shape
    tok = lambda w: pl.BlockSpec((1, rows, w), lambda i, j: (i, j, 0))
    head = pl.BlockSpec((1, N_HEADS, rows, HEAD_BLOCK), lambda i, j: (i, 0, j, 0))
    tab = pl.BlockSpec((rows, LANES), lambda i, j: (j, 0))
    consts = [wts[n] for n in _PRE_CONSTS]
    return pl.pallas_call(
        functools.partial(_pre_single_kernel, mix_len=mix_len),
        grid=(b, t // rows),
        in_specs=([tok(d), pl.BlockSpec((1,) + mod.shape[1:], lambda i, j: (i, 0, 0))]
                  + [_const_spec(a.shape) for a in consts] + [tab, tab]),
        out_specs=[tok(GMLP_WIDTH), head, head, head, tok(KV_LORA), tok(QK_ROPE), tok(GMLP_WIDTH)],
        out_shape=_pre_out_shapes(b, t) + [jax.ShapeDtypeStruct((b, t, GMLP_WIDTH), F32)],
        scratch_shapes=[pltpu.VMEM((rows, IN_WIDTH_PADDED), F32)],
        compiler_params=pltpu.CompilerParams(
            dimension_semantics=("parallel", "parallel"), vmem_limit_bytes=VMEM_LIMIT_BYTES),
        name="pre_single",
    )(x, mod, *consts, tabq, tabk)


def _pre_pipelined(x, mod, wts, tabq, tabk, *, rows, mix_len):
    b, t, d = x.shape
    per_batch = t // rows
    n_tiles = b * per_batch
    cur = lambda i: jnp.minimum(i, n_tiles - 1)
    prev = lambda i: jnp.maximum(i - 1, 0)
    tok_out = lambda w: pl.BlockSpec((1, rows, w), lambda i: (prev(i) // per_batch, prev(i) % per_batch, 0))
    head = pl.BlockSpec((1, N_HEADS, rows, HEAD_BLOCK), lambda i: (prev(i) // per_batch, 0, prev(i) % per_batch, 0))
    tab = pl.BlockSpec((rows, LANES), lambda i: (prev(i) % per_batch, 0))
    consts = [wts[n] for n in _PRE_CONSTS]
    return pl.pallas_call(
        functools.partial(_pre_pipelined_kernel, mix_len=mix_len),
        grid=(n_tiles + 1,),
        in_specs=([pl.BlockSpec((1, rows, d), lambda i: (cur(i) // per_batch, cur(i) % per_batch, 0)),
                   pl.BlockSpec((1,) + mod.shape[1:], lambda i: (cur(i) // per_batch, 0, 0))]
                  + [_const_spec(a.shape) for a in consts] + [tab, tab]),
        out_specs=[tok_out(GMLP_WIDTH), head, head, head, tok_out(KV_LORA), tok_out(QK_ROPE)],
        out_shape=_pre_out_shapes(b, t),
        scratch_shapes=[pltpu.VMEM((2, rows, IN_WIDTH_PADDED), F32)],
        compiler_params=pltpu.CompilerParams(
            dimension_semantics=("arbitrary",), vmem_limit_bytes=VMEM_LIMIT_BYTES),
        name="pre_pipelined",
    )(x, mod, *consts, tabq, tabk)


def _pair_out(acc0, acc1):
    first = lax.broadcasted_iota(jnp.int32, (1, LANES), 1) < V_HEAD
    num = jnp.where(first, pltpu.roll(acc0, V_HEAD, axis=1), acc1)
    den = jnp.where(first, acc0, pltpu.roll(acc1, V_HEAD, axis=1))
    return num * pl.reciprocal(den, approx=False)


def _attn_online_kernel(q_ref, k_ref, v_ref, o_ref, m_ref, acc_ref, *, tq, tk):
    t = q_ref.shape[2]
    diag_tiles = tq // tk

    def q_tile(qi, carry):
        q0 = pl.multiple_of(qi * tq, tq)
        accs = []
        for hh in range(2):
            q = q_ref[0, hh, pl.ds(q0, tq), :]
            m_ref[...] = jnp.full(m_ref.shape, -jnp.inf, F32)
            acc_ref[...] = jnp.zeros(acc_ref.shape, F32)

            def kv_step(j, masked):
                k0 = pl.multiple_of(j * tk, tk)
                k = k_ref[0, hh, pl.ds(k0, tk), :]
                v = v_ref[0, hh, pl.ds(k0, tk), :]
                s = lax.dot_general(q, k, (((1,), (1,)), ((), ())), preferred_element_type=F32)
                if masked:
                    qc = (q0 + lax.broadcasted_iota(jnp.int32, (tq, 1), 0)) // CHUNK
                    kc = (k0 + lax.broadcasted_iota(jnp.int32, (1, tk), 1)) // CHUNK
                    s = jnp.where(kc <= qc, s, _NEG)
                m_old = m_ref[...]
                m_new = jnp.maximum(m_old, jnp.max(s, axis=-1, keepdims=True))
                p = jnp.exp2(s - m_new)
                acc_ref[...] = jnp.exp2(m_old - m_new) * acc_ref[...] + jnp.dot(
                    p.astype(BF16), v, preferred_element_type=F32)
                m_ref[...] = m_new

            def body(j, c):
                kv_step(j, False)
                return c

            lax.fori_loop(0, qi * diag_tiles, body, 0)
            for dd in range(diag_tiles):
                kv_step(qi * diag_tiles + dd, True)
            accs.append(acc_ref[...])
        o_ref[0, pl.ds(q0, tq), :] = _pair_out(accs[0], accs[1]).astype(o_ref.dtype)
        return carry

    lax.fori_loop(0, t // tq, q_tile, 0)


def _attn_bounded_kernel(q_ref, k_ref, v_ref, o_ref, acc_ref, *, tq):
    t = q_ref.shape[2]
    hb = tq // 2
    mb = hb // 2
    ri = lax.broadcasted_iota(jnp.int32, (mb, mb), 0) // CHUNK
    ci = lax.broadcasted_iota(jnp.int32, (mb, mb), 1) // CHUNK
    visible = ci <= ri

    def pv(hh, r0, rows, k0, keys, masked=False):
        q = q_ref[0, hh, pl.ds(r0, rows), :]
        k = k_ref[0, hh, pl.ds(k0, keys), :]
        v = v_ref[0, hh, pl.ds(k0, keys), :]
        p = jnp.exp2(lax.dot_general(q, k, (((1,), (1,)), ((), ())), preferred_element_type=F32))
        if masked:
            p = jnp.where(visible, p, 0.0)
        return jnp.dot(p.astype(BF16), v, preferred_element_type=F32)

    def diag(hh, r0):
        r1 = pl.multiple_of(r0 + mb, mb)
        top = pv(hh, r0, mb, r0, mb, True)
        bot = pv(hh, r1, mb, r0, mb) + pv(hh, r1, mb, r1, mb, True)
        return top, bot

    def q_tile(qi, carry):
        q0 = pl.multiple_of(qi * tq, tq)
        q1 = pl.multiple_of(q0 + hb, hb)
        acc_ref[...] = jnp.zeros(acc_ref.shape, F32)

        def body(j, c):
            k0 = pl.multiple_of(j * hb, hb)
            for hh in range(2):
                acc_ref[hh] += pv(hh, q0, tq, k0, hb)
            return c

        lax.fori_loop(0, qi * 2, body, 0)

        parts = []
        for hh in range(2):
            top0, bot0 = diag(hh, q0)
            top1, bot1 = diag(hh, q1)
            full = pv(hh, q1, hb, q0, hb)
            parts.append([top0, bot0, top1 + full[:mb], bot1 + full[mb:]])
        for c in range(tq // mb):
            rows = slice(c * mb, (c + 1) * mb)
            out = _pair_out(acc_ref[0, rows, :] + parts[0][c], acc_ref[1, rows, :] + parts[1][c])
            o_ref[0, pl.ds(pl.multiple_of(q0 + c * mb, mb), mb), :] = out.astype(o_ref.dtype)
        return carry

    lax.fori_loop(0, t // tq, q_tile, 0)


def _attn_prompt(q, k, v, score_bound):
    b, nh, t, _ = q.shape
    tq, tk = ATTN_ONLINE_TQ, ATTN_ONLINE_TK
    qkv = pl.BlockSpec((1, 2, t, HEAD_BLOCK), lambda i, j: (i, j, 0, 0))
    common = dict(
        grid=(b, nh // 2),
        in_specs=[qkv, qkv, qkv],
        out_specs=pl.BlockSpec((1, t, 2 * V_HEAD), lambda i, j: (i, 0, j)),
        out_shape=jax.ShapeDtypeStruct((b, t, MLA_WIDTH), BF16),
        compiler_params=pltpu.CompilerParams(
            dimension_semantics=("parallel", "parallel"), vmem_limit_bytes=VMEM_LIMIT_BYTES),
    )
    bounded = pl.pallas_call(
        functools.partial(_attn_bounded_kernel, tq=ATTN_BOUNDED_TQ),
        scratch_shapes=[pltpu.VMEM((2, ATTN_BOUNDED_TQ, HEAD_BLOCK), F32)],
        name="attn_prompt_bounded", **common)
    online = pl.pallas_call(
        functools.partial(_attn_online_kernel, tq=tq, tk=tk),
        scratch_shapes=[pltpu.VMEM((tq, 1), F32), pltpu.VMEM((tq, HEAD_BLOCK), F32)],
        name="attn_prompt_online", **common)
    return lax.cond(score_bound <= MAX_UNSHIFTED_SCORE, bounded, online, q, k, v)


def _attn_sample_kernel(q_ref, ckv_ref, kr_ref, wukv_ref, o_ref):
    lane = lax.broadcasted_iota(jnp.int32, (1, LANES), 1)
    is_nope = lane < QK_NOPE
    kv = jnp.dot(ckv_ref[0], wukv_ref[...], preferred_element_type=F32)
    krope = kr_ref[0]
    accs = []
    for hd in range(N_HEADS):
        kvh = kv[:, hd * HEAD_BLOCK:(hd + 1) * HEAD_BLOCK]
        ks = jnp.sum(jnp.where(is_nope, kvh * kvh, 0.0), axis=-1, keepdims=True)
        kn = kvh * lax.rsqrt(ks + QK_NOPE * EPS)
        k = jnp.where(is_nope, kn, krope).astype(BF16)
        v = jnp.where(is_nope, 1.0, kvh).astype(BF16)
        s = lax.dot_general(q_ref[0, hd], k, (((1,), (1,)), ((), ())), preferred_element_type=F32)
        p = jnp.exp2(s - jnp.max(s, axis=-1, keepdims=True))
        accs.append(jnp.dot(p.astype(BF16), v, preferred_element_type=F32))
        if hd % 2 == 1:
            o_ref[0, :, (hd - 1) * V_HEAD:(hd + 1) * V_HEAD] = _pair_out(accs[hd - 1], accs[hd]).astype(o_ref.dtype)


def _attn_sample(q, ckv_all, kr_all, w_ukv):
    b, nh, t, _ = q.shape
    s = ckv_all.shape[1]
    return pl.pallas_call(
        _attn_sample_kernel,
        grid=(b,),
        in_specs=[pl.BlockSpec((1, nh, t, HEAD_BLOCK), lambda i: (i, 0, 0, 0)),
                  pl.BlockSpec((1, s, KV_LORA), lambda i: (i, 0, 0)),
                  pl.BlockSpec((1, s, LANES), lambda i: (i, 0, 0)),
                  _const_spec(w_ukv.shape)],
        out_specs=pl.BlockSpec((1, t, MLA_WIDTH), lambda i: (i, 0, 0)),
        out_shape=jax.ShapeDtypeStruct((b, t, MLA_WIDTH), BF16),
        compiler_params=pltpu.CompilerParams(
            dimension_semantics=("parallel",), vmem_limit_bytes=VMEM_LIMIT_BYTES),
        name="attn_sample",
    )(q, ckv_all, kr_all, w_ukv)


def _post_kernel(x_ref, ya_ref, yb_ref, mod_ref, wout_ref, n2g_ref, wfi_ref, wfo_ref, o_ref, acc_ref):
    mod = mod_ref[0]
    g1, sh2, sc2, g2 = mod[2:3], mod[3:4], mod[4:5], mod[5:6]
    mix = (jnp.dot(ya_ref[0], wout_ref[:GMLP_WIDTH, :], preferred_element_type=F32)
           + jnp.dot(yb_ref[0], wout_ref[GMLP_WIDTH:, :], preferred_element_type=F32))
    x1 = x_ref[0] + g1 * mix
    h2 = (x1 * _rms(x1, x1.shape[-1]) * n2g_ref[...] * (1.0 + sc2) + sh2).astype(BF16)
    d_ff = wfo_ref.shape[0]
    acc_ref[...] = x1
    for c in range(d_ff // FFN_CHUNK):
        gate = jnp.dot(h2, wfi_ref[:, c * FFN_CHUNK:(c + 1) * FFN_CHUNK], preferred_element_type=F32)
        up = jnp.dot(h2, wfi_ref[:, d_ff + c * FFN_CHUNK:d_ff + (c + 1) * FFN_CHUNK], preferred_element_type=F32)
        act = (gate * jax.nn.sigmoid(gate) * up).astype(BF16)
        acc_ref[...] += g2 * jnp.dot(act, wfo_ref[c * FFN_CHUNK:(c + 1) * FFN_CHUNK, :], preferred_element_type=F32)
    o_ref[0] = acc_ref[...]


def _post(x, ya, yb, mod, wts, *, rows):
    b, t, d = x.shape
    tok = lambda w: pl.BlockSpec((1, rows, w), lambda i, j: (i, j, 0))
    consts = [wts[n] for n in ("w_out", "n2g", "w_ffn_in", "w_ffn_out")]
    return pl.pallas_call(
        _post_kernel,
        grid=(b, t // rows),
        in_specs=[tok(d), tok(GMLP_WIDTH), tok(MLA_WIDTH),
                  pl.BlockSpec((1,) + mod.shape[1:], lambda i, j: (i, 0, 0)),
                  _const_spec(consts[0].shape), _const_spec(consts[1].shape),
                  _const_spec(consts[2].shape), _const_spec(consts[3].shape)],
        out_specs=tok(d),
        out_shape=jax.ShapeDtypeStruct((b, t, d), F32),
        scratch_shapes=[pltpu.VMEM((rows, d), F32)],
        compiler_params=pltpu.CompilerParams(
            dimension_semantics=("parallel", "parallel"), vmem_limit_bytes=VMEM_LIMIT_BYTES),
        name="post",
    )(x, ya, yb, mod, consts[0], consts[1], consts[2], consts[3])


def _rope_tables(first_pos, n):
    half = QK_ROPE // 2
    inv = ROPE_THETA ** (-np.arange(half, dtype=np.float64) / half)
    ang = (first_pos + np.arange(n, dtype=np.float64))[:, None] * inv[None, :]
    cos, sin = np.cos(ang), np.sin(ang)
    rot = np.concatenate([cos, cos, -sin, sin], axis=-1)
    tabq = (SCALE * LOG2E) * np.concatenate([np.ones((n, QK_NOPE)), rot], axis=-1)
    tabk = np.concatenate([rot, rot], axis=-1)
    return jnp.asarray(tabq, F32), jnp.asarray(tabk, F32)


def _prep_weights(w_in, w_s, b_s, q_norm_g, w_uq, kv_norm_g, w_ukv, qn_g, qr_g, kn_g, kr_g,
                  norm1_g, w_out, norm2_g, w_ffn_in, w_ffn_out):
    o_kr = 2 * GMLP_WIDTH + Q_LORA + KV_LORA
    w_kr = w_in[:, o_kr:o_kr + QK_ROPE]
    w_kr_sw = _swap_halves(w_kr)
    w_in_p = jnp.concatenate([w_in[:, :o_kr], w_kr, w_kr_sw, w_kr, w_kr_sw], axis=-1)
    d_q = w_uq.shape[0]
    wq = w_uq.reshape(d_q, N_HEADS, QK_HEAD)
    wq_r = wq[:, :, QK_NOPE:]
    w_uq_p = jnp.concatenate([wq, _swap_halves(wq_r)], axis=-1).reshape(d_q, N_HEADS * HEAD_BLOCK)
    row = lambda a: a.reshape(1, -1).astype(F32)
    return {
        "n1g": row(norm1_g), "w_in": w_in_p.astype(BF16), "w_s": w_s.astype(F32), "b_s": b_s.T.astype(F32),
        "qng": row(q_norm_g), "w_uq": w_uq_p.astype(BF16), "kvg": row(kv_norm_g), "w_ukv": w_ukv.astype(BF16),
        "qgain": row(jnp.concatenate([qn_g, qr_g, _swap_halves(qr_g)])),
        "kgain": row(jnp.concatenate([kn_g, jnp.ones((HEAD_BLOCK - QK_NOPE,), F32)])),
        "krg": row(jnp.concatenate([kr_g, _swap_halves(kr_g), kr_g, _swap_halves(kr_g)])),
        "w_out": w_out.astype(BF16), "n2g": row(norm2_g),
        "w_ffn_in": w_ffn_in.astype(BF16), "w_ffn_out": w_ffn_out.astype(BF16),
    }


def kernel(x_prompt, x_sample, cache_ckv, cache_krope, c_prompt, c_sample, w_ada, b_ada, norm1_g, w_in, w_s, b_s, q_norm_g, w_uq, kv_norm_g, w_ukv, qn_g, qr_g, kn_g, kr_g, w_out, norm2_g, w_ffn_in, w_ffn_out):
    depth = w_in.shape[0]
    assert depth == 1, "single trunk layer"
    bp, tp, d = x_prompt.shape
    bs, ts, _ = x_sample.shape
    past_len = cache_ckv.shape[2]

    wts = _prep_weights(w_in[0], w_s[0], b_s[0], q_norm_g[0], w_uq[0], kv_norm_g[0], w_ukv[0], qn_g[0], qr_g[0],
                        kn_g[0], kr_g[0], norm1_g[0], w_out[0], norm2_g[0], w_ffn_in[0], w_ffn_out[0])
    mod = _ada(jnp.concatenate([c_prompt, c_sample], axis=0), w_ada[0], b_ada[0]).reshape(bp + bs, 6, d)
    mod_p, mod_s = mod[:bp], mod[bp:]

    tabq_p, tabk_p = _rope_tables(0, tp)
    ya_p, q_p, k_p, v_p, ckv_p, kr_p = _pre_pipelined(x_prompt, mod_p, wts, tabq_p, tabk_p,
                                                      rows=PRE_ROWS, mix_len=GMLP_CHUNK)
    gmax = lambda g: jnp.max(jnp.abs(g))
    score_bound = SCALE * (QK_NOPE * gmax(qn_g) * gmax(kn_g) + QK_ROPE * gmax(qr_g) * gmax(kr_g))
    yb_p = _attn_prompt(q_p, k_p, v_p, score_bound)
    y_p = _post(x_prompt, ya_p, yb_p, mod_p, wts, rows=POST_ROWS)

    tabq_s, tabk_s = _rope_tables(past_len, ts)
    ya_s, q_s, _, _, ckv_s, kr_s, v_s = _pre_single(x_sample, mod_s, wts, tabq_s, tabk_s,
                                                    rows=ts, mix_len=min(ts, GMLP_CHUNK))
    ckv_all = jnp.concatenate([cache_ckv[0], ckv_s], axis=1).astype(BF16)
    kr_all = jnp.tile(jnp.concatenate([cache_krope[0], kr_s], axis=1), (1, 1, LANES // QK_ROPE))
    yb_s = _attn_sample(q_s, ckv_all, kr_all, wts["w_ukv"])
    y_s = _post(x_sample, ya_s, yb_s, mod_s, wts, rows=ts)

    return (y_p, y_s, ckv_p[None], kr_p[None], ckv_s[None], kr_s[None], v_s[None])
```

```python
import functools
import math

import jax
import jax.numpy as jnp
import numpy as np
from jax import lax
from jax.experimental import pallas as pl
from jax.experimental.pallas import tpu as pltpu

F32 = jnp.float32
BF16 = jnp.bfloat16

CHUNK = 64
GMLP_CHUNK = 128
GMLP_GROUPS = 4
GMLP_HEAD = 128
GMLP_WIDTH = GMLP_GROUPS * GMLP_HEAD
N_HEADS = 8
QK_NOPE = 64
QK_ROPE = 32
QK_HEAD = QK_NOPE + QK_ROPE
V_HEAD = 64
Q_LORA = 384
KV_LORA = 256
MLA_WIDTH = N_HEADS * V_HEAD
ROPE_THETA = 10000.0
EPS = 1e-6
SCALE = QK_HEAD ** -0.5

LANES = 128
HEAD_BLOCK = LANES
KR_BLOCK = LANES
IN_WIDTH_PADDED = 2 * GMLP_WIDTH + Q_LORA + KV_LORA + KR_BLOCK
VMEM_LIMIT_BYTES = 56 * 1024 * 1024

PRE_ROWS = 512
POST_ROWS = 512
FFN_CHUNK = 256
ATTN_BOUNDED_TQ = 1024
ATTN_ONLINE_TQ = 512
ATTN_ONLINE_TK = 256

_NEG = float(jnp.finfo(jnp.float32).min)
LOG2E = math.log2(math.e)
MAX_UNSHIFTED_SCORE = 60.0


def _swap_halves(a, axis=-1):
    n = a.shape[axis] // 2
    lo = lax.slice_in_dim(a, 0, n, axis=axis)
    hi = lax.slice_in_dim(a, n, 2 * n, axis=axis)
    return jnp.concatenate([hi, lo], axis=axis)


def _const_spec(shape):
    nd = len(shape)
    return pl.BlockSpec(shape, lambda *_: (0,) * nd, pipeline_mode=pl.Buffered(1))


def _ada_kernel(c_ref, w_ref, b_ref, o_ref):
    c = c_ref[...]
    a = c * jax.nn.sigmoid(c)
    o_ref[...] = jnp.dot(a, w_ref[...], preferred_element_type=F32,
                         precision=lax.Precision.HIGHEST) + b_ref[...]


def _ada(c_all, w_ada, b_ada):
    n, d = c_all.shape
    width = w_ada.shape[1]
    bn = d
    return pl.pallas_call(
        _ada_kernel,
        grid=(width // bn,),
        in_specs=[pl.BlockSpec((n, d), lambda j: (0, 0)),
                  pl.BlockSpec((d, bn), lambda j: (0, j)),
                  pl.BlockSpec((1, bn), lambda j: (0, j))],
        out_specs=pl.BlockSpec((n, bn), lambda j: (0, j)),
        out_shape=jax.ShapeDtypeStruct((n, width), F32),
        compiler_params=pltpu.CompilerParams(vmem_limit_bytes=VMEM_LIMIT_BYTES),
        name="ada",
    )(c_all, w_ada, b_ada.reshape(1, width))


def _rms(x, width):
    return lax.rsqrt(jnp.sum(x * x, axis=-1, keepdims=True) * (1.0 / width) + EPS)


def _pre_in_proj(x_ref, mod_ref, n1g_ref, win_ref):
    nb, rows, d = x_ref.shape
    x = x_ref[...]
    mod = mod_ref[...]
    sh1, sc1 = mod[:, 0:1], mod[:, 1:2]
    h = x * _rms(x, d) * (n1g_ref[...] * (1.0 + sc1)) + sh1
    return jnp.dot(h.reshape(nb * rows, d).astype(BF16), win_ref[...], preferred_element_type=F32)


def _pre_heads(z_ref, ws_ref, bs_ref, qng_ref, wuq_ref, kvg_ref, wukv_ref, qgain_ref, kgain_ref, krg_ref,
               tabq_ref, tabk_ref, ya_ref, q_ref, k_ref, v_ref, ckv_ref, kr_ref, maybe_vraw_ref, mix_len):
    nb, tile_rows = ya_ref.shape[0], ya_ref.shape[1]
    rows = z_ref.shape[0]
    per_head = lambda a: a.reshape(nb, tile_rows, HEAD_BLOCK)
    o_v, o_q, o_kv, o_kr = GMLP_WIDTH, 2 * GMLP_WIDTH, 2 * GMLP_WIDTH + Q_LORA, 2 * GMLP_WIDTH + Q_LORA + KV_LORA
    u = jax.nn.gelu(z_ref[:, :o_v])
    v = jax.nn.gelu(z_ref[:, o_v:o_q])
    if maybe_vraw_ref is not None:
        maybe_vraw_ref[...] = v.reshape(maybe_vraw_ref.shape)

    vb = v.astype(BF16)
    pi = lax.broadcasted_iota(jnp.int32, (mix_len, mix_len), 0) // CHUNK
    pj = lax.broadcasted_iota(jnp.int32, (mix_len, mix_len), 1) // CHUNK
    for g in range(GMLP_GROUPS):
        wsg = jnp.where(pj <= pi, ws_ref[g, :mix_len, :mix_len], 0.0).astype(BF16)
        bg = bs_ref[:mix_len, g:g + 1]
        cols = slice(g * GMLP_HEAD, (g + 1) * GMLP_HEAD)
        for c in range(rows // mix_len):
            rws = slice(c * mix_len, (c + 1) * mix_len)
            mixed = jnp.dot(wsg, vb[rws, cols], preferred_element_type=F32) + bg
            bi, off = divmod(c * mix_len, tile_rows)
            ya_ref[bi, off:off + mix_len, cols] = (u[rws, cols] * mixed).astype(ya_ref.dtype)

    lane = lax.broadcasted_iota(jnp.int32, (1, LANES), 1)
    is_nope = lane < QK_NOPE

    kr = z_ref[:, o_kr:o_kr + KR_BLOCK]
    kr_ss = jnp.sum(jnp.where(lane < QK_ROPE, kr * kr, 0.0), axis=-1, keepdims=True)
    kr_p = kr * lax.rsqrt(kr_ss * (1.0 / QK_ROPE) + EPS) * krg_ref[...] * tabk_ref[...]
    krope = kr_p + pltpu.roll(kr_p, LANES - QK_ROPE, axis=1)
    kr_ref[...] = krope[:, :QK_ROPE].reshape(kr_ref.shape)

    ckv_raw = z_ref[:, o_kv:o_kr]
    ckv = ckv_raw * _rms(ckv_raw, KV_LORA) * kvg_ref[...]
    ckv_ref[...] = ckv.reshape(ckv_ref.shape)
    kv = jnp.dot(ckv.astype(BF16), wukv_ref[...], preferred_element_type=F32)

    cq_raw = z_ref[:, o_q:o_kv]
    cq = cq_raw * _rms(cq_raw, Q_LORA) * qng_ref[...]
    q = jnp.dot(cq.astype(BF16), wuq_ref[...], preferred_element_type=F32)

    root = math.sqrt(QK_NOPE)
    qfac = tabq_ref[...] * (qgain_ref[...] * kgain_ref[...] * jnp.where(is_nope, root * root, root))
    krope_b = krope.astype(k_ref.dtype)
    one_b = jnp.ones((1, LANES), v_ref.dtype)
    for hd in range(N_HEADS):
        cols = slice(hd * HEAD_BLOCK, (hd + 1) * HEAD_BLOCK)
        qh = q[:, cols]
        sq = qh * qh
        ss_all = jnp.sum(sq, axis=-1, keepdims=True)
        ss_n = jnp.sum(jnp.where(is_nope, sq, 0.0), axis=-1, keepdims=True)
        ss = jnp.where(is_nope, ss_n, ss_all - ss_n)
        q_ref[:, hd] = per_head((qh * lax.rsqrt(ss + QK_NOPE * EPS) * qfac).astype(q_ref.dtype))

        kvh = kv[:, cols]
        ks = jnp.sum(jnp.where(is_nope, kvh * kvh, 0.0), axis=-1, keepdims=True)
        kn = kvh * lax.rsqrt(ks + QK_NOPE * EPS)
        k_ref[:, hd] = per_head(jnp.where(is_nope, kn.astype(k_ref.dtype), krope_b))
        v_ref[:, hd] = per_head(jnp.where(is_nope, one_b, kvh.astype(v_ref.dtype)))


def _pre_single_kernel(x_ref, mod_ref, n1g_ref, win_ref, *rest, mix_len):
    *head_refs, vraw_ref, z_ref = rest
    z_ref[...] = _pre_in_proj(x_ref, mod_ref, n1g_ref, win_ref)
    _pre_heads(z_ref, *head_refs, vraw_ref, mix_len)


def _pre_pipelined_kernel(x_ref, mod_ref, n1g_ref, win_ref, *rest, mix_len):
    *head_refs, z_ref = rest
    i = pl.program_id(0)

    @pl.when(i == 0)
    def _():
        z_ref[...] = jnp.zeros(z_ref.shape, F32)

    for parity in range(2):
        @pl.when(i % 2 == parity)
        def _():
            _pre_heads(z_ref.at[1 - parity], *head_refs, None, mix_len)
            z_ref[parity] = _pre_in_proj(x_ref, mod_ref, n1g_ref, win_ref)


_PRE_CONSTS = ("n1g", "w_in", "w_s", "b_s", "qng", "w_uq", "kvg", "w_ukv", "qgain", "kgain", "krg")


def _pre_out_shapes(b, t):
    return [jax.ShapeDtypeStruct((b, t, GMLP_WIDTH), BF16),
            jax.ShapeDtypeStruct((b, N_HEADS, t, HEAD_BLOCK), BF16),
            jax.ShapeDtypeStruct((b, N_HEADS, t, HEAD_BLOCK), BF16),
            jax.ShapeDtypeStruct((b, N_HEADS, t, HEAD_BLOCK), BF16),
            jax.ShapeDtypeStruct((b, t, KV_LORA), F32),
            jax.ShapeDtypeStruct((b, t, QK_ROPE), F32)]


def _pre_single(x, mod, wts, tabq, tabk, *, mix_len):
    b, t, d = x.shape
    whole = lambda *shape: pl.BlockSpec(shape, lambda i: (0,) * len(shape))
    head = whole(b, N_HEADS, t, HEAD_BLOCK)
    consts = [wts[n] for n in _PRE_CONSTS]
    return pl.pallas_call(
        functools.partial(_pre_single_kernel, mix_len=mix_len),
        grid=(1,),
        in_specs=([whole(b, t, d), whole(*mod.shape)]
                  + [_const_spec(a.shape) for a in consts] + [whole(b * t, LANES), whole(b * t, LANES)]),
        out_specs=[whole(b, t, GMLP_WIDTH), head, head, head, whole(b, t, KV_LORA), whole(b, t, QK_ROPE),
                   whole(b, t, GMLP_WIDTH)],
        out_shape=_pre_out_shapes(b, t) + [jax.ShapeDtypeStruct((b, t, GMLP_WIDTH), F32)],
        scratch_shapes=[pltpu.VMEM((b * t, IN_WIDTH_PADDED), F32)],
        compiler_params=pltpu.CompilerParams(
            dimension_semantics=("arbitrary",), vmem_limit_bytes=VMEM_LIMIT_BYTES),
        name="pre_single",
    )(x, mod, *consts, tabq, tabk)


def _pre_pipelined(x, mod, wts, tabq, tabk, *, rows, mix_len):
    b, t, d = x.shape
    per_batch = t // rows
    n_tiles = b * per_batch
    cur = lambda i: jnp.minimum(i, n_tiles - 1)
    prev = lambda i: jnp.maximum(i - 1, 0)
    tok_out = lambda w: pl.BlockSpec((1, rows, w), lambda i: (prev(i) // per_batch, prev(i) % per_batch, 0))
    head = pl.BlockSpec((1, N_HEADS, rows, HEAD_BLOCK), lambda i: (prev(i) // per_batch, 0, prev(i) % per_batch, 0))
    tab = pl.BlockSpec((rows, LANES), lambda i: (prev(i) % per_batch, 0))
    consts = [wts[n] for n in _PRE_CONSTS]
    return pl.pallas_call(
        functools.partial(_pre_pipelined_kernel, mix_len=mix_len),
        grid=(n_tiles + 1,),
        in_specs=([pl.BlockSpec((1, rows, d), lambda i: (cur(i) // per_batch, cur(i) % per_batch, 0)),
                   pl.BlockSpec((1,) + mod.shape[1:], lambda i: (cur(i) // per_batch, 0, 0))]
                  + [_const_spec(a.shape) for a in consts] + [tab, tab]),
        out_specs=[tok_out(GMLP_WIDTH), head, head, head, tok_out(KV_LORA), tok_out(QK_ROPE)],
        out_shape=_pre_out_shapes(b, t),
        scratch_shapes=[pltpu.VMEM((2, rows, IN_WIDTH_PADDED), F32)],
        compiler_params=pltpu.CompilerParams(
            dimension_semantics=("arbitrary",), vmem_limit_bytes=VMEM_LIMIT_BYTES),
        name="pre_pipelined",
    )(x, mod, *consts, tabq, tabk)


def _pair_out(acc0, acc1):
    first = lax.broadcasted_iota(jnp.int32, (1, LANES), 1) < V_HEAD
    num = jnp.where(first, pltpu.roll(acc0, V_HEAD, axis=1), acc1)
    den = jnp.where(first, acc0, pltpu.roll(acc1, V_HEAD, axis=1))
    return num * pl.reciprocal(den, approx=False)


def _attn_online_kernel(q_ref, k_ref, v_ref, o_ref, m_ref, acc_ref, *, tq, tk):
    t = q_ref.shape[2]
    diag_tiles = tq // tk

    def q_tile(qi, carry):
        q0 = pl.multiple_of(qi * tq, tq)
        accs = []
        for hh in range(2):
            q = q_ref[0, hh, pl.ds(q0, tq), :]
            m_ref[...] = jnp.full(m_ref.shape, -jnp.inf, F32)
            acc_ref[...] = jnp.zeros(acc_ref.shape, F32)

            def kv_step(j, masked):
                k0 = pl.multiple_of(j * tk, tk)
                k = k_ref[0, hh, pl.ds(k0, tk), :]
                v = v_ref[0, hh, pl.ds(k0, tk), :]
                s = lax.dot_general(q, k, (((1,), (1,)), ((), ())), preferred_element_type=F32)
                if masked:
                    qc = (q0 + lax.broadcasted_iota(jnp.int32, (tq, 1), 0)) // CHUNK
                    kc = (k0 + lax.broadcasted_iota(jnp.int32, (1, tk), 1)) // CHUNK
                    s = jnp.where(kc <= qc, s, _NEG)
                m_old = m_ref[...]
                m_new = jnp.maximum(m_old, jnp.max(s, axis=-1, keepdims=True))
                p = jnp.exp2(s - m_new)
                acc_ref[...] = jnp.exp2(m_old - m_new) * acc_ref[...] + jnp.dot(
                    p.astype(BF16), v, preferred_element_type=F32)
                m_ref[...] = m_new

            def body(j, c):
                kv_step(j, False)
                return c

            lax.fori_loop(0, qi * diag_tiles, body, 0)
            for dd in range(diag_tiles):
                kv_step(qi * diag_tiles + dd, True)
            accs.append(acc_ref[...])
        o_ref[0, pl.ds(q0, tq), :] = _pair_out(accs[0], accs[1]).astype(o_ref.dtype)
        return carry

    lax.fori_loop(0, t // tq, q_tile, 0)


def _attn_bounded_kernel(q_ref, k_ref, v_ref, o_ref, acc_ref, *, tq):
    t = q_ref.shape[2]
    hb = tq // 2
    mb = hb // 2
    ri = lax.broadcasted_iota(jnp.int32, (mb, mb), 0) // CHUNK
    ci = lax.broadcasted_iota(jnp.int32, (mb, mb), 1) // CHUNK
    visible = ci <= ri

    def pv(hh, r0, rows, k0, keys, masked=False):
        q = q_ref[0, hh, pl.ds(r0, rows), :]
        k = k_ref[0, hh, pl.ds(k0, keys), :]
        v = v_ref[0, hh, pl.ds(k0, keys), :]
        p = jnp.exp2(lax.dot_general(q, k, (((1,), (1,)), ((), ())), preferred_element_type=F32))
        if masked:
            p = jnp.where(visible, p, 0.0)
        return jnp.dot(p.astype(BF16), v, preferred_element_type=F32)

    def diag(hh, r0):
        r1 = pl.multiple_of(r0 + mb, mb)
        top = pv(hh, r0, mb, r0, mb, True)
        bot = pv(hh, r1, mb, r0, mb) + pv(hh, r1, mb, r1, mb, True)
        return top, bot

    def q_tile(qi, carry):
        q0 = pl.multiple_of(qi * tq, tq)
        q1 = pl.multiple_of(q0 + hb, hb)
        acc_ref[...] = jnp.zeros(acc_ref.shape, F32)

        def body(j, c):
            k0 = pl.multiple_of(j * hb, hb)
            for hh in range(2):
                acc_ref[hh] += pv(hh, q0, tq, k0, hb)
            return c

        lax.fori_loop(0, qi * 2, body, 0)

        parts = []
        for hh in range(2):
            top0, bot0 = diag(hh, q0)
            top1, bot1 = diag(hh, q1)
            full = pv(hh, q1, hb, q0, hb)
            parts.append([top0, bot0, top1 + full[:mb], bot1 + full[mb:]])
        for c in range(tq // mb):
            rows = slice(c * mb, (c + 1) * mb)
            out = _pair_out(acc_ref[0, rows, :] + parts[0][c], acc_ref[1, rows, :] + parts[1][c])
            o_ref[0, pl.ds(pl.multiple_of(q0 + c * mb, mb), mb), :] = out.astype(o_ref.dtype)
        return carry

    lax.fori_loop(0, t // tq, q_tile, 0)


def _attn_prompt(q, k, v, score_bound):
    b, nh, t, _ = q.shape
    tq, tk = ATTN_ONLINE_TQ, ATTN_ONLINE_TK
    qkv = pl.BlockSpec((1, 2, t, HEAD_BLOCK), lambda i, j: (i, j, 0, 0))
    common = dict(
        grid=(b, nh // 2),
        in_specs=[qkv, qkv, qkv],
        out_specs=pl.BlockSpec((1, t, 2 * V_HEAD), lambda i, j: (i, 0, j)),
        out_shape=jax.ShapeDtypeStruct((b, t, MLA_WIDTH), BF16),
        compiler_params=pltpu.CompilerParams(
            dimension_semantics=("parallel", "parallel"), vmem_limit_bytes=VMEM_LIMIT_BYTES),
    )
    bounded = pl.pallas_call(
        functools.partial(_attn_bounded_kernel, tq=ATTN_BOUNDED_TQ),
        scratch_shapes=[pltpu.VMEM((2, ATTN_BOUNDED_TQ, HEAD_BLOCK), F32)],
        name="attn_prompt_bounded", **common)
    online = pl.pallas_call(
        functools.partial(_attn_online_kernel, tq=tq, tk=tk),
        scratch_shapes=[pltpu.VMEM((tq, 1), F32), pltpu.VMEM((tq, HEAD_BLOCK), F32)],
        name="attn_prompt_online", **common)
    return lax.cond(score_bound <= MAX_UNSHIFTED_SCORE, bounded, online, q, k, v)


def _attn_sample_kernel(q_ref, past_ckv_ref, past_kr_ref, new_ckv_ref, new_kr_ref, wukv_ref, o_ref):
    lane = lax.broadcasted_iota(jnp.int32, (1, LANES), 1)
    is_nope = lane < QK_NOPE
    spread = (lax.broadcasted_iota(jnp.int32, (QK_ROPE, LANES), 1) % QK_ROPE
              == lax.broadcasted_iota(jnp.int32, (QK_ROPE, LANES), 0)).astype(BF16)
    one_b = jnp.ones((1, LANES), BF16)

    def segment(ckv_ref, kr_ref):
        kv = jnp.dot(ckv_ref[0].astype(BF16), wukv_ref[...], preferred_element_type=F32)
        krope_b = jnp.dot(kr_ref[0].astype(BF16), spread, preferred_element_type=F32).astype(BF16)
        return kv, krope_b

    def head_kv(seg, hd):
        kv, krope_b = seg
        kvh = kv[:, hd * HEAD_BLOCK:(hd + 1) * HEAD_BLOCK]
        ks = jnp.sum(jnp.where(is_nope, kvh * kvh, 0.0), axis=-1, keepdims=True)
        kn = kvh * lax.rsqrt(ks + QK_NOPE * EPS)
        return (jnp.where(is_nope, kn.astype(BF16), krope_b), jnp.where(is_nope, one_b, kvh.astype(BF16)))

    past, new = segment(past_ckv_ref, past_kr_ref), segment(new_ckv_ref, new_kr_ref)
    nt = (((1,), (1,)), ((), ()))
    accs = []
    for hd in range(N_HEADS):
        q = q_ref[0, hd]
        (k0, v0), (k1, v1) = head_kv(past, hd), head_kv(new, hd)
        s0 = lax.dot_general(q, k0, nt, preferred_element_type=F32)
        s1 = lax.dot_general(q, k1, nt, preferred_element_type=F32)
        m = jnp.maximum(jnp.max(s0, axis=-1, keepdims=True), jnp.max(s1, axis=-1, keepdims=True))
        accs.append(jnp.dot(jnp.exp2(s0 - m).astype(BF16), v0, preferred_element_type=F32)
                    + jnp.dot(jnp.exp2(s1 - m).astype(BF16), v1, preferred_element_type=F32))
        if hd % 2 == 1:
            o_ref[0, :, (hd - 1) * V_HEAD:(hd + 1) * V_HEAD] = _pair_out(accs[hd - 1], accs[hd]).astype(o_ref.dtype)


def _attn_sample(q, past_ckv, past_kr, new_ckv, new_kr, w_ukv):
    b, nh, t, _ = q.shape
    per_batch = lambda a: pl.BlockSpec((1,) + a.shape[1:], lambda i: (i,) + (0,) * (a.ndim - 1))
    return pl.pallas_call(
        _attn_sample_kernel,
        grid=(b,),
        in_specs=[per_batch(q), per_batch(past_ckv), per_batch(past_kr), per_batch(new_ckv), per_batch(new_kr),
                  _const_spec(w_ukv.shape)],
        out_specs=pl.BlockSpec((1, t, MLA_WIDTH), lambda i: (i, 0, 0)),
        out_shape=jax.ShapeDtypeStruct((b, t, MLA_WIDTH), BF16),
        compiler_params=pltpu.CompilerParams(
            dimension_semantics=("parallel",), vmem_limit_bytes=VMEM_LIMIT_BYTES),
        name="attn_sample",
    )(q, past_ckv, past_kr, new_ckv, new_kr, w_ukv)


def _post_kernel(x_ref, ya_ref, yb_ref, mod_ref, wout_ref, n2g_ref, wfi_ref, wfo_ref, o_ref, acc_ref):
    nb, rows, d = x_ref.shape
    flat = lambda a: a.reshape(nb * rows, a.shape[-1])
    mod = mod_ref[...]
    g1, sh2, sc2, g2 = mod[:, 2:3], mod[:, 3:4], mod[:, 4:5], mod[:, 5:6]
    mix = (jnp.dot(flat(ya_ref[...]), wout_ref[:GMLP_WIDTH, :], preferred_element_type=F32)
           + jnp.dot(flat(yb_ref[...]), wout_ref[GMLP_WIDTH:, :], preferred_element_type=F32))
    x1 = x_ref[...] + g1 * mix.reshape(nb, rows, d)
    o_ref[...] = x1
    h2 = flat(x1 * _rms(x1, d) * (n2g_ref[...] * (1.0 + sc2)) + sh2).astype(BF16)
    d_ff = wfo_ref.shape[0]
    for c in range(d_ff // FFN_CHUNK):
        gate = jnp.dot(h2, wfi_ref[:, c * FFN_CHUNK:(c + 1) * FFN_CHUNK], preferred_element_type=F32)
        up = jnp.dot(h2, wfi_ref[:, d_ff + c * FFN_CHUNK:d_ff + (c + 1) * FFN_CHUNK], preferred_element_type=F32)
        act = (gate * jax.nn.sigmoid(gate) * up).astype(BF16)
        part = jnp.dot(act, wfo_ref[c * FFN_CHUNK:(c + 1) * FFN_CHUNK, :], preferred_element_type=F32)
        if c == 0:
            acc_ref[...] = part
        else:
            acc_ref[...] += part
    o_ref[...] = o_ref[...] + g2 * acc_ref[...].reshape(nb, rows, d)


def _post(x, ya, yb, mod, wts, *, nb, rows):
    b, t, d = x.shape
    tok = lambda w: pl.BlockSpec((nb, rows, w), lambda i, j: (i, j, 0))
    consts = [wts[n] for n in ("w_out", "n2g", "w_ffn_in", "w_ffn_out")]
    return pl.pallas_call(
        _post_kernel,
        grid=(b // nb, t // rows),
        in_specs=[tok(d), tok(GMLP_WIDTH), tok(MLA_WIDTH),
                  pl.BlockSpec((nb,) + mod.shape[1:], lambda i, j: (i, 0, 0)),
                  _const_spec(consts[0].shape), _const_spec(consts[1].shape),
                  _const_spec(consts[2].shape), _const_spec(consts[3].shape)],
        out_specs=tok(d),
        out_shape=jax.ShapeDtypeStruct((b, t, d), F32),
        scratch_shapes=[pltpu.VMEM((nb * rows, d), F32)],
        compiler_params=pltpu.CompilerParams(
            dimension_semantics=("parallel", "parallel"), vmem_limit_bytes=VMEM_LIMIT_BYTES),
        name="post",
    )(x, ya, yb, mod, consts[0], consts[1], consts[2], consts[3])


def _rope_tables(first_pos, n, repeat=1):
    half = QK_ROPE // 2
    inv = ROPE_THETA ** (-np.arange(half, dtype=np.float64) / half)
    ang = (first_pos + np.arange(n, dtype=np.float64))[:, None] * inv[None, :]
    cos, sin = np.cos(ang), np.sin(ang)
    rot = np.concatenate([cos, cos, -sin, sin], axis=-1)
    tabq = (SCALE * LOG2E) * np.concatenate([np.ones((n, QK_NOPE)), rot], axis=-1)
    tabk = np.concatenate([rot, rot], axis=-1)
    return jnp.asarray(np.tile(tabq, (repeat, 1)), F32), jnp.asarray(np.tile(tabk, (repeat, 1)), F32)


def _prep_weights(w_in, w_s, b_s, q_norm_g, w_uq, kv_norm_g, w_ukv, qn_g, qr_g, kn_g, kr_g,
                  norm1_g, w_out, norm2_g, w_ffn_in, w_ffn_out):
    o_kr = 2 * GMLP_WIDTH + Q_LORA + KV_LORA
    w_kr = w_in[:, o_kr:o_kr + QK_ROPE]
    w_kr_sw = _swap_halves(w_kr)
    w_in_p = jnp.concatenate([w_in[:, :o_kr], w_kr, w_kr_sw, w_kr, w_kr_sw], axis=-1)
    d_q = w_uq.shape[0]
    wq = w_uq.reshape(d_q, N_HEADS, QK_HEAD)
    wq_r = wq[:, :, QK_NOPE:]
    w_uq_p = jnp.concatenate([wq, _swap_halves(wq_r)], axis=-1).reshape(d_q, N_HEADS * HEAD_BLOCK)
    row = lambda a: a.reshape(1, -1).astype(F32)
    return {
        "n1g": row(norm1_g), "w_in": w_in_p.astype(BF16), "w_s": w_s.astype(F32), "b_s": b_s.T.astype(F32),
        "qng": row(q_norm_g), "w_uq": w_uq_p.astype(BF16), "kvg": row(kv_norm_g), "w_ukv": w_ukv.astype(BF16),
        "qgain": row(jnp.concatenate([qn_g, qr_g, _swap_halves(qr_g)])),
        "kgain": row(jnp.concatenate([kn_g, jnp.ones((HEAD_BLOCK - QK_NOPE,), F32)])),
        "krg": row(jnp.concatenate([kr_g, _swap_halves(kr_g), kr_g, _swap_halves(kr_g)])),
        "w_out": w_out.astype(BF16), "n2g": row(norm2_g),
        "w_ffn_in": w_ffn_in.astype(BF16), "w_ffn_out": w_ffn_out.astype(BF16),
    }


def kernel(x_prompt, x_sample, cache_ckv, cache_krope, c_prompt, c_sample, w_ada, b_ada, norm1_g, w_in, w_s, b_s, q_norm_g, w_uq, kv_norm_g, w_ukv, qn_g, qr_g, kn_g, kr_g, w_out, norm2_g, w_ffn_in, w_ffn_out):
    depth = w_in.shape[0]
    assert depth == 1, "single trunk layer"
    bp, tp, d = x_prompt.shape
    bs, ts, _ = x_sample.shape
    past_len = cache_ckv.shape[2]

    wts = _prep_weights(w_in[0], w_s[0], b_s[0], q_norm_g[0], w_uq[0], kv_norm_g[0], w_ukv[0], qn_g[0], qr_g[0],
                        kn_g[0], kr_g[0], norm1_g[0], w_out[0], norm2_g[0], w_ffn_in[0], w_ffn_out[0])
    mod = _ada(jnp.concatenate([c_prompt, c_sample], axis=0), w_ada[0], b_ada[0]).reshape(bp + bs, 6, d)
    mod_p, mod_s = mod[:bp], mod[bp:]

    tabq_p, tabk_p = _rope_tables(0, tp)
    ya_p, q_p, k_p, v_p, ckv_p, kr_p = _pre_pipelined(x_prompt, mod_p, wts, tabq_p, tabk_p,
                                                      rows=PRE_ROWS, mix_len=GMLP_CHUNK)
    gmax = lambda g: jnp.max(jnp.abs(g))
    score_bound = SCALE * (QK_NOPE * gmax(qn_g) * gmax(kn_g) + QK_ROPE * gmax(qr_g) * gmax(kr_g))
    yb_p = _attn_prompt(q_p, k_p, v_p, score_bound)
    y_p = _post(x_prompt, ya_p, yb_p, mod_p, wts, nb=1, rows=POST_ROWS)

    tabq_s, tabk_s = _rope_tables(past_len, ts, repeat=bs)
    ya_s, q_s, _, _, ckv_s, kr_s, v_s = _pre_single(x_sample, mod_s, wts, tabq_s, tabk_s,
                                                    mix_len=min(ts, GMLP_CHUNK))
    yb_s = _attn_sample(q_s, cache_ckv[0], cache_krope[0], ckv_s, kr_s, wts["w_ukv"])
    y_s = _post(x_sample, ya_s, yb_s, mod_s, wts, nb=bs, rows=ts)

    return (y_p, y_s, ckv_p[None], kr_p[None], ckv_s[None], kr_s[None], v_s[None])
```

```python
import functools
import math

import jax
import jax.numpy as jnp
import numpy as np
from jax import lax
from jax.experimental import pallas as pl
from jax.experimental.pallas import tpu as pltpu

F32 = jnp.float32
BF16 = jnp.bfloat16

CHUNK = 64
GMLP_CHUNK = 128
GMLP_GROUPS = 4
GMLP_HEAD = 128
GMLP_WIDTH = GMLP_GROUPS * GMLP_HEAD
N_HEADS = 8
QK_NOPE = 64
QK_ROPE = 32
QK_HEAD = QK_NOPE + QK_ROPE
V_HEAD = 64
Q_LORA = 384
KV_LORA = 256
MLA_WIDTH = N_HEADS * V_HEAD
ROPE_THETA = 10000.0
EPS = 1e-6
SCALE = QK_HEAD ** -0.5

LANES = 128
HEAD_BLOCK = LANES
KR_BLOCK = LANES
IN_WIDTH_PADDED = 2 * GMLP_WIDTH + Q_LORA + KV_LORA + KR_BLOCK
VMEM_LIMIT_BYTES = 56 * 1024 * 1024

PRE_ROWS = 512
POST_ROWS = 512
FFN_CHUNK = 256
ATTN_TQ = 1024
ATTN_TK = ATTN_TQ // 2
DEN_ROWS = 16
ACC_ROWS = V_HEAD + DEN_ROWS

_NEG = float(jnp.finfo(jnp.float32).min)
_NT = (((1,), (1,)), ((), ()))
LOG2E = math.log2(math.e)
MAX_UNSHIFTED_SCORE = 60.0


def _swap_halves(a, axis=-1):
    n = a.shape[axis] // 2
    lo = lax.slice_in_dim(a, 0, n, axis=axis)
    hi = lax.slice_in_dim(a, n, 2 * n, axis=axis)
    return jnp.concatenate([hi, lo], axis=axis)


def _const_spec(shape):
    nd = len(shape)
    return pl.BlockSpec(shape, lambda *_: (0,) * nd, pipeline_mode=pl.Buffered(1))


def _ada_kernel(c_ref, w_ref, b_ref, o_ref):
    c = c_ref[...]
    a = c * jax.nn.sigmoid(c)
    o_ref[...] = jnp.dot(a, w_ref[...], preferred_element_type=F32,
                         precision=lax.Precision.HIGHEST) + b_ref[...]


def _ada(c_all, w_ada, b_ada):
    n, d = c_all.shape
    width = w_ada.shape[1]
    bn = d
    return pl.pallas_call(
        _ada_kernel,
        grid=(width // bn,),
        in_specs=[pl.BlockSpec((n, d), lambda j: (0, 0)),
                  pl.BlockSpec((d, bn), lambda j: (0, j)),
                  pl.BlockSpec((1, bn), lambda j: (0, j))],
        out_specs=pl.BlockSpec((n, bn), lambda j: (0, j)),
        out_shape=jax.ShapeDtypeStruct((n, width), F32),
        compiler_params=pltpu.CompilerParams(vmem_limit_bytes=VMEM_LIMIT_BYTES),
        name="ada",
    )(c_all, w_ada, b_ada.reshape(1, width))


def _rms(x, width):
    return lax.rsqrt(jnp.sum(x * x, axis=-1, keepdims=True) * (1.0 / width) + EPS)


def _pre_in_proj(x_ref, mod_ref, n1g_ref, win_ref):
    nb, rows, d = x_ref.shape
    x = x_ref[...]
    mod = mod_ref[...]
    sh1, sc1 = mod[:, 0:1], mod[:, 1:2]
    h = x * _rms(x, d) * (n1g_ref[...] * (1.0 + sc1)) + sh1
    return jnp.dot(h.reshape(nb * rows, d).astype(BF16), win_ref[...], preferred_element_type=F32)


def _half_sums(a, lo):
    sq = a * a
    ss_all = jnp.sum(sq, axis=-1, keepdims=True)
    ss_lo = jnp.sum(jnp.where(lo, sq, 0.0), axis=-1, keepdims=True)
    return jnp.where(lo, ss_lo, ss_all - ss_lo)


def _pre_heads(z_ref, ws_ref, bs_ref, qng_ref, wuq_ref, kvg_ref, qrow_ref, krg_ref, tabq_ref, tabk_ref,
               ya_ref, q_ref, ckv_ref, kr_ref, maybe_kv, maybe_vraw_ref, mix_len):
    nb, tile_rows = ya_ref.shape[0], ya_ref.shape[1]
    rows = z_ref.shape[0]
    per_head = lambda a: a.reshape(nb, tile_rows, HEAD_BLOCK)
    o_v, o_q, o_kv, o_kr = GMLP_WIDTH, 2 * GMLP_WIDTH, 2 * GMLP_WIDTH + Q_LORA, 2 * GMLP_WIDTH + Q_LORA + KV_LORA
    u = jax.nn.gelu(z_ref[:, :o_v])
    v = jax.nn.gelu(z_ref[:, o_v:o_q])
    if maybe_vraw_ref is not None:
        maybe_vraw_ref[...] = v.reshape(maybe_vraw_ref.shape)

    vb = v.astype(BF16)
    pi = lax.broadcasted_iota(jnp.int32, (mix_len, mix_len), 0) // CHUNK
    pj = lax.broadcasted_iota(jnp.int32, (mix_len, mix_len), 1) // CHUNK
    for g in range(GMLP_GROUPS):
        wsg = jnp.where(pj <= pi, ws_ref[g, :mix_len, :mix_len], 0.0).astype(BF16)
        bg = bs_ref[:mix_len, g:g + 1]
        cols = slice(g * GMLP_HEAD, (g + 1) * GMLP_HEAD)
        for c in range(rows // mix_len):
            rws = slice(c * mix_len, (c + 1) * mix_len)
            mixed = jnp.dot(wsg, vb[rws, cols], preferred_element_type=F32) + bg
            bi, off = divmod(c * mix_len, tile_rows)
            ya_ref[bi, off:off + mix_len, cols] = (u[rws, cols] * mixed).astype(ya_ref.dtype)

    lane = lax.broadcasted_iota(jnp.int32, (1, LANES), 1)
    lo = lane < LANES // 2

    kr = z_ref[:, o_kr:o_kr + KR_BLOCK]
    kr_ss = jnp.sum(jnp.where(lane < QK_ROPE, kr * kr, 0.0), axis=-1, keepdims=True)
    kr_p = kr * lax.rsqrt(kr_ss * (1.0 / QK_ROPE) + EPS) * krg_ref[...] * tabk_ref[...]
    krope = kr_p + pltpu.roll(kr_p, LANES - QK_ROPE, axis=1)
    if kr_ref.shape[-1] == QK_ROPE:
        kr_ref[...] = krope[:, :QK_ROPE].reshape(kr_ref.shape)
    else:
        kr_ref[0] = krope.T[:QK_ROPE, :]

    ckv_raw = z_ref[:, o_kv:o_kr]
    ckv = ckv_raw * _rms(ckv_raw, KV_LORA) * kvg_ref[...]
    ckv_ref[...] = ckv.reshape(ckv_ref.shape)

    cq_raw = z_ref[:, o_q:o_kv]
    cq = cq_raw * _rms(cq_raw, Q_LORA) * qng_ref[...]
    q = jnp.dot(cq.astype(BF16), wuq_ref[...], preferred_element_type=F32)
    qfac = tabq_ref[...] * qrow_ref[...]
    for hd in range(N_HEADS):
        qh = q[:, hd * HEAD_BLOCK:(hd + 1) * HEAD_BLOCK]
        r = lax.rsqrt(_half_sums(qh, lo) + QK_NOPE * EPS)
        fac = qfac[:, (hd % 2) * LANES:(hd % 2 + 1) * LANES]
        q_ref[:, hd] = per_head((qh * r * fac).astype(q_ref.dtype))

    if maybe_kv is not None:
        wuk_ref, wuvt_ref, k_ref, vt_ref = maybe_kv
        ckv_b = ckv.astype(BF16)
        kp_all = jnp.dot(ckv_b, wuk_ref[...], preferred_element_type=F32)
        vt_ref[0, 0] = lax.dot_general(wuvt_ref[...], ckv_b, _NT, preferred_element_type=F32).astype(vt_ref.dtype)
        krope_b = krope.astype(k_ref.dtype)
        for pair in range(N_HEADS // 2):
            kp = kp_all[:, pair * LANES:(pair + 1) * LANES]
            kn = (kp * lax.rsqrt(_half_sums(kp, lo) + QK_NOPE * EPS)).astype(k_ref.dtype)
            k_ref[:, 2 * pair] = per_head(jnp.where(lo, kn, krope_b))
            k_ref[:, 2 * pair + 1] = per_head(jnp.where(lo, krope_b, kn))


_PRE_SINGLE_CONSTS = ("n1g", "w_in", "w_s", "b_s", "qng", "w_uq_flat", "kvg", "qrow_flat", "krg")
_PRE_PIPELINED_CONSTS = ("n1g", "w_in", "w_s", "b_s", "qng", "w_uq", "kvg", "qrow", "krg", "w_uk", "w_uvt")


def _pre_single_kernel(x_ref, mod_ref, n1g_ref, win_ref, ws_ref, bs_ref, qng_ref, wuq_ref, kvg_ref, qrow_ref, krg_ref,
                       tabq_ref, tabk_ref, ya_ref, q_ref, ckv_ref, kr_ref, vraw_ref, z_ref, *, mix_len):
    z_ref[...] = _pre_in_proj(x_ref, mod_ref, n1g_ref, win_ref)
    _pre_heads(z_ref, ws_ref, bs_ref, qng_ref, wuq_ref, kvg_ref, qrow_ref, krg_ref, tabq_ref, tabk_ref,
               ya_ref, q_ref, ckv_ref, kr_ref, None, vraw_ref, mix_len)


def _pre_pipelined_kernel(x_ref, mod_ref, n1g_ref, win_ref, ws_ref, bs_ref, qng_ref, wuq_ref, kvg_ref, qrow_ref,
                          krg_ref, wuk_ref, wuvt_ref, tabq_ref, tabk_ref,
                          ya_ref, q_ref, k_ref, vt_ref, ckv_ref, kr_ref, z_ref, *, mix_len):
    i = pl.program_id(0)

    @pl.when(i == 0)
    def _():
        z_ref[...] = jnp.zeros(z_ref.shape, F32)

    for parity in range(2):
        @pl.when(i % 2 == parity)
        def _():
            _pre_heads(z_ref.at[1 - parity], ws_ref, bs_ref, qng_ref, wuq_ref, kvg_ref, qrow_ref, krg_ref,
                       tabq_ref, tabk_ref, ya_ref, q_ref, ckv_ref, kr_ref,
                       (wuk_ref, wuvt_ref, k_ref, vt_ref), None, mix_len)
            z_ref[parity] = _pre_in_proj(x_ref, mod_ref, n1g_ref, win_ref)


def _pre_single(x, mod, wts, tabq, tabk, *, mix_len):
    b, t, d = x.shape
    whole = lambda *shape: pl.BlockSpec(shape, lambda i: (0,) * len(shape))
    consts = [wts[n] for n in _PRE_SINGLE_CONSTS]
    return pl.pallas_call(
        functools.partial(_pre_single_kernel, mix_len=mix_len),
        grid=(1,),
        in_specs=([whole(b, t, d), whole(*mod.shape)] + [_const_spec(a.shape) for a in consts]
                  + [whole(*tabq.shape), whole(*tabk.shape)]),
        out_specs=[whole(b, t, GMLP_WIDTH), whole(b, N_HEADS, t, HEAD_BLOCK), whole(b, t, KV_LORA),
                   whole(b, t, QK_ROPE), whole(b, t, GMLP_WIDTH)],
        out_shape=[jax.ShapeDtypeStruct((b, t, GMLP_WIDTH), BF16),
                   jax.ShapeDtypeStruct((b, N_HEADS, t, HEAD_BLOCK), BF16),
                   jax.ShapeDtypeStruct((b, t, KV_LORA), F32),
                   jax.ShapeDtypeStruct((b, t, QK_ROPE), F32),
                   jax.ShapeDtypeStruct((b, t, GMLP_WIDTH), F32)],
        scratch_shapes=[pltpu.VMEM((b * t, IN_WIDTH_PADDED), F32)],
        compiler_params=pltpu.CompilerParams(
            dimension_semantics=("arbitrary",), vmem_limit_bytes=VMEM_LIMIT_BYTES),
        name="pre_single",
    )(x, mod, *consts, tabq, tabk)


def _pre_pipelined(x, mod, wts, tabq, tabk, *, rows, mix_len):
    b, t, d = x.shape
    per_batch = t // rows
    n_tiles = b * per_batch
    cur = lambda i: jnp.minimum(i, n_tiles - 1)
    prev = lambda i: jnp.maximum(i - 1, 0)
    pb, pj = (lambda i: prev(i) // per_batch), (lambda i: prev(i) % per_batch)
    tok_out = lambda w: pl.BlockSpec((1, rows, w), lambda i: (pb(i), pj(i), 0))
    head = pl.BlockSpec((1, N_HEADS, rows, HEAD_BLOCK), lambda i: (pb(i), 0, pj(i), 0))
    tab = lambda a: pl.BlockSpec((rows, a.shape[1]), lambda i: (pj(i), 0))
    consts = [wts[n] for n in _PRE_PIPELINED_CONSTS]
    return pl.pallas_call(
        functools.partial(_pre_pipelined_kernel, mix_len=mix_len),
        grid=(n_tiles + 1,),
        in_specs=([pl.BlockSpec((1, rows, d), lambda i: (cur(i) // per_batch, cur(i) % per_batch, 0)),
                   pl.BlockSpec((1,) + mod.shape[1:], lambda i: (cur(i) // per_batch, 0, 0))]
                  + [_const_spec(a.shape) for a in consts] + [tab(tabq), tab(tabk)]),
        out_specs=[tok_out(GMLP_WIDTH), head, head,
                   pl.BlockSpec((1, 1, MLA_WIDTH, rows), lambda i: (pb(i), pj(i), 0, 0)),
                   tok_out(KV_LORA),
                   pl.BlockSpec((1, QK_ROPE, rows), lambda i: (pb(i), 0, pj(i)))],
        out_shape=[jax.ShapeDtypeStruct((b, t, GMLP_WIDTH), BF16),
                   jax.ShapeDtypeStruct((b, N_HEADS, t, HEAD_BLOCK), BF16),
                   jax.ShapeDtypeStruct((b, N_HEADS, t, HEAD_BLOCK), BF16),
                   jax.ShapeDtypeStruct((b, per_batch, MLA_WIDTH, rows), BF16),
                   jax.ShapeDtypeStruct((b, t, KV_LORA), F32),
                   jax.ShapeDtypeStruct((b, QK_ROPE, t), F32)],
        scratch_shapes=[pltpu.VMEM((2, rows, IN_WIDTH_PADDED), F32)],
        compiler_params=pltpu.CompilerParams(
            dimension_semantics=("arbitrary",), vmem_limit_bytes=VMEM_LIMIT_BYTES),
        name="pre_pipelined",
    )(x, mod, *consts, tabq, tabk)


def _attn_kernel(q_ref, k_ref, vt_ref, o_ref, acc_ref, *maybe_m_ref, tq, online):
    t = q_ref.shape[2]
    hb = tq // 2
    mb = hb // 2
    key_chunk = lax.broadcasted_iota(jnp.int32, (mb, mb), 0) // CHUNK
    query_chunk = lax.broadcasted_iota(jnp.int32, (mb, mb), 1) // CHUNK
    visible = key_chunk <= query_chunk
    ones = jnp.ones((DEN_ROWS, hb), BF16)

    def operands(hh, q0, qo, rows, kt, c0, keys):
        q = q_ref[0, hh, pl.ds(pl.multiple_of(q0 + qo, mb), rows), :]
        k = k_ref[0, hh, pl.ds(pl.multiple_of(kt * hb + c0, mb), keys), :]
        lhs = jnp.concatenate([vt_ref[0, kt, hh * V_HEAD:(hh + 1) * V_HEAD, c0:c0 + keys], ones[:, :keys]], axis=0)
        return lhs, lax.dot_general(k, q, _NT, preferred_element_type=F32)

    def unshifted(hh, q0, qo, rows, kt, c0, keys, masked=False):
        lhs, s = operands(hh, q0, qo, rows, kt, c0, keys)
        p = jnp.exp2(s)
        if masked:
            p = jnp.where(visible, p, 0.0)
        return jnp.dot(lhs, p.astype(BF16), preferred_element_type=F32)

    def online_update(hh, q0, qo, rows, kt, c0, keys, masked=False):
        m_ref = maybe_m_ref[0]
        lhs, s = operands(hh, q0, qo, rows, kt, c0, keys)
        if masked:
            s = jnp.where(visible, s, _NEG)
        cols = slice(qo, qo + rows)
        m_old = m_ref[hh, :, cols]
        m_new = jnp.maximum(m_old, jnp.max(s, axis=0, keepdims=True))
        p = jnp.exp2(s - m_new).astype(BF16)
        acc_ref[hh, :, cols] = (jnp.exp2(m_old - m_new) * acc_ref[hh, :, cols]
                                + jnp.dot(lhs, p, preferred_element_type=F32))
        m_ref[hh, :, cols] = m_new

    diagonal = [(0, mb, 0, 0, mb, True), (mb, mb, 0, 0, mb, False), (mb, mb, 0, mb, mb, True),
                (hb, hb, 0, 0, hb, False),
                (hb, mb, 1, 0, mb, True), (hb + mb, mb, 1, 0, mb, False), (hb + mb, mb, 1, mb, mb, True)]

    def q_tile(qi, carry):
        q0 = pl.multiple_of(qi * tq, tq)
        acc_ref[...] = jnp.zeros(acc_ref.shape, F32)
        if online:
            maybe_m_ref[0][...] = jnp.full(maybe_m_ref[0].shape, -jnp.inf, F32)

        def body(j, c):
            tiles = [(hh, 2 * j + half) for half in range(2) for hh in range(2)]
            if online:
                for hh, kt in tiles:
                    online_update(hh, q0, 0, tq, kt, 0, hb)
            else:
                staged = [operands(hh, q0, 0, tq, kt, 0, hb) for hh, kt in tiles]
                probs = [jnp.exp2(s).astype(BF16) for _, s in staged]
                for hh in range(2):
                    acc_ref[hh] += sum(jnp.dot(lhs, p, preferred_element_type=F32)
                                       for (th, _), (lhs, _), p in zip(tiles, staged, probs) if th == hh)
            return c

        lax.fori_loop(0, qi, body, 0)

        totals = []
        for hh in range(2):
            if online:
                for qo, rows, kt_off, c0, keys, masked in diagonal:
                    online_update(hh, q0, qo, rows, qi * 2 + kt_off, c0, keys, masked)
                totals.append([acc_ref[hh, :, g * mb:(g + 1) * mb] for g in range(tq // mb)])
            else:
                staged = [operands(hh, q0, qo, rows, qi * 2 + kt_off, c0, keys)
                          for qo, rows, kt_off, c0, keys, _ in diagonal]
                probs = [jnp.exp2(s) for _, s in staged]
                probs = [jnp.where(visible, p, 0.0) if blk[5] else p for p, blk in zip(probs, diagonal)]
                parts = [jnp.dot(lhs, p.astype(BF16), preferred_element_type=F32)
                         for (lhs, _), p in zip(staged, probs)]
                groups = [acc_ref[hh, :, g * mb:(g + 1) * mb] for g in range(tq // mb)]
                for (qo, rows, *_), part in zip(diagonal, parts):
                    for g in range(rows // mb):
                        groups[qo // mb + g] = groups[qo // mb + g] + part[:, g * mb:(g + 1) * mb]
                totals.append(groups)
        for g in range(tq // mb):
            o_t = [tot[g][:V_HEAD] * pl.reciprocal(tot[g][V_HEAD:V_HEAD + 1], approx=False) for tot in totals]
            o_ref[0, pl.ds(pl.multiple_of(q0 + g * mb, mb), mb), :] = jnp.concatenate(o_t, axis=0).T.astype(o_ref.dtype)
        return carry

    lax.fori_loop(0, t // tq, q_tile, 0)


def _attn_prompt(q, k, vt, score_bound):
    b, nh, t, _ = q.shape
    n_kt, _, hb = vt.shape[1:]
    assert hb == ATTN_TK and n_kt * hb == t
    qk = pl.BlockSpec((1, 2, t, HEAD_BLOCK), lambda i, j: (i, j, 0, 0))

    def call(online):
        scratch = [pltpu.VMEM((2, ACC_ROWS, ATTN_TQ), F32)]
        if online:
            scratch.append(pltpu.VMEM((2, 1, ATTN_TQ), F32))
        return pl.pallas_call(
            functools.partial(_attn_kernel, tq=ATTN_TQ, online=online),
            grid=(b, nh // 2),
            in_specs=[qk, qk, pl.BlockSpec((1, n_kt, 2 * V_HEAD, hb), lambda i, j: (i, 0, j, 0))],
            out_specs=pl.BlockSpec((1, t, 2 * V_HEAD), lambda i, j: (i, 0, j)),
            out_shape=jax.ShapeDtypeStruct((b, t, MLA_WIDTH), BF16),
            scratch_shapes=scratch,
            compiler_params=pltpu.CompilerParams(
                dimension_semantics=("parallel", "parallel"), vmem_limit_bytes=VMEM_LIMIT_BYTES),
            name="attn_prompt_online" if online else "attn_prompt_bounded")

    return lax.cond(score_bound <= MAX_UNSHIFTED_SCORE, call(False), call(True), q, k, vt)


def _pair_out(acc0, acc1):
    first = lax.broadcasted_iota(jnp.int32, (1, LANES), 1) < V_HEAD
    num = jnp.where(first, pltpu.roll(acc0, V_HEAD, axis=1), acc1)
    den = jnp.where(first, acc0, pltpu.roll(acc1, V_HEAD, axis=1))
    return num * pl.reciprocal(den, approx=False)


def _attn_sample_kernel(q_ref, past_ckv_ref, past_kr_ref, new_ckv_ref, new_kr_ref, wukv_ref, o_ref):
    lane = lax.broadcasted_iota(jnp.int32, (1, LANES), 1)
    is_nope = lane < QK_NOPE
    spread = (lax.broadcasted_iota(jnp.int32, (QK_ROPE, LANES), 1) % QK_ROPE
              == lax.broadcasted_iota(jnp.int32, (QK_ROPE, LANES), 0)).astype(BF16)
    one_b = jnp.ones((1, LANES), BF16)

    def segment(ckv_ref, kr_ref):
        kv = jnp.dot(ckv_ref[0].astype(BF16), wukv_ref[...], preferred_element_type=F32)
        krope_b = jnp.dot(kr_ref[0].astype(BF16), spread, preferred_element_type=F32).astype(BF16)
        return kv, krope_b

    def head_kv(seg, hd):
        kv, krope_b = seg
        kvh = kv[:, hd * HEAD_BLOCK:(hd + 1) * HEAD_BLOCK]
        ks = jnp.sum(jnp.where(is_nope, kvh * kvh, 0.0), axis=-1, keepdims=True)
        kn = kvh * lax.rsqrt(ks + QK_NOPE * EPS)
        return (jnp.where(is_nope, kn.astype(BF16), krope_b), jnp.where(is_nope, one_b, kvh.astype(BF16)))

    past, new = segment(past_ckv_ref, past_kr_ref), segment(new_ckv_ref, new_kr_ref)
    accs = []
    for hd in range(N_HEADS):
        q = q_ref[0, hd]
        (k0, v0), (k1, v1) = head_kv(past, hd), head_kv(new, hd)
        s0 = lax.dot_general(q, k0, _NT, preferred_element_type=F32)
        s1 = lax.dot_general(q, k1, _NT, preferred_element_type=F32)
        m = jnp.maximum(jnp.max(s0, axis=-1, keepdims=True), jnp.max(s1, axis=-1, keepdims=True))
        accs.append(jnp.dot(jnp.exp2(s0 - m).astype(BF16), v0, preferred_element_type=F32)
                    + jnp.dot(jnp.exp2(s1 - m).astype(BF16), v1, preferred_element_type=F32))
        if hd % 2 == 1:
            o_ref[0, :, (hd - 1) * V_HEAD:(hd + 1) * V_HEAD] = _pair_out(accs[hd - 1], accs[hd]).astype(o_ref.dtype)


def _attn_sample(q, past_ckv, past_kr, new_ckv, new_kr, w_ukv):
    b, nh, t, _ = q.shape
    per_batch = lambda a: pl.BlockSpec((1,) + a.shape[1:], lambda i: (i,) + (0,) * (a.ndim - 1))
    return pl.pallas_call(
        _attn_sample_kernel,
        grid=(b,),
        in_specs=[per_batch(q), per_batch(past_ckv), per_batch(past_kr), per_batch(new_ckv), per_batch(new_kr),
                  _const_spec(w_ukv.shape)],
        out_specs=pl.BlockSpec((1, t, MLA_WIDTH), lambda i: (i, 0, 0)),
        out_shape=jax.ShapeDtypeStruct((b, t, MLA_WIDTH), BF16),
        compiler_params=pltpu.CompilerParams(
            dimension_semantics=("parallel",), vmem_limit_bytes=VMEM_LIMIT_BYTES),
        name="attn_sample",
    )(q, past_ckv, past_kr, new_ckv, new_kr, w_ukv)


def _post_kernel(x_ref, ya_ref, yb_ref, mod_ref, wout_ref, n2g_ref, wfi_ref, wfo_ref, o_ref, acc_ref):
    nb, rows, d = x_ref.shape
    flat = lambda a: a.reshape(nb * rows, a.shape[-1])
    mod = mod_ref[...]
    g1, sh2, sc2, g2 = mod[:, 2:3], mod[:, 3:4], mod[:, 4:5], mod[:, 5:6]
    mix = (jnp.dot(flat(ya_ref[...]), wout_ref[:GMLP_WIDTH, :], preferred_element_type=F32)
           + jnp.dot(flat(yb_ref[...]), wout_ref[GMLP_WIDTH:, :], preferred_element_type=F32))
    x1 = x_ref[...] + g1 * mix.reshape(nb, rows, d)
    o_ref[...] = x1
    h2 = flat(x1 * _rms(x1, d) * (n2g_ref[...] * (1.0 + sc2)) + sh2).astype(BF16)
    d_ff = wfo_ref.shape[0]
    for c in range(d_ff // FFN_CHUNK):
        gate = jnp.dot(h2, wfi_ref[:, c * FFN_CHUNK:(c + 1) * FFN_CHUNK], preferred_element_type=F32)
        up = jnp.dot(h2, wfi_ref[:, d_ff + c * FFN_CHUNK:d_ff + (c + 1) * FFN_CHUNK], preferred_element_type=F32)
        act = (gate * jax.nn.sigmoid(gate) * up).astype(BF16)
        part = jnp.dot(act, wfo_ref[c * FFN_CHUNK:(c + 1) * FFN_CHUNK, :], preferred_element_type=F32)
        if c == 0:
            acc_ref[...] = part
        else:
            acc_ref[...] += part
    o_ref[...] = o_ref[...] + g2 * acc_ref[...].reshape(nb, rows, d)


def _post(x, ya, yb, mod, wts, *, nb, rows):
    b, t, d = x.shape
    tok = lambda w: pl.BlockSpec((nb, rows, w), lambda i, j: (i, j, 0))
    consts = [wts[n] for n in ("w_out", "n2g", "w_ffn_in", "w_ffn_out")]
    return pl.pallas_call(
        _post_kernel,
        grid=(b // nb, t // rows),
        in_specs=[tok(d), tok(GMLP_WIDTH), tok(MLA_WIDTH),
                  pl.BlockSpec((nb,) + mod.shape[1:], lambda i, j: (i, 0, 0)),
                  _const_spec(consts[0].shape), _const_spec(consts[1].shape),
                  _const_spec(consts[2].shape), _const_spec(consts[3].shape)],
        out_specs=tok(d),
        out_shape=jax.ShapeDtypeStruct((b, t, d), F32),
        scratch_shapes=[pltpu.VMEM((nb * rows, d), F32)],
        compiler_params=pltpu.CompilerParams(
            dimension_semantics=("parallel", "parallel"), vmem_limit_bytes=VMEM_LIMIT_BYTES),
        name="post",
    )(x, ya, yb, mod, consts[0], consts[1], consts[2], consts[3])


def _rope_tables(first_pos, n, *, mirrored_odd, repeat=1):
    half = QK_ROPE // 2
    inv = ROPE_THETA ** (-np.arange(half, dtype=np.float64) / half)
    ang = (first_pos + np.arange(n, dtype=np.float64))[:, None] * inv[None, :]
    cos, sin = np.cos(ang), np.sin(ang)
    rot = np.concatenate([cos, cos, -sin, sin], axis=-1)
    nope = np.ones((n, QK_NOPE))
    tabq = (SCALE * LOG2E) * np.concatenate([nope, rot] + ([rot, nope] if mirrored_odd else [nope, rot]), axis=-1)
    tabk = np.concatenate([rot, rot], axis=-1)
    return jnp.asarray(np.tile(tabq, (repeat, 1)), F32), jnp.asarray(np.tile(tabk, (repeat, 1)), F32)


def _prep_weights(w_in, w_s, b_s, q_norm_g, w_uq, kv_norm_g, w_ukv, qn_g, qr_g, kn_g, kr_g,
                  norm1_g, w_out, norm2_g, w_ffn_in, w_ffn_out):
    o_kr = 2 * GMLP_WIDTH + Q_LORA + KV_LORA
    w_kr = w_in[:, o_kr:o_kr + QK_ROPE]
    w_kr_sw = _swap_halves(w_kr)
    w_in_p = jnp.concatenate([w_in[:, :o_kr], w_kr, w_kr_sw, w_kr, w_kr_sw], axis=-1)
    d_q = w_uq.shape[0]
    wq = w_uq.reshape(d_q, N_HEADS, QK_HEAD)
    wq_n, wq_r = wq[:, :, :QK_NOPE], wq[:, :, QK_NOPE:]
    wq_plain = jnp.concatenate([wq_n, wq_r, _swap_halves(wq_r)], axis=-1)
    wq_mirror = jnp.concatenate([wq_r, _swap_halves(wq_r), wq_n], axis=-1)
    odd = (jnp.arange(N_HEADS) % 2 == 1)[None, :, None]
    w_uq_flat = wq_plain.reshape(d_q, N_HEADS * HEAD_BLOCK)
    w_uq_pair = jnp.where(odd, wq_mirror, wq_plain).reshape(d_q, N_HEADS * HEAD_BLOCK)
    wkv = w_ukv.reshape(KV_LORA, N_HEADS, QK_NOPE + V_HEAD)
    w_uk = wkv[:, :, :QK_NOPE].reshape(KV_LORA, N_HEADS * QK_NOPE)
    w_uvt = wkv[:, :, QK_NOPE:].reshape(KV_LORA, MLA_WIDTH).T
    root = math.sqrt(QK_NOPE)
    q_nope_row = qn_g * kn_g * (root * root)
    q_rope_row = jnp.concatenate([qr_g, _swap_halves(qr_g)]) * root
    qrow_plain = jnp.concatenate([q_nope_row, q_rope_row])
    qrow_mirror = jnp.concatenate([q_rope_row, q_nope_row])
    row = lambda a: a.reshape(1, -1).astype(F32)
    return {
        "n1g": row(norm1_g), "w_in": w_in_p.astype(BF16), "w_s": w_s.astype(F32), "b_s": b_s.T.astype(F32),
        "qng": row(q_norm_g), "kvg": row(kv_norm_g),
        "w_uq": w_uq_pair.astype(BF16), "w_uq_flat": w_uq_flat.astype(BF16),
        "qrow": row(jnp.concatenate([qrow_plain, qrow_mirror])),
        "qrow_flat": row(jnp.concatenate([qrow_plain, qrow_plain])),
        "w_ukv": w_ukv.astype(BF16), "w_uk": w_uk.astype(BF16), "w_uvt": w_uvt.astype(BF16),
        "krg": row(jnp.concatenate([kr_g, _swap_halves(kr_g), kr_g, _swap_halves(kr_g)])),
        "w_out": w_out.astype(BF16), "n2g": row(norm2_g),
        "w_ffn_in": w_ffn_in.astype(BF16), "w_ffn_out": w_ffn_out.astype(BF16),
    }


def kernel(x_prompt, x_sample, cache_ckv, cache_krope, c_prompt, c_sample, w_ada, b_ada, norm1_g, w_in, w_s, b_s, q_norm_g, w_uq, kv_norm_g, w_ukv, qn_g, qr_g, kn_g, kr_g, w_out, norm2_g, w_ffn_in, w_ffn_out):
    depth = w_in.shape[0]
    assert depth == 1, "single trunk layer"
    assert PRE_ROWS == ATTN_TK
    bp, tp, d = x_prompt.shape
    bs, ts, _ = x_sample.shape
    past_len = cache_ckv.shape[2]

    wts = _prep_weights(w_in[0], w_s[0], b_s[0], q_norm_g[0], w_uq[0], kv_norm_g[0], w_ukv[0], qn_g[0], qr_g[0],
                        kn_g[0], kr_g[0], norm1_g[0], w_out[0], norm2_g[0], w_ffn_in[0], w_ffn_out[0])
    mod = _ada(jnp.concatenate([c_prompt, c_sample], axis=0), w_ada[0], b_ada[0]).reshape(bp + bs, 6, d)
    mod_p, mod_s = mod[:bp], mod[bp:]

    tabq_p, tabk_p = _rope_tables(0, tp, mirrored_odd=True)
    ya_p, q_p, k_p, vt_p, ckv_p, krt_p = _pre_pipelined(x_prompt, mod_p, wts, tabq_p, tabk_p,
                                                        rows=PRE_ROWS, mix_len=GMLP_CHUNK)
    gmax = lambda g: jnp.max(jnp.abs(g))
    score_bound = SCALE * (QK_NOPE * gmax(qn_g) * gmax(kn_g) + QK_ROPE * gmax(qr_g) * gmax(kr_g))
    yb_p = _attn_prompt(q_p, k_p, vt_p, score_bound)
    y_p = _post(x_prompt, ya_p, yb_p, mod_p, wts, nb=1, rows=POST_ROWS)

    tabq_s, tabk_s = _rope_tables(past_len, ts, mirrored_odd=False, repeat=bs)
    ya_s, q_s, ckv_s, kr_s, v_s = _pre_single(x_sample, mod_s, wts, tabq_s, tabk_s, mix_len=min(ts, GMLP_CHUNK))
    yb_s = _attn_sample(q_s, cache_ckv[0], cache_krope[0], ckv_s, kr_s, wts["w_ukv"])
    y_s = _post(x_sample, ya_s, yb_s, mod_s, wts, nb=bs, rows=ts)

    return (y_p, y_s, ckv_p[None], jnp.swapaxes(krt_p, 1, 2)[None], ckv_s[None], kr_s[None], v_s[None])
```

```python
import functools
import math

import jax
import jax.numpy as jnp
import numpy as np
from jax import lax
from jax.experimental import pallas as pl
from jax.experimental.pallas import tpu as pltpu

F32 = jnp.float32
BF16 = jnp.bfloat16

CHUNK = 64
GMLP_CHUNK = 128
GMLP_GROUPS = 4
GMLP_HEAD = 128
GMLP_WIDTH = GMLP_GROUPS * GMLP_HEAD
N_HEADS = 8
QK_NOPE = 64
QK_ROPE = 32
QK_HEAD = QK_NOPE + QK_ROPE
V_HEAD = 64
Q_LORA = 384
KV_LORA = 256
MLA_WIDTH = N_HEADS * V_HEAD
ROPE_THETA = 10000.0
EPS = 1e-6
SCALE = QK_HEAD ** -0.5

LANES = 128
HEAD_BLOCK = LANES
KR_BLOCK = LANES
IN_WIDTH_PADDED = 2 * GMLP_WIDTH + Q_LORA + KV_LORA + KR_BLOCK
VMEM_LIMIT_BYTES = 56 * 1024 * 1024

PRE_ROWS = 512
POST_ROWS = 512
FFN_CHUNK = 256
ATTN_TQ = 1024
ATTN_TK = ATTN_TQ // 2
DEN_ROWS = 16
ACC_ROWS = V_HEAD + DEN_ROWS

_NEG = float(jnp.finfo(jnp.float32).min)
_NT = (((1,), (1,)), ((), ()))
LOG2E = math.log2(math.e)
MAX_UNSHIFTED_SCORE = 60.0


def _swap_halves(a, axis=-1):
    n = a.shape[axis] // 2
    lo = lax.slice_in_dim(a, 0, n, axis=axis)
    hi = lax.slice_in_dim(a, n, 2 * n, axis=axis)
    return jnp.concatenate([hi, lo], axis=axis)


def _const_spec(shape):
    nd = len(shape)
    return pl.BlockSpec(shape, lambda *_: (0,) * nd, pipeline_mode=pl.Buffered(1))


def _ada_kernel(c_ref, w_ref, b_ref, o_ref):
    c = c_ref[...]
    a = c * jax.nn.sigmoid(c)
    o_ref[...] = jnp.dot(a, w_ref[...], preferred_element_type=F32,
                         precision=lax.Precision.HIGHEST) + b_ref[...]


def _ada(c_all, w_ada, b_ada):
    n, d = c_all.shape
    width = w_ada.shape[1]
    bn = d
    return pl.pallas_call(
        _ada_kernel,
        grid=(width // bn,),
        in_specs=[pl.BlockSpec((n, d), lambda j: (0, 0)),
                  pl.BlockSpec((d, bn), lambda j: (0, j)),
                  pl.BlockSpec((1, bn), lambda j: (0, j))],
        out_specs=pl.BlockSpec((n, bn), lambda j: (0, j)),
        out_shape=jax.ShapeDtypeStruct((n, width), F32),
        compiler_params=pltpu.CompilerParams(vmem_limit_bytes=VMEM_LIMIT_BYTES),
        name="ada",
    )(c_all, w_ada, b_ada.reshape(1, width))


def _rms(x, width):
    return lax.rsqrt(jnp.sum(x * x, axis=-1, keepdims=True) * (1.0 / width) + EPS)


def _pre_norm(x_ref, mod_ref, n1g_ref):
    nb, rows, d = x_ref.shape
    x = x_ref[...]
    mod = mod_ref[...]
    sh1, sc1 = mod[:, 0:1], mod[:, 1:2]
    h = x * _rms(x, d) * (n1g_ref[...] * (1.0 + sc1)) + sh1
    return h.reshape(nb * rows, d).astype(BF16)


def _gelu_tanh(x):
    k = -2.0 * math.sqrt(2.0 / math.pi) * LOG2E
    e = jnp.exp2(x * (x * x * (k * 0.044715) + k))
    return x * pl.reciprocal(1.0 + e, approx=False)


def _half_sums(a, lo):
    sq = a * a
    ss_all = jnp.sum(sq, axis=-1, keepdims=True)
    ss_lo = jnp.sum(jnp.where(lo, sq, 0.0), axis=-1, keepdims=True)
    return jnp.where(lo, ss_lo, ss_all - ss_lo)


def _pre_heads(z_ref, ws_ref, bs_ref, qng_ref, wuq_ref, kvg_ref, qrow_ref, krg_ref, tabq_ref, tabk_ref,
               ya_ref, q_ref, ckv_ref, kr_ref, maybe_kv, maybe_vraw_ref, mix_len):
    nb, tile_rows = ya_ref.shape[0], ya_ref.shape[1]
    rows = z_ref.shape[0]
    per_head = lambda a: a.reshape(nb, tile_rows, HEAD_BLOCK)
    o_v, o_q, o_kv, o_kr = GMLP_WIDTH, 2 * GMLP_WIDTH, 2 * GMLP_WIDTH + Q_LORA, 2 * GMLP_WIDTH + Q_LORA + KV_LORA
    u = _gelu_tanh(z_ref[:, :o_v])
    v = _gelu_tanh(z_ref[:, o_v:o_q])
    if maybe_vraw_ref is not None:
        maybe_vraw_ref[...] = v.reshape(maybe_vraw_ref.shape)

    vb = v.astype(BF16)
    pi = lax.broadcasted_iota(jnp.int32, (mix_len, mix_len), 0) // CHUNK
    pj = lax.broadcasted_iota(jnp.int32, (mix_len, mix_len), 1) // CHUNK
    for g in range(GMLP_GROUPS):
        wsg = jnp.where(pj <= pi, ws_ref[g, :mix_len, :mix_len], 0.0).astype(BF16)
        bg = bs_ref[:mix_len, g:g + 1]
        cols = slice(g * GMLP_HEAD, (g + 1) * GMLP_HEAD)
        for c in range(rows // mix_len):
            rws = slice(c * mix_len, (c + 1) * mix_len)
            mixed = jnp.dot(wsg, vb[rws, cols], preferred_element_type=F32) + bg
            bi, off = divmod(c * mix_len, tile_rows)
            ya_ref[bi, off:off + mix_len, cols] = (u[rws, cols] * mixed).astype(ya_ref.dtype)

    lane = lax.broadcasted_iota(jnp.int32, (1, LANES), 1)
    lo = lane < LANES // 2

    kr = z_ref[:, o_kr:o_kr + KR_BLOCK]
    kr_ss = jnp.sum(jnp.where(lane < QK_ROPE, kr * kr, 0.0), axis=-1, keepdims=True)
    kr_p = kr * lax.rsqrt(kr_ss * (1.0 / QK_ROPE) + EPS) * krg_ref[...] * tabk_ref[...]
    krope = kr_p + pltpu.roll(kr_p, LANES - QK_ROPE, axis=1)
    if kr_ref.shape[-1] == QK_ROPE:
        kr_ref[...] = krope[:, :QK_ROPE].reshape(kr_ref.shape)
    else:
        kr_ref[0] = krope.T[:QK_ROPE, :]

    ckv_raw = z_ref[:, o_kv:o_kr]
    ckv = ckv_raw * _rms(ckv_raw, KV_LORA) * kvg_ref[...]
    ckv_ref[...] = ckv.reshape(ckv_ref.shape)

    cq_raw = z_ref[:, o_q:o_kv]
    cq = cq_raw * _rms(cq_raw, Q_LORA) * qng_ref[...]
    q = jnp.dot(cq.astype(BF16), wuq_ref[...], preferred_element_type=F32)
    qfac = tabq_ref[...] * qrow_ref[...]
    for hd in range(N_HEADS):
        qh = q[:, hd * HEAD_BLOCK:(hd + 1) * HEAD_BLOCK]
        r = lax.rsqrt(_half_sums(qh, lo) + QK_NOPE * EPS)
        fac = qfac[:, (hd % 2) * LANES:(hd % 2 + 1) * LANES]
        q_ref[:, hd] = per_head((qh * r * fac).astype(q_ref.dtype))

    if maybe_kv is not None:
        wuk_ref, wuvt_ref, k_ref, vt_ref = maybe_kv
        ckv_b = ckv.astype(BF16)
        kp_all = jnp.dot(ckv_b, wuk_ref[...], preferred_element_type=F32)
        vt_ref[0, 0] = lax.dot_general(wuvt_ref[...], ckv_b, _NT, preferred_element_type=F32).astype(vt_ref.dtype)
        krope_b = krope.astype(k_ref.dtype)
        for pair in range(N_HEADS // 2):
            kp = kp_all[:, pair * LANES:(pair + 1) * LANES]
            kn = (kp * lax.rsqrt(_half_sums(kp, lo) + QK_NOPE * EPS)).astype(k_ref.dtype)
            k_ref[:, 2 * pair] = per_head(jnp.where(lo, kn, krope_b))
            k_ref[:, 2 * pair + 1] = per_head(jnp.where(lo, krope_b, kn))


_PRE_SINGLE_CONSTS = ("n1g", "w_in", "w_s", "b_s", "qng", "w_uq_flat", "kvg", "qrow_flat", "krg")
_PRE_PIPELINED_CONSTS = ("n1g", "w_in", "w_s", "b_s", "qng", "w_uq", "kvg", "qrow", "krg", "w_uk", "w_uvt")


def _pre_single_kernel(x_ref, mod_ref, n1g_ref, win_ref, ws_ref, bs_ref, qng_ref, wuq_ref, kvg_ref, qrow_ref, krg_ref,
                       tabq_ref, tabk_ref, ya_ref, q_ref, ckv_ref, kr_ref, vraw_ref, z_ref, *, mix_len):
    z_ref[...] = jnp.dot(_pre_norm(x_ref, mod_ref, n1g_ref), win_ref[...], preferred_element_type=F32)
    _pre_heads(z_ref, ws_ref, bs_ref, qng_ref, wuq_ref, kvg_ref, qrow_ref, krg_ref, tabq_ref, tabk_ref,
               ya_ref, q_ref, ckv_ref, kr_ref, None, vraw_ref, mix_len)


def _pre_pipelined_kernel(x_ref, mod_ref, n1g_ref, win_ref, ws_ref, bs_ref, qng_ref, wuq_ref, kvg_ref, qrow_ref,
                          krg_ref, wuk_ref, wuvt_ref, tabq_ref, tabk_ref,
                          ya_ref, q_ref, k_ref, vt_ref, ckv_ref, kr_ref, z_ref, *, mix_len):
    i = pl.program_id(0)

    @pl.when(i == 0)
    def _():
        z_ref[...] = jnp.zeros(z_ref.shape, F32)

    for parity in range(2):
        @pl.when(i % 2 == parity)
        def _():
            _pre_heads(z_ref.at[1 - parity], ws_ref, bs_ref, qng_ref, wuq_ref, kvg_ref, qrow_ref, krg_ref,
                       tabq_ref, tabk_ref, ya_ref, q_ref, ckv_ref, kr_ref,
                       (wuk_ref, wuvt_ref, k_ref, vt_ref), None, mix_len)
            z_ref[parity] = jnp.dot(_pre_norm(x_ref, mod_ref, n1g_ref), win_ref[...], preferred_element_type=F32)


def _pre_single(x, mod, wts, tabq, tabk, *, mix_len):
    b, t, d = x.shape
    whole = lambda *shape: pl.BlockSpec(shape, lambda i: (0,) * len(shape))
    consts = [wts[n] for n in _PRE_SINGLE_CONSTS]
    return pl.pallas_call(
        functools.partial(_pre_single_kernel, mix_len=mix_len),
        grid=(1,),
        in_specs=([whole(b, t, d), whole(*mod.shape)] + [_const_spec(a.shape) for a in consts]
                  + [whole(*tabq.shape), whole(*tabk.shape)]),
        out_specs=[whole(b, t, GMLP_WIDTH), whole(b, N_HEADS, t, HEAD_BLOCK), whole(b, t, KV_LORA),
                   whole(b, t, QK_ROPE), whole(b, t, GMLP_WIDTH)],
        out_shape=[jax.ShapeDtypeStruct((b, t, GMLP_WIDTH), BF16),
                   jax.ShapeDtypeStruct((b, N_HEADS, t, HEAD_BLOCK), BF16),
                   jax.ShapeDtypeStruct((b, t, KV_LORA), F32),
                   jax.ShapeDtypeStruct((b, t, QK_ROPE), F32),
                   jax.ShapeDtypeStruct((b, t, GMLP_WIDTH), F32)],
        scratch_shapes=[pltpu.VMEM((b * t, IN_WIDTH_PADDED), F32)],
        compiler_params=pltpu.CompilerParams(
            dimension_semantics=("arbitrary",), vmem_limit_bytes=VMEM_LIMIT_BYTES),
        name="pre_single",
    )(x, mod, *consts, tabq, tabk)


def _pre_pipelined(x, mod, wts, tabq, tabk, *, rows, mix_len):
    b, t, d = x.shape
    per_batch = t // rows
    n_tiles = b * per_batch
    cur = lambda i: jnp.minimum(i, n_tiles - 1)
    prev = lambda i: jnp.maximum(i - 1, 0)
    pb, pj = (lambda i: prev(i) // per_batch), (lambda i: prev(i) % per_batch)
    tok_out = lambda w: pl.BlockSpec((1, rows, w), lambda i: (pb(i), pj(i), 0))
    head = pl.BlockSpec((1, N_HEADS, rows, HEAD_BLOCK), lambda i: (pb(i), 0, pj(i), 0))
    tab = lambda a: pl.BlockSpec((rows, a.shape[1]), lambda i: (pj(i), 0))
    consts = [wts[n] for n in _PRE_PIPELINED_CONSTS]
    return pl.pallas_call(
        functools.partial(_pre_pipelined_kernel, mix_len=mix_len),
        grid=(n_tiles + 1,),
        in_specs=([pl.BlockSpec((1, rows, d), lambda i: (cur(i) // per_batch, cur(i) % per_batch, 0)),
                   pl.BlockSpec((1,) + mod.shape[1:], lambda i: (cur(i) // per_batch, 0, 0))]
                  + [_const_spec(a.shape) for a in consts] + [tab(tabq), tab(tabk)]),
        out_specs=[tok_out(GMLP_WIDTH), head, head,
                   pl.BlockSpec((1, 1, MLA_WIDTH, rows), lambda i: (pb(i), pj(i), 0, 0)),
                   tok_out(KV_LORA),
                   pl.BlockSpec((1, QK_ROPE, rows), lambda i: (pb(i), 0, pj(i)))],
        out_shape=[jax.ShapeDtypeStruct((b, t, GMLP_WIDTH), BF16),
                   jax.ShapeDtypeStruct((b, N_HEADS, t, HEAD_BLOCK), BF16),
                   jax.ShapeDtypeStruct((b, N_HEADS, t, HEAD_BLOCK), BF16),
                   jax.ShapeDtypeStruct((b, per_batch, MLA_WIDTH, rows), BF16),
                   jax.ShapeDtypeStruct((b, t, KV_LORA), F32),
                   jax.ShapeDtypeStruct((b, QK_ROPE, t), F32)],
        scratch_shapes=[pltpu.VMEM((2, rows, IN_WIDTH_PADDED), F32)],
        compiler_params=pltpu.CompilerParams(
            dimension_semantics=("arbitrary",), vmem_limit_bytes=VMEM_LIMIT_BYTES),
        name="pre_pipelined",
    )(x, mod, *consts, tabq, tabk)


def _attn_kernel(q_ref, k_ref, vt_ref, o_ref, acc_ref, *maybe_m_ref, tq, online):
    t = q_ref.shape[2]
    hb = tq // 2
    mb = hb // 2
    key_chunk = lax.broadcasted_iota(jnp.int32, (mb, mb), 0) // CHUNK
    query_chunk = lax.broadcasted_iota(jnp.int32, (mb, mb), 1) // CHUNK
    visible = key_chunk <= query_chunk
    ones = jnp.ones((DEN_ROWS, hb), BF16)

    def operands(hh, q0, qo, rows, kt, c0, keys):
        q = q_ref[0, hh, pl.ds(q0 + qo, rows), :]
        k = k_ref[0, hh, pl.ds(kt * hb + c0, keys), :]
        lhs = jnp.concatenate([vt_ref[0, kt, hh * V_HEAD:(hh + 1) * V_HEAD, c0:c0 + keys], ones[:, :keys]], axis=0)
        return lhs, lax.dot_general(k, q, _NT, preferred_element_type=F32)

    def unshifted(hh, q0, qo, rows, kt, c0, keys, masked=False):
        lhs, s = operands(hh, q0, qo, rows, kt, c0, keys)
        p = jnp.exp2(s)
        if masked:
            p = jnp.where(visible, p, 0.0)
        return jnp.dot(lhs, p.astype(BF16), preferred_element_type=F32)

    def online_update(hh, q0, qo, rows, kt, c0, keys, masked=False):
        m_ref = maybe_m_ref[0]
        lhs, s = operands(hh, q0, qo, rows, kt, c0, keys)
        if masked:
            s = jnp.where(visible, s, _NEG)
        cols = slice(qo, qo + rows)
        m_old = m_ref[hh, :, cols]
        m_new = jnp.maximum(m_old, jnp.max(s, axis=0, keepdims=True))
        p = jnp.exp2(s - m_new).astype(BF16)
        acc_ref[hh, :, cols] = (jnp.exp2(m_old - m_new) * acc_ref[hh, :, cols]
                                + jnp.dot(lhs, p, preferred_element_type=F32))
        m_ref[hh, :, cols] = m_new

    diagonal = [(0, mb, 0, 0, mb, True), (mb, mb, 0, 0, mb, False), (mb, mb, 0, mb, mb, True),
                (hb, hb, 0, 0, hb, False),
                (hb, mb, 1, 0, mb, True), (hb + mb, mb, 1, 0, mb, False), (hb + mb, mb, 1, mb, mb, True)]

    def q_tile(qi, carry):
        q0 = qi * tq
        acc_ref[...] = jnp.zeros(acc_ref.shape, F32)
        if online:
            maybe_m_ref[0][...] = jnp.full(maybe_m_ref[0].shape, -jnp.inf, F32)

        def body(j, c):
            tiles = [(hh, 2 * j + half) for half in range(2) for hh in range(2)]
            if online:
                for hh, kt in tiles:
                    online_update(hh, q0, 0, tq, kt, 0, hb)
            else:
                staged = [operands(hh, q0, 0, tq, kt, 0, hb) for hh, kt in tiles]
                probs = [jnp.exp2(s).astype(BF16) for _, s in staged]
                for hh in range(2):
                    acc_ref[hh] += sum(jnp.dot(lhs, p, preferred_element_type=F32)
                                       for (th, _), (lhs, _), p in zip(tiles, staged, probs) if th == hh)
            return c

        for j in range(qi):
            body(j, 0)

        totals = []
        for hh in range(2):
            if online:
                for qo, rows, kt_off, c0, keys, masked in diagonal:
                    online_update(hh, q0, qo, rows, qi * 2 + kt_off, c0, keys, masked)
                totals.append([acc_ref[hh, :, g * mb:(g + 1) * mb] for g in range(tq // mb)])
            else:
                staged = [operands(hh, q0, qo, rows, qi * 2 + kt_off, c0, keys)
                          for qo, rows, kt_off, c0, keys, _ in diagonal]
                probs = [jnp.exp2(s) for _, s in staged]
                probs = [jnp.where(visible, p, 0.0) if blk[5] else p for p, blk in zip(probs, diagonal)]
                parts = [jnp.dot(lhs, p.astype(BF16), preferred_element_type=F32)
                         for (lhs, _), p in zip(staged, probs)]
                groups = [acc_ref[hh, :, g * mb:(g + 1) * mb] for g in range(tq // mb)]
                for (qo, rows, *_), part in zip(diagonal, parts):
                    for g in range(rows // mb):
                        groups[qo // mb + g] = groups[qo // mb + g] + part[:, g * mb:(g + 1) * mb]
                totals.append(groups)
        for g in range(tq // mb):
            o_t = [tot[g][:V_HEAD] * pl.reciprocal(tot[g][V_HEAD:V_HEAD + 1], approx=False) for tot in totals]
            o_ref[0, pl.ds(q0 + g * mb, mb), :] = jnp.concatenate(o_t, axis=0).T.astype(o_ref.dtype)
        return carry

    for qi in range(t // tq):
        q_tile(qi, 0)


def _attn_prompt(q, k, vt, score_bound):
    b, nh, t, _ = q.shape
    n_kt, _, hb = vt.shape[1:]
    assert hb == ATTN_TK and n_kt * hb == t
    qk = pl.BlockSpec((1, 2, t, HEAD_BLOCK), lambda i, j: (i, j, 0, 0))

    def call(online):
        scratch = [pltpu.VMEM((2, ACC_ROWS, ATTN_TQ), F32)]
        if online:
            scratch.append(pltpu.VMEM((2, 1, ATTN_TQ), F32))
        return pl.pallas_call(
            functools.partial(_attn_kernel, tq=ATTN_TQ, online=online),
            grid=(b, nh // 2),
            in_specs=[qk, qk, pl.BlockSpec((1, n_kt, 2 * V_HEAD, hb), lambda i, j: (i, 0, j, 0))],
            out_specs=pl.BlockSpec((1, t, 2 * V_HEAD), lambda i, j: (i, 0, j)),
            out_shape=jax.ShapeDtypeStruct((b, t, MLA_WIDTH), BF16),
            scratch_shapes=scratch,
            compiler_params=pltpu.CompilerParams(
                dimension_semantics=("parallel", "parallel"), vmem_limit_bytes=VMEM_LIMIT_BYTES),
            name="attn_prompt_online" if online else "attn_prompt_bounded")

    return lax.cond(score_bound <= MAX_UNSHIFTED_SCORE, call(False), call(True), q, k, vt)


def _pair_out(acc0, acc1):
    first = lax.broadcasted_iota(jnp.int32, (1, LANES), 1) < V_HEAD
    num = jnp.where(first, pltpu.roll(acc0, V_HEAD, axis=1), acc1)
    den = jnp.where(first, acc0, pltpu.roll(acc1, V_HEAD, axis=1))
    return num * pl.reciprocal(den, approx=False)


def _attn_sample_kernel(q_ref, past_ckv_ref, past_kr_ref, new_ckv_ref, new_kr_ref, wukv_ref, o_ref):
    lane = lax.broadcasted_iota(jnp.int32, (1, LANES), 1)
    is_nope = lane < QK_NOPE
    spread = (lax.broadcasted_iota(jnp.int32, (QK_ROPE, LANES), 1) % QK_ROPE
              == lax.broadcasted_iota(jnp.int32, (QK_ROPE, LANES), 0)).astype(BF16)
    one_b = jnp.ones((1, LANES), BF16)

    def segment(ckv_ref, kr_ref):
        kv = jnp.dot(ckv_ref[0].astype(BF16), wukv_ref[...], preferred_element_type=F32)
        krope_b = jnp.dot(kr_ref[0].astype(BF16), spread, preferred_element_type=F32).astype(BF16)
        return kv, krope_b

    def head_kv(seg, hd):
        kv, krope_b = seg
        kvh = kv[:, hd * HEAD_BLOCK:(hd + 1) * HEAD_BLOCK]
        ks = jnp.sum(jnp.where(is_nope, kvh * kvh, 0.0), axis=-1, keepdims=True)
        kn = kvh * lax.rsqrt(ks + QK_NOPE * EPS)
        return (jnp.where(is_nope, kn.astype(BF16), krope_b), jnp.where(is_nope, one_b, kvh.astype(BF16)))

    past, new = segment(past_ckv_ref, past_kr_ref), segment(new_ckv_ref, new_kr_ref)
    accs = []
    for hd in range(N_HEADS):
        q = q_ref[0, hd]
        (k0, v0), (k1, v1) = head_kv(past, hd), head_kv(new, hd)
        s0 = lax.dot_general(q, k0, _NT, preferred_element_type=F32)
        s1 = lax.dot_general(q, k1, _NT, preferred_element_type=F32)
        m = jnp.maximum(jnp.max(s0, axis=-1, keepdims=True), jnp.max(s1, axis=-1, keepdims=True))
        accs.append(jnp.dot(jnp.exp2(s0 - m).astype(BF16), v0, preferred_element_type=F32)
                    + jnp.dot(jnp.exp2(s1 - m).astype(BF16), v1, preferred_element_type=F32))
        if hd % 2 == 1:
            o_ref[0, :, (hd - 1) * V_HEAD:(hd + 1) * V_HEAD] = _pair_out(accs[hd - 1], accs[hd]).astype(o_ref.dtype)


def _attn_sample(q, past_ckv, past_kr, new_ckv, new_kr, w_ukv):
    b, nh, t, _ = q.shape
    per_batch = lambda a: pl.BlockSpec((1,) + a.shape[1:], lambda i: (i,) + (0,) * (a.ndim - 1))
    return pl.pallas_call(
        _attn_sample_kernel,
        grid=(b,),
        in_specs=[per_batch(q), per_batch(past_ckv), per_batch(past_kr), per_batch(new_ckv), per_batch(new_kr),
                  _const_spec(w_ukv.shape)],
        out_specs=pl.BlockSpec((1, t, MLA_WIDTH), lambda i: (i, 0, 0)),
        out_shape=jax.ShapeDtypeStruct((b, t, MLA_WIDTH), BF16),
        compiler_params=pltpu.CompilerParams(
            dimension_semantics=("parallel",), vmem_limit_bytes=VMEM_LIMIT_BYTES),
        name="attn_sample",
    )(q, past_ckv, past_kr, new_ckv, new_kr, w_ukv)


def _post_kernel(x_ref, ya_ref, yb_ref, mod_ref, wout_ref, n2g_ref, wfi_ref, wfo_ref, o_ref, acc_ref):
    nb, rows, d = x_ref.shape
    flat = lambda a: a.reshape(nb * rows, a.shape[-1])
    mod = mod_ref[...]
    g1, sh2, sc2, g2 = mod[:, 2:3], mod[:, 3:4], mod[:, 4:5], mod[:, 5:6]
    mix = (jnp.dot(flat(ya_ref[...]), wout_ref[:GMLP_WIDTH, :], preferred_element_type=F32)
           + jnp.dot(flat(yb_ref[...]), wout_ref[GMLP_WIDTH:, :], preferred_element_type=F32))
    x1 = x_ref[...] + g1 * mix.reshape(nb, rows, d)
    o_ref[...] = x1
    h2 = flat(x1 * _rms(x1, d) * (n2g_ref[...] * (1.0 + sc2)) + sh2).astype(BF16)
    d_ff = wfo_ref.shape[0]
    for c in range(d_ff // FFN_CHUNK):
        gate = jnp.dot(h2, wfi_ref[:, c * FFN_CHUNK:(c + 1) * FFN_CHUNK], preferred_element_type=F32)
        up = jnp.dot(h2, wfi_ref[:, d_ff + c * FFN_CHUNK:d_ff + (c + 1) * FFN_CHUNK], preferred_element_type=F32)
        act = (gate * jax.nn.sigmoid(gate) * up).astype(BF16)
        part = jnp.dot(act, wfo_ref[c * FFN_CHUNK:(c + 1) * FFN_CHUNK, :], preferred_element_type=F32)
        if c == 0:
            acc_ref[...] = part
        else:
            acc_ref[...] += part
    o_ref[...] = o_ref[...] + g2 * acc_ref[...].reshape(nb, rows, d)


def _post(x, ya, yb, mod, wts, *, nb, rows):
    b, t, d = x.shape
    tok = lambda w: pl.BlockSpec((nb, rows, w), lambda i, j: (i, j, 0))
    consts = [wts[n] for n in ("w_out", "n2g", "w_ffn_in", "w_ffn_out")]
    return pl.pallas_call(
        _post_kernel,
        grid=(b // nb, t // rows),
        in_specs=[tok(d), tok(GMLP_WIDTH), tok(MLA_WIDTH),
                  pl.BlockSpec((nb,) + mod.shape[1:], lambda i, j: (i, 0, 0)),
                  _const_spec(consts[0].shape), _const_spec(consts[1].shape),
                  _const_spec(consts[2].shape), _const_spec(consts[3].shape)],
        out_specs=tok(d),
        out_shape=jax.ShapeDtypeStruct((b, t, d), F32),
        scratch_shapes=[pltpu.VMEM((nb * rows, d), F32)],
        compiler_params=pltpu.CompilerParams(
            dimension_semantics=("parallel", "parallel"), vmem_limit_bytes=VMEM_LIMIT_BYTES),
        name="post",
    )(x, ya, yb, mod, consts[0], consts[1], consts[2], consts[3])


def _rope_tables(first_pos, n, *, mirrored_odd, repeat=1):
    half = QK_ROPE // 2
    inv = ROPE_THETA ** (-np.arange(half, dtype=np.float64) / half)
    ang = (first_pos + np.arange(n, dtype=np.float64))[:, None] * inv[None, :]
    cos, sin = np.cos(ang), np.sin(ang)
    rot = np.concatenate([cos, cos, -sin, sin], axis=-1)
    nope = np.ones((n, QK_NOPE))
    tabq = (SCALE * LOG2E) * np.concatenate([nope, rot] + ([rot, nope] if mirrored_odd else [nope, rot]), axis=-1)
    tabk = np.concatenate([rot, rot], axis=-1)
    return jnp.asarray(np.tile(tabq, (repeat, 1)), F32), jnp.asarray(np.tile(tabk, (repeat, 1)), F32)


def _prep_weights(w_in, w_s, b_s, q_norm_g, w_uq, kv_norm_g, w_ukv, qn_g, qr_g, kn_g, kr_g,
                  norm1_g, w_out, norm2_g, w_ffn_in, w_ffn_out):
    o_kr = 2 * GMLP_WIDTH + Q_LORA + KV_LORA
    w_kr = w_in[:, o_kr:o_kr + QK_ROPE]
    w_kr_sw = _swap_halves(w_kr)
    w_in_p = jnp.concatenate([w_in[:, :o_kr], w_kr, w_kr_sw, w_kr, w_kr_sw], axis=-1)
    d_q = w_uq.shape[0]
    wq = w_uq.reshape(d_q, N_HEADS, QK_HEAD)
    wq_n, wq_r = wq[:, :, :QK_NOPE], wq[:, :, QK_NOPE:]
    wq_plain = jnp.concatenate([wq_n, wq_r, _swap_halves(wq_r)], axis=-1)
    wq_mirror = jnp.concatenate([wq_r, _swap_halves(wq_r), wq_n], axis=-1)
    odd = (jnp.arange(N_HEADS) % 2 == 1)[None, :, None]
    w_uq_flat = wq_plain.reshape(d_q, N_HEADS * HEAD_BLOCK)
    w_uq_pair = jnp.where(odd, wq_mirror, wq_plain).reshape(d_q, N_HEADS * HEAD_BLOCK)
    wkv = w_ukv.reshape(KV_LORA, N_HEADS, QK_NOPE + V_HEAD)
    w_uk = wkv[:, :, :QK_NOPE].reshape(KV_LORA, N_HEADS * QK_NOPE)
    w_uvt = wkv[:, :, QK_NOPE:].reshape(KV_LORA, MLA_WIDTH).T
    root = math.sqrt(QK_NOPE)
    q_nope_row = qn_g * kn_g * (root * root)
    q_rope_row = jnp.concatenate([qr_g, _swap_halves(qr_g)]) * root
    qrow_plain = jnp.concatenate([q_nope_row, q_rope_row])
    qrow_mirror = jnp.concatenate([q_rope_row, q_nope_row])
    row = lambda a: a.reshape(1, -1).astype(F32)
    return {
        "n1g": row(norm1_g), "w_in": w_in_p.astype(BF16), "w_s": w_s.astype(F32), "b_s": b_s.T.astype(F32),
        "qng": row(q_norm_g), "kvg": row(kv_norm_g),
        "w_uq": w_uq_pair.astype(BF16), "w_uq_flat": w_uq_flat.astype(BF16),
        "qrow": row(jnp.concatenate([qrow_plain, qrow_mirror])),
        "qrow_flat": row(jnp.concatenate([qrow_plain, qrow_plain])),
        "w_ukv": w_ukv.astype(BF16), "w_uk": w_uk.astype(BF16), "w_uvt": w_uvt.astype(BF16),
        "krg": row(jnp.concatenate([kr_g, _swap_halves(kr_g), kr_g, _swap_halves(kr_g)])),
        "w_out": w_out.astype(BF16), "n2g": row(norm2_g),
        "w_ffn_in": w_ffn_in.astype(BF16), "w_ffn_out": w_ffn_out.astype(BF16),
    }


def kernel(x_prompt, x_sample, cache_ckv, cache_krope, c_prompt, c_sample, w_ada, b_ada, norm1_g, w_in, w_s, b_s, q_norm_g, w_uq, kv_norm_g, w_ukv, qn_g, qr_g, kn_g, kr_g, w_out, norm2_g, w_ffn_in, w_ffn_out):
    depth = w_in.shape[0]
    assert depth == 1, "single trunk layer"
    assert PRE_ROWS == ATTN_TK
    bp, tp, d = x_prompt.shape
    bs, ts, _ = x_sample.shape
    past_len = cache_ckv.shape[2]

    wts = _prep_weights(w_in[0], w_s[0], b_s[0], q_norm_g[0], w_uq[0], kv_norm_g[0], w_ukv[0], qn_g[0], qr_g[0],
                        kn_g[0], kr_g[0], norm1_g[0], w_out[0], norm2_g[0], w_ffn_in[0], w_ffn_out[0])
    mod = _ada(jnp.concatenate([c_prompt, c_sample], axis=0), w_ada[0], b_ada[0]).reshape(bp + bs, 6, d)
    mod_p, mod_s = mod[:bp], mod[bp:]

    tabq_p, tabk_p = _rope_tables(0, tp, mirrored_odd=True)
    ya_p, q_p, k_p, vt_p, ckv_p, krt_p = _pre_pipelined(x_prompt, mod_p, wts, tabq_p, tabk_p,
                                                        rows=PRE_ROWS, mix_len=GMLP_CHUNK)
    gmax = lambda g: jnp.max(jnp.abs(g))
    score_bound = SCALE * (QK_NOPE * gmax(qn_g) * gmax(kn_g) + QK_ROPE * gmax(qr_g) * gmax(kr_g))
    yb_p = _attn_prompt(q_p, k_p, vt_p, score_bound)
    y_p = _post(x_prompt, ya_p, yb_p, mod_p, wts, nb=1, rows=POST_ROWS)

    tabq_s, tabk_s = _rope_tables(past_len, ts, mirrored_odd=False, repeat=bs)
    ya_s, q_s, ckv_s, kr_s, v_s = _pre_single(x_sample, mod_s, wts, tabq_s, tabk_s, mix_len=min(ts, GMLP_CHUNK))
    yb_s = _attn_sample(q_s, cache_ckv[0], cache_krope[0], ckv_s, kr_s, wts["w_ukv"])
    y_s = _post(x_sample, ya_s, yb_s, mod_s, wts, nb=bs, rows=ts)

    return (y_p, y_s, ckv_p[None], jnp.swapaxes(krt_p, 1, 2)[None], ckv_s[None], kr_s[None], v_s[None])
```

```python
import functools
import math

import jax
import jax.numpy as jnp
import numpy as np
from jax import lax
from jax.experimental import pallas as pl
from jax.experimental.pallas import tpu as pltpu

F32 = jnp.float32
BF16 = jnp.bfloat16

CHUNK = 64
GMLP_CHUNK = 128
GMLP_GROUPS = 4
GMLP_HEAD = 128
GMLP_WIDTH = GMLP_GROUPS * GMLP_HEAD
N_HEADS = 8
QK_NOPE = 64
QK_ROPE = 32
QK_HEAD = QK_NOPE + QK_ROPE
V_HEAD = 64
Q_LORA = 384
KV_LORA = 256
MLA_WIDTH = N_HEADS * V_HEAD
ROPE_THETA = 10000.0
EPS = 1e-6
SCALE = QK_HEAD ** -0.5

LANES = 128
HEAD_BLOCK = LANES
KR_BLOCK = LANES
IN_WIDTH_PADDED = 2 * GMLP_WIDTH + Q_LORA + KV_LORA + KR_BLOCK
VMEM_LIMIT_BYTES = 56 * 1024 * 1024

PRE_ROWS = 512
POST_ROWS = 1024
FFN_CHUNK = 256
ATTN_TQ = 1024
ATTN_TK = ATTN_TQ // 2
DEN_ROWS = 16
ACC_ROWS = V_HEAD + DEN_ROWS

_NEG = float(jnp.finfo(jnp.float32).min)
_NT = (((1,), (1,)), ((), ()))
LOG2E = math.log2(math.e)
MAX_UNSHIFTED_SCORE = 60.0


def _swap_halves(a, axis=-1):
    n = a.shape[axis] // 2
    lo = lax.slice_in_dim(a, 0, n, axis=axis)
    hi = lax.slice_in_dim(a, n, 2 * n, axis=axis)
    return jnp.concatenate([hi, lo], axis=axis)


def _const_spec(shape):
    nd = len(shape)
    return pl.BlockSpec(shape, lambda *_: (0,) * nd, pipeline_mode=pl.Buffered(1))


def _ada_kernel(c_ref, w_ref, b_ref, o_ref):
    c = c_ref[...]
    a = c * jax.nn.sigmoid(c)
    w = w_ref[...]
    a_hi, w_hi = a.astype(BF16), w.astype(BF16)
    a_lo, w_lo = (a - a_hi.astype(F32)).astype(BF16), (w - w_hi.astype(F32)).astype(BF16)
    n = a.shape[0]
    both = jnp.dot(jnp.concatenate([a_hi, a_lo], axis=0), w_hi, preferred_element_type=F32)
    o_ref[...] = both[:n] + both[n:] + jnp.dot(a_hi, w_lo, preferred_element_type=F32) + b_ref[...]


def _ada(c_all, w_ada, b_ada):
    n, d = c_all.shape
    width = w_ada.shape[1]
    bn = d
    return pl.pallas_call(
        _ada_kernel,
        grid=(width // bn,),
        in_specs=[pl.BlockSpec((n, d), lambda j: (0, 0)),
                  pl.BlockSpec((d, bn), lambda j: (0, j)),
                  pl.BlockSpec((1, bn), lambda j: (0, j))],
        out_specs=pl.BlockSpec((n, bn), lambda j: (0, j)),
        out_shape=jax.ShapeDtypeStruct((n, width), F32),
        compiler_params=pltpu.CompilerParams(vmem_limit_bytes=VMEM_LIMIT_BYTES),
        name="ada",
    )(c_all, w_ada, b_ada.reshape(1, width))


def _rms(x, width):
    return lax.rsqrt(jnp.sum(x * x, axis=-1, keepdims=True) * (1.0 / width) + EPS)


def _pre_norm(x_ref, mod_ref, n1g_ref):
    nb, rows, d = x_ref.shape
    x = x_ref[...]
    mod = mod_ref[...]
    sh1, sc1 = mod[:, 0:1], mod[:, 1:2]
    h = x * _rms(x, d) * (n1g_ref[...] * (1.0 + sc1)) + sh1
    return h.reshape(nb * rows, d).astype(BF16)


def _gelu_tanh(x):
    k = -2.0 * math.sqrt(2.0 / math.pi) * LOG2E
    e = jnp.exp2(x * (x * x * (k * 0.044715) + k))
    return x * pl.reciprocal(1.0 + e, approx=False)


def _half_sums(a, lo):
    sq = a * a
    ss_all = jnp.sum(sq, axis=-1, keepdims=True)
    ss_lo = jnp.sum(jnp.where(lo, sq, 0.0), axis=-1, keepdims=True)
    return jnp.where(lo, ss_lo, ss_all - ss_lo)


def _pre_heads(z_ref, ws_ref, bs_ref, qng_ref, wuq_ref, kvg_ref, qrow_ref, krg_ref, tabq_ref, tabk_ref,
               ya_ref, q_ref, ckv_ref, kr_ref, maybe_kv, maybe_vraw_ref, mix_len):
    nb, tile_rows = ya_ref.shape[0], ya_ref.shape[1]
    rows = z_ref.shape[0]
    per_head = lambda a: a.reshape(nb, tile_rows, HEAD_BLOCK)
    o_v, o_q, o_kv, o_kr = GMLP_WIDTH, 2 * GMLP_WIDTH, 2 * GMLP_WIDTH + Q_LORA, 2 * GMLP_WIDTH + Q_LORA + KV_LORA
    u = _gelu_tanh(z_ref[:, :o_v])
    v = _gelu_tanh(z_ref[:, o_v:o_q])
    if maybe_vraw_ref is not None:
        maybe_vraw_ref[...] = v.reshape(maybe_vraw_ref.shape)

    vb = v.astype(BF16)
    pi = lax.broadcasted_iota(jnp.int32, (mix_len, mix_len), 0) // CHUNK
    pj = lax.broadcasted_iota(jnp.int32, (mix_len, mix_len), 1) // CHUNK
    for g in range(GMLP_GROUPS):
        wsg = jnp.where(pj <= pi, ws_ref[g, :mix_len, :mix_len], 0.0).astype(BF16)
        bg = bs_ref[:mix_len, g:g + 1]
        cols = slice(g * GMLP_HEAD, (g + 1) * GMLP_HEAD)
        for c in range(rows // mix_len):
            rws = slice(c * mix_len, (c + 1) * mix_len)
            mixed = jnp.dot(wsg, vb[rws, cols], preferred_element_type=F32) + bg
            bi, off = divmod(c * mix_len, tile_rows)
            ya_ref[bi, off:off + mix_len, cols] = (u[rws, cols] * mixed).astype(ya_ref.dtype)

    lane = lax.broadcasted_iota(jnp.int32, (1, LANES), 1)
    lo = lane < LANES // 2

    kr = z_ref[:, o_kr:o_kr + KR_BLOCK]
    kr_ss = jnp.sum(jnp.where(lane < QK_ROPE, kr * kr, 0.0), axis=-1, keepdims=True)
    kr_p = kr * lax.rsqrt(kr_ss * (1.0 / QK_ROPE) + EPS) * krg_ref[...] * tabk_ref[...]
    krope = kr_p + pltpu.roll(kr_p, LANES - QK_ROPE, axis=1)
    if kr_ref.shape[-1] == QK_ROPE:
        kr_ref[...] = krope[:, :QK_ROPE].reshape(kr_ref.shape)
    else:
        kr_ref[0] = krope.T[:QK_ROPE, :]

    ckv_raw = z_ref[:, o_kv:o_kr]
    ckv = ckv_raw * _rms(ckv_raw, KV_LORA) * kvg_ref[...]
    ckv_ref[...] = ckv.reshape(ckv_ref.shape)

    cq_raw = z_ref[:, o_q:o_kv]
    cq = cq_raw * _rms(cq_raw, Q_LORA) * qng_ref[...]
    q = jnp.dot(cq.astype(BF16), wuq_ref[...], preferred_element_type=F32)
    qfac = tabq_ref[...] * qrow_ref[...]
    for hd in range(N_HEADS):
        qh = q[:, hd * HEAD_BLOCK:(hd + 1) * HEAD_BLOCK]
        r = lax.rsqrt(_half_sums(qh, lo) + QK_NOPE * EPS)
        fac = qfac[:, (hd % 2) * LANES:(hd % 2 + 1) * LANES]
        q_ref[:, hd] = per_head((qh * r * fac).astype(q_ref.dtype))

    if maybe_kv is not None:
        wuk_ref, wuvt_ref, k_ref, vt_ref = maybe_kv
        ckv_b = ckv.astype(BF16)
        kp_all = jnp.dot(ckv_b, wuk_ref[...], preferred_element_type=F32)
        vt_ref[0, 0] = lax.dot_general(wuvt_ref[...], ckv_b, _NT, preferred_element_type=F32).astype(vt_ref.dtype)
        krope_b = krope.astype(k_ref.dtype)
        for pair in range(N_HEADS // 2):
            kp = kp_all[:, pair * LANES:(pair + 1) * LANES]
            kn = (kp * lax.rsqrt(_half_sums(kp, lo) + QK_NOPE * EPS)).astype(k_ref.dtype)
            k_ref[:, 2 * pair] = per_head(jnp.where(lo, kn, krope_b))
            k_ref[:, 2 * pair + 1] = per_head(jnp.where(lo, krope_b, kn))


_PRE_SINGLE_CONSTS = ("n1g", "w_in", "w_s", "b_s", "qng", "w_uq_flat", "kvg", "qrow_flat", "krg")
_PRE_PIPELINED_CONSTS = ("n1g", "w_in", "w_s", "b_s", "qng", "w_uq", "kvg", "qrow", "krg", "w_uk", "w_uvt")


def _pre_single_kernel(x_ref, mod_ref, n1g_ref, win_ref, ws_ref, bs_ref, qng_ref, wuq_ref, kvg_ref, qrow_ref, krg_ref,
                       tabq_ref, tabk_ref, ya_ref, q_ref, ckv_ref, kr_ref, vraw_ref, z_ref, *, mix_len):
    z_ref[...] = jnp.dot(_pre_norm(x_ref, mod_ref, n1g_ref), win_ref[...], preferred_element_type=F32)
    _pre_heads(z_ref, ws_ref, bs_ref, qng_ref, wuq_ref, kvg_ref, qrow_ref, krg_ref, tabq_ref, tabk_ref,
               ya_ref, q_ref, ckv_ref, kr_ref, None, vraw_ref, mix_len)


def _pre_pipelined_kernel(x_ref, mod_ref, n1g_ref, win_ref, ws_ref, bs_ref, qng_ref, wuq_ref, kvg_ref, qrow_ref,
                          krg_ref, wuk_ref, wuvt_ref, tabq_ref, tabk_ref,
                          ya_ref, q_ref, k_ref, vt_ref, ckv_ref, kr_ref, z_ref, *, mix_len):
    i = pl.program_id(0)

    @pl.when(i == 0)
    def _():
        z_ref[...] = jnp.zeros(z_ref.shape, F32)

    for parity in range(2):
        @pl.when(i % 2 == parity)
        def _():
            _pre_heads(z_ref.at[1 - parity], ws_ref, bs_ref, qng_ref, wuq_ref, kvg_ref, qrow_ref, krg_ref,
                       tabq_ref, tabk_ref, ya_ref, q_ref, ckv_ref, kr_ref,
                       (wuk_ref, wuvt_ref, k_ref, vt_ref), None, mix_len)
            z_ref[parity] = jnp.dot(_pre_norm(x_ref, mod_ref, n1g_ref), win_ref[...], preferred_element_type=F32)


def _pre_single(x, mod, wts, tabq, tabk, *, mix_len):
    b, t, d = x.shape
    whole = lambda *shape: pl.BlockSpec(shape, lambda i: (0,) * len(shape))
    consts = [wts[n] for n in _PRE_SINGLE_CONSTS]
    return pl.pallas_call(
        functools.partial(_pre_single_kernel, mix_len=mix_len),
        grid=(1,),
        in_specs=([whole(b, t, d), whole(*mod.shape)] + [_const_spec(a.shape) for a in consts]
                  + [whole(*tabq.shape), whole(*tabk.shape)]),
        out_specs=[whole(b, t, GMLP_WIDTH), whole(b, N_HEADS, t, HEAD_BLOCK), whole(b, t, KV_LORA),
                   whole(b, t, QK_ROPE), whole(b, t, GMLP_WIDTH)],
        out_shape=[jax.ShapeDtypeStruct((b, t, GMLP_WIDTH), BF16),
                   jax.ShapeDtypeStruct((b, N_HEADS, t, HEAD_BLOCK), BF16),
                   jax.ShapeDtypeStruct((b, t, KV_LORA), F32),
                   jax.ShapeDtypeStruct((b, t, QK_ROPE), F32),
                   jax.ShapeDtypeStruct((b, t, GMLP_WIDTH), F32)],
        scratch_shapes=[pltpu.VMEM((b * t, IN_WIDTH_PADDED), F32)],
        compiler_params=pltpu.CompilerParams(
            dimension_semantics=("arbitrary",), vmem_limit_bytes=VMEM_LIMIT_BYTES),
        name="pre_single",
    )(x, mod, *consts, tabq, tabk)


def _pre_pipelined(x, mod, wts, tabq, tabk, *, rows, mix_len):
    b, t, d = x.shape
    per_batch = t // rows
    n_tiles = b * per_batch
    cur = lambda i: jnp.minimum(i, n_tiles - 1)
    prev = lambda i: jnp.maximum(i - 1, 0)
    pb, pj = (lambda i: prev(i) // per_batch), (lambda i: prev(i) % per_batch)
    tok_out = lambda w: pl.BlockSpec((1, rows, w), lambda i: (pb(i), pj(i), 0))
    head = pl.BlockSpec((1, N_HEADS, rows, HEAD_BLOCK), lambda i: (pb(i), 0, pj(i), 0))
    tab = lambda a: pl.BlockSpec((rows, a.shape[1]), lambda i: (pj(i), 0))
    consts = [wts[n] for n in _PRE_PIPELINED_CONSTS]
    return pl.pallas_call(
        functools.partial(_pre_pipelined_kernel, mix_len=mix_len),
        grid=(n_tiles + 1,),
        in_specs=([pl.BlockSpec((1, rows, d), lambda i: (cur(i) // per_batch, cur(i) % per_batch, 0)),
                   pl.BlockSpec((1,) + mod.shape[1:], lambda i: (cur(i) // per_batch, 0, 0))]
                  + [_const_spec(a.shape) for a in consts] + [tab(tabq), tab(tabk)]),
        out_specs=[tok_out(GMLP_WIDTH), head, head,
                   pl.BlockSpec((1, 1, MLA_WIDTH, rows), lambda i: (pb(i), pj(i), 0, 0)),
                   tok_out(KV_LORA),
                   pl.BlockSpec((1, QK_ROPE, rows), lambda i: (pb(i), 0, pj(i)))],
        out_shape=[jax.ShapeDtypeStruct((b, t, GMLP_WIDTH), BF16),
                   jax.ShapeDtypeStruct((b, N_HEADS, t, HEAD_BLOCK), BF16),
                   jax.ShapeDtypeStruct((b, N_HEADS, t, HEAD_BLOCK), BF16),
                   jax.ShapeDtypeStruct((b, per_batch, MLA_WIDTH, rows), BF16),
                   jax.ShapeDtypeStruct((b, t, KV_LORA), F32),
                   jax.ShapeDtypeStruct((b, QK_ROPE, t), F32)],
        scratch_shapes=[pltpu.VMEM((2, rows, IN_WIDTH_PADDED), F32)],
        compiler_params=pltpu.CompilerParams(
            dimension_semantics=("arbitrary",), vmem_limit_bytes=VMEM_LIMIT_BYTES),
        name="pre_pipelined",
    )(x, mod, *consts, tabq, tabk)


def _attn_kernel(q_ref, k_ref, vt_ref, o_ref, acc_ref, *maybe_m_ref, tq, online):
    t = q_ref.shape[2]
    hb = tq // 2
    mb = hb // 2
    key_chunk = lax.broadcasted_iota(jnp.int32, (mb, mb), 0) // CHUNK
    query_chunk = lax.broadcasted_iota(jnp.int32, (mb, mb), 1) // CHUNK
    visible = key_chunk <= query_chunk
    ones = jnp.ones((DEN_ROWS, hb), BF16)

    def operands(hh, q0, qo, rows, kt, c0, keys):
        q = q_ref[0, hh, pl.ds(q0 + qo, rows), :]
        k = k_ref[0, hh, pl.ds(kt * hb + c0, keys), :]
        lhs = jnp.concatenate([vt_ref[0, kt, hh * V_HEAD:(hh + 1) * V_HEAD, c0:c0 + keys], ones[:, :keys]], axis=0)
        return lhs, lax.dot_general(k, q, _NT, preferred_element_type=F32)

    def unshifted(hh, q0, qo, rows, kt, c0, keys, masked=False):
        lhs, s = operands(hh, q0, qo, rows, kt, c0, keys)
        p = jnp.exp2(s)
        if masked:
            p = jnp.where(visible, p, 0.0)
        return jnp.dot(lhs, p.astype(BF16), preferred_element_type=F32)

    def online_update(hh, q0, qo, rows, kt, c0, keys, masked=False):
        m_ref = maybe_m_ref[0]
        lhs, s = operands(hh, q0, qo, rows, kt, c0, keys)
        if masked:
            s = jnp.where(visible, s, _NEG)
        cols = slice(qo, qo + rows)
        m_old = m_ref[hh, :, cols]
        m_new = jnp.maximum(m_old, jnp.max(s, axis=0, keepdims=True))
        p = jnp.exp2(s - m_new).astype(BF16)
        acc_ref[hh, :, cols] = (jnp.exp2(m_old - m_new) * acc_ref[hh, :, cols]
                                + jnp.dot(lhs, p, preferred_element_type=F32))
        m_ref[hh, :, cols] = m_new

    diagonal = [(0, mb, 0, 0, mb, True), (mb, mb, 0, 0, mb, False), (mb, mb, 0, mb, mb, True),
                (hb, hb, 0, 0, hb, False),
                (hb, mb, 1, 0, mb, True), (hb + mb, mb, 1, 0, mb, False), (hb + mb, mb, 1, mb, mb, True)]

    def q_tile(qi, carry):
        q0 = qi * tq
        acc_ref[...] = jnp.zeros(acc_ref.shape, F32)
        if online:
            maybe_m_ref[0][...] = jnp.full(maybe_m_ref[0].shape, -jnp.inf, F32)

        def body(j, c):
            tiles = [(hh, 2 * j + half) for half in range(2) for hh in range(2)]
            if online:
                for hh, kt in tiles:
                    online_update(hh, q0, 0, tq, kt, 0, hb)
            else:
                staged = [operands(hh, q0, 0, tq, kt, 0, hb) for hh, kt in tiles]
                probs = [jnp.exp2(s).astype(BF16) for _, s in staged]
                for hh in range(2):
                    acc_ref[hh] += sum(jnp.dot(lhs, p, preferred_element_type=F32)
                                       for (th, _), (lhs, _), p in zip(tiles, staged, probs) if th == hh)
            return c

        for j in range(qi):
            body(j, 0)

        totals = []
        for hh in range(2):
            if online:
                for qo, rows, kt_off, c0, keys, masked in diagonal:
                    online_update(hh, q0, qo, rows, qi * 2 + kt_off, c0, keys, masked)
                totals.append([acc_ref[hh, :, g * mb:(g + 1) * mb] for g in range(tq // mb)])
            else:
                staged = [operands(hh, q0, qo, rows, qi * 2 + kt_off, c0, keys)
                          for qo, rows, kt_off, c0, keys, _ in diagonal]
                probs = [jnp.exp2(s) for _, s in staged]
                probs = [jnp.where(visible, p, 0.0) if blk[5] else p for p, blk in zip(probs, diagonal)]
                parts = [jnp.dot(lhs, p.astype(BF16), preferred_element_type=F32)
                         for (lhs, _), p in zip(staged, probs)]
                groups = [acc_ref[hh, :, g * mb:(g + 1) * mb] for g in range(tq // mb)]
                for (qo, rows, *_), part in zip(diagonal, parts):
                    for g in range(rows // mb):
                        groups[qo // mb + g] = groups[qo // mb + g] + part[:, g * mb:(g + 1) * mb]
                totals.append(groups)
        for g in range(tq // mb):
            o_t = [tot[g][:V_HEAD] * pl.reciprocal(tot[g][V_HEAD:V_HEAD + 1], approx=False) for tot in totals]
            o_ref[0, pl.ds(q0 + g * mb, mb), :] = jnp.concatenate(o_t, axis=0).T.astype(o_ref.dtype)
        return carry

    for qi in range(t // tq):
        q_tile(qi, 0)


def _attn_prompt(q, k, vt, score_bound):
    b, nh, t, _ = q.shape
    n_kt, _, hb = vt.shape[1:]
    assert hb == ATTN_TK and n_kt * hb == t
    qk = pl.BlockSpec((1, 2, t, HEAD_BLOCK), lambda i, j: (i, j, 0, 0))

    def call(online):
        scratch = [pltpu.VMEM((2, ACC_ROWS, ATTN_TQ), F32)]
        if online:
            scratch.append(pltpu.VMEM((2, 1, ATTN_TQ), F32))
        return pl.pallas_call(
            functools.partial(_attn_kernel, tq=ATTN_TQ, online=online),
            grid=(b, nh // 2),
            in_specs=[qk, qk, pl.BlockSpec((1, n_kt, 2 * V_HEAD, hb), lambda i, j: (i, 0, j, 0))],
            out_specs=pl.BlockSpec((1, t, 2 * V_HEAD), lambda i, j: (i, 0, j)),
            out_shape=jax.ShapeDtypeStruct((b, t, MLA_WIDTH), BF16),
            scratch_shapes=scratch,
            compiler_params=pltpu.CompilerParams(
                dimension_semantics=("parallel", "parallel"), vmem_limit_bytes=VMEM_LIMIT_BYTES),
            name="attn_prompt_online" if online else "attn_prompt_bounded")

    return lax.cond(score_bound <= MAX_UNSHIFTED_SCORE, call(False), call(True), q, k, vt)


def _pair_out(acc0, acc1):
    first = lax.broadcasted_iota(jnp.int32, (1, LANES), 1) < V_HEAD
    num = jnp.where(first, pltpu.roll(acc0, V_HEAD, axis=1), acc1)
    den = jnp.where(first, acc0, pltpu.roll(acc1, V_HEAD, axis=1))
    return num * pl.reciprocal(den, approx=False)


def _attn_sample_kernel(q_ref, past_ckv_ref, past_kr_ref, new_ckv_ref, new_kr_ref, wukv_ref, o_ref):
    lane = lax.broadcasted_iota(jnp.int32, (1, LANES), 1)
    is_nope = lane < QK_NOPE
    spread = (lax.broadcasted_iota(jnp.int32, (QK_ROPE, LANES), 1) % QK_ROPE
              == lax.broadcasted_iota(jnp.int32, (QK_ROPE, LANES), 0)).astype(BF16)
    one_b = jnp.ones((1, LANES), BF16)

    def segment(ckv_ref, kr_ref):
        kv = jnp.dot(ckv_ref[0].astype(BF16), wukv_ref[...], preferred_element_type=F32)
        krope_b = jnp.dot(kr_ref[0].astype(BF16), spread, preferred_element_type=F32).astype(BF16)
        return kv, krope_b

    def head_kv(seg, hd):
        kv, krope_b = seg
        kvh = kv[:, hd * HEAD_BLOCK:(hd + 1) * HEAD_BLOCK]
        ks = jnp.sum(jnp.where(is_nope, kvh * kvh, 0.0), axis=-1, keepdims=True)
        kn = kvh * lax.rsqrt(ks + QK_NOPE * EPS)
        return (jnp.where(is_nope, kn.astype(BF16), krope_b), jnp.where(is_nope, one_b, kvh.astype(BF16)))

    past, new = segment(past_ckv_ref, past_kr_ref), segment(new_ckv_ref, new_kr_ref)
    accs = []
    for hd in range(N_HEADS):
        q = q_ref[0, hd]
        (k0, v0), (k1, v1) = head_kv(past, hd), head_kv(new, hd)
        s0 = lax.dot_general(q, k0, _NT, preferred_element_type=F32)
        s1 = lax.dot_general(q, k1, _NT, preferred_element_type=F32)
        m = jnp.maximum(jnp.max(s0, axis=-1, keepdims=True), jnp.max(s1, axis=-1, keepdims=True))
        accs.append(jnp.dot(jnp.exp2(s0 - m).astype(BF16), v0, preferred_element_type=F32)
                    + jnp.dot(jnp.exp2(s1 - m).astype(BF16), v1, preferred_element_type=F32))
        if hd % 2 == 1:
            o_ref[0, :, (hd - 1) * V_HEAD:(hd + 1) * V_HEAD] = _pair_out(accs[hd - 1], accs[hd]).astype(o_ref.dtype)


def _attn_sample(q, past_ckv, past_kr, new_ckv, new_kr, w_ukv):
    b, nh, t, _ = q.shape
    per_batch = lambda a: pl.BlockSpec((1,) + a.shape[1:], lambda i: (i,) + (0,) * (a.ndim - 1))
    return pl.pallas_call(
        _attn_sample_kernel,
        grid=(b,),
        in_specs=[per_batch(q), per_batch(past_ckv), per_batch(past_kr), per_batch(new_ckv), per_batch(new_kr),
                  _const_spec(w_ukv.shape)],
        out_specs=pl.BlockSpec((1, t, MLA_WIDTH), lambda i: (i, 0, 0)),
        out_shape=jax.ShapeDtypeStruct((b, t, MLA_WIDTH), BF16),
        compiler_params=pltpu.CompilerParams(
            dimension_semantics=("parallel",), vmem_limit_bytes=VMEM_LIMIT_BYTES),
        name="attn_sample",
    )(q, past_ckv, past_kr, new_ckv, new_kr, w_ukv)


def _post_kernel(x_ref, ya_ref, yb_ref, mod_ref, wout_ref, n2g_ref, wfi_ref, wfo_ref, o_ref, acc_ref):
    nb, rows, d = x_ref.shape
    flat = lambda a: a.reshape(nb * rows, a.shape[-1])
    mod = mod_ref[...]
    g1, sh2, sc2, g2 = mod[:, 2:3], mod[:, 3:4], mod[:, 4:5], mod[:, 5:6]
    mixer = jnp.concatenate([flat(ya_ref[...]), flat(yb_ref[...])], axis=-1)
    mix = jnp.dot(mixer, wout_ref[...], preferred_element_type=F32)
    x1 = x_ref[...] + g1 * mix.reshape(nb, rows, d)
    o_ref[...] = x1
    h2 = flat(x1 * _rms(x1, d) * (n2g_ref[...] * (1.0 + sc2)) + sh2).astype(BF16)
    d_ff = wfo_ref.shape[0]
    for c in range(d_ff // FFN_CHUNK):
        gate = jnp.dot(h2, wfi_ref[:, c * FFN_CHUNK:(c + 1) * FFN_CHUNK], preferred_element_type=F32)
        up = jnp.dot(h2, wfi_ref[:, d_ff + c * FFN_CHUNK:d_ff + (c + 1) * FFN_CHUNK], preferred_element_type=F32)
        act = (gate * jax.nn.sigmoid(gate) * up).astype(BF16)
        part = jnp.dot(act, wfo_ref[c * FFN_CHUNK:(c + 1) * FFN_CHUNK, :], preferred_element_type=F32)
        if c == 0:
            acc_ref[...] = part
        else:
            acc_ref[...] += part
    o_ref[...] = o_ref[...] + g2 * acc_ref[...].reshape(nb, rows, d)


def _post(x, ya, yb, mod, wts, *, nb, rows):
    b, t, d = x.shape
    tok = lambda w: pl.BlockSpec((nb, rows, w), lambda i, j: (i, j, 0))
    consts = [wts[n] for n in ("w_out", "n2g", "w_ffn_in", "w_ffn_out")]
    return pl.pallas_call(
        _post_kernel,
        grid=(b // nb, t // rows),
        in_specs=[tok(d), tok(GMLP_WIDTH), tok(MLA_WIDTH),
                  pl.BlockSpec((nb,) + mod.shape[1:], lambda i, j: (i, 0, 0)),
                  _const_spec(consts[0].shape), _const_spec(consts[1].shape),
                  _const_spec(consts[2].shape), _const_spec(consts[3].shape)],
        out_specs=tok(d),
        out_shape=jax.ShapeDtypeStruct((b, t, d), F32),
        scratch_shapes=[pltpu.VMEM((nb * rows, d), F32)],
        compiler_params=pltpu.CompilerParams(
            dimension_semantics=("parallel", "parallel"), vmem_limit_bytes=VMEM_LIMIT_BYTES),
        name="post",
    )(x, ya, yb, mod, consts[0], consts[1], consts[2], consts[3])


def _rope_tables(first_pos, n, *, mirrored_odd, repeat=1):
    half = QK_ROPE // 2
    inv = ROPE_THETA ** (-np.arange(half, dtype=np.float64) / half)
    ang = (first_pos + np.arange(n, dtype=np.float64))[:, None] * inv[None, :]
    cos, sin = np.cos(ang), np.sin(ang)
    rot = np.concatenate([cos, cos, -sin, sin], axis=-1)
    nope = np.ones((n, QK_NOPE))
    tabq = (SCALE * LOG2E) * np.concatenate([nope, rot] + ([rot, nope] if mirrored_odd else [nope, rot]), axis=-1)
    tabk = np.concatenate([rot, rot], axis=-1)
    return jnp.asarray(np.tile(tabq, (repeat, 1)), F32), jnp.asarray(np.tile(tabk, (repeat, 1)), F32)


def _prep_weights(w_in, w_s, b_s, q_norm_g, w_uq, kv_norm_g, w_ukv, qn_g, qr_g, kn_g, kr_g,
                  norm1_g, w_out, norm2_g, w_ffn_in, w_ffn_out):
    o_kr = 2 * GMLP_WIDTH + Q_LORA + KV_LORA
    w_kr = w_in[:, o_kr:o_kr + QK_ROPE]
    w_kr_sw = _swap_halves(w_kr)
    w_in_p = jnp.concatenate([w_in[:, :o_kr], w_kr, w_kr_sw, w_kr, w_kr_sw], axis=-1)
    d_q = w_uq.shape[0]
    wq = w_uq.reshape(d_q, N_HEADS, QK_HEAD)
    wq_n, wq_r = wq[:, :, :QK_NOPE], wq[:, :, QK_NOPE:]
    wq_plain = jnp.concatenate([wq_n, wq_r, _swap_halves(wq_r)], axis=-1)
    wq_mirror = jnp.concatenate([wq_r, _swap_halves(wq_r), wq_n], axis=-1)
    odd = (jnp.arange(N_HEADS) % 2 == 1)[None, :, None]
    w_uq_flat = wq_plain.reshape(d_q, N_HEADS * HEAD_BLOCK)
    w_uq_pair = jnp.where(odd, wq_mirror, wq_plain).reshape(d_q, N_HEADS * HEAD_BLOCK)
    wkv = w_ukv.reshape(KV_LORA, N_HEADS, QK_NOPE + V_HEAD)
    w_uk = wkv[:, :, :QK_NOPE].reshape(KV_LORA, N_HEADS * QK_NOPE)
    w_uvt = wkv[:, :, QK_NOPE:].reshape(KV_LORA, MLA_WIDTH).T
    root = math.sqrt(QK_NOPE)
    q_nope_row = qn_g * kn_g * (root * root)
    q_rope_row = jnp.concatenate([qr_g, _swap_halves(qr_g)]) * root
    qrow_plain = jnp.concatenate([q_nope_row, q_rope_row])
    qrow_mirror = jnp.concatenate([q_rope_row, q_nope_row])
    row = lambda a: a.reshape(1, -1).astype(F32)
    return {
        "n1g": row(norm1_g), "w_in": w_in_p.astype(BF16), "w_s": w_s.astype(F32), "b_s": b_s.T.astype(F32),
        "qng": row(q_norm_g), "kvg": row(kv_norm_g),
        "w_uq": w_uq_pair.astype(BF16), "w_uq_flat": w_uq_flat.astype(BF16),
        "qrow": row(jnp.concatenate([qrow_plain, qrow_mirror])),
        "qrow_flat": row(jnp.concatenate([qrow_plain, qrow_plain])),
        "w_ukv": w_ukv.astype(BF16), "w_uk": w_uk.astype(BF16), "w_uvt": w_uvt.astype(BF16),
        "krg": row(jnp.concatenate([kr_g, _swap_halves(kr_g), kr_g, _swap_halves(kr_g)])),
        "w_out": w_out.astype(BF16), "n2g": row(norm2_g),
        "w_ffn_in": w_ffn_in.astype(BF16), "w_ffn_out": w_ffn_out.astype(BF16),
    }


def kernel(x_prompt, x_sample, cache_ckv, cache_krope, c_prompt, c_sample, w_ada, b_ada, norm1_g, w_in, w_s, b_s, q_norm_g, w_uq, kv_norm_g, w_ukv, qn_g, qr_g, kn_g, kr_g, w_out, norm2_g, w_ffn_in, w_ffn_out):
    depth = w_in.shape[0]
    assert depth == 1, "single trunk layer"
    assert PRE_ROWS == ATTN_TK
    bp, tp, d = x_prompt.shape
    bs, ts, _ = x_sample.shape
    past_len = cache_ckv.shape[2]

    wts = _prep_weights(w_in[0], w_s[0], b_s[0], q_norm_g[0], w_uq[0], kv_norm_g[0], w_ukv[0], qn_g[0], qr_g[0],
                        kn_g[0], kr_g[0], norm1_g[0], w_out[0], norm2_g[0], w_ffn_in[0], w_ffn_out[0])
    mod = _ada(jnp.concatenate([c_prompt, c_sample], axis=0), w_ada[0], b_ada[0]).reshape(bp + bs, 6, d)
    mod_p, mod_s = mod[:bp], mod[bp:]

    tabq_p, tabk_p = _rope_tables(0, tp, mirrored_odd=True)
    ya_p, q_p, k_p, vt_p, ckv_p, krt_p = _pre_pipelined(x_prompt, mod_p, wts, tabq_p, tabk_p,
                                                        rows=PRE_ROWS, mix_len=GMLP_CHUNK)
    gmax = lambda g: jnp.max(jnp.abs(g))
    score_bound = SCALE * (QK_NOPE * gmax(qn_g) * gmax(kn_g) + QK_ROPE * gmax(qr_g) * gmax(kr_g))
    yb_p = _attn_prompt(q_p, k_p, vt_p, score_bound)
    y_p = _post(x_prompt, ya_p, yb_p, mod_p, wts, nb=1, rows=POST_ROWS)

    tabq_s, tabk_s = _rope_tables(past_len, ts, mirrored_odd=False, repeat=bs)
    ya_s, q_s, ckv_s, kr_s, v_s = _pre_single(x_sample, mod_s, wts, tabq_s, tabk_s, mix_len=min(ts, GMLP_CHUNK))
    yb_s = _attn_sample(q_s, cache_ckv[0], cache_krope[0], ckv_s, kr_s, wts["w_ukv"])
    y_s = _post(x_sample, ya_s, yb_s, mod_s, wts, nb=bs, rows=ts)

    return (y_p, y_s, ckv_p[None], jnp.swapaxes(krt_p, 1, 2)[None], ckv_s[None], kr_s[None], v_s[None])
```

```python
import functools
import math

import jax
import jax.numpy as jnp
import numpy as np
from jax import lax
from jax.experimental import pallas as pl
from jax.experimental.pallas import tpu as pltpu

F32 = jnp.float32
BF16 = jnp.bfloat16

CHUNK = 64
GMLP_CHUNK = 128
GMLP_GROUPS = 4
GMLP_HEAD = 128
GMLP_WIDTH = GMLP_GROUPS * GMLP_HEAD
N_HEADS = 8
QK_NOPE = 64
QK_ROPE = 32
QK_HEAD = QK_NOPE + QK_ROPE
V_HEAD = 64
Q_LORA = 384
KV_LORA = 256
MLA_WIDTH = N_HEADS * V_HEAD
ROPE_THETA = 10000.0
EPS = 1e-6
SCALE = QK_HEAD ** -0.5

LANES = 128
HEAD_BLOCK = LANES
KR_BLOCK = LANES
IN_WIDTH_PADDED = 2 * GMLP_WIDTH + Q_LORA + KV_LORA + KR_BLOCK
VMEM_LIMIT_BYTES = 56 * 1024 * 1024

PRE_ROWS = 512
POST_ROWS = 1024
FFN_CHUNK = 256
ATTN_TQ = 1024
ATTN_TK = ATTN_TQ // 2
DEN_ROWS = 16
ACC_ROWS = V_HEAD + DEN_ROWS

_NEG = float(jnp.finfo(jnp.float32).min)
_NT = (((1,), (1,)), ((), ()))
LOG2E = math.log2(math.e)
MAX_UNSHIFTED_SCORE = 60.0


def _swap_halves(a, axis=-1):
    n = a.shape[axis] // 2
    lo = lax.slice_in_dim(a, 0, n, axis=axis)
    hi = lax.slice_in_dim(a, n, 2 * n, axis=axis)
    return jnp.concatenate([hi, lo], axis=axis)


def _const_spec(shape):
    nd = len(shape)
    return pl.BlockSpec(shape, lambda *_: (0,) * nd, pipeline_mode=pl.Buffered(1))


def _ada_kernel(c_ref, w_ref, b_ref, o_ref):
    c = c_ref[...]
    a = c * jax.nn.sigmoid(c)
    w = w_ref[...]
    a_hi, w_hi = a.astype(BF16), w.astype(BF16)
    a_lo, w_lo = (a - a_hi.astype(F32)).astype(BF16), (w - w_hi.astype(F32)).astype(BF16)
    n = a.shape[0]
    both = jnp.dot(jnp.concatenate([a_hi, a_lo], axis=0), w_hi, preferred_element_type=F32)
    o_ref[...] = both[:n] + both[n:] + jnp.dot(a_hi, w_lo, preferred_element_type=F32) + b_ref[...]


def _ada(c_all, w_ada, b_ada):
    n, d = c_all.shape
    width = w_ada.shape[1]
    bn = d
    return pl.pallas_call(
        _ada_kernel,
        grid=(width // bn,),
        in_specs=[pl.BlockSpec((n, d), lambda j: (0, 0)),
                  pl.BlockSpec((d, bn), lambda j: (0, j)),
                  pl.BlockSpec((1, bn), lambda j: (0, j))],
        out_specs=pl.BlockSpec((n, bn), lambda j: (0, j)),
        out_shape=jax.ShapeDtypeStruct((n, width), F32),
        compiler_params=pltpu.CompilerParams(vmem_limit_bytes=VMEM_LIMIT_BYTES),
        name="ada",
    )(c_all, w_ada, b_ada.reshape(1, width))


def _rms(x, width):
    return lax.rsqrt(jnp.sum(x * x, axis=-1, keepdims=True) * (1.0 / width) + EPS)


def _pre_norm(x_ref, mod_ref, n1g_ref):
    nb, rows, d = x_ref.shape
    x = x_ref[...]
    mod = mod_ref[...]
    sh1, sc1 = mod[:, 0:1], mod[:, 1:2]
    h = x * _rms(x, d) * (n1g_ref[...] * (1.0 + sc1)) + sh1
    return h.reshape(nb * rows, d).astype(BF16)


def _gelu_tanh(x):
    k = -2.0 * math.sqrt(2.0 / math.pi) * LOG2E
    e = jnp.exp2(x * (x * x * (k * 0.044715) + k))
    return x * pl.reciprocal(1.0 + e, approx=False)


def _half_sums(a, lo):
    sq = a * a
    ss_all = jnp.sum(sq, axis=-1, keepdims=True)
    ss_lo = jnp.sum(jnp.where(lo, sq, 0.0), axis=-1, keepdims=True)
    return jnp.where(lo, ss_lo, ss_all - ss_lo)


def _pre_heads(z_ref, ws_ref, bs_ref, qng_ref, wuq_ref, kvg_ref, qrow_ref, krg_ref, tabq_ref, tabk_ref,
               ya_ref, q_ref, ckv_ref, kr_ref, maybe_kv, maybe_vraw_ref, mix_len):
    nb, tile_rows = ya_ref.shape[0], ya_ref.shape[1]
    rows = z_ref.shape[0]
    per_head = lambda a: a.reshape(nb, tile_rows, HEAD_BLOCK)
    o_v, o_q, o_kv, o_kr = GMLP_WIDTH, 2 * GMLP_WIDTH, 2 * GMLP_WIDTH + Q_LORA, 2 * GMLP_WIDTH + Q_LORA + KV_LORA
    u = _gelu_tanh(z_ref[:, :o_v])
    v = _gelu_tanh(z_ref[:, o_v:o_q])
    if maybe_vraw_ref is not None:
        maybe_vraw_ref[...] = v.reshape(maybe_vraw_ref.shape)

    vb = v.astype(BF16)
    pi = lax.broadcasted_iota(jnp.int32, (mix_len, mix_len), 0) // CHUNK
    pj = lax.broadcasted_iota(jnp.int32, (mix_len, mix_len), 1) // CHUNK
    for g in range(GMLP_GROUPS):
        wsg = jnp.where(pj <= pi, ws_ref[g, :mix_len, :mix_len], 0.0).astype(BF16)
        bg = bs_ref[:mix_len, g:g + 1]
        cols = slice(g * GMLP_HEAD, (g + 1) * GMLP_HEAD)
        for c in range(rows // mix_len):
            rws = slice(c * mix_len, (c + 1) * mix_len)
            mixed = jnp.dot(wsg, vb[rws, cols], preferred_element_type=F32) + bg
            bi, off = divmod(c * mix_len, tile_rows)
            ya_ref[bi, off:off + mix_len, cols] = (u[rws, cols] * mixed).astype(ya_ref.dtype)

    lane = lax.broadcasted_iota(jnp.int32, (1, LANES), 1)
    lo = lane < LANES // 2

    kr = z_ref[:, o_kr:o_kr + KR_BLOCK]
    kr_ss = jnp.sum(jnp.where(lane < QK_ROPE, kr * kr, 0.0), axis=-1, keepdims=True)
    kr_p = kr * lax.rsqrt(kr_ss * (1.0 / QK_ROPE) + EPS) * krg_ref[...] * tabk_ref[...]
    krope = kr_p + pltpu.roll(kr_p, LANES - QK_ROPE, axis=1)
    if kr_ref.shape[-1] == QK_ROPE:
        kr_ref[...] = krope[:, :QK_ROPE].reshape(kr_ref.shape)
    else:
        kr_ref[0] = krope.T[:QK_ROPE, :]

    ckv_raw = z_ref[:, o_kv:o_kr]
    ckv = ckv_raw * _rms(ckv_raw, KV_LORA) * kvg_ref[...]
    ckv_ref[...] = ckv.reshape(ckv_ref.shape)

    cq_raw = z_ref[:, o_q:o_kv]
    cq = cq_raw * _rms(cq_raw, Q_LORA) * qng_ref[...]
    q = jnp.dot(cq.astype(BF16), wuq_ref[...], preferred_element_type=F32)
    qfac = tabq_ref[...] * qrow_ref[...]
    blocks = [q[:, hd * HEAD_BLOCK:(hd + 1) * HEAD_BLOCK] for hd in range(N_HEADS)]
    if maybe_kv is not None:
        wuk_ref, wuvt_ref, k_ref, vt_ref = maybe_kv
        ckv_b = ckv.astype(BF16)
        kp_all = jnp.dot(ckv_b, wuk_ref[...], preferred_element_type=F32)
        vt_ref[0, 0] = lax.dot_general(wuvt_ref[...], ckv_b, _NT, preferred_element_type=F32).astype(vt_ref.dtype)
        blocks += [kp_all[:, pair * LANES:(pair + 1) * LANES] for pair in range(N_HEADS // 2)]
    scales = [lax.rsqrt(_half_sums(blk, lo) + QK_NOPE * EPS) for blk in blocks]
    for hd in range(N_HEADS):
        fac = qfac[:, (hd % 2) * LANES:(hd % 2 + 1) * LANES]
        q_ref[:, hd] = per_head((blocks[hd] * scales[hd] * fac).astype(q_ref.dtype))
    if maybe_kv is not None:
        krope_b = krope.astype(k_ref.dtype)
        for pair in range(N_HEADS // 2):
            kn = (blocks[N_HEADS + pair] * scales[N_HEADS + pair]).astype(k_ref.dtype)
            k_ref[:, 2 * pair] = per_head(jnp.where(lo, kn, krope_b))
            k_ref[:, 2 * pair + 1] = per_head(jnp.where(lo, krope_b, kn))


_PRE_SINGLE_CONSTS = ("n1g", "w_in", "w_s", "b_s", "qng", "w_uq_flat", "kvg", "qrow_flat", "krg")
_PRE_PIPELINED_CONSTS = ("n1g", "w_in", "w_s", "b_s", "qng", "w_uq", "kvg", "qrow", "krg", "w_uk", "w_uvt")


def _pre_single_kernel(x_ref, mod_ref, n1g_ref, win_ref, ws_ref, bs_ref, qng_ref, wuq_ref, kvg_ref, qrow_ref, krg_ref,
                       tabq_ref, tabk_ref, ya_ref, q_ref, ckv_ref, kr_ref, vraw_ref, z_ref, *, mix_len):
    z_ref[...] = jnp.dot(_pre_norm(x_ref, mod_ref, n1g_ref), win_ref[...], preferred_element_type=F32)
    _pre_heads(z_ref, ws_ref, bs_ref, qng_ref, wuq_ref, kvg_ref, qrow_ref, krg_ref, tabq_ref, tabk_ref,
               ya_ref, q_ref, ckv_ref, kr_ref, None, vraw_ref, mix_len)


def _pre_pipelined_kernel(x_ref, mod_ref, n1g_ref, win_ref, ws_ref, bs_ref, qng_ref, wuq_ref, kvg_ref, qrow_ref,
                          krg_ref, wuk_ref, wuvt_ref, tabq_ref, tabk_ref,
                          ya_ref, q_ref, k_ref, vt_ref, ckv_ref, kr_ref, z_ref, *, mix_len):
    i = pl.program_id(0)

    @pl.when(i == 0)
    def _():
        z_ref[...] = jnp.zeros(z_ref.shape, F32)

    for parity in range(2):
        @pl.when(i % 2 == parity)
        def _():
            _pre_heads(z_ref.at[1 - parity], ws_ref, bs_ref, qng_ref, wuq_ref, kvg_ref, qrow_ref, krg_ref,
                       tabq_ref, tabk_ref, ya_ref, q_ref, ckv_ref, kr_ref,
                       (wuk_ref, wuvt_ref, k_ref, vt_ref), None, mix_len)
            z_ref[parity] = jnp.dot(_pre_norm(x_ref, mod_ref, n1g_ref), win_ref[...], preferred_element_type=F32)


def _pre_single(x, mod, wts, tabq, tabk, *, mix_len):
    b, t, d = x.shape
    whole = lambda *shape: pl.BlockSpec(shape, lambda i: (0,) * len(shape))
    consts = [wts[n] for n in _PRE_SINGLE_CONSTS]
    return pl.pallas_call(
        functools.partial(_pre_single_kernel, mix_len=mix_len),
        grid=(1,),
        in_specs=([whole(b, t, d), whole(*mod.shape)] + [_const_spec(a.shape) for a in consts]
                  + [whole(*tabq.shape), whole(*tabk.shape)]),
        out_specs=[whole(b, t, GMLP_WIDTH), whole(b, N_HEADS, t, HEAD_BLOCK), whole(b, t, KV_LORA),
                   whole(b, t, QK_ROPE), whole(b, t, GMLP_WIDTH)],
        out_shape=[jax.ShapeDtypeStruct((b, t, GMLP_WIDTH), BF16),
                   jax.ShapeDtypeStruct((b, N_HEADS, t, HEAD_BLOCK), BF16),
                   jax.ShapeDtypeStruct((b, t, KV_LORA), F32),
                   jax.ShapeDtypeStruct((b, t, QK_ROPE), F32),
                   jax.ShapeDtypeStruct((b, t, GMLP_WIDTH), F32)],
        scratch_shapes=[pltpu.VMEM((b * t, IN_WIDTH_PADDED), F32)],
        compiler_params=pltpu.CompilerParams(
            dimension_semantics=("arbitrary",), vmem_limit_bytes=VMEM_LIMIT_BYTES),
        name="pre_single",
    )(x, mod, *consts, tabq, tabk)


def _pre_pipelined(x, mod, wts, tabq, tabk, *, rows, mix_len):
    b, t, d = x.shape
    per_batch = t // rows
    n_tiles = b * per_batch
    cur = lambda i: jnp.minimum(i, n_tiles - 1)
    prev = lambda i: jnp.maximum(i - 1, 0)
    pb, pj = (lambda i: prev(i) // per_batch), (lambda i: prev(i) % per_batch)
    tok_out = lambda w: pl.BlockSpec((1, rows, w), lambda i: (pb(i), pj(i), 0))
    head = pl.BlockSpec((1, N_HEADS, rows, HEAD_BLOCK), lambda i: (pb(i), 0, pj(i), 0))
    tab = lambda a: pl.BlockSpec((rows, a.shape[1]), lambda i: (pj(i), 0))
    consts = [wts[n] for n in _PRE_PIPELINED_CONSTS]
    return pl.pallas_call(
        functools.partial(_pre_pipelined_kernel, mix_len=mix_len),
        grid=(n_tiles + 1,),
        in_specs=([pl.BlockSpec((1, rows, d), lambda i: (cur(i) // per_batch, cur(i) % per_batch, 0)),
                   pl.BlockSpec((1,) + mod.shape[1:], lambda i: (cur(i) // per_batch, 0, 0))]
                  + [_const_spec(a.shape) for a in consts] + [tab(tabq), tab(tabk)]),
        out_specs=[tok_out(GMLP_WIDTH), head, head,
                   pl.BlockSpec((1, 1, MLA_WIDTH, rows), lambda i: (pb(i), pj(i), 0, 0)),
                   tok_out(KV_LORA),
                   pl.BlockSpec((1, QK_ROPE, rows), lambda i: (pb(i), 0, pj(i)))],
        out_shape=[jax.ShapeDtypeStruct((b, t, GMLP_WIDTH), BF16),
                   jax.ShapeDtypeStruct((b, N_HEADS, t, HEAD_BLOCK), BF16),
                   jax.ShapeDtypeStruct((b, N_HEADS, t, HEAD_BLOCK), BF16),
                   jax.ShapeDtypeStruct((b, per_batch, MLA_WIDTH, rows), BF16),
                   jax.ShapeDtypeStruct((b, t, KV_LORA), F32),
                   jax.ShapeDtypeStruct((b, QK_ROPE, t), F32)],
        scratch_shapes=[pltpu.VMEM((2, rows, IN_WIDTH_PADDED), F32)],
        compiler_params=pltpu.CompilerParams(
            dimension_semantics=("arbitrary",), vmem_limit_bytes=VMEM_LIMIT_BYTES),
        name="pre_pipelined",
    )(x, mod, *consts, tabq, tabk)


def _aligned(index, multiple):
    return index if isinstance(index, int) else pl.multiple_of(index, multiple)


def _attn_kernel(q_ref, k_ref, vt_ref, o_ref, acc_ref, *maybe_m_ref, tq, online):
    t = q_ref.shape[2]
    unrolled = not online
    hb = tq // 2
    mb = hb // 2
    key_chunk = lax.broadcasted_iota(jnp.int32, (mb, mb), 0) // CHUNK
    query_chunk = lax.broadcasted_iota(jnp.int32, (mb, mb), 1) // CHUNK
    visible = key_chunk <= query_chunk
    ones = jnp.ones((DEN_ROWS, hb), BF16)

    def operands(hh, q0, qo, rows, kt, c0, keys):
        q = q_ref[0, hh, pl.ds(_aligned(q0 + qo, mb), rows), :]
        k = k_ref[0, hh, pl.ds(_aligned(kt * hb + c0, mb), keys), :]
        lhs = jnp.concatenate([vt_ref[0, kt, hh * V_HEAD:(hh + 1) * V_HEAD, c0:c0 + keys], ones[:, :keys]], axis=0)
        return lhs, lax.dot_general(k, q, _NT, preferred_element_type=F32)

    def unshifted(hh, q0, qo, rows, kt, c0, keys, masked=False):
        lhs, s = operands(hh, q0, qo, rows, kt, c0, keys)
        p = jnp.exp2(s)
        if masked:
            p = jnp.where(visible, p, 0.0)
        return jnp.dot(lhs, p.astype(BF16), preferred_element_type=F32)

    def online_update(hh, q0, qo, rows, kt, c0, keys, masked=False):
        m_ref = maybe_m_ref[0]
        lhs, s = operands(hh, q0, qo, rows, kt, c0, keys)
        if masked:
            s = jnp.where(visible, s, _NEG)
        cols = slice(qo, qo + rows)
        m_old = m_ref[hh, :, cols]
        m_new = jnp.maximum(m_old, jnp.max(s, axis=0, keepdims=True))
        p = jnp.exp2(s - m_new).astype(BF16)
        acc_ref[hh, :, cols] = (jnp.exp2(m_old - m_new) * acc_ref[hh, :, cols]
                                + jnp.dot(lhs, p, preferred_element_type=F32))
        m_ref[hh, :, cols] = m_new

    diagonal = [(0, mb, 0, 0, mb, True), (mb, mb, 0, 0, mb, False), (mb, mb, 0, mb, mb, True),
                (hb, hb, 0, 0, hb, False),
                (hb, mb, 1, 0, mb, True), (hb + mb, mb, 1, 0, mb, False), (hb + mb, mb, 1, mb, mb, True)]

    def q_tile(qi, carry):
        q0 = qi * tq
        acc_ref[...] = jnp.zeros(acc_ref.shape, F32)
        if online:
            maybe_m_ref[0][...] = jnp.full(maybe_m_ref[0].shape, -jnp.inf, F32)

        def body(j, c):
            tiles = [(hh, 2 * j + half) for half in range(2) for hh in range(2)]
            if online:
                for hh, kt in tiles:
                    online_update(hh, q0, 0, tq, kt, 0, hb)
            else:
                staged = [operands(hh, q0, 0, tq, kt, 0, hb) for hh, kt in tiles]
                probs = [jnp.exp2(s).astype(BF16) for _, s in staged]
                for hh in range(2):
                    acc_ref[hh] += sum(jnp.dot(lhs, p, preferred_element_type=F32)
                                       for (th, _), (lhs, _), p in zip(tiles, staged, probs) if th == hh)
            return c

        if unrolled:
            for j in range(qi):
                body(j, 0)
        else:
            lax.fori_loop(0, qi, body, 0)

        totals = []
        for hh in range(2):
            if online:
                for qo, rows, kt_off, c0, keys, masked in diagonal:
                    online_update(hh, q0, qo, rows, qi * 2 + kt_off, c0, keys, masked)
                totals.append([acc_ref[hh, :, g * mb:(g + 1) * mb] for g in range(tq // mb)])
            else:
                staged = [operands(hh, q0, qo, rows, qi * 2 + kt_off, c0, keys)
                          for qo, rows, kt_off, c0, keys, _ in diagonal]
                probs = [jnp.exp2(s) for _, s in staged]
                probs = [jnp.where(visible, p, 0.0) if blk[5] else p for p, blk in zip(probs, diagonal)]
                parts = [jnp.dot(lhs, p.astype(BF16), preferred_element_type=F32)
                         for (lhs, _), p in zip(staged, probs)]
                groups = [acc_ref[hh, :, g * mb:(g + 1) * mb] for g in range(tq // mb)]
                for (qo, rows, *_), part in zip(diagonal, parts):
                    for g in range(rows // mb):
                        groups[qo // mb + g] = groups[qo // mb + g] + part[:, g * mb:(g + 1) * mb]
                totals.append(groups)
        for g in range(tq // mb):
            o_t = [tot[g][:V_HEAD] * pl.reciprocal(tot[g][V_HEAD:V_HEAD + 1], approx=False) for tot in totals]
            o_ref[0, pl.ds(_aligned(q0 + g * mb, mb), mb), :] = jnp.concatenate(o_t, axis=0).T.astype(o_ref.dtype)
        return carry

    if unrolled:
        for qi in range(t // tq):
            q_tile(qi, 0)
    else:
        lax.fori_loop(0, t // tq, q_tile, 0)


def _attn_prompt(q, k, vt, score_bound):
    b, nh, t, _ = q.shape
    n_kt, _, hb = vt.shape[1:]
    assert hb == ATTN_TK and n_kt * hb == t
    qk = pl.BlockSpec((1, 2, t, HEAD_BLOCK), lambda i, j: (i, j, 0, 0))

    def call(online):
        scratch = [pltpu.VMEM((2, ACC_ROWS, ATTN_TQ), F32)]
        if online:
            scratch.append(pltpu.VMEM((2, 1, ATTN_TQ), F32))
        return pl.pallas_call(
            functools.partial(_attn_kernel, tq=ATTN_TQ, online=online),
            grid=(b, nh // 2),
            in_specs=[qk, qk, pl.BlockSpec((1, n_kt, 2 * V_HEAD, hb), lambda i, j: (i, 0, j, 0))],
            out_specs=pl.BlockSpec((1, t, 2 * V_HEAD), lambda i, j: (i, 0, j)),
            out_shape=jax.ShapeDtypeStruct((b, t, MLA_WIDTH), BF16),
            scratch_shapes=scratch,
            compiler_params=pltpu.CompilerParams(
                dimension_semantics=("parallel", "parallel"), vmem_limit_bytes=VMEM_LIMIT_BYTES),
            name="attn_prompt_online" if online else "attn_prompt_bounded")

    return lax.cond(score_bound <= MAX_UNSHIFTED_SCORE, call(False), call(True), q, k, vt)


def _pair_out(acc0, acc1):
    first = lax.broadcasted_iota(jnp.int32, (1, LANES), 1) < V_HEAD
    num = jnp.where(first, pltpu.roll(acc0, V_HEAD, axis=1), acc1)
    den = jnp.where(first, acc0, pltpu.roll(acc1, V_HEAD, axis=1))
    return num * pl.reciprocal(den, approx=False)


def _attn_sample_kernel(q_ref, past_ckv_ref, past_kr_ref, new_ckv_ref, new_kr_ref, wukv_ref, o_ref):
    lane = lax.broadcasted_iota(jnp.int32, (1, LANES), 1)
    is_nope = lane < QK_NOPE
    spread = (lax.broadcasted_iota(jnp.int32, (QK_ROPE, LANES), 1) % QK_ROPE
              == lax.broadcasted_iota(jnp.int32, (QK_ROPE, LANES), 0)).astype(BF16)
    one_b = jnp.ones((1, LANES), BF16)

    def segment(ckv_ref, kr_ref):
        kv = jnp.dot(ckv_ref[0].astype(BF16), wukv_ref[...], preferred_element_type=F32)
        krope_b = jnp.dot(kr_ref[0].astype(BF16), spread, preferred_element_type=F32).astype(BF16)
        return kv, krope_b

    def head_kv(seg, hd):
        kv, krope_b = seg
        kvh = kv[:, hd * HEAD_BLOCK:(hd + 1) * HEAD_BLOCK]
        ks = jnp.sum(jnp.where(is_nope, kvh * kvh, 0.0), axis=-1, keepdims=True)
        kn = kvh * lax.rsqrt(ks + QK_NOPE * EPS)
        return (jnp.where(is_nope, kn.astype(BF16), krope_b), jnp.where(is_nope, one_b, kvh.astype(BF16)))

    past, new = segment(past_ckv_ref, past_kr_ref), segment(new_ckv_ref, new_kr_ref)
    heads = range(N_HEADS)
    kv = [(head_kv(past, hd), head_kv(new, hd)) for hd in heads]
    scores = [[lax.dot_general(q_ref[0, hd], k, _NT, preferred_element_type=F32) for k, _ in kv[hd]] for hd in heads]
    peaks = [jnp.maximum(*(jnp.max(s, axis=-1, keepdims=True) for s in scores[hd])) for hd in heads]
    probs = [[jnp.exp2(s - peaks[hd]).astype(BF16) for s in scores[hd]] for hd in heads]
    accs = [sum(jnp.dot(p, v, preferred_element_type=F32) for p, (_, v) in zip(probs[hd], kv[hd])) for hd in heads]
    for hd in range(0, N_HEADS, 2):
        o_ref[0, :, hd * V_HEAD:(hd + 2) * V_HEAD] = _pair_out(accs[hd], accs[hd + 1]).astype(o_ref.dtype)


def _attn_sample(q, past_ckv, past_kr, new_ckv, new_kr, w_ukv):
    b, nh, t, _ = q.shape
    per_batch = lambda a: pl.BlockSpec((1,) + a.shape[1:], lambda i: (i,) + (0,) * (a.ndim - 1))
    return pl.pallas_call(
        _attn_sample_kernel,
        grid=(b,),
        in_specs=[per_batch(q), per_batch(past_ckv), per_batch(past_kr), per_batch(new_ckv), per_batch(new_kr),
                  _const_spec(w_ukv.shape)],
        out_specs=pl.BlockSpec((1, t, MLA_WIDTH), lambda i: (i, 0, 0)),
        out_shape=jax.ShapeDtypeStruct((b, t, MLA_WIDTH), BF16),
        compiler_params=pltpu.CompilerParams(
            dimension_semantics=("parallel",), vmem_limit_bytes=VMEM_LIMIT_BYTES),
        name="attn_sample",
    )(q, past_ckv, past_kr, new_ckv, new_kr, w_ukv)


def _post_kernel(x_ref, ya_ref, yb_ref, mod_ref, wout_ref, n2g_ref, wfi_ref, wfo_ref, o_ref, acc_ref):
    nb, rows, d = x_ref.shape
    flat = lambda a: a.reshape(nb * rows, a.shape[-1])
    mod = mod_ref[...]
    g1, sh2, sc2, g2 = mod[:, 2:3], mod[:, 3:4], mod[:, 4:5], mod[:, 5:6]
    mixer = jnp.concatenate([flat(ya_ref[...]), flat(yb_ref[...])], axis=-1)
    mix = jnp.dot(mixer, wout_ref[...], preferred_element_type=F32)
    x1 = x_ref[...] + g1 * mix.reshape(nb, rows, d)
    o_ref[...] = x1
    h2 = flat(x1 * _rms(x1, d) * (n2g_ref[...] * (1.0 + sc2)) + sh2).astype(BF16)
    d_ff = wfo_ref.shape[0]
    for c in range(d_ff // FFN_CHUNK):
        gate = jnp.dot(h2, wfi_ref[:, c * FFN_CHUNK:(c + 1) * FFN_CHUNK], preferred_element_type=F32)
        up = jnp.dot(h2, wfi_ref[:, d_ff + c * FFN_CHUNK:d_ff + (c + 1) * FFN_CHUNK], preferred_element_type=F32)
        act = (gate * jax.nn.sigmoid(gate) * up).astype(BF16)
        part = jnp.dot(act, wfo_ref[c * FFN_CHUNK:(c + 1) * FFN_CHUNK, :], preferred_element_type=F32)
        if c == 0:
            acc_ref[...] = part
        else:
            acc_ref[...] += part
    o_ref[...] = o_ref[...] + g2 * acc_ref[...].reshape(nb, rows, d)


def _post(x, ya, yb, mod, wts, *, nb, rows):
    b, t, d = x.shape
    tok = lambda w: pl.BlockSpec((nb, rows, w), lambda i, j: (i, j, 0))
    consts = [wts[n] for n in ("w_out", "n2g", "w_ffn_in", "w_ffn_out")]
    return pl.pallas_call(
        _post_kernel,
        grid=(b // nb, t // rows),
        in_specs=[tok(d), tok(GMLP_WIDTH), tok(MLA_WIDTH),
                  pl.BlockSpec((nb,) + mod.shape[1:], lambda i, j: (i, 0, 0)),
                  _const_spec(consts[0].shape), _const_spec(consts[1].shape),
                  _const_spec(consts[2].shape), _const_spec(consts[3].shape)],
        out_specs=tok(d),
        out_shape=jax.ShapeDtypeStruct((b, t, d), F32),
        scratch_shapes=[pltpu.VMEM((nb * rows, d), F32)],
        compiler_params=pltpu.CompilerParams(
            dimension_semantics=("parallel", "parallel"), vmem_limit_bytes=VMEM_LIMIT_BYTES),
        name="post",
    )(x, ya, yb, mod, consts[0], consts[1], consts[2], consts[3])


def _rope_tables(first_pos, n, *, mirrored_odd, repeat=1):
    half = QK_ROPE // 2
    inv = ROPE_THETA ** (-np.arange(half, dtype=np.float64) / half)
    ang = (first_pos + np.arange(n, dtype=np.float64))[:, None] * inv[None, :]
    cos, sin = np.cos(ang), np.sin(ang)
    rot = np.concatenate([cos, cos, -sin, sin], axis=-1)
    nope = np.ones((n, QK_NOPE))
    tabq = (SCALE * LOG2E) * np.concatenate([nope, rot] + ([rot, nope] if mirrored_odd else [nope, rot]), axis=-1)
    tabk = np.concatenate([rot, rot], axis=-1)
    return jnp.asarray(np.tile(tabq, (repeat, 1)), F32), jnp.asarray(np.tile(tabk, (repeat, 1)), F32)


def _prep_weights(w_in, w_s, b_s, q_norm_g, w_uq, kv_norm_g, w_ukv, qn_g, qr_g, kn_g, kr_g,
                  norm1_g, w_out, norm2_g, w_ffn_in, w_ffn_out):
    o_kr = 2 * GMLP_WIDTH + Q_LORA + KV_LORA
    w_kr = w_in[:, o_kr:o_kr + QK_ROPE]
    w_kr_sw = _swap_halves(w_kr)
    w_in_p = jnp.concatenate([w_in[:, :o_kr], w_kr, w_kr_sw, w_kr, w_kr_sw], axis=-1)
    d_q = w_uq.shape[0]
    wq = w_uq.reshape(d_q, N_HEADS, QK_HEAD)
    wq_n, wq_r = wq[:, :, :QK_NOPE], wq[:, :, QK_NOPE:]
    wq_plain = jnp.concatenate([wq_n, wq_r, _swap_halves(wq_r)], axis=-1)
    wq_mirror = jnp.concatenate([wq_r, _swap_halves(wq_r), wq_n], axis=-1)
    odd = (jnp.arange(N_HEADS) % 2 == 1)[None, :, None]
    w_uq_flat = wq_plain.reshape(d_q, N_HEADS * HEAD_BLOCK)
    w_uq_pair = jnp.where(odd, wq_mirror, wq_plain).reshape(d_q, N_HEADS * HEAD_BLOCK)
    wkv = w_ukv.reshape(KV_LORA, N_HEADS, QK_NOPE + V_HEAD)
    w_uk = wkv[:, :, :QK_NOPE].reshape(KV_LORA, N_HEADS * QK_NOPE)
    w_uvt = wkv[:, :, QK_NOPE:].reshape(KV_LORA, MLA_WIDTH).T
    root = math.sqrt(QK_NOPE)
    q_nope_row = qn_g * kn_g * (root * root)
    q_rope_row = jnp.concatenate([qr_g, _swap_halves(qr_g)]) * root
    qrow_plain = jnp.concatenate([q_nope_row, q_rope_row])
    qrow_mirror = jnp.concatenate([q_rope_row, q_nope_row])
    row = lambda a: a.reshape(1, -1).astype(F32)
    return {
        "n1g": row(norm1_g), "w_in": w_in_p.astype(BF16), "w_s": w_s.astype(F32), "b_s": b_s.T.astype(F32),
        "qng": row(q_norm_g), "kvg": row(kv_norm_g),
        "w_uq": w_uq_pair.astype(BF16), "w_uq_flat": w_uq_flat.astype(BF16),
        "qrow": row(jnp.concatenate([qrow_plain, qrow_mirror])),
        "qrow_flat": row(jnp.concatenate([qrow_plain, qrow_plain])),
        "w_ukv": w_ukv.astype(BF16), "w_uk": w_uk.astype(BF16), "w_uvt": w_uvt.astype(BF16),
        "krg": row(jnp.concatenate([kr_g, _swap_halves(kr_g), kr_g, _swap_halves(kr_g)])),
        "w_out": w_out.astype(BF16), "n2g": row(norm2_g),
        "w_ffn_in": w_ffn_in.astype(BF16), "w_ffn_out": w_ffn_out.astype(BF16),
    }


def kernel(x_prompt, x_sample, cache_ckv, cache_krope, c_prompt, c_sample, w_ada, b_ada, norm1_g, w_in, w_s, b_s, q_norm_g, w_uq, kv_norm_g, w_ukv, qn_g, qr_g, kn_g, kr_g, w_out, norm2_g, w_ffn_in, w_ffn_out):
    depth = w_in.shape[0]
    assert depth == 1, "single trunk layer"
    assert PRE_ROWS == ATTN_TK
    bp, tp, d = x_prompt.shape
    bs, ts, _ = x_sample.shape
    past_len = cache_ckv.shape[2]

    wts = _prep_weights(w_in[0], w_s[0], b_s[0], q_norm_g[0], w_uq[0], kv_norm_g[0], w_ukv[0], qn_g[0], qr_g[0],
                        kn_g[0], kr_g[0], norm1_g[0], w_out[0], norm2_g[0], w_ffn_in[0], w_ffn_out[0])
    mod = _ada(jnp.concatenate([c_prompt, c_sample], axis=0), w_ada[0], b_ada[0]).reshape(bp + bs, 6, d)
    mod_p, mod_s = mod[:bp], mod[bp:]

    tabq_p, tabk_p = _rope_tables(0, tp, mirrored_odd=True)
    ya_p, q_p, k_p, vt_p, ckv_p, krt_p = _pre_pipelined(x_prompt, mod_p, wts, tabq_p, tabk_p,
                                                        rows=PRE_ROWS, mix_len=GMLP_CHUNK)
    gmax = lambda g: jnp.max(jnp.abs(g))
    score_bound = SCALE * (QK_NOPE * gmax(qn_g) * gmax(kn_g) + QK_ROPE * gmax(qr_g) * gmax(kr_g))
    yb_p = _attn_prompt(q_p, k_p, vt_p, score_bound)
    y_p = _post(x_prompt, ya_p, yb_p, mod_p, wts, nb=1, rows=POST_ROWS)

    tabq_s, tabk_s = _rope_tables(past_len, ts, mirrored_odd=False, repeat=bs)
    ya_s, q_s, ckv_s, kr_s, v_s = _pre_single(x_sample, mod_s, wts, tabq_s, tabk_s, mix_len=min(ts, GMLP_CHUNK))
    yb_s = _attn_sample(q_s, cache_ckv[0], cache_krope[0], ckv_s, kr_s, wts["w_ukv"])
    y_s = _post(x_sample, ya_s, yb_s, mod_s, wts, nb=bs, rows=ts)

    return (y_p, y_s, ckv_p[None], jnp.swapaxes(krt_p, 1, 2)[None], ckv_s[None], kr_s[None], v_s[None])
```

```python
import functools
import math

import jax
import jax.numpy as jnp
import numpy as np
from jax import lax
from jax.experimental import pallas as pl
from jax.experimental.pallas import tpu as pltpu

F32 = jnp.float32
BF16 = jnp.bfloat16

CHUNK = 64
GMLP_CHUNK = 128
GMLP_GROUPS = 4
GMLP_HEAD = 128
GMLP_WIDTH = GMLP_GROUPS * GMLP_HEAD
N_HEADS = 8
QK_NOPE = 64
QK_ROPE = 32
QK_HEAD = QK_NOPE + QK_ROPE
V_HEAD = 64
Q_LORA = 384
KV_LORA = 256
MLA_WIDTH = N_HEADS * V_HEAD
ROPE_THETA = 10000.0
EPS = 1e-6
SCALE = QK_HEAD ** -0.5

LANES = 128
HEAD_BLOCK = LANES
KR_BLOCK = LANES
IN_WIDTH_PADDED = 2 * GMLP_WIDTH + Q_LORA + KV_LORA + KR_BLOCK
VMEM_LIMIT_BYTES = 56 * 1024 * 1024

PRE_ROWS = 512
POST_ROWS = 1024
FFN_CHUNK = 256
ATTN_TQ = 1024
ATTN_TK = ATTN_TQ // 2
DEN_ROWS = 16
ACC_ROWS = V_HEAD + DEN_ROWS

_NEG = float(jnp.finfo(jnp.float32).min)
_NT = (((1,), (1,)), ((), ()))
LOG2E = math.log2(math.e)
MAX_UNSHIFTED_SCORE = 60.0


def _swap_halves(a, axis=-1):
    n = a.shape[axis] // 2
    lo = lax.slice_in_dim(a, 0, n, axis=axis)
    hi = lax.slice_in_dim(a, n, 2 * n, axis=axis)
    return jnp.concatenate([hi, lo], axis=axis)


def _const_spec(shape):
    nd = len(shape)
    return pl.BlockSpec(shape, lambda *_: (0,) * nd, pipeline_mode=pl.Buffered(1))


def _ada_kernel(c_ref, w_ref, b_ref, o_ref):
    c = c_ref[...]
    a = c * jax.nn.sigmoid(c)
    w = w_ref[...]
    a_hi, w_hi = a.astype(BF16), w.astype(BF16)
    a_lo, w_lo = (a - a_hi.astype(F32)).astype(BF16), (w - w_hi.astype(F32)).astype(BF16)
    n = a.shape[0]
    both = jnp.dot(jnp.concatenate([a_hi, a_lo], axis=0), w_hi, preferred_element_type=F32)
    o_ref[...] = both[:n] + both[n:] + jnp.dot(a_hi, w_lo, preferred_element_type=F32) + b_ref[...]


def _ada(c_all, w_ada, b_ada):
    n, d = c_all.shape
    width = w_ada.shape[1]
    bn = d
    return pl.pallas_call(
        _ada_kernel,
        grid=(width // bn,),
        in_specs=[pl.BlockSpec((n, d), lambda j: (0, 0)),
                  pl.BlockSpec((d, bn), lambda j: (0, j)),
                  pl.BlockSpec((1, bn), lambda j: (0, j))],
        out_specs=pl.BlockSpec((n, bn), lambda j: (0, j)),
        out_shape=jax.ShapeDtypeStruct((n, width), F32),
        compiler_params=pltpu.CompilerParams(vmem_limit_bytes=VMEM_LIMIT_BYTES),
        name="ada",
    )(c_all, w_ada, b_ada.reshape(1, width))


def _rms(x, width):
    return lax.rsqrt(jnp.sum(x * x, axis=-1, keepdims=True) * (1.0 / width) + EPS)


def _pre_norm(x_ref, mod_ref, n1g_ref):
    nb, rows, d = x_ref.shape
    x = x_ref[...]
    mod = mod_ref[...]
    sh1, sc1 = mod[:, 0:1], mod[:, 1:2]
    h = x * _rms(x, d) * (n1g_ref[...] * (1.0 + sc1)) + sh1
    return h.reshape(nb * rows, d).astype(BF16)


def _gelu_tanh(x):
    k = -2.0 * math.sqrt(2.0 / math.pi) * LOG2E
    e = jnp.exp2(x * (x * x * (k * 0.044715) + k))
    return x * pl.reciprocal(1.0 + e, approx=False)


def _half_sums(a, lo):
    sq = a * a
    ss_all = jnp.sum(sq, axis=-1, keepdims=True)
    ss_lo = jnp.sum(jnp.where(lo, sq, 0.0), axis=-1, keepdims=True)
    return jnp.where(lo, ss_lo, ss_all - ss_lo)


def _pre_heads(z_ref, ws_ref, bs_ref, qng_ref, wuq_ref, kvg_ref, qrow_ref, krg_ref, tabq_ref, tabk_ref,
               ya_ref, q_ref, ckv_ref, kr_ref, maybe_kv, maybe_vraw_ref, mix_len):
    nb, tile_rows = ya_ref.shape[0], ya_ref.shape[1]
    rows = z_ref.shape[0]
    per_head = lambda a: a.reshape(nb, tile_rows, HEAD_BLOCK)
    o_v, o_q, o_kv, o_kr = GMLP_WIDTH, 2 * GMLP_WIDTH, 2 * GMLP_WIDTH + Q_LORA, 2 * GMLP_WIDTH + Q_LORA + KV_LORA
    u = _gelu_tanh(z_ref[:, :o_v])
    v = _gelu_tanh(z_ref[:, o_v:o_q])
    if maybe_vraw_ref is not None:
        maybe_vraw_ref[...] = v.reshape(maybe_vraw_ref.shape)

    vb = v.astype(BF16)
    pi = lax.broadcasted_iota(jnp.int32, (mix_len, mix_len), 0) // CHUNK
    pj = lax.broadcasted_iota(jnp.int32, (mix_len, mix_len), 1) // CHUNK
    for g in range(GMLP_GROUPS):
        wsg = jnp.where(pj <= pi, ws_ref[g, :mix_len, :mix_len], 0.0).astype(BF16)
        bg = bs_ref[:mix_len, g:g + 1]
        cols = slice(g * GMLP_HEAD, (g + 1) * GMLP_HEAD)
        for c in range(rows // mix_len):
            rws = slice(c * mix_len, (c + 1) * mix_len)
            mixed = jnp.dot(wsg, vb[rws, cols], preferred_element_type=F32) + bg
            bi, off = divmod(c * mix_len, tile_rows)
            ya_ref[bi, off:off + mix_len, cols] = (u[rws, cols] * mixed).astype(ya_ref.dtype)

    lane = lax.broadcasted_iota(jnp.int32, (1, LANES), 1)
    lo = lane < LANES // 2

    kr = z_ref[:, o_kr:o_kr + KR_BLOCK]
    kr_ss = jnp.sum(jnp.where(lane < QK_ROPE, kr * kr, 0.0), axis=-1, keepdims=True)
    kr_p = kr * lax.rsqrt(kr_ss * (1.0 / QK_ROPE) + EPS) * krg_ref[...] * tabk_ref[...]
    krope = kr_p + pltpu.roll(kr_p, LANES - QK_ROPE, axis=1)
    if kr_ref.shape[-1] == QK_ROPE:
        kr_ref[...] = krope[:, :QK_ROPE].reshape(kr_ref.shape)
    else:
        kr_ref[0] = krope.T[:QK_ROPE, :]

    ckv_raw = z_ref[:, o_kv:o_kr]
    ckv = ckv_raw * _rms(ckv_raw, KV_LORA) * kvg_ref[...]
    ckv_ref[...] = ckv.reshape(ckv_ref.shape)

    cq_raw = z_ref[:, o_q:o_kv]
    cq = cq_raw * _rms(cq_raw, Q_LORA) * qng_ref[...]
    q = jnp.dot(cq.astype(BF16), wuq_ref[...], preferred_element_type=F32)
    qfac = tabq_ref[...] * qrow_ref[...]
    blocks = [q[:, hd * HEAD_BLOCK:(hd + 1) * HEAD_BLOCK] for hd in range(N_HEADS)]
    if maybe_kv is not None:
        wuk_ref, wuvt_ref, k_ref, vt_ref = maybe_kv
        ckv_b = ckv.astype(BF16)
        kp_all = jnp.dot(ckv_b, wuk_ref[...], preferred_element_type=F32)
        vt_ref[0, 0] = lax.dot_general(wuvt_ref[...], ckv_b, _NT, preferred_element_type=F32).astype(vt_ref.dtype)
        blocks += [kp_all[:, pair * LANES:(pair + 1) * LANES] for pair in range(N_HEADS // 2)]
    scales = [lax.rsqrt(_half_sums(blk, lo) + QK_NOPE * EPS) for blk in blocks]
    for hd in range(N_HEADS):
        fac = qfac[:, (hd % 2) * LANES:(hd % 2 + 1) * LANES]
        q_ref[:, hd] = per_head((blocks[hd] * scales[hd] * fac).astype(q_ref.dtype))
    if maybe_kv is not None:
        krope_b = krope.astype(k_ref.dtype)
        for pair in range(N_HEADS // 2):
            kn = (blocks[N_HEADS + pair] * scales[N_HEADS + pair]).astype(k_ref.dtype)
            k_ref[:, 2 * pair] = per_head(jnp.where(lo, kn, krope_b))
            k_ref[:, 2 * pair + 1] = per_head(jnp.where(lo, krope_b, kn))


_PRE_SINGLE_CONSTS = ("n1g", "w_in", "w_s", "b_s", "qng", "w_uq_flat", "kvg", "qrow_flat", "krg")
_PRE_PIPELINED_CONSTS = ("n1g", "w_in", "w_s", "b_s", "qng", "w_uq", "kvg", "qrow", "krg", "w_uk", "w_uvt")


def _pre_single_kernel(x_ref, mod_ref, n1g_ref, win_ref, ws_ref, bs_ref, qng_ref, wuq_ref, kvg_ref, qrow_ref, krg_ref,
                       tabq_ref, tabk_ref, ya_ref, q_ref, ckv_ref, kr_ref, vraw_ref, z_ref, *, mix_len):
    z_ref[...] = jnp.dot(_pre_norm(x_ref, mod_ref, n1g_ref), win_ref[...], preferred_element_type=F32)
    _pre_heads(z_ref, ws_ref, bs_ref, qng_ref, wuq_ref, kvg_ref, qrow_ref, krg_ref, tabq_ref, tabk_ref,
               ya_ref, q_ref, ckv_ref, kr_ref, None, vraw_ref, mix_len)


def _pre_pipelined_kernel(x_ref, mod_ref, n1g_ref, win_ref, ws_ref, bs_ref, qng_ref, wuq_ref, kvg_ref, qrow_ref,
                          krg_ref, wuk_ref, wuvt_ref, tabq_ref, tabk_ref,
                          ya_ref, q_ref, k_ref, vt_ref, ckv_ref, kr_ref, z_ref, *, mix_len):
    i = pl.program_id(0)

    @pl.when(i == 0)
    def _():
        z_ref[...] = jnp.zeros(z_ref.shape, F32)

    for parity in range(2):
        @pl.when(i % 2 == parity)
        def _():
            _pre_heads(z_ref.at[1 - parity], ws_ref, bs_ref, qng_ref, wuq_ref, kvg_ref, qrow_ref, krg_ref,
                       tabq_ref, tabk_ref, ya_ref, q_ref, ckv_ref, kr_ref,
                       (wuk_ref, wuvt_ref, k_ref, vt_ref), None, mix_len)
            z_ref[parity] = jnp.dot(_pre_norm(x_ref, mod_ref, n1g_ref), win_ref[...], preferred_element_type=F32)


def _pre_single(x, mod, wts, tabq, tabk, *, mix_len):
    b, t, d = x.shape
    whole = lambda *shape: pl.BlockSpec(shape, lambda i: (0,) * len(shape))
    consts = [wts[n] for n in _PRE_SINGLE_CONSTS]
    return pl.pallas_call(
        functools.partial(_pre_single_kernel, mix_len=mix_len),
        grid=(1,),
        in_specs=([whole(b, t, d), whole(*mod.shape)] + [_const_spec(a.shape) for a in consts]
                  + [whole(*tabq.shape), whole(*tabk.shape)]),
        out_specs=[whole(b, t, GMLP_WIDTH), whole(b, N_HEADS, t, HEAD_BLOCK), whole(b, t, KV_LORA),
                   whole(b, t, QK_ROPE), whole(b, t, GMLP_WIDTH)],
        out_shape=[jax.ShapeDtypeStruct((b, t, GMLP_WIDTH), BF16),
                   jax.ShapeDtypeStruct((b, N_HEADS, t, HEAD_BLOCK), BF16),
                   jax.ShapeDtypeStruct((b, t, KV_LORA), F32),
                   jax.ShapeDtypeStruct((b, t, QK_ROPE), F32),
                   jax.ShapeDtypeStruct((b, t, GMLP_WIDTH), F32)],
        scratch_shapes=[pltpu.VMEM((b * t, IN_WIDTH_PADDED), F32)],
        compiler_params=pltpu.CompilerParams(
            dimension_semantics=("arbitrary",), vmem_limit_bytes=VMEM_LIMIT_BYTES),
        name="pre_single",
    )(x, mod, *consts, tabq, tabk)


def _pre_pipelined(x, mod, wts, tabq, tabk, *, rows, mix_len):
    b, t, d = x.shape
    per_batch = t // rows
    n_tiles = b * per_batch
    cur = lambda i: jnp.minimum(i, n_tiles - 1)
    prev = lambda i: jnp.maximum(i - 1, 0)
    pb, pj = (lambda i: prev(i) // per_batch), (lambda i: prev(i) % per_batch)
    tok_out = lambda w: pl.BlockSpec((1, rows, w), lambda i: (pb(i), pj(i), 0))
    head = pl.BlockSpec((1, N_HEADS, rows, HEAD_BLOCK), lambda i: (pb(i), 0, pj(i), 0))
    tab = lambda a: pl.BlockSpec((rows, a.shape[1]), lambda i: (pj(i), 0))
    consts = [wts[n] for n in _PRE_PIPELINED_CONSTS]
    return pl.pallas_call(
        functools.partial(_pre_pipelined_kernel, mix_len=mix_len),
        grid=(n_tiles + 1,),
        in_specs=([pl.BlockSpec((1, rows, d), lambda i: (cur(i) // per_batch, cur(i) % per_batch, 0)),
                   pl.BlockSpec((1,) + mod.shape[1:], lambda i: (cur(i) // per_batch, 0, 0))]
                  + [_const_spec(a.shape) for a in consts] + [tab(tabq), tab(tabk)]),
        out_specs=[tok_out(GMLP_WIDTH), head, head,
                   pl.BlockSpec((1, 1, MLA_WIDTH, rows), lambda i: (pb(i), pj(i), 0, 0)),
                   tok_out(KV_LORA),
                   pl.BlockSpec((1, QK_ROPE, rows), lambda i: (pb(i), 0, pj(i)))],
        out_shape=[jax.ShapeDtypeStruct((b, t, GMLP_WIDTH), BF16),
                   jax.ShapeDtypeStruct((b, N_HEADS, t, HEAD_BLOCK), BF16),
                   jax.ShapeDtypeStruct((b, N_HEADS, t, HEAD_BLOCK), BF16),
                   jax.ShapeDtypeStruct((b, per_batch, MLA_WIDTH, rows), BF16),
                   jax.ShapeDtypeStruct((b, t, KV_LORA), F32),
                   jax.ShapeDtypeStruct((b, QK_ROPE, t), F32)],
        scratch_shapes=[pltpu.VMEM((2, rows, IN_WIDTH_PADDED), F32)],
        compiler_params=pltpu.CompilerParams(
            dimension_semantics=("arbitrary",), vmem_limit_bytes=VMEM_LIMIT_BYTES),
        name="pre_pipelined",
    )(x, mod, *consts, tabq, tabk)


def _aligned(index, multiple):
    return index if isinstance(index, int) else pl.multiple_of(index, multiple)


def _attn_kernel(q_ref, k_ref, vt_ref, o_ref, acc_ref, *maybe_m_ref, tq, online):
    t = q_ref.shape[2]
    unrolled = not online
    hb = tq // 2
    mb = hb // 2
    key_chunk = lax.broadcasted_iota(jnp.int32, (mb, mb), 0) // CHUNK
    query_chunk = lax.broadcasted_iota(jnp.int32, (mb, mb), 1) // CHUNK
    visible = key_chunk <= query_chunk
    ones = jnp.ones((DEN_ROWS, hb), BF16)

    def operands(hh, q0, qo, rows, kt, c0, keys):
        q = q_ref[0, hh, pl.ds(_aligned(q0 + qo, mb), rows), :]
        k = k_ref[0, hh, pl.ds(_aligned(kt * hb + c0, mb), keys), :]
        lhs = jnp.concatenate([vt_ref[0, kt, hh * V_HEAD:(hh + 1) * V_HEAD, c0:c0 + keys], ones[:, :keys]], axis=0)
        return lhs, lax.dot_general(k, q, _NT, preferred_element_type=F32)

    def unshifted(hh, q0, qo, rows, kt, c0, keys, masked=False):
        lhs, s = operands(hh, q0, qo, rows, kt, c0, keys)
        p = jnp.exp2(s)
        if masked:
            p = jnp.where(visible, p, 0.0)
        return jnp.dot(lhs, p.astype(BF16), preferred_element_type=F32)

    def online_update(hh, q0, qo, rows, kt, c0, keys, masked=False):
        m_ref = maybe_m_ref[0]
        lhs, s = operands(hh, q0, qo, rows, kt, c0, keys)
        if masked:
            s = jnp.where(visible, s, _NEG)
        cols = slice(qo, qo + rows)
        m_old = m_ref[hh, :, cols]
        m_new = jnp.maximum(m_old, jnp.max(s, axis=0, keepdims=True))
        p = jnp.exp2(s - m_new).astype(BF16)
        acc_ref[hh, :, cols] = (jnp.exp2(m_old - m_new) * acc_ref[hh, :, cols]
                                + jnp.dot(lhs, p, preferred_element_type=F32))
        m_ref[hh, :, cols] = m_new

    diagonal = [(0, mb, 0, 0, mb, True), (mb, mb, 0, 0, mb, False), (mb, mb, 0, mb, mb, True),
                (hb, hb, 0, 0, hb, False),
                (hb, mb, 1, 0, mb, True), (hb + mb, mb, 1, 0, mb, False), (hb + mb, mb, 1, mb, mb, True)]

    def q_tile(qi, carry):
        q0 = qi * tq
        acc_ref[...] = jnp.zeros(acc_ref.shape, F32)
        if online:
            maybe_m_ref[0][...] = jnp.full(maybe_m_ref[0].shape, -jnp.inf, F32)

        def body(j, c):
            tiles = [(hh, 2 * j + half) for half in range(2) for hh in range(2)]
            if online:
                for hh, kt in tiles:
                    online_update(hh, q0, 0, tq, kt, 0, hb)
            else:
                staged = [operands(hh, q0, 0, tq, kt, 0, hb) for hh, kt in tiles]
                probs = [jnp.exp2(s).astype(BF16) for _, s in staged]
                for hh in range(2):
                    acc_ref[hh] += sum(jnp.dot(lhs, p, preferred_element_type=F32)
                                       for (th, _), (lhs, _), p in zip(tiles, staged, probs) if th == hh)
            return c

        if unrolled:
            for j in range(qi):
                body(j, 0)
        else:
            lax.fori_loop(0, qi, body, 0)

        totals = []
        for hh in range(2):
            if online:
                for qo, rows, kt_off, c0, keys, masked in diagonal:
                    online_update(hh, q0, qo, rows, qi * 2 + kt_off, c0, keys, masked)
                totals.append([acc_ref[hh, :, g * mb:(g + 1) * mb] for g in range(tq // mb)])
            else:
                staged = [operands(hh, q0, qo, rows, qi * 2 + kt_off, c0, keys)
                          for qo, rows, kt_off, c0, keys, _ in diagonal]
                probs = [jnp.exp2(s) for _, s in staged]
                probs = [jnp.where(visible, p, 0.0) if blk[5] else p for p, blk in zip(probs, diagonal)]
                parts = [jnp.dot(lhs, p.astype(BF16), preferred_element_type=F32)
                         for (lhs, _), p in zip(staged, probs)]
                groups = [acc_ref[hh, :, g * mb:(g + 1) * mb] for g in range(tq // mb)]
                for (qo, rows, *_), part in zip(diagonal, parts):
                    for g in range(rows // mb):
                        groups[qo // mb + g] = groups[qo // mb + g] + part[:, g * mb:(g + 1) * mb]
                totals.append(groups)
        for g in range(tq // mb):
            o_t = [tot[g][:V_HEAD] * pl.reciprocal(tot[g][V_HEAD:V_HEAD + 1], approx=False) for tot in totals]
            o_ref[0, pl.ds(_aligned(q0 + g * mb, mb), mb), :] = jnp.concatenate(o_t, axis=0).T.astype(o_ref.dtype)
        return carry

    if unrolled:
        for qi in range(t // tq):
            q_tile(qi, 0)
    else:
        lax.fori_loop(0, t // tq, q_tile, 0)


def _attn_prompt(q, k, vt, score_bound):
    b, nh, t, _ = q.shape
    n_kt, _, hb = vt.shape[1:]
    assert hb == ATTN_TK and n_kt * hb == t
    qk = pl.BlockSpec((1, 2, t, HEAD_BLOCK), lambda i, j: (i, j, 0, 0))

    def call(online):
        scratch = [pltpu.VMEM((2, ACC_ROWS, ATTN_TQ), F32)]
        if online:
            scratch.append(pltpu.VMEM((2, 1, ATTN_TQ), F32))
        return pl.pallas_call(
            functools.partial(_attn_kernel, tq=ATTN_TQ, online=online),
            grid=(b, nh // 2),
            in_specs=[qk, qk, pl.BlockSpec((1, n_kt, 2 * V_HEAD, hb), lambda i, j: (i, 0, j, 0))],
            out_specs=pl.BlockSpec((1, t, 2 * V_HEAD), lambda i, j: (i, 0, j)),
            out_shape=jax.ShapeDtypeStruct((b, t, MLA_WIDTH), BF16),
            scratch_shapes=scratch,
            compiler_params=pltpu.CompilerParams(
                dimension_semantics=("parallel", "parallel"), vmem_limit_bytes=VMEM_LIMIT_BYTES),
            name="attn_prompt_online" if online else "attn_prompt_bounded")

    return lax.cond(score_bound <= MAX_UNSHIFTED_SCORE, call(False), call(True), q, k, vt)


def _pair_out(acc0, acc1):
    first = lax.broadcasted_iota(jnp.int32, (1, LANES), 1) < V_HEAD
    num = jnp.where(first, pltpu.roll(acc0, V_HEAD, axis=1), acc1)
    den = jnp.where(first, acc0, pltpu.roll(acc1, V_HEAD, axis=1))
    return num * pl.reciprocal(den, approx=False)


def _attn_sample_kernel(q_ref, past_ckv_ref, past_kr_ref, new_ckv_ref, new_kr_ref, wukv_ref, o_ref):
    lane = lax.broadcasted_iota(jnp.int32, (1, LANES), 1)
    is_nope = lane < QK_NOPE
    spread = (lax.broadcasted_iota(jnp.int32, (QK_ROPE, LANES), 1) % QK_ROPE
              == lax.broadcasted_iota(jnp.int32, (QK_ROPE, LANES), 0)).astype(BF16)
    one_b = jnp.ones((1, LANES), BF16)

    def segment(ckv_ref, kr_ref):
        kv = jnp.dot(ckv_ref[0].astype(BF16), wukv_ref[...], preferred_element_type=F32)
        krope_b = jnp.dot(kr_ref[0].astype(BF16), spread, preferred_element_type=F32).astype(BF16)
        return kv, krope_b

    def head_kv(seg, hd):
        kv, krope_b = seg
        kvh = kv[:, hd * HEAD_BLOCK:(hd + 1) * HEAD_BLOCK]
        ks = jnp.sum(jnp.where(is_nope, kvh * kvh, 0.0), axis=-1, keepdims=True)
        kn = kvh * lax.rsqrt(ks + QK_NOPE * EPS)
        return (jnp.where(is_nope, kn.astype(BF16), krope_b), jnp.where(is_nope, one_b, kvh.astype(BF16)))

    past, new = segment(past_ckv_ref, past_kr_ref), segment(new_ckv_ref, new_kr_ref)
    heads = range(N_HEADS)
    kv = [(head_kv(past, hd), head_kv(new, hd)) for hd in heads]
    scores = [[lax.dot_general(q_ref[0, hd], k, _NT, preferred_element_type=F32) for k, _ in kv[hd]] for hd in heads]
    peaks = [jnp.maximum(*(jnp.max(s, axis=-1, keepdims=True) for s in scores[hd])) for hd in heads]
    probs = [[jnp.exp2(s - peaks[hd]).astype(BF16) for s in scores[hd]] for hd in heads]
    accs = [sum(jnp.dot(p, v, preferred_element_type=F32) for p, (_, v) in zip(probs[hd], kv[hd])) for hd in heads]
    for hd in range(0, N_HEADS, 2):
        o_ref[0, :, hd * V_HEAD:(hd + 2) * V_HEAD] = _pair_out(accs[hd], accs[hd + 1]).astype(o_ref.dtype)


def _attn_sample(q, past_ckv, past_kr, new_ckv, new_kr, w_ukv):
    b, nh, t, _ = q.shape
    per_batch = lambda a: pl.BlockSpec((1,) + a.shape[1:], lambda i: (i,) + (0,) * (a.ndim - 1))
    return pl.pallas_call(
        _attn_sample_kernel,
        grid=(b,),
        in_specs=[per_batch(q), per_batch(past_ckv), per_batch(past_kr), per_batch(new_ckv), per_batch(new_kr),
                  _const_spec(w_ukv.shape)],
        out_specs=pl.BlockSpec((1, t, MLA_WIDTH), lambda i: (i, 0, 0)),
        out_shape=jax.ShapeDtypeStruct((b, t, MLA_WIDTH), BF16),
        compiler_params=pltpu.CompilerParams(
            dimension_semantics=("parallel",), vmem_limit_bytes=VMEM_LIMIT_BYTES),
        name="attn_sample",
    )(q, past_ckv, past_kr, new_ckv, new_kr, w_ukv)


def _post_mixer(x_ref, ya_ref, yb_ref, mod_ref, wout_ref, n2g_ref):
    nb, rows, d = x_ref.shape
    flat = lambda a: a.reshape(nb * rows, a.shape[-1])
    mod = mod_ref[...]
    g1, sh2, sc2 = mod[:, 2:3], mod[:, 3:4], mod[:, 4:5]
    mixer = jnp.concatenate([flat(ya_ref[...]), flat(yb_ref[...])], axis=-1)
    mix = jnp.dot(mixer, wout_ref[...], preferred_element_type=F32)
    x1 = x_ref[...] + g1 * mix.reshape(nb, rows, d)
    h2 = flat(x1 * _rms(x1, d) * (n2g_ref[...] * (1.0 + sc2)) + sh2).astype(BF16)
    return x1, h2


def _post_ffn_chunk(h2, c, wfi_ref, wfo_ref, acc_ref):
    d_ff = wfo_ref.shape[0]
    gate = jnp.dot(h2, wfi_ref[:, c * FFN_CHUNK:(c + 1) * FFN_CHUNK], preferred_element_type=F32)
    up = jnp.dot(h2, wfi_ref[:, d_ff + c * FFN_CHUNK:d_ff + (c + 1) * FFN_CHUNK], preferred_element_type=F32)
    act = (gate * jax.nn.sigmoid(gate) * up).astype(BF16)
    part = jnp.dot(act, wfo_ref[c * FFN_CHUNK:(c + 1) * FFN_CHUNK, :], preferred_element_type=F32)
    if c == 0:
        acc_ref[...] = part
    else:
        acc_ref[...] += part


def _post_kernel(x_ref, ya_ref, yb_ref, mod_ref, wout_ref, n2g_ref, wfi_ref, wfo_ref, o_ref, acc_ref):
    x1, h2 = _post_mixer(x_ref, ya_ref, yb_ref, mod_ref, wout_ref, n2g_ref)
    o_ref[...] = x1
    for c in range(wfo_ref.shape[0] // FFN_CHUNK):
        _post_ffn_chunk(h2, c, wfi_ref, wfo_ref, acc_ref)
    o_ref[...] = o_ref[...] + mod_ref[:, 5:6] * acc_ref[...].reshape(o_ref.shape)


def _post(x, ya, yb, mod, wts, *, nb, rows):
    b, t, d = x.shape
    tok = lambda w: pl.BlockSpec((nb, rows, w), lambda i, j: (i, j, 0))
    consts = [wts[n] for n in ("w_out", "n2g", "w_ffn_in", "w_ffn_out")]
    return pl.pallas_call(
        _post_kernel,
        grid=(b // nb, t // rows),
        in_specs=[tok(d), tok(GMLP_WIDTH), tok(MLA_WIDTH),
                  pl.BlockSpec((nb,) + mod.shape[1:], lambda i, j: (i, 0, 0)),
                  _const_spec(consts[0].shape), _const_spec(consts[1].shape),
                  _const_spec(consts[2].shape), _const_spec(consts[3].shape)],
        out_specs=tok(d),
        out_shape=jax.ShapeDtypeStruct((b, t, d), F32),
        scratch_shapes=[pltpu.VMEM((nb * rows, d), F32)],
        compiler_params=pltpu.CompilerParams(
            dimension_semantics=("parallel", "parallel"), vmem_limit_bytes=VMEM_LIMIT_BYTES),
        name="post",
    )(x, ya, yb, mod, consts[0], consts[1], consts[2], consts[3])


def _rope_tables(first_pos, n, *, mirrored_odd, repeat=1):
    half = QK_ROPE // 2
    inv = ROPE_THETA ** (-np.arange(half, dtype=np.float64) / half)
    ang = (first_pos + np.arange(n, dtype=np.float64))[:, None] * inv[None, :]
    cos, sin = np.cos(ang), np.sin(ang)
    rot = np.concatenate([cos, cos, -sin, sin], axis=-1)
    nope = np.ones((n, QK_NOPE))
    tabq = (SCALE * LOG2E) * np.concatenate([nope, rot] + ([rot, nope] if mirrored_odd else [nope, rot]), axis=-1)
    tabk = np.concatenate([rot, rot], axis=-1)
    return jnp.asarray(np.tile(tabq, (repeat, 1)), F32), jnp.asarray(np.tile(tabk, (repeat, 1)), F32)


def _prep_weights(w_in, w_s, b_s, q_norm_g, w_uq, kv_norm_g, w_ukv, qn_g, qr_g, kn_g, kr_g,
                  norm1_g, w_out, norm2_g, w_ffn_in, w_ffn_out):
    o_kr = 2 * GMLP_WIDTH + Q_LORA + KV_LORA
    w_kr = w_in[:, o_kr:o_kr + QK_ROPE]
    w_kr_sw = _swap_halves(w_kr)
    w_in_p = jnp.concatenate([w_in[:, :o_kr], w_kr, w_kr_sw, w_kr, w_kr_sw], axis=-1)
    d_q = w_uq.shape[0]
    wq = w_uq.reshape(d_q, N_HEADS, QK_HEAD)
    wq_n, wq_r = wq[:, :, :QK_NOPE], wq[:, :, QK_NOPE:]
    wq_plain = jnp.concatenate([wq_n, wq_r, _swap_halves(wq_r)], axis=-1)
    wq_mirror = jnp.concatenate([wq_r, _swap_halves(wq_r), wq_n], axis=-1)
    odd = (jnp.arange(N_HEADS) % 2 == 1)[None, :, None]
    w_uq_flat = wq_plain.reshape(d_q, N_HEADS * HEAD_BLOCK)
    w_uq_pair = jnp.where(odd, wq_mirror, wq_plain).reshape(d_q, N_HEADS * HEAD_BLOCK)
    wkv = w_ukv.reshape(KV_LORA, N_HEADS, QK_NOPE + V_HEAD)
    w_uk = wkv[:, :, :QK_NOPE].reshape(KV_LORA, N_HEADS * QK_NOPE)
    w_uvt = wkv[:, :, QK_NOPE:].reshape(KV_LORA, MLA_WIDTH).T
    root = math.sqrt(QK_NOPE)
    q_nope_row = qn_g * kn_g * (root * root)
    q_rope_row = jnp.concatenate([qr_g, _swap_halves(qr_g)]) * root
    qrow_plain = jnp.concatenate([q_nope_row, q_rope_row])
    qrow_mirror = jnp.concatenate([q_rope_row, q_nope_row])
    row = lambda a: a.reshape(1, -1).astype(F32)
    return {
        "n1g": row(norm1_g), "w_in": w_in_p.astype(BF16), "w_s": w_s.astype(F32), "b_s": b_s.T.astype(F32),
        "qng": row(q_norm_g), "kvg": row(kv_norm_g),
        "w_uq": w_uq_pair.astype(BF16), "w_uq_flat": w_uq_flat.astype(BF16),
        "qrow": row(jnp.concatenate([qrow_plain, qrow_mirror])),
        "qrow_flat": row(jnp.concatenate([qrow_plain, qrow_plain])),
        "w_ukv": w_ukv.astype(BF16), "w_uk": w_uk.astype(BF16), "w_uvt": w_uvt.astype(BF16),
        "krg": row(jnp.concatenate([kr_g, _swap_halves(kr_g), kr_g, _swap_halves(kr_g)])),
        "w_out": w_out.astype(BF16), "n2g": row(norm2_g),
        "w_ffn_in": w_ffn_in.astype(BF16), "w_ffn_out": w_ffn_out.astype(BF16),
    }


def kernel(x_prompt, x_sample, cache_ckv, cache_krope, c_prompt, c_sample, w_ada, b_ada, norm1_g, w_in, w_s, b_s, q_norm_g, w_uq, kv_norm_g, w_ukv, qn_g, qr_g, kn_g, kr_g, w_out, norm2_g, w_ffn_in, w_ffn_out):
    depth = w_in.shape[0]
    assert depth == 1, "single trunk layer"
    assert PRE_ROWS == ATTN_TK
    bp, tp, d = x_prompt.shape
    bs, ts, _ = x_sample.shape
    past_len = cache_ckv.shape[2]

    wts = _prep_weights(w_in[0], w_s[0], b_s[0], q_norm_g[0], w_uq[0], kv_norm_g[0], w_ukv[0], qn_g[0], qr_g[0],
                        kn_g[0], kr_g[0], norm1_g[0], w_out[0], norm2_g[0], w_ffn_in[0], w_ffn_out[0])
    mod = _ada(jnp.concatenate([c_prompt, c_sample], axis=0), w_ada[0], b_ada[0]).reshape(bp + bs, 6, d)
    mod_p, mod_s = mod[:bp], mod[bp:]

    tabq_p, tabk_p = _rope_tables(0, tp, mirrored_odd=True)
    ya_p, q_p, k_p, vt_p, ckv_p, krt_p = _pre_pipelined(x_prompt, mod_p, wts, tabq_p, tabk_p,
                                                        rows=PRE_ROWS, mix_len=GMLP_CHUNK)
    gmax = lambda g: jnp.max(jnp.abs(g))
    score_bound = SCALE * (QK_NOPE * gmax(qn_g) * gmax(kn_g) + QK_ROPE * gmax(qr_g) * gmax(kr_g))
    yb_p = _attn_prompt(q_p, k_p, vt_p, score_bound)
    y_p = _post(x_prompt, ya_p, yb_p, mod_p, wts, nb=1, rows=POST_ROWS)

    tabq_s, tabk_s = _rope_tables(past_len, ts, mirrored_odd=False, repeat=bs)
    ya_s, q_s, ckv_s, kr_s, v_s = _pre_single(x_sample, mod_s, wts, tabq_s, tabk_s, mix_len=min(ts, GMLP_CHUNK))
    yb_s = _attn_sample(q_s, cache_ckv[0], cache_krope[0], ckv_s, kr_s, wts["w_ukv"])
    y_s = _post(x_sample, ya_s, yb_s, mod_s, wts, nb=bs, rows=ts)

    return (y_p, y_s, ckv_p[None], jnp.swapaxes(krt_p, 1, 2)[None], ckv_s[None], kr_s[None], v_s[None])
```

```python
import functools
import math

import jax
import jax.numpy as jnp
import numpy as np
from jax import lax
from jax.experimental import pallas as pl
from jax.experimental.pallas import tpu as pltpu

F32 = jnp.float32
BF16 = jnp.bfloat16

CHUNK = 64
GMLP_CHUNK = 128
GMLP_GROUPS = 4
GMLP_HEAD = 128
GMLP_WIDTH = GMLP_GROUPS * GMLP_HEAD
N_HEADS = 8
QK_NOPE = 64
QK_ROPE = 32
QK_HEAD = QK_NOPE + QK_ROPE
V_HEAD = 64
Q_LORA = 384
KV_LORA = 256
MLA_WIDTH = N_HEADS * V_HEAD
ROPE_THETA = 10000.0
EPS = 1e-6
SCALE = QK_HEAD ** -0.5

LANES = 128
HEAD_BLOCK = LANES
KR_BLOCK = LANES
IN_WIDTH_PADDED = 2 * GMLP_WIDTH + Q_LORA + KV_LORA + KR_BLOCK
VMEM_LIMIT_BYTES = 56 * 1024 * 1024

PRE_ROWS = 512
POST_ROWS = 1024
FFN_CHUNK = 256
ATTN_TQ = 1024
ATTN_TK = ATTN_TQ // 2
DEN_ROWS = 16
ACC_ROWS = V_HEAD + DEN_ROWS

_NEG = float(jnp.finfo(jnp.float32).min)
_NT = (((1,), (1,)), ((), ()))
LOG2E = math.log2(math.e)
MAX_UNSHIFTED_SCORE = 60.0


def _swap_halves(a, axis=-1):
    n = a.shape[axis] // 2
    lo = lax.slice_in_dim(a, 0, n, axis=axis)
    hi = lax.slice_in_dim(a, n, 2 * n, axis=axis)
    return jnp.concatenate([hi, lo], axis=axis)


def _const_spec(shape):
    nd = len(shape)
    return pl.BlockSpec(shape, lambda *_: (0,) * nd, pipeline_mode=pl.Buffered(1))


def _ada_kernel(c_ref, w_ref, b_ref, o_ref):
    c = c_ref[...]
    a = c * jax.nn.sigmoid(c)
    w = w_ref[...]
    a_hi, w_hi = a.astype(BF16), w.astype(BF16)
    a_lo, w_lo = (a - a_hi.astype(F32)).astype(BF16), (w - w_hi.astype(F32)).astype(BF16)
    n = a.shape[0]
    both = jnp.dot(jnp.concatenate([a_hi, a_lo], axis=0), w_hi, preferred_element_type=F32)
    o_ref[...] = both[:n] + both[n:] + jnp.dot(a_hi, w_lo, preferred_element_type=F32) + b_ref[...]


def _ada(c_all, w_ada, b_ada):
    n, d = c_all.shape
    width = w_ada.shape[1]
    bn = d
    return pl.pallas_call(
        _ada_kernel,
        grid=(width // bn,),
        in_specs=[pl.BlockSpec((n, d), lambda j: (0, 0)),
                  pl.BlockSpec((d, bn), lambda j: (0, j)),
                  pl.BlockSpec((1, bn), lambda j: (0, j))],
        out_specs=pl.BlockSpec((n, bn), lambda j: (0, j)),
        out_shape=jax.ShapeDtypeStruct((n, width), F32),
        compiler_params=pltpu.CompilerParams(vmem_limit_bytes=VMEM_LIMIT_BYTES),
        name="ada",
    )(c_all, w_ada, b_ada.reshape(1, width))


def _rms(x, width):
    return lax.rsqrt(jnp.sum(x * x, axis=-1, keepdims=True) * (1.0 / width) + EPS)


def _pre_norm(x_ref, mod_ref, n1g_ref):
    nb, rows, d = x_ref.shape
    x = x_ref[...]
    mod = mod_ref[...]
    sh1, sc1 = mod[:, 0:1], mod[:, 1:2]
    h = x * _rms(x, d) * (n1g_ref[...] * (1.0 + sc1)) + sh1
    return h.reshape(nb * rows, d).astype(BF16)


def _gelu_tanh(x):
    k = -2.0 * math.sqrt(2.0 / math.pi) * LOG2E
    e = jnp.exp2(x * (x * x * (k * 0.044715) + k))
    return x * pl.reciprocal(1.0 + e, approx=False)


def _half_sums(a, lo):
    sq = a * a
    ss_all = jnp.sum(sq, axis=-1, keepdims=True)
    ss_lo = jnp.sum(jnp.where(lo, sq, 0.0), axis=-1, keepdims=True)
    return jnp.where(lo, ss_lo, ss_all - ss_lo)


def _pre_heads(z_ref, ws_ref, bs_ref, qng_ref, wuq_ref, kvg_ref, qrow_ref, krg_ref, tabq_ref, tabk_ref,
               ya_ref, q_ref, ckv_ref, kr_ref, maybe_kv, maybe_vraw_ref, mix_len, tick=lambda: None):
    nb, tile_rows = ya_ref.shape[0], ya_ref.shape[1]
    rows = z_ref.shape[0]
    per_head = lambda a: a.reshape(nb, tile_rows, HEAD_BLOCK)
    o_v, o_q, o_kv, o_kr = GMLP_WIDTH, 2 * GMLP_WIDTH, 2 * GMLP_WIDTH + Q_LORA, 2 * GMLP_WIDTH + Q_LORA + KV_LORA
    lane = lax.broadcasted_iota(jnp.int32, (1, LANES), 1)
    lo = lane < LANES // 2

    kr = z_ref[:, o_kr:o_kr + KR_BLOCK]
    kr_ss = jnp.sum(jnp.where(lane < QK_ROPE, kr * kr, 0.0), axis=-1, keepdims=True)
    kr_p = kr * lax.rsqrt(kr_ss * (1.0 / QK_ROPE) + EPS) * krg_ref[...] * tabk_ref[...]
    krope = kr_p + pltpu.roll(kr_p, LANES - QK_ROPE, axis=1)
    if kr_ref.shape[-1] == QK_ROPE:
        kr_ref[...] = krope[:, :QK_ROPE].reshape(kr_ref.shape)
    else:
        kr_ref[0] = krope.T[:QK_ROPE, :]

    ckv_raw = z_ref[:, o_kv:o_kr]
    ckv = ckv_raw * _rms(ckv_raw, KV_LORA) * kvg_ref[...]
    ckv_ref[...] = ckv.reshape(ckv_ref.shape)
    cq_raw = z_ref[:, o_q:o_kv]
    cq = cq_raw * _rms(cq_raw, Q_LORA) * qng_ref[...]
    tick()
    q = jnp.dot(cq.astype(BF16), wuq_ref[...], preferred_element_type=F32)
    blocks = [q[:, hd * HEAD_BLOCK:(hd + 1) * HEAD_BLOCK] for hd in range(N_HEADS)]
    if maybe_kv is not None:
        wuk_ref, wuvt_ref, k_ref, vt_ref = maybe_kv
        ckv_b = ckv.astype(BF16)
        kp_all = jnp.dot(ckv_b, wuk_ref[...], preferred_element_type=F32)
        vt_ref[0, 0] = lax.dot_general(wuvt_ref[...], ckv_b, _NT, preferred_element_type=F32).astype(vt_ref.dtype)
        blocks += [kp_all[:, pair * LANES:(pair + 1) * LANES] for pair in range(N_HEADS // 2)]

    tick()
    u = _gelu_tanh(z_ref[:, :o_v])
    tick()
    v = _gelu_tanh(z_ref[:, o_v:o_q])
    if maybe_vraw_ref is not None:
        maybe_vraw_ref[...] = v.reshape(maybe_vraw_ref.shape)
    tick()

    vb = v.astype(BF16)
    pi = lax.broadcasted_iota(jnp.int32, (mix_len, mix_len), 0) // CHUNK
    pj = lax.broadcasted_iota(jnp.int32, (mix_len, mix_len), 1) // CHUNK
    for g in range(GMLP_GROUPS):
        wsg = jnp.where(pj <= pi, ws_ref[g, :mix_len, :mix_len], 0.0).astype(BF16)
        bg = bs_ref[:mix_len, g:g + 1]
        cols = slice(g * GMLP_HEAD, (g + 1) * GMLP_HEAD)
        for c in range(rows // mix_len):
            rws = slice(c * mix_len, (c + 1) * mix_len)
            mixed = jnp.dot(wsg, vb[rws, cols], preferred_element_type=F32) + bg
            bi, off = divmod(c * mix_len, tile_rows)
            ya_ref[bi, off:off + mix_len, cols] = (u[rws, cols] * mixed).astype(ya_ref.dtype)
        tick()

    qfac = tabq_ref[...] * qrow_ref[...]
    scales = []
    for blk in blocks:
        scales.append(lax.rsqrt(_half_sums(blk, lo) + QK_NOPE * EPS))
    for hd in range(N_HEADS):
        fac = qfac[:, (hd % 2) * LANES:(hd % 2 + 1) * LANES]
        q_ref[:, hd] = per_head((blocks[hd] * scales[hd] * fac).astype(q_ref.dtype))
    if maybe_kv is not None:
        krope_b = krope.astype(k_ref.dtype)
        for pair in range(N_HEADS // 2):
            kn = (blocks[N_HEADS + pair] * scales[N_HEADS + pair]).astype(k_ref.dtype)
            k_ref[:, 2 * pair] = per_head(jnp.where(lo, kn, krope_b))
            k_ref[:, 2 * pair + 1] = per_head(jnp.where(lo, krope_b, kn))


_PRE_SINGLE_CONSTS = ("n1g", "w_in", "w_s", "b_s", "qng", "w_uq_flat", "kvg", "qrow_flat", "krg")
_PRE_PIPELINED_CONSTS = ("n1g", "w_in", "w_s", "b_s", "qng", "w_uq", "kvg", "qrow", "krg", "w_uk", "w_uvt")


def _pre_single_kernel(x_ref, mod_ref, n1g_ref, win_ref, ws_ref, bs_ref, qng_ref, wuq_ref, kvg_ref, qrow_ref, krg_ref,
                       tabq_ref, tabk_ref, ya_ref, q_ref, ckv_ref, kr_ref, vraw_ref, z_ref, *, mix_len):
    z_ref[...] = jnp.dot(_pre_norm(x_ref, mod_ref, n1g_ref), win_ref[...], preferred_element_type=F32)
    _pre_heads(z_ref, ws_ref, bs_ref, qng_ref, wuq_ref, kvg_ref, qrow_ref, krg_ref, tabq_ref, tabk_ref,
               ya_ref, q_ref, ckv_ref, kr_ref, None, vraw_ref, mix_len)


def _pre_pipelined_kernel(x_ref, mod_ref, n1g_ref, win_ref, ws_ref, bs_ref, qng_ref, wuq_ref, kvg_ref, qrow_ref,
                          krg_ref, wuk_ref, wuvt_ref, tabq_ref, tabk_ref,
                          ya_ref, q_ref, k_ref, vt_ref, ckv_ref, kr_ref, z_ref, *, mix_len):
    i = pl.program_id(0)

    @pl.when(i == 0)
    def _():
        z_ref[...] = jnp.zeros(z_ref.shape, F32)

    for parity in range(2):
        @pl.when(i % 2 == parity)
        def _():
            chunk = IN_WIDTH_PADDED // 7
            normed = []

            def project(c):
                if c is None:
                    normed.append(_pre_norm(x_ref, mod_ref, n1g_ref))
                    return
                cols = slice(c * chunk, (c + 1) * chunk)
                z_ref[parity, :, cols] = jnp.dot(normed[0], win_ref[:, cols], preferred_element_type=F32)

            pending = iter([[None, 0, 1], [2], [3], [4], [5], [6]])

            def tick():
                for c in next(pending, []):
                    project(c)

            _pre_heads(z_ref.at[1 - parity], ws_ref, bs_ref, qng_ref, wuq_ref, kvg_ref, qrow_ref, krg_ref,
                       tabq_ref, tabk_ref, ya_ref, q_ref, ckv_ref, kr_ref,
                       (wuk_ref, wuvt_ref, k_ref, vt_ref), None, mix_len, tick)
            for group in pending:
                for c in group:
                    project(c)


def _pre_single(x, mod, wts, tabq, tabk, *, mix_len):
    b, t, d = x.shape
    whole = lambda *shape: pl.BlockSpec(shape, lambda i: (0,) * len(shape))
    consts = [wts[n] for n in _PRE_SINGLE_CONSTS]
    return pl.pallas_call(
        functools.partial(_pre_single_kernel, mix_len=mix_len),
        grid=(1,),
        in_specs=([whole(b, t, d), whole(*mod.shape)] + [_const_spec(a.shape) for a in consts]
                  + [whole(*tabq.shape), whole(*tabk.shape)]),
        out_specs=[whole(b, t, GMLP_WIDTH), whole(b, N_HEADS, t, HEAD_BLOCK), whole(b, t, KV_LORA),
                   whole(b, t, QK_ROPE), whole(b, t, GMLP_WIDTH)],
        out_shape=[jax.ShapeDtypeStruct((b, t, GMLP_WIDTH), BF16),
                   jax.ShapeDtypeStruct((b, N_HEADS, t, HEAD_BLOCK), BF16),
                   jax.ShapeDtypeStruct((b, t, KV_LORA), F32),
                   jax.ShapeDtypeStruct((b, t, QK_ROPE), F32),
                   jax.ShapeDtypeStruct((b, t, GMLP_WIDTH), F32)],
        scratch_shapes=[pltpu.VMEM((b * t, IN_WIDTH_PADDED), F32)],
        compiler_params=pltpu.CompilerParams(
            dimension_semantics=("arbitrary",), vmem_limit_bytes=VMEM_LIMIT_BYTES),
        name="pre_single",
    )(x, mod, *consts, tabq, tabk)


def _pre_pipelined(x, mod, wts, tabq, tabk, *, rows, mix_len):
    b, t, d = x.shape
    per_batch = t // rows
    n_tiles = b * per_batch
    cur = lambda i: jnp.minimum(i, n_tiles - 1)
    prev = lambda i: jnp.maximum(i - 1, 0)
    pb, pj = (lambda i: prev(i) // per_batch), (lambda i: prev(i) % per_batch)
    tok_out = lambda w: pl.BlockSpec((1, rows, w), lambda i: (pb(i), pj(i), 0))
    head = pl.BlockSpec((1, N_HEADS, rows, HEAD_BLOCK), lambda i: (pb(i), 0, pj(i), 0))
    tab = lambda a: pl.BlockSpec((rows, a.shape[1]), lambda i: (pj(i), 0))
    consts = [wts[n] for n in _PRE_PIPELINED_CONSTS]
    return pl.pallas_call(
        functools.partial(_pre_pipelined_kernel, mix_len=mix_len),
        grid=(n_tiles + 1,),
        in_specs=([pl.BlockSpec((1, rows, d), lambda i: (cur(i) // per_batch, cur(i) % per_batch, 0)),
                   pl.BlockSpec((1,) + mod.shape[1:], lambda i: (cur(i) // per_batch, 0, 0))]
                  + [_const_spec(a.shape) for a in consts] + [tab(tabq), tab(tabk)]),
        out_specs=[tok_out(GMLP_WIDTH), head, head,
                   pl.BlockSpec((1, 1, MLA_WIDTH, rows), lambda i: (pb(i), pj(i), 0, 0)),
                   tok_out(KV_LORA),
                   pl.BlockSpec((1, QK_ROPE, rows), lambda i: (pb(i), 0, pj(i)))],
        out_shape=[jax.ShapeDtypeStruct((b, t, GMLP_WIDTH), BF16),
                   jax.ShapeDtypeStruct((b, N_HEADS, t, HEAD_BLOCK), BF16),
                   jax.ShapeDtypeStruct((b, N_HEADS, t, HEAD_BLOCK), BF16),
                   jax.ShapeDtypeStruct((b, per_batch, MLA_WIDTH, rows), BF16),
                   jax.ShapeDtypeStruct((b, t, KV_LORA), F32),
                   jax.ShapeDtypeStruct((b, QK_ROPE, t), F32)],
        scratch_shapes=[pltpu.VMEM((2, rows, IN_WIDTH_PADDED), F32)],
        compiler_params=pltpu.CompilerParams(
            dimension_semantics=("arbitrary",), vmem_limit_bytes=VMEM_LIMIT_BYTES),
        name="pre_pipelined",
    )(x, mod, *consts, tabq, tabk)


def _aligned(index, multiple):
    return index if isinstance(index, int) else pl.multiple_of(index, multiple)


def _attn_kernel(q_ref, k_ref, vt_ref, o_ref, acc_ref, *maybe_m_ref, tq, online):
    t = q_ref.shape[2]
    unrolled = not online
    hb = tq // 2
    mb = hb // 2
    key_chunk = lax.broadcasted_iota(jnp.int32, (mb, mb), 0) // CHUNK
    query_chunk = lax.broadcasted_iota(jnp.int32, (mb, mb), 1) // CHUNK
    visible = key_chunk <= query_chunk
    ones = jnp.ones((DEN_ROWS, hb), BF16)

    def operands(hh, q0, qo, rows, kt, c0, keys):
        q = q_ref[0, hh, pl.ds(_aligned(q0 + qo, mb), rows), :]
        k = k_ref[0, hh, pl.ds(_aligned(kt * hb + c0, mb), keys), :]
        lhs = jnp.concatenate([vt_ref[0, kt, hh * V_HEAD:(hh + 1) * V_HEAD, c0:c0 + keys], ones[:, :keys]], axis=0)
        return lhs, lax.dot_general(k, q, _NT, preferred_element_type=F32)

    def unshifted(hh, q0, qo, rows, kt, c0, keys, masked=False):
        lhs, s = operands(hh, q0, qo, rows, kt, c0, keys)
        p = jnp.exp2(s)
        if masked:
            p = jnp.where(visible, p, 0.0)
        return jnp.dot(lhs, p.astype(BF16), preferred_element_type=F32)

    def online_update(hh, q0, qo, rows, kt, c0, keys, masked=False):
        m_ref = maybe_m_ref[0]
        lhs, s = operands(hh, q0, qo, rows, kt, c0, keys)
        if masked:
            s = jnp.where(visible, s, _NEG)
        cols = slice(qo, qo + rows)
        m_old = m_ref[hh, :, cols]
        m_new = jnp.maximum(m_old, jnp.max(s, axis=0, keepdims=True))
        p = jnp.exp2(s - m_new).astype(BF16)
        acc_ref[hh, :, cols] = (jnp.exp2(m_old - m_new) * acc_ref[hh, :, cols]
                                + jnp.dot(lhs, p, preferred_element_type=F32))
        m_ref[hh, :, cols] = m_new

    diagonal = [(0, mb, 0, 0, mb, True), (mb, mb, 0, 0, mb, False), (mb, mb, 0, mb, mb, True),
                (hb, hb, 0, 0, hb, False),
                (hb, mb, 1, 0, mb, True), (hb + mb, mb, 1, 0, mb, False), (hb + mb, mb, 1, mb, mb, True)]

    def q_tile(qi, carry):
        q0 = qi * tq
        acc_ref[...] = jnp.zeros(acc_ref.shape, F32)
        if online:
            maybe_m_ref[0][...] = jnp.full(maybe_m_ref[0].shape, -jnp.inf, F32)

        def body(j, c):
            tiles = [(hh, 2 * j + half) for half in range(2) for hh in range(2)]
            if online:
                for hh, kt in tiles:
                    online_update(hh, q0, 0, tq, kt, 0, hb)
            else:
                staged = [operands(hh, q0, 0, tq, kt, 0, hb) for hh, kt in tiles]
                probs = [jnp.exp2(s).astype(BF16) for _, s in staged]
                for hh in range(2):
                    acc_ref[hh] += sum(jnp.dot(lhs, p, preferred_element_type=F32)
                                       for (th, _), (lhs, _), p in zip(tiles, staged, probs) if th == hh)
            return c

        if unrolled:
            for j in range(qi):
                body(j, 0)
        else:
            lax.fori_loop(0, qi, body, 0)

        totals = []
        for hh in range(2):
            if online:
                for qo, rows, kt_off, c0, keys, masked in diagonal:
                    online_update(hh, q0, qo, rows, qi * 2 + kt_off, c0, keys, masked)
                totals.append([acc_ref[hh, :, g * mb:(g + 1) * mb] for g in range(tq // mb)])
            else:
                staged = [operands(hh, q0, qo, rows, qi * 2 + kt_off, c0, keys)
                          for qo, rows, kt_off, c0, keys, _ in diagonal]
                probs = [jnp.exp2(s) for _, s in staged]
                probs = [jnp.where(visible, p, 0.0) if blk[5] else p for p, blk in zip(probs, diagonal)]
                parts = [jnp.dot(lhs, p.astype(BF16), preferred_element_type=F32)
                         for (lhs, _), p in zip(staged, probs)]
                groups = [acc_ref[hh, :, g * mb:(g + 1) * mb] for g in range(tq // mb)]
                for (qo, rows, *_), part in zip(diagonal, parts):
                    for g in range(rows // mb):
                        groups[qo // mb + g] = groups[qo // mb + g] + part[:, g * mb:(g + 1) * mb]
                totals.append(groups)
        for g in range(tq // mb):
            o_t = [tot[g][:V_HEAD] * pl.reciprocal(tot[g][V_HEAD:V_HEAD + 1], approx=False) for tot in totals]
            o_ref[0, pl.ds(_aligned(q0 + g * mb, mb), mb), :] = jnp.concatenate(o_t, axis=0).T.astype(o_ref.dtype)
        return carry

    if unrolled:
        for qi in range(t // tq):
            q_tile(qi, 0)
    else:
        lax.fori_loop(0, t // tq, q_tile, 0)


def _attn_prompt(q, k, vt, score_bound):
    b, nh, t, _ = q.shape
    n_kt, _, hb = vt.shape[1:]
    assert hb == ATTN_TK and n_kt * hb == t
    qk = pl.BlockSpec((1, 2, t, HEAD_BLOCK), lambda i, j: (i, j, 0, 0))

    def call(online):
        scratch = [pltpu.VMEM((2, ACC_ROWS, ATTN_TQ), F32)]
        if online:
            scratch.append(pltpu.VMEM((2, 1, ATTN_TQ), F32))
        return pl.pallas_call(
            functools.partial(_attn_kernel, tq=ATTN_TQ, online=online),
            grid=(b, nh // 2),
            in_specs=[qk, qk, pl.BlockSpec((1, n_kt, 2 * V_HEAD, hb), lambda i, j: (i, 0, j, 0))],
            out_specs=pl.BlockSpec((1, t, 2 * V_HEAD), lambda i, j: (i, 0, j)),
            out_shape=jax.ShapeDtypeStruct((b, t, MLA_WIDTH), BF16),
            scratch_shapes=scratch,
            compiler_params=pltpu.CompilerParams(
                dimension_semantics=("parallel", "parallel"), vmem_limit_bytes=VMEM_LIMIT_BYTES),
            name="attn_prompt_online" if online else "attn_prompt_bounded")

    return lax.cond(score_bound <= MAX_UNSHIFTED_SCORE, call(False), call(True), q, k, vt)


def _pair_out(acc0, acc1):
    first = lax.broadcasted_iota(jnp.int32, (1, LANES), 1) < V_HEAD
    num = jnp.where(first, pltpu.roll(acc0, V_HEAD, axis=1), acc1)
    den = jnp.where(first, acc0, pltpu.roll(acc1, V_HEAD, axis=1))
    return num * pl.reciprocal(den, approx=False)


def _attn_sample_kernel(q_ref, past_ckv_ref, past_kr_ref, new_ckv_ref, new_kr_ref, wukv_ref, o_ref):
    lane = lax.broadcasted_iota(jnp.int32, (1, LANES), 1)
    is_nope = lane < QK_NOPE
    spread = (lax.broadcasted_iota(jnp.int32, (QK_ROPE, LANES), 1) % QK_ROPE
              == lax.broadcasted_iota(jnp.int32, (QK_ROPE, LANES), 0)).astype(BF16)
    one_b = jnp.ones((1, LANES), BF16)

    def segment(ckv_ref, kr_ref):
        kv = jnp.dot(ckv_ref[0].astype(BF16), wukv_ref[...], preferred_element_type=F32)
        krope_b = jnp.dot(kr_ref[0].astype(BF16), spread, preferred_element_type=F32).astype(BF16)
        return kv, krope_b

    def head_kv(seg, hd):
        kv, krope_b = seg
        kvh = kv[:, hd * HEAD_BLOCK:(hd + 1) * HEAD_BLOCK]
        ks = jnp.sum(jnp.where(is_nope, kvh * kvh, 0.0), axis=-1, keepdims=True)
        kn = kvh * lax.rsqrt(ks + QK_NOPE * EPS)
        return (jnp.where(is_nope, kn.astype(BF16), krope_b), jnp.where(is_nope, one_b, kvh.astype(BF16)))

    past, new = segment(past_ckv_ref, past_kr_ref), segment(new_ckv_ref, new_kr_ref)
    heads = range(N_HEADS)
    kv = [(head_kv(past, hd), head_kv(new, hd)) for hd in heads]
    scores = [[lax.dot_general(q_ref[0, hd], k, _NT, preferred_element_type=F32) for k, _ in kv[hd]] for hd in heads]
    peaks = [jnp.maximum(*(jnp.max(s, axis=-1, keepdims=True) for s in scores[hd])) for hd in heads]
    probs = [[jnp.exp2(s - peaks[hd]).astype(BF16) for s in scores[hd]] for hd in heads]
    accs = [sum(jnp.dot(p, v, preferred_element_type=F32) for p, (_, v) in zip(probs[hd], kv[hd])) for hd in heads]
    for hd in range(0, N_HEADS, 2):
        o_ref[0, :, hd * V_HEAD:(hd + 2) * V_HEAD] = _pair_out(accs[hd], accs[hd + 1]).astype(o_ref.dtype)


def _attn_sample(q, past_ckv, past_kr, new_ckv, new_kr, w_ukv):
    b, nh, t, _ = q.shape
    per_batch = lambda a: pl.BlockSpec((1,) + a.shape[1:], lambda i: (i,) + (0,) * (a.ndim - 1))
    return pl.pallas_call(
        _attn_sample_kernel,
        grid=(b,),
        in_specs=[per_batch(q), per_batch(past_ckv), per_batch(past_kr), per_batch(new_ckv), per_batch(new_kr),
                  _const_spec(w_ukv.shape)],
        out_specs=pl.BlockSpec((1, t, MLA_WIDTH), lambda i: (i, 0, 0)),
        out_shape=jax.ShapeDtypeStruct((b, t, MLA_WIDTH), BF16),
        compiler_params=pltpu.CompilerParams(
            dimension_semantics=("parallel",), vmem_limit_bytes=VMEM_LIMIT_BYTES),
        name="attn_sample",
    )(q, past_ckv, past_kr, new_ckv, new_kr, w_ukv)


def _post_mixer(x_ref, ya_ref, yb_ref, mod_ref, wout_ref, n2g_ref):
    nb, rows, d = x_ref.shape
    flat = lambda a: a.reshape(nb * rows, a.shape[-1])
    mod = mod_ref[...]
    g1, sh2, sc2 = mod[:, 2:3], mod[:, 3:4], mod[:, 4:5]
    mixer = jnp.concatenate([flat(ya_ref[...]), flat(yb_ref[...])], axis=-1)
    mix = jnp.dot(mixer, wout_ref[...], preferred_element_type=F32)
    x1 = x_ref[...] + g1 * mix.reshape(nb, rows, d)
    h2 = flat(x1 * _rms(x1, d) * (n2g_ref[...] * (1.0 + sc2)) + sh2).astype(BF16)
    return x1, h2


def _post_ffn_chunk(h2, c, wfi_ref, wfo_ref, acc_ref):
    d_ff = wfo_ref.shape[0]
    gate = jnp.dot(h2, wfi_ref[:, c * FFN_CHUNK:(c + 1) * FFN_CHUNK], preferred_element_type=F32)
    up = jnp.dot(h2, wfi_ref[:, d_ff + c * FFN_CHUNK:d_ff + (c + 1) * FFN_CHUNK], preferred_element_type=F32)
    act = (gate * jax.nn.sigmoid(gate) * up).astype(BF16)
    part = jnp.dot(act, wfo_ref[c * FFN_CHUNK:(c + 1) * FFN_CHUNK, :], preferred_element_type=F32)
    if c == 0:
        acc_ref[...] = part
    else:
        acc_ref[...] += part


def _post_kernel(x_ref, ya_ref, yb_ref, mod_ref, wout_ref, n2g_ref, wfi_ref, wfo_ref, o_ref, acc_ref):
    x1, h2 = _post_mixer(x_ref, ya_ref, yb_ref, mod_ref, wout_ref, n2g_ref)
    o_ref[...] = x1
    for c in range(wfo_ref.shape[0] // FFN_CHUNK):
        _post_ffn_chunk(h2, c, wfi_ref, wfo_ref, acc_ref)
    o_ref[...] = o_ref[...] + mod_ref[:, 5:6] * acc_ref[...].reshape(o_ref.shape)


def _post(x, ya, yb, mod, wts, *, nb, rows):
    b, t, d = x.shape
    tok = lambda w: pl.BlockSpec((nb, rows, w), lambda i, j: (i, j, 0))
    consts = [wts[n] for n in ("w_out", "n2g", "w_ffn_in", "w_ffn_out")]
    return pl.pallas_call(
        _post_kernel,
        grid=(b // nb, t // rows),
        in_specs=[tok(d), tok(GMLP_WIDTH), tok(MLA_WIDTH),
                  pl.BlockSpec((nb,) + mod.shape[1:], lambda i, j: (i, 0, 0)),
                  _const_spec(consts[0].shape), _const_spec(consts[1].shape),
                  _const_spec(consts[2].shape), _const_spec(consts[3].shape)],
        out_specs=tok(d),
        out_shape=jax.ShapeDtypeStruct((b, t, d), F32),
        scratch_shapes=[pltpu.VMEM((nb * rows, d), F32)],
        compiler_params=pltpu.CompilerParams(
            dimension_semantics=("parallel", "parallel"), vmem_limit_bytes=VMEM_LIMIT_BYTES),
        name="post",
    )(x, ya, yb, mod, consts[0], consts[1], consts[2], consts[3])


def _rope_tables(first_pos, n, *, mirrored_odd, repeat=1):
    half = QK_ROPE // 2
    inv = ROPE_THETA ** (-np.arange(half, dtype=np.float64) / half)
    ang = (first_pos + np.arange(n, dtype=np.float64))[:, None] * inv[None, :]
    cos, sin = np.cos(ang), np.sin(ang)
    rot = np.concatenate([cos, cos, -sin, sin], axis=-1)
    nope = np.ones((n, QK_NOPE))
    tabq = (SCALE * LOG2E) * np.concatenate([nope, rot] + ([rot, nope] if mirrored_odd else [nope, rot]), axis=-1)
    tabk = np.concatenate([rot, rot], axis=-1)
    return jnp.asarray(np.tile(tabq, (repeat, 1)), F32), jnp.asarray(np.tile(tabk, (repeat, 1)), F32)


def _prep_weights(w_in, w_s, b_s, q_norm_g, w_uq, kv_norm_g, w_ukv, qn_g, qr_g, kn_g, kr_g,
                  norm1_g, w_out, norm2_g, w_ffn_in, w_ffn_out):
    o_kr = 2 * GMLP_WIDTH + Q_LORA + KV_LORA
    w_kr = w_in[:, o_kr:o_kr + QK_ROPE]
    w_kr_sw = _swap_halves(w_kr)
    w_in_p = jnp.concatenate([w_in[:, :o_kr], w_kr, w_kr_sw, w_kr, w_kr_sw], axis=-1)
    d_q = w_uq.shape[0]
    wq = w_uq.reshape(d_q, N_HEADS, QK_HEAD)
    wq_n, wq_r = wq[:, :, :QK_NOPE], wq[:, :, QK_NOPE:]
    wq_plain = jnp.concatenate([wq_n, wq_r, _swap_halves(wq_r)], axis=-1)
    wq_mirror = jnp.concatenate([wq_r, _swap_halves(wq_r), wq_n], axis=-1)
    odd = (jnp.arange(N_HEADS) % 2 == 1)[None, :, None]
    w_uq_flat = wq_plain.reshape(d_q, N_HEADS * HEAD_BLOCK)
    w_uq_pair = jnp.where(odd, wq_mirror, wq_plain).reshape(d_q, N_HEADS * HEAD_BLOCK)
    wkv = w_ukv.reshape(KV_LORA, N_HEADS, QK_NOPE + V_HEAD)
    w_uk = wkv[:, :, :QK_NOPE].reshape(KV_LORA, N_HEADS * QK_NOPE)
    w_uvt = wkv[:, :, QK_NOPE:].reshape(KV_LORA, MLA_WIDTH).T
    root = math.sqrt(QK_NOPE)
    q_nope_row = qn_g * kn_g * (root * root)
    q_rope_row = jnp.concatenate([qr_g, _swap_halves(qr_g)]) * root
    qrow_plain = jnp.concatenate([q_nope_row, q_rope_row])
    qrow_mirror = jnp.concatenate([q_rope_row, q_nope_row])
    row = lambda a: a.reshape(1, -1).astype(F32)
    return {
        "n1g": row(norm1_g), "w_in": w_in_p.astype(BF16), "w_s": w_s.astype(F32), "b_s": b_s.T.astype(F32),
        "qng": row(q_norm_g), "kvg": row(kv_norm_g),
        "w_uq": w_uq_pair.astype(BF16), "w_uq_flat": w_uq_flat.astype(BF16),
        "qrow": row(jnp.concatenate([qrow_plain, qrow_mirror])),
        "qrow_flat": row(jnp.concatenate([qrow_plain, qrow_plain])),
        "w_ukv": w_ukv.astype(BF16), "w_uk": w_uk.astype(BF16), "w_uvt": w_uvt.astype(BF16),
        "krg": row(jnp.concatenate([kr_g, _swap_halves(kr_g), kr_g, _swap_halves(kr_g)])),
        "w_out": w_out.astype(BF16), "n2g": row(norm2_g),
        "w_ffn_in": w_ffn_in.astype(BF16), "w_ffn_out": w_ffn_out.astype(BF16),
    }


def kernel(x_prompt, x_sample, cache_ckv, cache_krope, c_prompt, c_sample, w_ada, b_ada, norm1_g, w_in, w_s, b_s, q_norm_g, w_uq, kv_norm_g, w_ukv, qn_g, qr_g, kn_g, kr_g, w_out, norm2_g, w_ffn_in, w_ffn_out):
    depth = w_in.shape[0]
    assert depth == 1, "single trunk layer"
    assert PRE_ROWS == ATTN_TK
    bp, tp, d = x_prompt.shape
    bs, ts, _ = x_sample.shape
    past_len = cache_ckv.shape[2]

    wts = _prep_weights(w_in[0], w_s[0], b_s[0], q_norm_g[0], w_uq[0], kv_norm_g[0], w_ukv[0], qn_g[0], qr_g[0],
                        kn_g[0], kr_g[0], norm1_g[0], w_out[0], norm2_g[0], w_ffn_in[0], w_ffn_out[0])
    mod = _ada(jnp.concatenate([c_prompt, c_sample], axis=0), w_ada[0], b_ada[0]).reshape(bp + bs, 6, d)
    mod_p, mod_s = mod[:bp], mod[bp:]

    tabq_p, tabk_p = _rope_tables(0, tp, mirrored_odd=True)
    ya_p, q_p, k_p, vt_p, ckv_p, krt_p = _pre_pipelined(x_prompt, mod_p, wts, tabq_p, tabk_p,
                                                        rows=PRE_ROWS, mix_len=GMLP_CHUNK)
    gmax = lambda g: jnp.max(jnp.abs(g))
    score_bound = SCALE * (QK_NOPE * gmax(qn_g) * gmax(kn_g) + QK_ROPE * gmax(qr_g) * gmax(kr_g))
    yb_p = _attn_prompt(q_p, k_p, vt_p, score_bound)
    y_p = _post(x_prompt, ya_p, yb_p, mod_p, wts, nb=1, rows=POST_ROWS)

    tabq_s, tabk_s = _rope_tables(past_len, ts, mirrored_odd=False, repeat=bs)
    ya_s, q_s, ckv_s, kr_s, v_s = _pre_single(x_sample, mod_s, wts, tabq_s, tabk_s, mix_len=min(ts, GMLP_CHUNK))
    yb_s = _attn_sample(q_s, cache_ckv[0], cache_krope[0], ckv_s, kr_s, wts["w_ukv"])
    y_s = _post(x_sample, ya_s, yb_s, mod_s, wts, nb=bs, rows=ts)

    return (y_p, y_s, ckv_p[None], jnp.swapaxes(krt_p, 1, 2)[None], ckv_s[None], kr_s[None], v_s[None])
```

```python
import functools
import math

import jax
import jax.numpy as jnp
import numpy as np
from jax import lax
from jax.experimental import pallas as pl
from jax.experimental.pallas import tpu as pltpu

F32 = jnp.float32
BF16 = jnp.bfloat16

CHUNK = 64
GMLP_CHUNK = 128
GMLP_GROUPS = 4
GMLP_HEAD = 128
GMLP_WIDTH = GMLP_GROUPS * GMLP_HEAD
N_HEADS = 8
QK_NOPE = 64
QK_ROPE = 32
QK_HEAD = QK_NOPE + QK_ROPE
V_HEAD = 64
Q_LORA = 384
KV_LORA = 256
MLA_WIDTH = N_HEADS * V_HEAD
ROPE_THETA = 10000.0
EPS = 1e-6
SCALE = QK_HEAD ** -0.5

LANES = 128
HEAD_BLOCK = LANES
KR_BLOCK = LANES
IN_WIDTH_PADDED = 2 * GMLP_WIDTH + Q_LORA + KV_LORA + KR_BLOCK
VMEM_LIMIT_BYTES = 56 * 1024 * 1024

PRE_ROWS = 512
POST_ROWS = 1024
FFN_CHUNK = 256
PROJ_CHUNK = 256
ATTN_TQ = 1024
ATTN_TK = ATTN_TQ // 2
DEN_ROWS = 16
ACC_ROWS = V_HEAD + DEN_ROWS

_NEG = float(jnp.finfo(jnp.float32).min)
_NT = (((1,), (1,)), ((), ()))
LOG2E = math.log2(math.e)
MAX_UNSHIFTED_SCORE = 60.0


def _swap_halves(a, axis=-1):
    n = a.shape[axis] // 2
    lo = lax.slice_in_dim(a, 0, n, axis=axis)
    hi = lax.slice_in_dim(a, n, 2 * n, axis=axis)
    return jnp.concatenate([hi, lo], axis=axis)


def _const_spec(shape):
    nd = len(shape)
    return pl.BlockSpec(shape, lambda *_: (0,) * nd, pipeline_mode=pl.Buffered(1))


def _ada_kernel(c_ref, w_ref, b_ref, o_ref):
    c = c_ref[...]
    a = c * jax.nn.sigmoid(c)
    w = w_ref[...]
    a_hi, w_hi = a.astype(BF16), w.astype(BF16)
    a_lo, w_lo = (a - a_hi.astype(F32)).astype(BF16), (w - w_hi.astype(F32)).astype(BF16)
    n = a.shape[0]
    both = jnp.dot(jnp.concatenate([a_hi, a_lo], axis=0), w_hi, preferred_element_type=F32)
    o_ref[...] = both[:n] + both[n:] + jnp.dot(a_hi, w_lo, preferred_element_type=F32) + b_ref[...]


def _ada(c_all, w_ada, b_ada):
    n, d = c_all.shape
    width = w_ada.shape[1]
    bn = d
    return pl.pallas_call(
        _ada_kernel,
        grid=(width // bn,),
        in_specs=[pl.BlockSpec((n, d), lambda j: (0, 0)),
                  pl.BlockSpec((d, bn), lambda j: (0, j)),
                  pl.BlockSpec((1, bn), lambda j: (0, j))],
        out_specs=pl.BlockSpec((n, bn), lambda j: (0, j)),
        out_shape=jax.ShapeDtypeStruct((n, width), F32),
        compiler_params=pltpu.CompilerParams(vmem_limit_bytes=VMEM_LIMIT_BYTES),
        name="ada",
    )(c_all, w_ada, b_ada.reshape(1, width))


def _rms(x, width):
    return lax.rsqrt(jnp.sum(x * x, axis=-1, keepdims=True) * (1.0 / width) + EPS)


def _pre_norm(x_ref, mod_ref, n1g_ref):
    nb, rows, d = x_ref.shape
    x = x_ref[...]
    mod = mod_ref[...]
    sh1, sc1 = mod[:, 0:1], mod[:, 1:2]
    h = x * _rms(x, d) * (n1g_ref[...] * (1.0 + sc1)) + sh1
    return h.reshape(nb * rows, d).astype(BF16)


def _gelu_tanh(x):
    k = -2.0 * math.sqrt(2.0 / math.pi) * LOG2E
    e = jnp.exp2(x * (x * x * (k * 0.044715) + k))
    return x * pl.reciprocal(1.0 + e, approx=False)


def _half_sums(a, lo):
    sq = a * a
    ss_all = jnp.sum(sq, axis=-1, keepdims=True)
    ss_lo = jnp.sum(jnp.where(lo, sq, 0.0), axis=-1, keepdims=True)
    return jnp.where(lo, ss_lo, ss_all - ss_lo)


def _pre_heads(z_ref, ws_ref, bs_ref, qng_ref, wuq_ref, kvg_ref, qrow_ref, krg_ref, tabq_ref, tabk_ref,
               ya_ref, q_ref, ckv_ref, kr_ref, maybe_kv, maybe_vraw_ref, mix_len, tick=lambda: None):
    nb, tile_rows = ya_ref.shape[0], ya_ref.shape[1]
    rows = z_ref.shape[0]
    per_head = lambda a: a.reshape(nb, tile_rows, HEAD_BLOCK)
    o_v, o_q, o_kv, o_kr = GMLP_WIDTH, 2 * GMLP_WIDTH, 2 * GMLP_WIDTH + Q_LORA, 2 * GMLP_WIDTH + Q_LORA + KV_LORA
    lane = lax.broadcasted_iota(jnp.int32, (1, LANES), 1)
    lo = lane < LANES // 2

    kr = z_ref[:, o_kr:o_kr + KR_BLOCK]
    kr_ss = jnp.sum(jnp.where(lane < QK_ROPE, kr * kr, 0.0), axis=-1, keepdims=True)
    kr_p = kr * lax.rsqrt(kr_ss * (1.0 / QK_ROPE) + EPS) * krg_ref[...] * tabk_ref[...]
    krope = kr_p + pltpu.roll(kr_p, LANES - QK_ROPE, axis=1)
    if kr_ref.shape[-1] == QK_ROPE:
        kr_ref[...] = krope[:, :QK_ROPE].reshape(kr_ref.shape)
    else:
        kr_ref[0] = krope.T[:QK_ROPE, :]

    ckv_raw = z_ref[:, o_kv:o_kr]
    ckv = ckv_raw * _rms(ckv_raw, KV_LORA) * kvg_ref[...]
    ckv_ref[...] = ckv.reshape(ckv_ref.shape)
    cq_raw = z_ref[:, o_q:o_kv]
    cq = cq_raw * _rms(cq_raw, Q_LORA) * qng_ref[...]
    tick()
    q = jnp.dot(cq.astype(BF16), wuq_ref[...], preferred_element_type=F32)
    blocks = [q[:, hd * HEAD_BLOCK:(hd + 1) * HEAD_BLOCK] for hd in range(N_HEADS)]
    if maybe_kv is not None:
        wuk_ref, wuvt_ref, k_ref, vt_ref = maybe_kv
        ckv_b = ckv.astype(BF16)
        kp_all = jnp.dot(ckv_b, wuk_ref[...], preferred_element_type=F32)
        vt_ref[0, 0] = lax.dot_general(wuvt_ref[...], ckv_b, _NT, preferred_element_type=F32).astype(vt_ref.dtype)
        blocks += [kp_all[:, pair * LANES:(pair + 1) * LANES] for pair in range(N_HEADS // 2)]

    tick()
    u = _gelu_tanh(z_ref[:, :o_v])
    tick()
    v = _gelu_tanh(z_ref[:, o_v:o_q])
    if maybe_vraw_ref is not None:
        maybe_vraw_ref[...] = v.reshape(maybe_vraw_ref.shape)
    tick()

    vb = v.astype(BF16)
    pi = lax.broadcasted_iota(jnp.int32, (mix_len, mix_len), 0) // CHUNK
    pj = lax.broadcasted_iota(jnp.int32, (mix_len, mix_len), 1) // CHUNK
    for g in range(GMLP_GROUPS):
        wsg = jnp.where(pj <= pi, ws_ref[g, :mix_len, :mix_len], 0.0).astype(BF16)
        bg = bs_ref[:mix_len, g:g + 1]
        cols = slice(g * GMLP_HEAD, (g + 1) * GMLP_HEAD)
        for c in range(rows // mix_len):
            rws = slice(c * mix_len, (c + 1) * mix_len)
            mixed = jnp.dot(wsg, vb[rws, cols], preferred_element_type=F32) + bg
            bi, off = divmod(c * mix_len, tile_rows)
            ya_ref[bi, off:off + mix_len, cols] = (u[rws, cols] * mixed).astype(ya_ref.dtype)
        tick()

    qfac = tabq_ref[...] * qrow_ref[...]
    scales = []
    for blk in blocks:
        scales.append(lax.rsqrt(_half_sums(blk, lo) + QK_NOPE * EPS))
    for hd in range(N_HEADS):
        fac = qfac[:, (hd % 2) * LANES:(hd % 2 + 1) * LANES]
        q_ref[:, hd] = per_head((blocks[hd] * scales[hd] * fac).astype(q_ref.dtype))
    if maybe_kv is not None:
        krope_b = krope.astype(k_ref.dtype)
        for pair in range(N_HEADS // 2):
            kn = (blocks[N_HEADS + pair] * scales[N_HEADS + pair]).astype(k_ref.dtype)
            k_ref[:, 2 * pair] = per_head(jnp.where(lo, kn, krope_b))
            k_ref[:, 2 * pair + 1] = per_head(jnp.where(lo, krope_b, kn))


_PRE_SINGLE_CONSTS = ("n1g", "w_in", "w_s", "b_s", "qng", "w_uq_flat", "kvg", "qrow_flat", "krg")
_PRE_PIPELINED_CONSTS = ("n1g", "w_in", "w_s", "b_s", "qng", "w_uq", "kvg", "qrow", "krg", "w_uk", "w_uvt")


def _pre_single_kernel(x_ref, mod_ref, n1g_ref, win_ref, ws_ref, bs_ref, qng_ref, wuq_ref, kvg_ref, qrow_ref, krg_ref,
                       tabq_ref, tabk_ref, ya_ref, q_ref, ckv_ref, kr_ref, vraw_ref, z_ref, *, mix_len):
    z_ref[...] = jnp.dot(_pre_norm(x_ref, mod_ref, n1g_ref), win_ref[...], preferred_element_type=F32)
    _pre_heads(z_ref, ws_ref, bs_ref, qng_ref, wuq_ref, kvg_ref, qrow_ref, krg_ref, tabq_ref, tabk_ref,
               ya_ref, q_ref, ckv_ref, kr_ref, None, vraw_ref, mix_len)


def _pre_pipelined_kernel(x_ref, mod_ref, n1g_ref, win_ref, ws_ref, bs_ref, qng_ref, wuq_ref, kvg_ref, qrow_ref,
                          krg_ref, wuk_ref, wuvt_ref, tabq_ref, tabk_ref,
                          ya_ref, q_ref, k_ref, vt_ref, ckv_ref, kr_ref, z_ref, *, mix_len):
    i = pl.program_id(0)

    @pl.when(i == 0)
    def _():
        z_ref[...] = jnp.zeros(z_ref.shape, F32)

    for parity in range(2):
        @pl.when(i % 2 == parity)
        def _():
            normed = []

            def project(c):
                if not normed:
                    normed.append(_pre_norm(x_ref, mod_ref, n1g_ref))
                cols = slice(c * PROJ_CHUNK, (c + 1) * PROJ_CHUNK)
                z_ref[parity, :, cols] = jnp.dot(normed[0], win_ref[:, cols], preferred_element_type=F32)

            chunks = list(range(IN_WIDTH_PADDED // PROJ_CHUNK))
            pending = iter([chunks[:2]] + [[c] for c in chunks[2:]])

            def tick():
                for c in next(pending, []):
                    project(c)

            _pre_heads(z_ref.at[1 - parity], ws_ref, bs_ref, qng_ref, wuq_ref, kvg_ref, qrow_ref, krg_ref,
                       tabq_ref, tabk_ref, ya_ref, q_ref, ckv_ref, kr_ref,
                       (wuk_ref, wuvt_ref, k_ref, vt_ref), None, mix_len, tick)
            for group in pending:
                for c in group:
                    project(c)


def _pre_single(x, mod, wts, tabq, tabk, *, mix_len):
    b, t, d = x.shape
    whole = lambda *shape: pl.BlockSpec(shape, lambda i: (0,) * len(shape))
    consts = [wts[n] for n in _PRE_SINGLE_CONSTS]
    return pl.pallas_call(
        functools.partial(_pre_single_kernel, mix_len=mix_len),
        grid=(1,),
        in_specs=([whole(b, t, d), whole(*mod.shape)] + [_const_spec(a.shape) for a in consts]
                  + [whole(*tabq.shape), whole(*tabk.shape)]),
        out_specs=[whole(b, t, GMLP_WIDTH), whole(b, N_HEADS, t, HEAD_BLOCK), whole(b, t, KV_LORA),
                   whole(b, t, QK_ROPE), whole(b, t, GMLP_WIDTH)],
        out_shape=[jax.ShapeDtypeStruct((b, t, GMLP_WIDTH), BF16),
                   jax.ShapeDtypeStruct((b, N_HEADS, t, HEAD_BLOCK), BF16),
                   jax.ShapeDtypeStruct((b, t, KV_LORA), F32),
                   jax.ShapeDtypeStruct((b, t, QK_ROPE), F32),
                   jax.ShapeDtypeStruct((b, t, GMLP_WIDTH), F32)],
        scratch_shapes=[pltpu.VMEM((b * t, IN_WIDTH_PADDED), F32)],
        compiler_params=pltpu.CompilerParams(
            dimension_semantics=("arbitrary",), vmem_limit_bytes=VMEM_LIMIT_BYTES),
        name="pre_single",
    )(x, mod, *consts, tabq, tabk)


def _pre_pipelined(x, mod, wts, tabq, tabk, *, rows, mix_len):
    b, t, d = x.shape
    per_batch = t // rows
    n_tiles = b * per_batch
    cur = lambda i: jnp.minimum(i, n_tiles - 1)
    prev = lambda i: jnp.maximum(i - 1, 0)
    pb, pj = (lambda i: prev(i) // per_batch), (lambda i: prev(i) % per_batch)
    tok_out = lambda w: pl.BlockSpec((1, rows, w), lambda i: (pb(i), pj(i), 0))
    head = pl.BlockSpec((1, N_HEADS, rows, HEAD_BLOCK), lambda i: (pb(i), 0, pj(i), 0))
    tab = lambda a: pl.BlockSpec((rows, a.shape[1]), lambda i: (pj(i), 0))
    consts = [wts[n] for n in _PRE_PIPELINED_CONSTS]
    return pl.pallas_call(
        functools.partial(_pre_pipelined_kernel, mix_len=mix_len),
        grid=(n_tiles + 1,),
        in_specs=([pl.BlockSpec((1, rows, d), lambda i: (cur(i) // per_batch, cur(i) % per_batch, 0)),
                   pl.BlockSpec((1,) + mod.shape[1:], lambda i: (cur(i) // per_batch, 0, 0))]
                  + [_const_spec(a.shape) for a in consts] + [tab(tabq), tab(tabk)]),
        out_specs=[tok_out(GMLP_WIDTH), head, head,
                   pl.BlockSpec((1, 1, MLA_WIDTH, rows), lambda i: (pb(i), pj(i), 0, 0)),
                   tok_out(KV_LORA),
                   pl.BlockSpec((1, QK_ROPE, rows), lambda i: (pb(i), 0, pj(i)))],
        out_shape=[jax.ShapeDtypeStruct((b, t, GMLP_WIDTH), BF16),
                   jax.ShapeDtypeStruct((b, N_HEADS, t, HEAD_BLOCK), BF16),
                   jax.ShapeDtypeStruct((b, N_HEADS, t, HEAD_BLOCK), BF16),
                   jax.ShapeDtypeStruct((b, per_batch, MLA_WIDTH, rows), BF16),
                   jax.ShapeDtypeStruct((b, t, KV_LORA), F32),
                   jax.ShapeDtypeStruct((b, QK_ROPE, t), F32)],
        scratch_shapes=[pltpu.VMEM((2, rows, IN_WIDTH_PADDED), F32)],
        compiler_params=pltpu.CompilerParams(
            dimension_semantics=("arbitrary",), vmem_limit_bytes=VMEM_LIMIT_BYTES),
        name="pre_pipelined",
    )(x, mod, *consts, tabq, tabk)


def _aligned(index, multiple):
    return index if isinstance(index, int) else pl.multiple_of(index, multiple)


def _attn_kernel(q_ref, k_ref, vt_ref, o_ref, acc_ref, *maybe_m_ref, tq, online):
    t = q_ref.shape[2]
    unrolled = not online
    hb = tq // 2
    mb = hb // 2
    key_chunk = lax.broadcasted_iota(jnp.int32, (mb, mb), 0) // CHUNK
    query_chunk = lax.broadcasted_iota(jnp.int32, (mb, mb), 1) // CHUNK
    visible = key_chunk <= query_chunk
    ones = jnp.ones((DEN_ROWS, hb), BF16)

    def operands(hh, q0, qo, rows, kt, c0, keys):
        q = q_ref[0, hh, pl.ds(_aligned(q0 + qo, mb), rows), :]
        k = k_ref[0, hh, pl.ds(_aligned(kt * hb + c0, mb), keys), :]
        lhs = jnp.concatenate([vt_ref[0, kt, hh * V_HEAD:(hh + 1) * V_HEAD, c0:c0 + keys], ones[:, :keys]], axis=0)
        return lhs, lax.dot_general(k, q, _NT, preferred_element_type=F32)

    def unshifted(hh, q0, qo, rows, kt, c0, keys, masked=False):
        lhs, s = operands(hh, q0, qo, rows, kt, c0, keys)
        p = jnp.exp2(s)
        if masked:
            p = jnp.where(visible, p, 0.0)
        return jnp.dot(lhs, p.astype(BF16), preferred_element_type=F32)

    def online_update(hh, q0, qo, rows, kt, c0, keys, masked=False):
        m_ref = maybe_m_ref[0]
        lhs, s = operands(hh, q0, qo, rows, kt, c0, keys)
        if masked:
            s = jnp.where(visible, s, _NEG)
        cols = slice(qo, qo + rows)
        m_old = m_ref[hh, :, cols]
        m_new = jnp.maximum(m_old, jnp.max(s, axis=0, keepdims=True))
        p = jnp.exp2(s - m_new).astype(BF16)
        acc_ref[hh, :, cols] = (jnp.exp2(m_old - m_new) * acc_ref[hh, :, cols]
                                + jnp.dot(lhs, p, preferred_element_type=F32))
        m_ref[hh, :, cols] = m_new

    diagonal = [(0, mb, 0, 0, mb, True), (mb, mb, 0, 0, mb, False), (mb, mb, 0, mb, mb, True),
                (hb, hb, 0, 0, hb, False),
                (hb, mb, 1, 0, mb, True), (hb + mb, mb, 1, 0, mb, False), (hb + mb, mb, 1, mb, mb, True)]

    def q_tile(qi, carry):
        q0 = qi * tq
        acc_ref[...] = jnp.zeros(acc_ref.shape, F32)
        if online:
            maybe_m_ref[0][...] = jnp.full(maybe_m_ref[0].shape, -jnp.inf, F32)

        def body(j, c):
            tiles = [(hh, 2 * j + half) for half in range(2) for hh in range(2)]
            if online:
                for hh, kt in tiles:
                    online_update(hh, q0, 0, tq, kt, 0, hb)
            else:
                staged = [operands(hh, q0, 0, tq, kt, 0, hb) for hh, kt in tiles]
                probs = [jnp.exp2(s).astype(BF16) for _, s in staged]
                for hh in range(2):
                    acc_ref[hh] += sum(jnp.dot(lhs, p, preferred_element_type=F32)
                                       for (th, _), (lhs, _), p in zip(tiles, staged, probs) if th == hh)
            return c

        if unrolled:
            for j in range(qi):
                body(j, 0)
        else:
            lax.fori_loop(0, qi, body, 0)

        if online:
            for hh in range(2):
                for qo, rows, kt_off, c0, keys, masked in diagonal:
                    online_update(hh, q0, qo, rows, qi * 2 + kt_off, c0, keys, masked)
            totals = [[acc_ref[hh, :, g * mb:(g + 1) * mb] for g in range(tq // mb)] for hh in range(2)]
        else:
            work = [(hh, blk) for hh in range(2) for blk in diagonal]
            staged = [operands(hh, q0, qo, rows, qi * 2 + kt_off, c0, keys)
                      for hh, (qo, rows, kt_off, c0, keys, _) in work]
            probs = [jnp.exp2(s) for _, s in staged]
            probs = [jnp.where(visible, p, 0.0) if blk[5] else p for p, (_, blk) in zip(probs, work)]
            parts = [jnp.dot(lhs, p.astype(BF16), preferred_element_type=F32) for (lhs, _), p in zip(staged, probs)]
            totals = [[acc_ref[hh, :, g * mb:(g + 1) * mb] for g in range(tq // mb)] for hh in range(2)]
            for (hh, (qo, rows, *_)), part in zip(work, parts):
                for g in range(rows // mb):
                    totals[hh][qo // mb + g] = totals[hh][qo // mb + g] + part[:, g * mb:(g + 1) * mb]
        for g in range(tq // mb):
            o_t = [tot[g][:V_HEAD] * pl.reciprocal(tot[g][V_HEAD:V_HEAD + 1], approx=False) for tot in totals]
            o_ref[0, pl.ds(_aligned(q0 + g * mb, mb), mb), :] = jnp.concatenate(o_t, axis=0).T.astype(o_ref.dtype)
        return carry

    if unrolled:
        for qi in range(t // tq):
            q_tile(qi, 0)
    else:
        lax.fori_loop(0, t // tq, q_tile, 0)


def _attn_prompt(q, k, vt, score_bound):
    b, nh, t, _ = q.shape
    n_kt, _, hb = vt.shape[1:]
    assert hb == ATTN_TK and n_kt * hb == t
    qk = pl.BlockSpec((1, 2, t, HEAD_BLOCK), lambda i, j: (i, j, 0, 0))

    def call(online):
        scratch = [pltpu.VMEM((2, ACC_ROWS, ATTN_TQ), F32)]
        if online:
            scratch.append(pltpu.VMEM((2, 1, ATTN_TQ), F32))
        return pl.pallas_call(
            functools.partial(_attn_kernel, tq=ATTN_TQ, online=online),
            grid=(b, nh // 2),
            in_specs=[qk, qk, pl.BlockSpec((1, n_kt, 2 * V_HEAD, hb), lambda i, j: (i, 0, j, 0))],
            out_specs=pl.BlockSpec((1, t, 2 * V_HEAD), lambda i, j: (i, 0, j)),
            out_shape=jax.ShapeDtypeStruct((b, t, MLA_WIDTH), BF16),
            scratch_shapes=scratch,
            compiler_params=pltpu.CompilerParams(
                dimension_semantics=("parallel", "parallel"), vmem_limit_bytes=VMEM_LIMIT_BYTES),
            name="attn_prompt_online" if online else "attn_prompt_bounded")

    return lax.cond(score_bound <= MAX_UNSHIFTED_SCORE, call(False), call(True), q, k, vt)


def _pair_out(acc0, acc1):
    first = lax.broadcasted_iota(jnp.int32, (1, LANES), 1) < V_HEAD
    num = jnp.where(first, pltpu.roll(acc0, V_HEAD, axis=1), acc1)
    den = jnp.where(first, acc0, pltpu.roll(acc1, V_HEAD, axis=1))
    return num * pl.reciprocal(den, approx=False)


def _attn_sample_kernel(q_ref, past_ckv_ref, past_kr_ref, new_ckv_ref, new_kr_ref, wukv_ref, o_ref):
    lane = lax.broadcasted_iota(jnp.int32, (1, LANES), 1)
    is_nope = lane < QK_NOPE
    spread = (lax.broadcasted_iota(jnp.int32, (QK_ROPE, LANES), 1) % QK_ROPE
              == lax.broadcasted_iota(jnp.int32, (QK_ROPE, LANES), 0)).astype(BF16)
    one_b = jnp.ones((1, LANES), BF16)

    def segment(ckv_ref, kr_ref):
        kv = jnp.dot(ckv_ref[0].astype(BF16), wukv_ref[...], preferred_element_type=F32)
        krope_b = jnp.dot(kr_ref[0].astype(BF16), spread, preferred_element_type=F32).astype(BF16)
        return kv, krope_b

    def head_kv(seg, hd):
        kv, krope_b = seg
        kvh = kv[:, hd * HEAD_BLOCK:(hd + 1) * HEAD_BLOCK]
        ks = jnp.sum(jnp.where(is_nope, kvh * kvh, 0.0), axis=-1, keepdims=True)
        kn = kvh * lax.rsqrt(ks + QK_NOPE * EPS)
        return (jnp.where(is_nope, kn.astype(BF16), krope_b), jnp.where(is_nope, one_b, kvh.astype(BF16)))

    past, new = segment(past_ckv_ref, past_kr_ref), segment(new_ckv_ref, new_kr_ref)
    heads = range(N_HEADS)
    kv = [(head_kv(past, hd), head_kv(new, hd)) for hd in heads]
    scores = [[lax.dot_general(q_ref[0, hd], k, _NT, preferred_element_type=F32) for k, _ in kv[hd]] for hd in heads]
    peaks = [jnp.maximum(*(jnp.max(s, axis=-1, keepdims=True) for s in scores[hd])) for hd in heads]
    probs = [[jnp.exp2(s - peaks[hd]).astype(BF16) for s in scores[hd]] for hd in heads]
    accs = [sum(jnp.dot(p, v, preferred_element_type=F32) for p, (_, v) in zip(probs[hd], kv[hd])) for hd in heads]
    for hd in range(0, N_HEADS, 2):
        o_ref[0, :, hd * V_HEAD:(hd + 2) * V_HEAD] = _pair_out(accs[hd], accs[hd + 1]).astype(o_ref.dtype)


def _attn_sample(q, past_ckv, past_kr, new_ckv, new_kr, w_ukv):
    b, nh, t, _ = q.shape
    per_batch = lambda a: pl.BlockSpec((1,) + a.shape[1:], lambda i: (i,) + (0,) * (a.ndim - 1))
    return pl.pallas_call(
        _attn_sample_kernel,
        grid=(b,),
        in_specs=[per_batch(q), per_batch(past_ckv), per_batch(past_kr), per_batch(new_ckv), per_batch(new_kr),
                  _const_spec(w_ukv.shape)],
        out_specs=pl.BlockSpec((1, t, MLA_WIDTH), lambda i: (i, 0, 0)),
        out_shape=jax.ShapeDtypeStruct((b, t, MLA_WIDTH), BF16),
        compiler_params=pltpu.CompilerParams(
            dimension_semantics=("parallel",), vmem_limit_bytes=VMEM_LIMIT_BYTES),
        name="attn_sample",
    )(q, past_ckv, past_kr, new_ckv, new_kr, w_ukv)


def _post_mixer(x_ref, ya_ref, yb_ref, mod_ref, wout_ref, n2g_ref):
    nb, rows, d = x_ref.shape
    flat = lambda a: a.reshape(nb * rows, a.shape[-1])
    mod = mod_ref[...]
    g1, sh2, sc2 = mod[:, 2:3], mod[:, 3:4], mod[:, 4:5]
    mixer = jnp.concatenate([flat(ya_ref[...]), flat(yb_ref[...])], axis=-1)
    mix = jnp.dot(mixer, wout_ref[...], preferred_element_type=F32)
    x1 = x_ref[...] + g1 * mix.reshape(nb, rows, d)
    h2 = flat(x1 * _rms(x1, d) * (n2g_ref[...] * (1.0 + sc2)) + sh2).astype(BF16)
    return x1, h2


def _post_ffn_chunk(h2, c, wfi_ref, wfo_ref, acc_ref):
    d_ff = wfo_ref.shape[0]
    gate = jnp.dot(h2, wfi_ref[:, c * FFN_CHUNK:(c + 1) * FFN_CHUNK], preferred_element_type=F32)
    up = jnp.dot(h2, wfi_ref[:, d_ff + c * FFN_CHUNK:d_ff + (c + 1) * FFN_CHUNK], preferred_element_type=F32)
    act = (gate * jax.nn.sigmoid(gate) * up).astype(BF16)
    part = jnp.dot(act, wfo_ref[c * FFN_CHUNK:(c + 1) * FFN_CHUNK, :], preferred_element_type=F32)
    if c == 0:
        acc_ref[...] = part
    else:
        acc_ref[...] += part


def _post_kernel(x_ref, ya_ref, yb_ref, mod_ref, wout_ref, n2g_ref, wfi_ref, wfo_ref, o_ref, acc_ref):
    x1, h2 = _post_mixer(x_ref, ya_ref, yb_ref, mod_ref, wout_ref, n2g_ref)
    o_ref[...] = x1
    for c in range(wfo_ref.shape[0] // FFN_CHUNK):
        _post_ffn_chunk(h2, c, wfi_ref, wfo_ref, acc_ref)
    o_ref[...] = o_ref[...] + mod_ref[:, 5:6] * acc_ref[...].reshape(o_ref.shape)


def _post(x, ya, yb, mod, wts, *, nb, rows):
    b, t, d = x.shape
    tok = lambda w: pl.BlockSpec((nb, rows, w), lambda i, j: (i, j, 0))
    consts = [wts[n] for n in ("w_out", "n2g", "w_ffn_in", "w_ffn_out")]
    return pl.pallas_call(
        _post_kernel,
        grid=(b // nb, t // rows),
        in_specs=[tok(d), tok(GMLP_WIDTH), tok(MLA_WIDTH),
                  pl.BlockSpec((nb,) + mod.shape[1:], lambda i, j: (i, 0, 0)),
                  _const_spec(consts[0].shape), _const_spec(consts[1].shape),
                  _const_spec(consts[2].shape), _const_spec(consts[3].shape)],
        out_specs=tok(d),
        out_shape=jax.ShapeDtypeStruct((b, t, d), F32),
        scratch_shapes=[pltpu.VMEM((nb * rows, d), F32)],
        compiler_params=pltpu.CompilerParams(
            dimension_semantics=("parallel", "parallel"), vmem_limit_bytes=VMEM_LIMIT_BYTES),
        name="post",
    )(x, ya, yb, mod, consts[0], consts[1], consts[2], consts[3])


def _rope_tables(first_pos, n, *, mirrored_odd, repeat=1):
    half = QK_ROPE // 2
    inv = ROPE_THETA ** (-np.arange(half, dtype=np.float64) / half)
    ang = (first_pos + np.arange(n, dtype=np.float64))[:, None] * inv[None, :]
    cos, sin = np.cos(ang), np.sin(ang)
    rot = np.concatenate([cos, cos, -sin, sin], axis=-1)
    nope = np.ones((n, QK_NOPE))
    tabq = (SCALE * LOG2E) * np.concatenate([nope, rot] + ([rot, nope] if mirrored_odd else [nope, rot]), axis=-1)
    tabk = np.concatenate([rot, rot], axis=-1)
    return jnp.asarray(np.tile(tabq, (repeat, 1)), F32), jnp.asarray(np.tile(tabk, (repeat, 1)), F32)


def _prep_weights(w_in, w_s, b_s, q_norm_g, w_uq, kv_norm_g, w_ukv, qn_g, qr_g, kn_g, kr_g,
                  norm1_g, w_out, norm2_g, w_ffn_in, w_ffn_out):
    o_kr = 2 * GMLP_WIDTH + Q_LORA + KV_LORA
    w_kr = w_in[:, o_kr:o_kr + QK_ROPE]
    w_kr_sw = _swap_halves(w_kr)
    w_in_p = jnp.concatenate([w_in[:, :o_kr], w_kr, w_kr_sw, w_kr, w_kr_sw], axis=-1)
    d_q = w_uq.shape[0]
    wq = w_uq.reshape(d_q, N_HEADS, QK_HEAD)
    wq_n, wq_r = wq[:, :, :QK_NOPE], wq[:, :, QK_NOPE:]
    wq_plain = jnp.concatenate([wq_n, wq_r, _swap_halves(wq_r)], axis=-1)
    wq_mirror = jnp.concatenate([wq_r, _swap_halves(wq_r), wq_n], axis=-1)
    odd = (jnp.arange(N_HEADS) % 2 == 1)[None, :, None]
    w_uq_flat = wq_plain.reshape(d_q, N_HEADS * HEAD_BLOCK)
    w_uq_pair = jnp.where(odd, wq_mirror, wq_plain).reshape(d_q, N_HEADS * HEAD_BLOCK)
    wkv = w_ukv.reshape(KV_LORA, N_HEADS, QK_NOPE + V_HEAD)
    w_uk = wkv[:, :, :QK_NOPE].reshape(KV_LORA, N_HEADS * QK_NOPE)
    w_uvt = wkv[:, :, QK_NOPE:].reshape(KV_LORA, MLA_WIDTH).T
    root = math.sqrt(QK_NOPE)
    q_nope_row = qn_g * kn_g * (root * root)
    q_rope_row = jnp.concatenate([qr_g, _swap_halves(qr_g)]) * root
    qrow_plain = jnp.concatenate([q_nope_row, q_rope_row])
    qrow_mirror = jnp.concatenate([q_rope_row, q_nope_row])
    row = lambda a: a.reshape(1, -1).astype(F32)
    return {
        "n1g": row(norm1_g), "w_in": w_in_p.astype(BF16), "w_s": w_s.astype(F32), "b_s": b_s.T.astype(F32),
        "qng": row(q_norm_g), "kvg": row(kv_norm_g),
        "w_uq": w_uq_pair.astype(BF16), "w_uq_flat": w_uq_flat.astype(BF16),
        "qrow": row(jnp.concatenate([qrow_plain, qrow_mirror])),
        "qrow_flat": row(jnp.concatenate([qrow_plain, qrow_plain])),
        "w_ukv": w_ukv.astype(BF16), "w_uk": w_uk.astype(BF16), "w_uvt": w_uvt.astype(BF16),
        "krg": row(jnp.concatenate([kr_g, _swap_halves(kr_g), kr_g, _swap_halves(kr_g)])),
        "w_out": w_out.astype(BF16), "n2g": row(norm2_g),
        "w_ffn_in": w_ffn_in.astype(BF16), "w_ffn_out": w_ffn_out.astype(BF16),
    }


def kernel(x_prompt, x_sample, cache_ckv, cache_krope, c_prompt, c_sample, w_ada, b_ada, norm1_g, w_in, w_s, b_s, q_norm_g, w_uq, kv_norm_g, w_ukv, qn_g, qr_g, kn_g, kr_g, w_out, norm2_g, w_ffn_in, w_ffn_out):
    depth = w_in.shape[0]
    assert depth == 1, "single trunk layer"
    assert PRE_ROWS == ATTN_TK
    bp, tp, d = x_prompt.shape
    bs, ts, _ = x_sample.shape
    past_len = cache_ckv.shape[2]

    wts = _prep_weights(w_in[0], w_s[0], b_s[0], q_norm_g[0], w_uq[0], kv_norm_g[0], w_ukv[0], qn_g[0], qr_g[0],
                        kn_g[0], kr_g[0], norm1_g[0], w_out[0], norm2_g[0], w_ffn_in[0], w_ffn_out[0])
    mod = _ada(jnp.concatenate([c_prompt, c_sample], axis=0), w_ada[0], b_ada[0]).reshape(bp + bs, 6, d)
    mod_p, mod_s = mod[:bp], mod[bp:]

    tabq_p, tabk_p = _rope_tables(0, tp, mirrored_odd=True)
    ya_p, q_p, k_p, vt_p, ckv_p, krt_p = _pre_pipelined(x_prompt, mod_p, wts, tabq_p, tabk_p,
                                                        rows=PRE_ROWS, mix_len=GMLP_CHUNK)
    gmax = lambda g: jnp.max(jnp.abs(g))
    score_bound = SCALE * (QK_NOPE * gmax(qn_g) * gmax(kn_g) + QK_ROPE * gmax(qr_g) * gmax(kr_g))
    yb_p = _attn_prompt(q_p, k_p, vt_p, score_bound)
    y_p = _post(x_prompt, ya_p, yb_p, mod_p, wts, nb=1, rows=POST_ROWS)

    tabq_s, tabk_s = _rope_tables(past_len, ts, mirrored_odd=False, repeat=bs)
    ya_s, q_s, ckv_s, kr_s, v_s = _pre_single(x_sample, mod_s, wts, tabq_s, tabk_s, mix_len=min(ts, GMLP_CHUNK))
    yb_s = _attn_sample(q_s, cache_ckv[0], cache_krope[0], ckv_s, kr_s, wts["w_ukv"])
    y_s = _post(x_sample, ya_s, yb_s, mod_s, wts, nb=bs, rows=ts)

    return (y_p, y_s, ckv_p[None], jnp.swapaxes(krt_p, 1, 2)[None], ckv_s[None], kr_s[None], v_s[None])
```

```python
import functools
import math

import jax
import jax.numpy as jnp
import numpy as np
from jax import lax
from jax.experimental import pallas as pl
from jax.experimental.pallas import tpu as pltpu

F32 = jnp.float32
BF16 = jnp.bfloat16

CHUNK = 64
GMLP_CHUNK = 128
GMLP_GROUPS = 4
GMLP_HEAD = 128
GMLP_WIDTH = GMLP_GROUPS * GMLP_HEAD
N_HEADS = 8
QK_NOPE = 64
QK_ROPE = 32
QK_HEAD = QK_NOPE + QK_ROPE
V_HEAD = 64
Q_LORA = 384
KV_LORA = 256
MLA_WIDTH = N_HEADS * V_HEAD
ROPE_THETA = 10000.0
EPS = 1e-6
SCALE = QK_HEAD ** -0.5

LANES = 128
HEAD_BLOCK = LANES
KR_BLOCK = LANES
IN_WIDTH_PADDED = 2 * GMLP_WIDTH + Q_LORA + KV_LORA + KR_BLOCK
VMEM_LIMIT_BYTES = 56 * 1024 * 1024

PRE_ROWS = 512
POST_ROWS = 1024
FFN_CHUNK = 256
PROJ_CHUNK = 256
ATTN_TQ = 1024
ATTN_TK = ATTN_TQ // 2
DEN_ROWS = 16
ACC_ROWS = V_HEAD + DEN_ROWS

_NEG = float(jnp.finfo(jnp.float32).min)
_NT = (((1,), (1,)), ((), ()))
LOG2E = math.log2(math.e)
MAX_UNSHIFTED_SCORE = 60.0


def _swap_halves(a, axis=-1):
    n = a.shape[axis] // 2
    lo = lax.slice_in_dim(a, 0, n, axis=axis)
    hi = lax.slice_in_dim(a, n, 2 * n, axis=axis)
    return jnp.concatenate([hi, lo], axis=axis)


def _const_spec(shape):
    nd = len(shape)
    return pl.BlockSpec(shape, lambda *_: (0,) * nd, pipeline_mode=pl.Buffered(1))


def _ada_kernel(c_ref, w_ref, b_ref, o_ref):
    c = c_ref[...]
    a = c * jax.nn.sigmoid(c)
    w = w_ref[...]
    a_hi, w_hi = a.astype(BF16), w.astype(BF16)
    a_lo, w_lo = (a - a_hi.astype(F32)).astype(BF16), (w - w_hi.astype(F32)).astype(BF16)
    n = a.shape[0]
    both = jnp.dot(jnp.concatenate([a_hi, a_lo], axis=0), w_hi, preferred_element_type=F32)
    o_ref[...] = both[:n] + both[n:] + jnp.dot(a_hi, w_lo, preferred_element_type=F32) + b_ref[...]


def _ada(c_all, w_ada, b_ada):
    n, d = c_all.shape
    width = w_ada.shape[1]
    bn = d
    return pl.pallas_call(
        _ada_kernel,
        grid=(width // bn,),
        in_specs=[pl.BlockSpec((n, d), lambda j: (0, 0)),
                  pl.BlockSpec((d, bn), lambda j: (0, j)),
                  pl.BlockSpec((1, bn), lambda j: (0, j))],
        out_specs=pl.BlockSpec((n, bn), lambda j: (0, j)),
        out_shape=jax.ShapeDtypeStruct((n, width), F32),
        compiler_params=pltpu.CompilerParams(vmem_limit_bytes=VMEM_LIMIT_BYTES),
        name="ada",
    )(c_all, w_ada, b_ada.reshape(1, width))


def _rms(x, width):
    return lax.rsqrt(jnp.sum(x * x, axis=-1, keepdims=True) * (1.0 / width) + EPS)


def _pre_norm(x_ref, mod_ref, n1g_ref):
    nb, rows, d = x_ref.shape
    x = x_ref[...]
    mod = mod_ref[...]
    sh1, sc1 = mod[:, 0:1], mod[:, 1:2]
    h = x * _rms(x, d) * (n1g_ref[...] * (1.0 + sc1)) + sh1
    return h.reshape(nb * rows, d).astype(BF16)


def _gelu_tanh(x):
    k = -2.0 * math.sqrt(2.0 / math.pi) * LOG2E
    e = jnp.exp2(x * (x * x * (k * 0.044715) + k))
    return x * pl.reciprocal(1.0 + e, approx=False)


def _half_sums(a, lo):
    sq = a * a
    ss_all = jnp.sum(sq, axis=-1, keepdims=True)
    ss_lo = jnp.sum(jnp.where(lo, sq, 0.0), axis=-1, keepdims=True)
    return jnp.where(lo, ss_lo, ss_all - ss_lo)


def _pre_heads(z_ref, ws_ref, bs_ref, qng_ref, wuq_ref, kvg_ref, qrow_ref, krg_ref, tabq_ref, tabk_ref,
               ya_ref, q_ref, ckv_ref, kr_ref, maybe_kv, maybe_vraw_ref, mix_len, tick=lambda: None):
    nb, tile_rows = ya_ref.shape[0], ya_ref.shape[1]
    rows = z_ref.shape[0]
    per_head = lambda a: a.reshape(nb, tile_rows, HEAD_BLOCK)
    o_v, o_q, o_kv, o_kr = GMLP_WIDTH, 2 * GMLP_WIDTH, 2 * GMLP_WIDTH + Q_LORA, 2 * GMLP_WIDTH + Q_LORA + KV_LORA
    lane = lax.broadcasted_iota(jnp.int32, (1, LANES), 1)
    lo = lane < LANES // 2

    kr = z_ref[:, o_kr:o_kr + KR_BLOCK]
    kr_ss = jnp.sum(jnp.where(lane < QK_ROPE, kr * kr, 0.0), axis=-1, keepdims=True)
    kr_p = kr * lax.rsqrt(kr_ss * (1.0 / QK_ROPE) + EPS) * krg_ref[...] * tabk_ref[...]
    krope = kr_p + pltpu.roll(kr_p, LANES - QK_ROPE, axis=1)
    if kr_ref.shape[-1] == QK_ROPE:
        kr_ref[...] = krope[:, :QK_ROPE].reshape(kr_ref.shape)
    else:
        kr_ref[0] = krope.T[:QK_ROPE, :]

    ckv_raw = z_ref[:, o_kv:o_kr]
    ckv = ckv_raw * _rms(ckv_raw, KV_LORA) * kvg_ref[...]
    ckv_ref[...] = ckv.reshape(ckv_ref.shape)
    cq_raw = z_ref[:, o_q:o_kv]
    cq = cq_raw * _rms(cq_raw, Q_LORA) * qng_ref[...]
    tick()
    q = jnp.dot(cq.astype(BF16), wuq_ref[...], preferred_element_type=F32)
    blocks = [q[:, hd * HEAD_BLOCK:(hd + 1) * HEAD_BLOCK] for hd in range(N_HEADS)]
    if maybe_kv is not None:
        wuk_ref, wuvt_ref, k_ref, vt_ref = maybe_kv
        ckv_b = ckv.astype(BF16)
        kp_all = jnp.dot(ckv_b, wuk_ref[...], preferred_element_type=F32)
        vt_ref[0, 0] = lax.dot_general(wuvt_ref[...], ckv_b, _NT, preferred_element_type=F32).astype(vt_ref.dtype)
        blocks += [kp_all[:, pair * LANES:(pair + 1) * LANES] for pair in range(N_HEADS // 2)]

    tick()
    u = _gelu_tanh(z_ref[:, :o_v])
    tick()
    v = _gelu_tanh(z_ref[:, o_v:o_q])
    if maybe_vraw_ref is not None:
        maybe_vraw_ref[...] = v.reshape(maybe_vraw_ref.shape)
    tick()

    vb = v.astype(BF16)
    pi = lax.broadcasted_iota(jnp.int32, (mix_len, mix_len), 0) // CHUNK
    pj = lax.broadcasted_iota(jnp.int32, (mix_len, mix_len), 1) // CHUNK
    for g in range(GMLP_GROUPS):
        wsg = jnp.where(pj <= pi, ws_ref[g, :mix_len, :mix_len], 0.0).astype(BF16)
        bg = bs_ref[:mix_len, g:g + 1]
        cols = slice(g * GMLP_HEAD, (g + 1) * GMLP_HEAD)
        for c in range(rows // mix_len):
            rws = slice(c * mix_len, (c + 1) * mix_len)
            mixed = jnp.dot(wsg, vb[rws, cols], preferred_element_type=F32) + bg
            bi, off = divmod(c * mix_len, tile_rows)
            ya_ref[bi, off:off + mix_len, cols] = (u[rws, cols] * mixed).astype(ya_ref.dtype)
        tick()

    qfac = tabq_ref[...] * qrow_ref[...]
    scales = []
    for blk in blocks:
        scales.append(lax.rsqrt(_half_sums(blk, lo) + QK_NOPE * EPS))
    for hd in range(N_HEADS):
        fac = qfac[:, (hd % 2) * LANES:(hd % 2 + 1) * LANES]
        q_ref[:, hd] = per_head((blocks[hd] * scales[hd] * fac).astype(q_ref.dtype))
    if maybe_kv is not None:
        krope_b = krope.astype(k_ref.dtype)
        for pair in range(N_HEADS // 2):
            kn = (blocks[N_HEADS + pair] * scales[N_HEADS + pair]).astype(k_ref.dtype)
            k_ref[:, 2 * pair] = per_head(jnp.where(lo, kn, krope_b))
            k_ref[:, 2 * pair + 1] = per_head(jnp.where(lo, krope_b, kn))


_PRE_SINGLE_CONSTS = ("n1g", "w_in", "w_s", "b_s", "qng", "w_uq_flat", "kvg", "qrow_flat", "krg")
_PRE_PIPELINED_CONSTS = ("n1g", "w_in", "w_s", "b_s", "qng", "w_uq", "kvg", "qrow", "krg", "w_uk", "w_uvt")


def _pre_single_kernel(x_ref, mod_ref, n1g_ref, win_ref, ws_ref, bs_ref, qng_ref, wuq_ref, kvg_ref, qrow_ref, krg_ref,
                       tabq_ref, tabk_ref, ya_ref, q_ref, ckv_ref, kr_ref, vraw_ref, z_ref, *, mix_len):
    z_ref[...] = jnp.dot(_pre_norm(x_ref, mod_ref, n1g_ref), win_ref[...], preferred_element_type=F32)
    _pre_heads(z_ref, ws_ref, bs_ref, qng_ref, wuq_ref, kvg_ref, qrow_ref, krg_ref, tabq_ref, tabk_ref,
               ya_ref, q_ref, ckv_ref, kr_ref, None, vraw_ref, mix_len)


def _pre_pipelined_kernel(x_ref, mod_ref, n1g_ref, win_ref, ws_ref, bs_ref, qng_ref, wuq_ref, kvg_ref, qrow_ref,
                          krg_ref, wuk_ref, wuvt_ref, tabq_ref, tabk_ref,
                          ya_ref, q_ref, k_ref, vt_ref, ckv_ref, kr_ref, z_ref, *, mix_len):
    i = pl.program_id(0)

    @pl.when(i == 0)
    def _():
        z_ref[...] = jnp.zeros(z_ref.shape, F32)

    for parity in range(2):
        @pl.when(i % 2 == parity)
        def _():
            normed = []

            def project(c):
                if not normed:
                    normed.append(_pre_norm(x_ref, mod_ref, n1g_ref))
                cols = slice(c * PROJ_CHUNK, (c + 1) * PROJ_CHUNK)
                z_ref[parity, :, cols] = jnp.dot(normed[0], win_ref[:, cols], preferred_element_type=F32)

            chunks = list(range(IN_WIDTH_PADDED // PROJ_CHUNK))
            pending = iter([chunks[:2]] + [[c] for c in chunks[2:]])

            def tick():
                for c in next(pending, []):
                    project(c)

            _pre_heads(z_ref.at[1 - parity], ws_ref, bs_ref, qng_ref, wuq_ref, kvg_ref, qrow_ref, krg_ref,
                       tabq_ref, tabk_ref, ya_ref, q_ref, ckv_ref, kr_ref,
                       (wuk_ref, wuvt_ref, k_ref, vt_ref), None, mix_len, tick)
            for group in pending:
                for c in group:
                    project(c)


def _pre_single(x, mod, wts, tabq, tabk, *, mix_len):
    b, t, d = x.shape
    whole = lambda *shape: pl.BlockSpec(shape, lambda i: (0,) * len(shape))
    consts = [wts[n] for n in _PRE_SINGLE_CONSTS]
    return pl.pallas_call(
        functools.partial(_pre_single_kernel, mix_len=mix_len),
        grid=(1,),
        in_specs=([whole(b, t, d), whole(*mod.shape)] + [_const_spec(a.shape) for a in consts]
                  + [whole(*tabq.shape), whole(*tabk.shape)]),
        out_specs=[whole(b, t, GMLP_WIDTH), whole(b, N_HEADS, t, HEAD_BLOCK), whole(b, t, KV_LORA),
                   whole(b, t, QK_ROPE), whole(b, t, GMLP_WIDTH)],
        out_shape=[jax.ShapeDtypeStruct((b, t, GMLP_WIDTH), BF16),
                   jax.ShapeDtypeStruct((b, N_HEADS, t, HEAD_BLOCK), BF16),
                   jax.ShapeDtypeStruct((b, t, KV_LORA), F32),
                   jax.ShapeDtypeStruct((b, t, QK_ROPE), F32),
                   jax.ShapeDtypeStruct((b, t, GMLP_WIDTH), F32)],
        scratch_shapes=[pltpu.VMEM((b * t, IN_WIDTH_PADDED), F32)],
        compiler_params=pltpu.CompilerParams(
            dimension_semantics=("arbitrary",), vmem_limit_bytes=VMEM_LIMIT_BYTES),
        name="pre_single",
    )(x, mod, *consts, tabq, tabk)


def _pre_pipelined(x, mod, wts, tabq, tabk, *, rows, mix_len):
    b, t, d = x.shape
    per_batch = t // rows
    n_tiles = b * per_batch
    cur = lambda i: jnp.minimum(i, n_tiles - 1)
    prev = lambda i: jnp.maximum(i - 1, 0)
    pb, pj = (lambda i: prev(i) // per_batch), (lambda i: prev(i) % per_batch)
    tok_out = lambda w: pl.BlockSpec((1, rows, w), lambda i: (pb(i), pj(i), 0))
    head = pl.BlockSpec((1, N_HEADS, rows, HEAD_BLOCK), lambda i: (pb(i), 0, pj(i), 0))
    tab = lambda a: pl.BlockSpec((rows, a.shape[1]), lambda i: (pj(i), 0))
    consts = [wts[n] for n in _PRE_PIPELINED_CONSTS]
    return pl.pallas_call(
        functools.partial(_pre_pipelined_kernel, mix_len=mix_len),
        grid=(n_tiles + 1,),
        in_specs=([pl.BlockSpec((1, rows, d), lambda i: (cur(i) // per_batch, cur(i) % per_batch, 0)),
                   pl.BlockSpec((1,) + mod.shape[1:], lambda i: (cur(i) // per_batch, 0, 0))]
                  + [_const_spec(a.shape) for a in consts] + [tab(tabq), tab(tabk)]),
        out_specs=[tok_out(GMLP_WIDTH), head, head,
                   pl.BlockSpec((1, 1, MLA_WIDTH, rows), lambda i: (pb(i), pj(i), 0, 0)),
                   tok_out(KV_LORA),
                   pl.BlockSpec((1, QK_ROPE, rows), lambda i: (pb(i), 0, pj(i)))],
        out_shape=[jax.ShapeDtypeStruct((b, t, GMLP_WIDTH), BF16),
                   jax.ShapeDtypeStruct((b, N_HEADS, t, HEAD_BLOCK), BF16),
                   jax.ShapeDtypeStruct((b, N_HEADS, t, HEAD_BLOCK), BF16),
                   jax.ShapeDtypeStruct((b, per_batch, MLA_WIDTH, rows), BF16),
                   jax.ShapeDtypeStruct((b, t, KV_LORA), F32),
                   jax.ShapeDtypeStruct((b, QK_ROPE, t), F32)],
        scratch_shapes=[pltpu.VMEM((2, rows, IN_WIDTH_PADDED), F32)],
        compiler_params=pltpu.CompilerParams(
            dimension_semantics=("arbitrary",), vmem_limit_bytes=VMEM_LIMIT_BYTES),
        name="pre_pipelined",
    )(x, mod, *consts, tabq, tabk)


def _aligned(index, multiple):
    return index if isinstance(index, int) else pl.multiple_of(index, multiple)


def _attn_kernel(q_ref, k_ref, vt_ref, o_ref, acc_ref, *maybe_m_ref, tq, online):
    t = q_ref.shape[2]
    unrolled = not online
    hb = tq // 2
    mb = hb // 2
    key_chunk = lax.broadcasted_iota(jnp.int32, (mb, mb), 0) // CHUNK
    query_chunk = lax.broadcasted_iota(jnp.int32, (mb, mb), 1) // CHUNK
    visible = key_chunk <= query_chunk
    ones = jnp.ones((DEN_ROWS, hb), BF16)

    def operands(hh, q0, qo, rows, kt, c0, keys):
        q = q_ref[0, hh, pl.ds(_aligned(q0 + qo, mb), rows), :]
        k = k_ref[0, hh, pl.ds(_aligned(kt * hb + c0, mb), keys), :]
        lhs = jnp.concatenate([vt_ref[0, kt, hh * V_HEAD:(hh + 1) * V_HEAD, c0:c0 + keys], ones[:, :keys]], axis=0)
        return lhs, lax.dot_general(k, q, _NT, preferred_element_type=F32)

    def unshifted(hh, q0, qo, rows, kt, c0, keys, masked=False):
        lhs, s = operands(hh, q0, qo, rows, kt, c0, keys)
        p = jnp.exp2(s)
        if masked:
            p = jnp.where(visible, p, 0.0)
        return jnp.dot(lhs, p.astype(BF16), preferred_element_type=F32)

    def online_update(hh, q0, qo, rows, kt, c0, keys, masked=False):
        m_ref = maybe_m_ref[0]
        lhs, s = operands(hh, q0, qo, rows, kt, c0, keys)
        if masked:
            s = jnp.where(visible, s, _NEG)
        cols = slice(qo, qo + rows)
        m_old = m_ref[hh, :, cols]
        m_new = jnp.maximum(m_old, jnp.max(s, axis=0, keepdims=True))
        p = jnp.exp2(s - m_new).astype(BF16)
        acc_ref[hh, :, cols] = (jnp.exp2(m_old - m_new) * acc_ref[hh, :, cols]
                                + jnp.dot(lhs, p, preferred_element_type=F32))
        m_ref[hh, :, cols] = m_new

    diagonal = [(0, mb, 0, 0, mb, True), (mb, mb, 0, 0, mb, False), (mb, mb, 0, mb, mb, True),
                (hb, hb, 0, 0, hb, False),
                (hb, mb, 1, 0, mb, True), (hb + mb, mb, 1, 0, mb, False), (hb + mb, mb, 1, mb, mb, True)]

    def q_tile(qi, carry):
        q0 = qi * tq
        acc_ref[...] = jnp.zeros(acc_ref.shape, F32)
        if online:
            maybe_m_ref[0][...] = jnp.full(maybe_m_ref[0].shape, -jnp.inf, F32)

        def body(j, c):
            tiles = [(hh, 2 * j + half) for half in range(2) for hh in range(2)]
            if online:
                for hh, kt in tiles:
                    online_update(hh, q0, 0, tq, kt, 0, hb)
            else:
                staged = [operands(hh, q0, 0, tq, kt, 0, hb) for hh, kt in tiles]
                probs = [jnp.exp2(s).astype(BF16) for _, s in staged]
                for hh in range(2):
                    acc_ref[hh] += sum(jnp.dot(lhs, p, preferred_element_type=F32)
                                       for (th, _), (lhs, _), p in zip(tiles, staged, probs) if th == hh)
            return c

        if unrolled:
            for j in range(qi):
                body(j, 0)
        else:
            lax.fori_loop(0, qi, body, 0)

        totals = []
        for hh in range(2):
            if online:
                for qo, rows, kt_off, c0, keys, masked in diagonal:
                    online_update(hh, q0, qo, rows, qi * 2 + kt_off, c0, keys, masked)
                totals.append([acc_ref[hh, :, g * mb:(g + 1) * mb] for g in range(tq // mb)])
            else:
                staged = [operands(hh, q0, qo, rows, qi * 2 + kt_off, c0, keys)
                          for qo, rows, kt_off, c0, keys, _ in diagonal]
                probs = [jnp.exp2(s) for _, s in staged]
                probs = [jnp.where(visible, p, 0.0) if blk[5] else p for p, blk in zip(probs, diagonal)]
                parts = [jnp.dot(lhs, p.astype(BF16), preferred_element_type=F32)
                         for (lhs, _), p in zip(staged, probs)]
                groups = [acc_ref[hh, :, g * mb:(g + 1) * mb] for g in range(tq // mb)]
                for (qo, rows, *_), part in zip(diagonal, parts):
                    for g in range(rows // mb):
                        groups[qo // mb + g] = groups[qo // mb + g] + part[:, g * mb:(g + 1) * mb]
                totals.append(groups)
        for g in range(tq // mb):
            o_t = [tot[g][:V_HEAD] * pl.reciprocal(tot[g][V_HEAD:V_HEAD + 1], approx=False) for tot in totals]
            o_ref[0, pl.ds(_aligned(q0 + g * mb, mb), mb), :] = jnp.concatenate(o_t, axis=0).T.astype(o_ref.dtype)
        return carry

    if unrolled:
        for qi in range(t // tq):
            q_tile(qi, 0)
    else:
        lax.fori_loop(0, t // tq, q_tile, 0)


def _attn_prompt(q, k, vt, score_bound):
    b, nh, t, _ = q.shape
    n_kt, _, hb = vt.shape[1:]
    assert hb == ATTN_TK and n_kt * hb == t
    qk = pl.BlockSpec((1, 2, t, HEAD_BLOCK), lambda i, j: (i, j, 0, 0))

    def call(online):
        scratch = [pltpu.VMEM((2, ACC_ROWS, ATTN_TQ), F32)]
        if online:
            scratch.append(pltpu.VMEM((2, 1, ATTN_TQ), F32))
        return pl.pallas_call(
            functools.partial(_attn_kernel, tq=ATTN_TQ, online=online),
            grid=(b, nh // 2),
            in_specs=[qk, qk, pl.BlockSpec((1, n_kt, 2 * V_HEAD, hb), lambda i, j: (i, 0, j, 0))],
            out_specs=pl.BlockSpec((1, t, 2 * V_HEAD), lambda i, j: (i, 0, j)),
            out_shape=jax.ShapeDtypeStruct((b, t, MLA_WIDTH), BF16),
            scratch_shapes=scratch,
            compiler_params=pltpu.CompilerParams(
                dimension_semantics=("parallel", "parallel"), vmem_limit_bytes=VMEM_LIMIT_BYTES),
            name="attn_prompt_online" if online else "attn_prompt_bounded")

    return lax.cond(score_bound <= MAX_UNSHIFTED_SCORE, call(False), call(True), q, k, vt)


def _pair_out(acc0, acc1):
    first = lax.broadcasted_iota(jnp.int32, (1, LANES), 1) < V_HEAD
    num = jnp.where(first, pltpu.roll(acc0, V_HEAD, axis=1), acc1)
    den = jnp.where(first, acc0, pltpu.roll(acc1, V_HEAD, axis=1))
    return num * pl.reciprocal(den, approx=False)


def _attn_sample_kernel(q_ref, past_ckv_ref, past_kr_ref, new_ckv_ref, new_kr_ref, wukv_ref, o_ref):
    lane = lax.broadcasted_iota(jnp.int32, (1, LANES), 1)
    is_nope = lane < QK_NOPE
    spread = (lax.broadcasted_iota(jnp.int32, (QK_ROPE, LANES), 1) % QK_ROPE
              == lax.broadcasted_iota(jnp.int32, (QK_ROPE, LANES), 0)).astype(BF16)
    one_b = jnp.ones((1, LANES), BF16)

    def segment(ckv_ref, kr_ref):
        kv = jnp.dot(ckv_ref[0].astype(BF16), wukv_ref[...], preferred_element_type=F32)
        krope_b = jnp.dot(kr_ref[0].astype(BF16), spread, preferred_element_type=F32).astype(BF16)
        return kv, krope_b

    def head_kv(seg, hd):
        kv, krope_b = seg
        kvh = kv[:, hd * HEAD_BLOCK:(hd + 1) * HEAD_BLOCK]
        ks = jnp.sum(jnp.where(is_nope, kvh * kvh, 0.0), axis=-1, keepdims=True)
        kn = kvh * lax.rsqrt(ks + QK_NOPE * EPS)
        return (jnp.where(is_nope, kn.astype(BF16), krope_b), jnp.where(is_nope, one_b, kvh.astype(BF16)))

    past, new = segment(past_ckv_ref, past_kr_ref), segment(new_ckv_ref, new_kr_ref)
    heads = range(N_HEADS)
    kv = [(head_kv(past, hd), head_kv(new, hd)) for hd in heads]
    scores = [[lax.dot_general(q_ref[0, hd], k, _NT, preferred_element_type=F32) for k, _ in kv[hd]] for hd in heads]
    peaks = [jnp.maximum(*(jnp.max(s, axis=-1, keepdims=True) for s in scores[hd])) for hd in heads]
    probs = [[jnp.exp2(s - peaks[hd]).astype(BF16) for s in scores[hd]] for hd in heads]
    accs = [sum(jnp.dot(p, v, preferred_element_type=F32) for p, (_, v) in zip(probs[hd], kv[hd])) for hd in heads]
    for hd in range(0, N_HEADS, 2):
        o_ref[0, :, hd * V_HEAD:(hd + 2) * V_HEAD] = _pair_out(accs[hd], accs[hd + 1]).astype(o_ref.dtype)


def _attn_sample(q, past_ckv, past_kr, new_ckv, new_kr, w_ukv):
    b, nh, t, _ = q.shape
    per_batch = lambda a: pl.BlockSpec((1,) + a.shape[1:], lambda i: (i,) + (0,) * (a.ndim - 1))
    return pl.pallas_call(
        _attn_sample_kernel,
        grid=(b,),
        in_specs=[per_batch(q), per_batch(past_ckv), per_batch(past_kr), per_batch(new_ckv), per_batch(new_kr),
                  _const_spec(w_ukv.shape)],
        out_specs=pl.BlockSpec((1, t, MLA_WIDTH), lambda i: (i, 0, 0)),
        out_shape=jax.ShapeDtypeStruct((b, t, MLA_WIDTH), BF16),
        compiler_params=pltpu.CompilerParams(
            dimension_semantics=("parallel",), vmem_limit_bytes=VMEM_LIMIT_BYTES),
        name="attn_sample",
    )(q, past_ckv, past_kr, new_ckv, new_kr, w_ukv)


def _post_mixer(x_ref, ya_ref, yb_ref, mod_ref, wout_ref, n2g_ref):
    nb, rows, d = x_ref.shape
    flat = lambda a: a.reshape(nb * rows, a.shape[-1])
    mod = mod_ref[...]
    g1, sh2, sc2 = mod[:, 2:3], mod[:, 3:4], mod[:, 4:5]
    mixer = jnp.concatenate([flat(ya_ref[...]), flat(yb_ref[...])], axis=-1)
    mix = jnp.dot(mixer, wout_ref[...], preferred_element_type=F32)
    x1 = x_ref[...] + g1 * mix.reshape(nb, rows, d)
    h2 = flat(x1 * _rms(x1, d) * (n2g_ref[...] * (1.0 + sc2)) + sh2).astype(BF16)
    return x1, h2


def _post_ffn_chunk(h2, c, wfi_ref, wfo_ref, acc_ref):
    d_ff = wfo_ref.shape[0]
    gate = jnp.dot(h2, wfi_ref[:, c * FFN_CHUNK:(c + 1) * FFN_CHUNK], preferred_element_type=F32)
    up = jnp.dot(h2, wfi_ref[:, d_ff + c * FFN_CHUNK:d_ff + (c + 1) * FFN_CHUNK], preferred_element_type=F32)
    act = (gate * jax.nn.sigmoid(gate) * up).astype(BF16)
    part = jnp.dot(act, wfo_ref[c * FFN_CHUNK:(c + 1) * FFN_CHUNK, :], preferred_element_type=F32)
    if c == 0:
        acc_ref[...] = part
    else:
        acc_ref[...] += part


def _post_kernel(x_ref, ya_ref, yb_ref, mod_ref, wout_ref, n2g_ref, wfi_ref, wfo_ref, o_ref, acc_ref):
    x1, h2 = _post_mixer(x_ref, ya_ref, yb_ref, mod_ref, wout_ref, n2g_ref)
    o_ref[...] = x1
    for c in range(wfo_ref.shape[0] // FFN_CHUNK):
        _post_ffn_chunk(h2, c, wfi_ref, wfo_ref, acc_ref)
    o_ref[...] = o_ref[...] + mod_ref[:, 5:6] * acc_ref[...].reshape(o_ref.shape)


def _post(x, ya, yb, mod, wts, *, nb, rows):
    b, t, d = x.shape
    tok = lambda w: pl.BlockSpec((nb, rows, w), lambda i, j: (i, j, 0))
    consts = [wts[n] for n in ("w_out", "n2g", "w_ffn_in", "w_ffn_out")]
    return pl.pallas_call(
        _post_kernel,
        grid=(b // nb, t // rows),
        in_specs=[tok(d), tok(GMLP_WIDTH), tok(MLA_WIDTH),
                  pl.BlockSpec((nb,) + mod.shape[1:], lambda i, j: (i, 0, 0)),
                  _const_spec(consts[0].shape), _const_spec(consts[1].shape),
                  _const_spec(consts[2].shape), _const_spec(consts[3].shape)],
        out_specs=tok(d),
        out_shape=jax.ShapeDtypeStruct((b, t, d), F32),
        scratch_shapes=[pltpu.VMEM((nb * rows, d), F32)],
        compiler_params=pltpu.CompilerParams(
            dimension_semantics=("parallel", "parallel"), vmem_limit_bytes=VMEM_LIMIT_BYTES),
        name="post",
    )(x, ya, yb, mod, consts[0], consts[1], consts[2], consts[3])


def _rope_tables(first_pos, n, *, mirrored_odd, repeat=1):
    half = QK_ROPE // 2
    inv = ROPE_THETA ** (-np.arange(half, dtype=np.float64) / half)
    ang = (first_pos + np.arange(n, dtype=np.float64))[:, None] * inv[None, :]
    cos, sin = np.cos(ang), np.sin(ang)
    rot = np.concatenate([cos, cos, -sin, sin], axis=-1)
    nope = np.ones((n, QK_NOPE))
    tabq = (SCALE * LOG2E) * np.concatenate([nope, rot] + ([rot, nope] if mirrored_odd else [nope, rot]), axis=-1)
    tabk = np.concatenate([rot, rot], axis=-1)
    return jnp.asarray(np.tile(tabq, (repeat, 1)), F32), jnp.asarray(np.tile(tabk, (repeat, 1)), F32)


def _prep_weights(w_in, w_s, b_s, q_norm_g, w_uq, kv_norm_g, w_ukv, qn_g, qr_g, kn_g, kr_g,
                  norm1_g, w_out, norm2_g, w_ffn_in, w_ffn_out):
    o_kr = 2 * GMLP_WIDTH + Q_LORA + KV_LORA
    w_kr = w_in[:, o_kr:o_kr + QK_ROPE]
    w_kr_sw = _swap_halves(w_kr)
    w_in_p = jnp.concatenate([w_in[:, :o_kr], w_kr, w_kr_sw, w_kr, w_kr_sw], axis=-1)
    d_q = w_uq.shape[0]
    wq = w_uq.reshape(d_q, N_HEADS, QK_HEAD)
    wq_n, wq_r = wq[:, :, :QK_NOPE], wq[:, :, QK_NOPE:]
    wq_plain = jnp.concatenate([wq_n, wq_r, _swap_halves(wq_r)], axis=-1)
    wq_mirror = jnp.concatenate([wq_r, _swap_halves(wq_r), wq_n], axis=-1)
    odd = (jnp.arange(N_HEADS) % 2 == 1)[None, :, None]
    w_uq_flat = wq_plain.reshape(d_q, N_HEADS * HEAD_BLOCK)
    w_uq_pair = jnp.where(odd, wq_mirror, wq_plain).reshape(d_q, N_HEADS * HEAD_BLOCK)
    wkv = w_ukv.reshape(KV_LORA, N_HEADS, QK_NOPE + V_HEAD)
    w_uk = wkv[:, :, :QK_NOPE].reshape(KV_LORA, N_HEADS * QK_NOPE)
    w_uvt = wkv[:, :, QK_NOPE:].reshape(KV_LORA, MLA_WIDTH).T
    root = math.sqrt(QK_NOPE)
    q_nope_row = qn_g * kn_g * (root * root)
    q_rope_row = jnp.concatenate([qr_g, _swap_halves(qr_g)]) * root
    qrow_plain = jnp.concatenate([q_nope_row, q_rope_row])
    qrow_mirror = jnp.concatenate([q_rope_row, q_nope_row])
    row = lambda a: a.reshape(1, -1).astype(F32)
    return {
        "n1g": row(norm1_g), "w_in": w_in_p.astype(BF16), "w_s": w_s.astype(F32), "b_s": b_s.T.astype(F32),
        "qng": row(q_norm_g), "kvg": row(kv_norm_g),
        "w_uq": w_uq_pair.astype(BF16), "w_uq_flat": w_uq_flat.astype(BF16),
        "qrow": row(jnp.concatenate([qrow_plain, qrow_mirror])),
        "qrow_flat": row(jnp.concatenate([qrow_plain, qrow_plain])),
        "w_ukv": w_ukv.astype(BF16), "w_uk": w_uk.astype(BF16), "w_uvt": w_uvt.astype(BF16),
        "krg": row(jnp.concatenate([kr_g, _swap_halves(kr_g), kr_g, _swap_halves(kr_g)])),
        "w_out": w_out.astype(BF16), "n2g": row(norm2_g),
        "w_ffn_in": w_ffn_in.astype(BF16), "w_ffn_out": w_ffn_out.astype(BF16),
    }


def kernel(x_prompt, x_sample, cache_ckv, cache_krope, c_prompt, c_sample, w_ada, b_ada, norm1_g, w_in, w_s, b_s, q_norm_g, w_uq, kv_norm_g, w_ukv, qn_g, qr_g, kn_g, kr_g, w_out, norm2_g, w_ffn_in, w_ffn_out):
    depth = w_in.shape[0]
    assert depth == 1, "single trunk layer"
    assert PRE_ROWS == ATTN_TK
    bp, tp, d = x_prompt.shape
    bs, ts, _ = x_sample.shape
    past_len = cache_ckv.shape[2]

    wts = _prep_weights(w_in[0], w_s[0], b_s[0], q_norm_g[0], w_uq[0], kv_norm_g[0], w_ukv[0], qn_g[0], qr_g[0],
                        kn_g[0], kr_g[0], norm1_g[0], w_out[0], norm2_g[0], w_ffn_in[0], w_ffn_out[0])
    mod = _ada(jnp.concatenate([c_prompt, c_sample], axis=0), w_ada[0], b_ada[0]).reshape(bp + bs, 6, d)
    mod_p, mod_s = mod[:bp], mod[bp:]

    tabq_p, tabk_p = _rope_tables(0, tp, mirrored_odd=True)
    ya_p, q_p, k_p, vt_p, ckv_p, krt_p = _pre_pipelined(x_prompt, mod_p, wts, tabq_p, tabk_p,
                                                        rows=PRE_ROWS, mix_len=GMLP_CHUNK)
    gmax = lambda g: jnp.max(jnp.abs(g))
    score_bound = SCALE * (QK_NOPE * gmax(qn_g) * gmax(kn_g) + QK_ROPE * gmax(qr_g) * gmax(kr_g))
    yb_p = _attn_prompt(q_p, k_p, vt_p, score_bound)
    y_p = _post(x_prompt, ya_p, yb_p, mod_p, wts, nb=1, rows=POST_ROWS)

    tabq_s, tabk_s = _rope_tables(past_len, ts, mirrored_odd=False, repeat=bs)
    ya_s, q_s, ckv_s, kr_s, v_s = _pre_single(x_sample, mod_s, wts, tabq_s, tabk_s, mix_len=min(ts, GMLP_CHUNK))
    yb_s = _attn_sample(q_s, cache_ckv[0], cache_krope[0], ckv_s, kr_s, wts["w_ukv"])
    y_s = _post(x_sample, ya_s, yb_s, mod_s, wts, nb=bs, rows=ts)

    return (y_p, y_s, ckv_p[None], jnp.swapaxes(krt_p, 1, 2)[None], ckv_s[None], kr_s[None], v_s[None])
```

```python
import functools
import math

import jax
import jax.numpy as jnp
import numpy as np
from jax import lax
from jax.experimental import pallas as pl
from jax.experimental.pallas import tpu as pltpu

F32 = jnp.float32
BF16 = jnp.bfloat16

CHUNK = 64
GMLP_CHUNK = 128
GMLP_GROUPS = 4
GMLP_HEAD = 128
GMLP_WIDTH = GMLP_GROUPS * GMLP_HEAD
N_HEADS = 8
QK_NOPE = 64
QK_ROPE = 32
QK_HEAD = QK_NOPE + QK_ROPE
V_HEAD = 64
Q_LORA = 384
KV_LORA = 256
MLA_WIDTH = N_HEADS * V_HEAD
ROPE_THETA = 10000.0
EPS = 1e-6
SCALE = QK_HEAD ** -0.5

LANES = 128
HEAD_BLOCK = LANES
KR_BLOCK = LANES
IN_WIDTH_PADDED = 2 * GMLP_WIDTH + Q_LORA + KV_LORA + KR_BLOCK
VMEM_LIMIT_BYTES = 56 * 1024 * 1024

PRE_ROWS = 512
POST_ROWS = 1024
FFN_CHUNK = 256
PROJ_CHUNK = 256
ATTN_TQ = 1024
ATTN_TK = ATTN_TQ // 2
DEN_ROWS = 16
ACC_ROWS = V_HEAD + DEN_ROWS

_NEG = float(jnp.finfo(jnp.float32).min)
_NT = (((1,), (1,)), ((), ()))
LOG2E = math.log2(math.e)
MAX_UNSHIFTED_SCORE = 60.0


def _swap_halves(a, axis=-1):
    n = a.shape[axis] // 2
    lo = lax.slice_in_dim(a, 0, n, axis=axis)
    hi = lax.slice_in_dim(a, n, 2 * n, axis=axis)
    return jnp.concatenate([hi, lo], axis=axis)


def _const_spec(shape):
    nd = len(shape)
    return pl.BlockSpec(shape, lambda *_: (0,) * nd, pipeline_mode=pl.Buffered(1))


def _ada_kernel(c_ref, w_ref, b_ref, o_ref):
    c = c_ref[...]
    a = c * jax.nn.sigmoid(c)
    w = w_ref[...]
    a_hi, w_hi = a.astype(BF16), w.astype(BF16)
    a_lo, w_lo = (a - a_hi.astype(F32)).astype(BF16), (w - w_hi.astype(F32)).astype(BF16)
    n = a.shape[0]
    both = jnp.dot(jnp.concatenate([a_hi, a_lo], axis=0), w_hi, preferred_element_type=F32)
    o_ref[...] = both[:n] + both[n:] + jnp.dot(a_hi, w_lo, preferred_element_type=F32) + b_ref[...]


def _ada(c_all, w_ada, b_ada):
    n, d = c_all.shape
    width = w_ada.shape[1]
    bn = d
    return pl.pallas_call(
        _ada_kernel,
        grid=(width // bn,),
        in_specs=[pl.BlockSpec((n, d), lambda j: (0, 0)),
                  pl.BlockSpec((d, bn), lambda j: (0, j)),
                  pl.BlockSpec((1, bn), lambda j: (0, j))],
        out_specs=pl.BlockSpec((n, bn), lambda j: (0, j)),
        out_shape=jax.ShapeDtypeStruct((n, width), F32),
        compiler_params=pltpu.CompilerParams(vmem_limit_bytes=VMEM_LIMIT_BYTES),
        name="ada",
    )(c_all, w_ada, b_ada.reshape(1, width))


def _rms(x, width):
    return lax.rsqrt(jnp.sum(x * x, axis=-1, keepdims=True) * (1.0 / width) + EPS)


def _pre_norm(x_ref, mod_ref, n1g_ref):
    nb, rows, d = x_ref.shape
    x = x_ref[...]
    mod = mod_ref[...]
    sh1, sc1 = mod[:, 0:1], mod[:, 1:2]
    h = x * _rms(x, d) * (n1g_ref[...] * (1.0 + sc1)) + sh1
    return h.reshape(nb * rows, d).astype(BF16)


def _gelu_tanh(x):
    k = -2.0 * math.sqrt(2.0 / math.pi) * LOG2E
    e = jnp.exp2(x * (x * x * (k * 0.044715) + k))
    return x * pl.reciprocal(1.0 + e, approx=False)


def _half_sums(a, lo):
    sq = a * a
    ss_all = jnp.sum(sq, axis=-1, keepdims=True)
    ss_lo = jnp.sum(jnp.where(lo, sq, 0.0), axis=-1, keepdims=True)
    return jnp.where(lo, ss_lo, ss_all - ss_lo)


def _pre_heads(z_ref, ws_ref, bs_ref, qng_ref, wuq_ref, kvg_ref, qrow_ref, krg_ref, tabq_ref, tabk_ref,
               ya_ref, q_ref, ckv_ref, kr_ref, maybe_kv, maybe_vraw_ref, mix_len, tick=lambda: None):
    nb, tile_rows = ya_ref.shape[0], ya_ref.shape[1]
    rows = z_ref.shape[0]
    per_head = lambda a: a.reshape(nb, tile_rows, HEAD_BLOCK)
    o_v, o_q, o_kv, o_kr = GMLP_WIDTH, 2 * GMLP_WIDTH, 2 * GMLP_WIDTH + Q_LORA, 2 * GMLP_WIDTH + Q_LORA + KV_LORA
    lane = lax.broadcasted_iota(jnp.int32, (1, LANES), 1)
    lo = lane < LANES // 2

    kr = z_ref[:, o_kr:o_kr + KR_BLOCK]
    kr_ss = jnp.sum(jnp.where(lane < QK_ROPE, kr * kr, 0.0), axis=-1, keepdims=True)
    kr_p = kr * lax.rsqrt(kr_ss * (1.0 / QK_ROPE) + EPS) * krg_ref[...] * tabk_ref[...]
    krope = kr_p + pltpu.roll(kr_p, LANES - QK_ROPE, axis=1)
    if kr_ref.shape[-1] == QK_ROPE:
        kr_ref[...] = krope[:, :QK_ROPE].reshape(kr_ref.shape)
    else:
        kr_ref[0] = krope.T[:QK_ROPE, :]

    ckv_raw = z_ref[:, o_kv:o_kr]
    ckv = ckv_raw * _rms(ckv_raw, KV_LORA) * kvg_ref[...]
    ckv_ref[...] = ckv.reshape(ckv_ref.shape)
    cq_raw = z_ref[:, o_q:o_kv]
    cq = cq_raw * _rms(cq_raw, Q_LORA) * qng_ref[...]
    tick()
    q = jnp.dot(cq.astype(BF16), wuq_ref[...], preferred_element_type=F32)
    blocks = [q[:, hd * HEAD_BLOCK:(hd + 1) * HEAD_BLOCK] for hd in range(N_HEADS)]
    if maybe_kv is not None:
        wuk_ref, wuvt_ref, k_ref, vt_ref = maybe_kv
        ckv_b = ckv.astype(BF16)
        kp_all = jnp.dot(ckv_b, wuk_ref[...], preferred_element_type=F32)
        vt_ref[0, 0] = lax.dot_general(wuvt_ref[...], ckv_b, _NT, preferred_element_type=F32).astype(vt_ref.dtype)
        blocks += [kp_all[:, pair * LANES:(pair + 1) * LANES] for pair in range(N_HEADS // 2)]

    tick()
    u = _gelu_tanh(z_ref[:, :o_v])
    tick()
    v = _gelu_tanh(z_ref[:, o_v:o_q])
    if maybe_vraw_ref is not None:
        maybe_vraw_ref[...] = v.reshape(maybe_vraw_ref.shape)
    tick()

    vb = v.astype(BF16)
    pi = lax.broadcasted_iota(jnp.int32, (mix_len, mix_len), 0) // CHUNK
    pj = lax.broadcasted_iota(jnp.int32, (mix_len, mix_len), 1) // CHUNK
    for g in range(GMLP_GROUPS):
        wsg = jnp.where(pj <= pi, ws_ref[g, :mix_len, :mix_len], 0.0).astype(BF16)
        bg = bs_ref[:mix_len, g:g + 1]
        cols = slice(g * GMLP_HEAD, (g + 1) * GMLP_HEAD)
        for c in range(rows // mix_len):
            rws = slice(c * mix_len, (c + 1) * mix_len)
            mixed = jnp.dot(wsg, vb[rws, cols], preferred_element_type=F32) + bg
            bi, off = divmod(c * mix_len, tile_rows)
            ya_ref[bi, off:off + mix_len, cols] = (u[rws, cols] * mixed).astype(ya_ref.dtype)
        tick()

    qfac = tabq_ref[...] * qrow_ref[...]
    scales = []
    for blk in blocks:
        scales.append(lax.rsqrt(_half_sums(blk, lo) + QK_NOPE * EPS))
    for hd in range(N_HEADS):
        fac = qfac[:, (hd % 2) * LANES:(hd % 2 + 1) * LANES]
        q_ref[:, hd] = per_head((blocks[hd] * scales[hd] * fac).astype(q_ref.dtype))
    if maybe_kv is not None:
        krope_b = krope.astype(k_ref.dtype)
        for pair in range(N_HEADS // 2):
            kn = (blocks[N_HEADS + pair] * scales[N_HEADS + pair]).astype(k_ref.dtype)
            k_ref[:, 2 * pair] = per_head(jnp.where(lo, kn, krope_b))
            k_ref[:, 2 * pair + 1] = per_head(jnp.where(lo, krope_b, kn))


_PRE_SINGLE_CONSTS = ("n1g", "w_in", "w_s", "b_s", "qng", "w_uq_flat", "kvg", "qrow_flat", "krg")
_PRE_PIPELINED_CONSTS = ("n1g", "w_in", "w_s", "b_s", "qng", "w_uq", "kvg", "qrow", "krg", "w_uk", "w_uvt")


def _pre_single_kernel(x_ref, mod_ref, n1g_ref, win_ref, ws_ref, bs_ref, qng_ref, wuq_ref, kvg_ref, qrow_ref, krg_ref,
                       tabq_ref, tabk_ref, ya_ref, q_ref, ckv_ref, kr_ref, vraw_ref, z_ref, *, mix_len):
    z_ref[...] = jnp.dot(_pre_norm(x_ref, mod_ref, n1g_ref), win_ref[...], preferred_element_type=F32)
    _pre_heads(z_ref, ws_ref, bs_ref, qng_ref, wuq_ref, kvg_ref, qrow_ref, krg_ref, tabq_ref, tabk_ref,
               ya_ref, q_ref, ckv_ref, kr_ref, None, vraw_ref, mix_len)


def _pre_pipelined_kernel(x_ref, mod_ref, n1g_ref, win_ref, ws_ref, bs_ref, qng_ref, wuq_ref, kvg_ref, qrow_ref,
                          krg_ref, wuk_ref, wuvt_ref, tabq_ref, tabk_ref,
                          ya_ref, q_ref, k_ref, vt_ref, ckv_ref, kr_ref, z_ref, *, mix_len):
    i = pl.program_id(0)

    @pl.when(i == 0)
    def _():
        z_ref[...] = jnp.zeros(z_ref.shape, F32)

    for parity in range(2):
        @pl.when(i % 2 == parity)
        def _():
            normed = []

            def project(c):
                if not normed:
                    normed.append(_pre_norm(x_ref, mod_ref, n1g_ref))
                cols = slice(c * PROJ_CHUNK, (c + 1) * PROJ_CHUNK)
                z_ref[parity, :, cols] = jnp.dot(normed[0], win_ref[:, cols], preferred_element_type=F32)

            chunks = list(range(IN_WIDTH_PADDED // PROJ_CHUNK))
            pending = iter([chunks[:2]] + [[c] for c in chunks[2:]])

            def tick():
                for c in next(pending, []):
                    project(c)

            _pre_heads(z_ref.at[1 - parity], ws_ref, bs_ref, qng_ref, wuq_ref, kvg_ref, qrow_ref, krg_ref,
                       tabq_ref, tabk_ref, ya_ref, q_ref, ckv_ref, kr_ref,
                       (wuk_ref, wuvt_ref, k_ref, vt_ref), None, mix_len, tick)
            for group in pending:
                for c in group:
                    project(c)


def _pre_single(x, mod, wts, tabq, tabk, *, mix_len):
    b, t, d = x.shape
    whole = lambda *shape: pl.BlockSpec(shape, lambda i: (0,) * len(shape))
    consts = [wts[n] for n in _PRE_SINGLE_CONSTS]
    return pl.pallas_call(
        functools.partial(_pre_single_kernel, mix_len=mix_len),
        grid=(1,),
        in_specs=([whole(b, t, d), whole(*mod.shape)] + [_const_spec(a.shape) for a in consts]
                  + [whole(*tabq.shape), whole(*tabk.shape)]),
        out_specs=[whole(b, t, GMLP_WIDTH), whole(b, N_HEADS, t, HEAD_BLOCK), whole(b, t, KV_LORA),
                   whole(b, t, QK_ROPE), whole(b, t, GMLP_WIDTH)],
        out_shape=[jax.ShapeDtypeStruct((b, t, GMLP_WIDTH), BF16),
                   jax.ShapeDtypeStruct((b, N_HEADS, t, HEAD_BLOCK), BF16),
                   jax.ShapeDtypeStruct((b, t, KV_LORA), F32),
                   jax.ShapeDtypeStruct((b, t, QK_ROPE), F32),
                   jax.ShapeDtypeStruct((b, t, GMLP_WIDTH), F32)],
        scratch_shapes=[pltpu.VMEM((b * t, IN_WIDTH_PADDED), F32)],
        compiler_params=pltpu.CompilerParams(
            dimension_semantics=("arbitrary",), vmem_limit_bytes=VMEM_LIMIT_BYTES),
        name="pre_single",
    )(x, mod, *consts, tabq, tabk)


def _pre_pipelined(x, mod, wts, tabq, tabk, *, rows, mix_len):
    b, t, d = x.shape
    per_batch = t // rows
    n_tiles = b * per_batch
    cur = lambda i: jnp.minimum(i, n_tiles - 1)
    prev = lambda i: jnp.maximum(i - 1, 0)
    pb, pj = (lambda i: prev(i) // per_batch), (lambda i: prev(i) % per_batch)
    tok_out = lambda w: pl.BlockSpec((1, rows, w), lambda i: (pb(i), pj(i), 0))
    head = pl.BlockSpec((1, N_HEADS, rows, HEAD_BLOCK), lambda i: (pb(i), 0, pj(i), 0))
    tab = lambda a: pl.BlockSpec((rows, a.shape[1]), lambda i: (pj(i), 0))
    consts = [wts[n] for n in _PRE_PIPELINED_CONSTS]
    return pl.pallas_call(
        functools.partial(_pre_pipelined_kernel, mix_len=mix_len),
        grid=(n_tiles + 1,),
        in_specs=([pl.BlockSpec((1, rows, d), lambda i: (cur(i) // per_batch, cur(i) % per_batch, 0)),
                   pl.BlockSpec((1,) + mod.shape[1:], lambda i: (cur(i) // per_batch, 0, 0))]
                  + [_const_spec(a.shape) for a in consts] + [tab(tabq), tab(tabk)]),
        out_specs=[tok_out(GMLP_WIDTH), head, head,
                   pl.BlockSpec((1, 1, MLA_WIDTH, rows), lambda i: (pb(i), pj(i), 0, 0)),
                   tok_out(KV_LORA),
                   pl.BlockSpec((1, QK_ROPE, rows), lambda i: (pb(i), 0, pj(i)))],
        out_shape=[jax.ShapeDtypeStruct((b, t, GMLP_WIDTH), BF16),
                   jax.ShapeDtypeStruct((b, N_HEADS, t, HEAD_BLOCK), BF16),
                   jax.ShapeDtypeStruct((b, N_HEADS, t, HEAD_BLOCK), BF16),
                   jax.ShapeDtypeStruct((b, per_batch, MLA_WIDTH, rows), BF16),
                   jax.ShapeDtypeStruct((b, t, KV_LORA), F32),
                   jax.ShapeDtypeStruct((b, QK_ROPE, t), F32)],
        scratch_shapes=[pltpu.VMEM((2, rows, IN_WIDTH_PADDED), F32)],
        compiler_params=pltpu.CompilerParams(
            dimension_semantics=("arbitrary",), vmem_limit_bytes=VMEM_LIMIT_BYTES),
        name="pre_pipelined",
    )(x, mod, *consts, tabq, tabk)


def _aligned(index, multiple):
    return index if isinstance(index, int) else pl.multiple_of(index, multiple)


def _attn_kernel(q_ref, k_ref, vt_ref, o_ref, acc_ref, *maybe_m_ref, tq, online):
    t = q_ref.shape[2]
    unrolled = not online
    hb = tq // 2
    mb = hb // 2
    key_chunk = lax.broadcasted_iota(jnp.int32, (mb, mb), 0) // CHUNK
    query_chunk = lax.broadcasted_iota(jnp.int32, (mb, mb), 1) // CHUNK
    visible = key_chunk <= query_chunk
    ones = jnp.ones((DEN_ROWS, hb), BF16)

    def operands(hh, q0, qo, rows, kt, c0, keys):
        q = q_ref[0, hh, pl.ds(_aligned(q0 + qo, mb), rows), :]
        k = k_ref[0, hh, pl.ds(_aligned(kt * hb + c0, mb), keys), :]
        lhs = jnp.concatenate([vt_ref[0, kt, hh * V_HEAD:(hh + 1) * V_HEAD, c0:c0 + keys], ones[:, :keys]], axis=0)
        return lhs, lax.dot_general(k, q, _NT, preferred_element_type=F32)

    def unshifted(hh, q0, qo, rows, kt, c0, keys, masked=False):
        lhs, s = operands(hh, q0, qo, rows, kt, c0, keys)
        p = jnp.exp2(s)
        if masked:
            p = jnp.where(visible, p, 0.0)
        return jnp.dot(lhs, p.astype(BF16), preferred_element_type=F32)

    def online_update(hh, q0, qo, rows, kt, c0, keys, masked=False):
        m_ref = maybe_m_ref[0]
        lhs, s = operands(hh, q0, qo, rows, kt, c0, keys)
        if masked:
            s = jnp.where(visible, s, _NEG)
        cols = slice(qo, qo + rows)
        m_old = m_ref[hh, :, cols]
        m_new = jnp.maximum(m_old, jnp.max(s, axis=0, keepdims=True))
        p = jnp.exp2(s - m_new).astype(BF16)
        acc_ref[hh, :, cols] = (jnp.exp2(m_old - m_new) * acc_ref[hh, :, cols]
                                + jnp.dot(lhs, p, preferred_element_type=F32))
        m_ref[hh, :, cols] = m_new

    diagonal = [(0, mb, 0, 0, mb, True), (mb, mb, 0, 0, mb, False), (mb, mb, 0, mb, mb, True),
                (hb, hb, 0, 0, hb, False),
                (hb, mb, 1, 0, mb, True), (hb + mb, mb, 1, 0, mb, False), (hb + mb, mb, 1, mb, mb, True)]

    def q_tile(qi, carry):
        q0 = qi * tq
        acc_ref[...] = jnp.zeros(acc_ref.shape, F32)
        if online:
            maybe_m_ref[0][...] = jnp.full(maybe_m_ref[0].shape, -jnp.inf, F32)

        def body(j, c, tiles_per_trip=2):
            tiles = [(hh, tiles_per_trip * j + part) for part in range(tiles_per_trip) for hh in range(2)]
            if online:
                for hh, kt in tiles:
                    online_update(hh, q0, 0, tq, kt, 0, hb)
            else:
                staged = [operands(hh, q0, 0, tq, kt, 0, hb) for hh, kt in tiles]
                probs = [jnp.exp2(s).astype(BF16) for _, s in staged]
                for hh in range(2):
                    acc_ref[hh] += sum(jnp.dot(lhs, p, preferred_element_type=F32)
                                       for (th, _), (lhs, _), p in zip(tiles, staged, probs) if th == hh)
            return c

        if unrolled:
            for j in range(2 * qi):
                body(j, 0, tiles_per_trip=1)
        else:
            lax.fori_loop(0, qi, body, 0)

        totals = []
        for hh in range(2):
            if online:
                for qo, rows, kt_off, c0, keys, masked in diagonal:
                    online_update(hh, q0, qo, rows, qi * 2 + kt_off, c0, keys, masked)
                totals.append([acc_ref[hh, :, g * mb:(g + 1) * mb] for g in range(tq // mb)])
            else:
                staged = [operands(hh, q0, qo, rows, qi * 2 + kt_off, c0, keys)
                          for qo, rows, kt_off, c0, keys, _ in diagonal]
                probs = [jnp.exp2(s) for _, s in staged]
                probs = [jnp.where(visible, p, 0.0) if blk[5] else p for p, blk in zip(probs, diagonal)]
                parts = [jnp.dot(lhs, p.astype(BF16), preferred_element_type=F32)
                         for (lhs, _), p in zip(staged, probs)]
                groups = [acc_ref[hh, :, g * mb:(g + 1) * mb] for g in range(tq // mb)]
                for (qo, rows, *_), part in zip(diagonal, parts):
                    for g in range(rows // mb):
                        groups[qo // mb + g] = groups[qo // mb + g] + part[:, g * mb:(g + 1) * mb]
                totals.append(groups)
        for g in range(tq // mb):
            o_t = [tot[g][:V_HEAD] * pl.reciprocal(tot[g][V_HEAD:V_HEAD + 1], approx=False) for tot in totals]
            o_ref[0, pl.ds(_aligned(q0 + g * mb, mb), mb), :] = jnp.concatenate(o_t, axis=0).T.astype(o_ref.dtype)
        return carry

    if unrolled:
        for qi in range(t // tq):
            q_tile(qi, 0)
    else:
        lax.fori_loop(0, t // tq, q_tile, 0)


def _attn_prompt(q, k, vt, score_bound):
    b, nh, t, _ = q.shape
    n_kt, _, hb = vt.shape[1:]
    assert hb == ATTN_TK and n_kt * hb == t
    qk = pl.BlockSpec((1, 2, t, HEAD_BLOCK), lambda i, j: (i, j, 0, 0))

    def call(online):
        scratch = [pltpu.VMEM((2, ACC_ROWS, ATTN_TQ), F32)]
        if online:
            scratch.append(pltpu.VMEM((2, 1, ATTN_TQ), F32))
        return pl.pallas_call(
            functools.partial(_attn_kernel, tq=ATTN_TQ, online=online),
            grid=(b, nh // 2),
            in_specs=[qk, qk, pl.BlockSpec((1, n_kt, 2 * V_HEAD, hb), lambda i, j: (i, 0, j, 0))],
            out_specs=pl.BlockSpec((1, t, 2 * V_HEAD), lambda i, j: (i, 0, j)),
            out_shape=jax.ShapeDtypeStruct((b, t, MLA_WIDTH), BF16),
            scratch_shapes=scratch,
            compiler_params=pltpu.CompilerParams(
                dimension_semantics=("parallel", "parallel"), vmem_limit_bytes=VMEM_LIMIT_BYTES),
            name="attn_prompt_online" if online else "attn_prompt_bounded")

    return lax.cond(score_bound <= MAX_UNSHIFTED_SCORE, call(False), call(True), q, k, vt)


def _pair_out(acc0, acc1):
    first = lax.broadcasted_iota(jnp.int32, (1, LANES), 1) < V_HEAD
    num = jnp.where(first, pltpu.roll(acc0, V_HEAD, axis=1), acc1)
    den = jnp.where(first, acc0, pltpu.roll(acc1, V_HEAD, axis=1))
    return num * pl.reciprocal(den, approx=False)


def _attn_sample_kernel(q_ref, past_ckv_ref, past_kr_ref, new_ckv_ref, new_kr_ref, wukv_ref, o_ref):
    lane = lax.broadcasted_iota(jnp.int32, (1, LANES), 1)
    is_nope = lane < QK_NOPE
    spread = (lax.broadcasted_iota(jnp.int32, (QK_ROPE, LANES), 1) % QK_ROPE
              == lax.broadcasted_iota(jnp.int32, (QK_ROPE, LANES), 0)).astype(BF16)
    one_b = jnp.ones((1, LANES), BF16)

    def segment(ckv_ref, kr_ref):
        kv = jnp.dot(ckv_ref[0].astype(BF16), wukv_ref[...], preferred_element_type=F32)
        krope_b = jnp.dot(kr_ref[0].astype(BF16), spread, preferred_element_type=F32).astype(BF16)
        return kv, krope_b

    def head_kv(seg, hd):
        kv, krope_b = seg
        kvh = kv[:, hd * HEAD_BLOCK:(hd + 1) * HEAD_BLOCK]
        ks = jnp.sum(jnp.where(is_nope, kvh * kvh, 0.0), axis=-1, keepdims=True)
        kn = kvh * lax.rsqrt(ks + QK_NOPE * EPS)
        return (jnp.where(is_nope, kn.astype(BF16), krope_b), jnp.where(is_nope, one_b, kvh.astype(BF16)))

    past, new = segment(past_ckv_ref, past_kr_ref), segment(new_ckv_ref, new_kr_ref)
    heads = range(N_HEADS)
    kv = [(head_kv(past, hd), head_kv(new, hd)) for hd in heads]
    scores = [[lax.dot_general(q_ref[0, hd], k, _NT, preferred_element_type=F32) for k, _ in kv[hd]] for hd in heads]
    peaks = [jnp.maximum(*(jnp.max(s, axis=-1, keepdims=True) for s in scores[hd])) for hd in heads]
    probs = [[jnp.exp2(s - peaks[hd]).astype(BF16) for s in scores[hd]] for hd in heads]
    accs = [sum(jnp.dot(p, v, preferred_element_type=F32) for p, (_, v) in zip(probs[hd], kv[hd])) for hd in heads]
    for hd in range(0, N_HEADS, 2):
        o_ref[0, :, hd * V_HEAD:(hd + 2) * V_HEAD] = _pair_out(accs[hd], accs[hd + 1]).astype(o_ref.dtype)


def _attn_sample(q, past_ckv, past_kr, new_ckv, new_kr, w_ukv):
    b, nh, t, _ = q.shape
    per_batch = lambda a: pl.BlockSpec((1,) + a.shape[1:], lambda i: (i,) + (0,) * (a.ndim - 1))
    return pl.pallas_call(
        _attn_sample_kernel,
        grid=(b,),
        in_specs=[per_batch(q), per_batch(past_ckv), per_batch(past_kr), per_batch(new_ckv), per_batch(new_kr),
                  _const_spec(w_ukv.shape)],
        out_specs=pl.BlockSpec((1, t, MLA_WIDTH), lambda i: (i, 0, 0)),
        out_shape=jax.ShapeDtypeStruct((b, t, MLA_WIDTH), BF16),
        compiler_params=pltpu.CompilerParams(
            dimension_semantics=("parallel",), vmem_limit_bytes=VMEM_LIMIT_BYTES),
        name="attn_sample",
    )(q, past_ckv, past_kr, new_ckv, new_kr, w_ukv)


def _post_mixer(x_ref, ya_ref, yb_ref, mod_ref, wout_ref, n2g_ref):
    nb, rows, d = x_ref.shape
    flat = lambda a: a.reshape(nb * rows, a.shape[-1])
    mod = mod_ref[...]
    g1, sh2, sc2 = mod[:, 2:3], mod[:, 3:4], mod[:, 4:5]
    mixer = jnp.concatenate([flat(ya_ref[...]), flat(yb_ref[...])], axis=-1)
    mix = jnp.dot(mixer, wout_ref[...], preferred_element_type=F32)
    x1 = x_ref[...] + g1 * mix.reshape(nb, rows, d)
    h2 = flat(x1 * _rms(x1, d) * (n2g_ref[...] * (1.0 + sc2)) + sh2).astype(BF16)
    return x1, h2


def _post_ffn_chunk(h2, c, wfi_ref, wfo_ref, acc_ref):
    d_ff = wfo_ref.shape[0]
    gate = jnp.dot(h2, wfi_ref[:, c * FFN_CHUNK:(c + 1) * FFN_CHUNK], preferred_element_type=F32)
    up = jnp.dot(h2, wfi_ref[:, d_ff + c * FFN_CHUNK:d_ff + (c + 1) * FFN_CHUNK], preferred_element_type=F32)
    act = (gate * jax.nn.sigmoid(gate) * up).astype(BF16)
    part = jnp.dot(act, wfo_ref[c * FFN_CHUNK:(c + 1) * FFN_CHUNK, :], preferred_element_type=F32)
    if c == 0:
        acc_ref[...] = part
    else:
        acc_ref[...] += part


def _post_kernel(x_ref, ya_ref, yb_ref, mod_ref, wout_ref, n2g_ref, wfi_ref, wfo_ref, o_ref, acc_ref):
    x1, h2 = _post_mixer(x_ref, ya_ref, yb_ref, mod_ref, wout_ref, n2g_ref)
    o_ref[...] = x1
    for c in range(wfo_ref.shape[0] // FFN_CHUNK):
        _post_ffn_chunk(h2, c, wfi_ref, wfo_ref, acc_ref)
    o_ref[...] = o_ref[...] + mod_ref[:, 5:6] * acc_ref[...].reshape(o_ref.shape)


def _post(x, ya, yb, mod, wts, *, nb, rows):
    b, t, d = x.shape
    tok = lambda w: pl.BlockSpec((nb, rows, w), lambda i, j: (i, j, 0))
    consts = [wts[n] for n in ("w_out", "n2g", "w_ffn_in", "w_ffn_out")]
    return pl.pallas_call(
        _post_kernel,
        grid=(b // nb, t // rows),
        in_specs=[tok(d), tok(GMLP_WIDTH), tok(MLA_WIDTH),
                  pl.BlockSpec((nb,) + mod.shape[1:], lambda i, j: (i, 0, 0)),
                  _const_spec(consts[0].shape), _const_spec(consts[1].shape),
                  _const_spec(consts[2].shape), _const_spec(consts[3].shape)],
        out_specs=tok(d),
        out_shape=jax.ShapeDtypeStruct((b, t, d), F32),
        scratch_shapes=[pltpu.VMEM((nb * rows, d), F32)],
        compiler_params=pltpu.CompilerParams(
            dimension_semantics=("parallel", "parallel"), vmem_limit_bytes=VMEM_LIMIT_BYTES),
        name="post",
    )(x, ya, yb, mod, consts[0], consts[1], consts[2], consts[3])


def _rope_tables(first_pos, n, *, mirrored_odd, repeat=1):
    half = QK_ROPE // 2
    inv = ROPE_THETA ** (-np.arange(half, dtype=np.float64) / half)
    ang = (first_pos + np.arange(n, dtype=np.float64))[:, None] * inv[None, :]
    cos, sin = np.cos(ang), np.sin(ang)
    rot = np.concatenate([cos, cos, -sin, sin], axis=-1)
    nope = np.ones((n, QK_NOPE))
    tabq = (SCALE * LOG2E) * np.concatenate([nope, rot] + ([rot, nope] if mirrored_odd else [nope, rot]), axis=-1)
    tabk = np.concatenate([rot, rot], axis=-1)
    return jnp.asarray(np.tile(tabq, (repeat, 1)), F32), jnp.asarray(np.tile(tabk, (repeat, 1)), F32)


def _prep_weights(w_in, w_s, b_s, q_norm_g, w_uq, kv_norm_g, w_ukv, qn_g, qr_g, kn_g, kr_g,
                  norm1_g, w_out, norm2_g, w_ffn_in, w_ffn_out):
    o_kr = 2 * GMLP_WIDTH + Q_LORA + KV_LORA
    w_kr = w_in[:, o_kr:o_kr + QK_ROPE]
    w_kr_sw = _swap_halves(w_kr)
    w_in_p = jnp.concatenate([w_in[:, :o_kr], w_kr, w_kr_sw, w_kr, w_kr_sw], axis=-1)
    d_q = w_uq.shape[0]
    wq = w_uq.reshape(d_q, N_HEADS, QK_HEAD)
    wq_n, wq_r = wq[:, :, :QK_NOPE], wq[:, :, QK_NOPE:]
    wq_plain = jnp.concatenate([wq_n, wq_r, _swap_halves(wq_r)], axis=-1)
    wq_mirror = jnp.concatenate([wq_r, _swap_halves(wq_r), wq_n], axis=-1)
    odd = (jnp.arange(N_HEADS) % 2 == 1)[None, :, None]
    w_uq_flat = wq_plain.reshape(d_q, N_HEADS * HEAD_BLOCK)
    w_uq_pair = jnp.where(odd, wq_mirror, wq_plain).reshape(d_q, N_HEADS * HEAD_BLOCK)
    wkv = w_ukv.reshape(KV_LORA, N_HEADS, QK_NOPE + V_HEAD)
    w_uk = wkv[:, :, :QK_NOPE].reshape(KV_LORA, N_HEADS * QK_NOPE)
    w_uvt = wkv[:, :, QK_NOPE:].reshape(KV_LORA, MLA_WIDTH).T
    root = math.sqrt(QK_NOPE)
    q_nope_row = qn_g * kn_g * (root * root)
    q_rope_row = jnp.concatenate([qr_g, _swap_halves(qr_g)]) * root
    qrow_plain = jnp.concatenate([q_nope_row, q_rope_row])
    qrow_mirror = jnp.concatenate([q_rope_row, q_nope_row])
    row = lambda a: a.reshape(1, -1).astype(F32)
    return {
        "n1g": row(norm1_g), "w_in": w_in_p.astype(BF16), "w_s": w_s.astype(F32), "b_s": b_s.T.astype(F32),
        "qng": row(q_norm_g), "kvg": row(kv_norm_g),
        "w_uq": w_uq_pair.astype(BF16), "w_uq_flat": w_uq_flat.astype(BF16),
        "qrow": row(jnp.concatenate([qrow_plain, qrow_mirror])),
        "qrow_flat": row(jnp.concatenate([qrow_plain, qrow_plain])),
        "w_ukv": w_ukv.astype(BF16), "w_uk": w_uk.astype(BF16), "w_uvt": w_uvt.astype(BF16),
        "krg": row(jnp.concatenate([kr_g, _swap_halves(kr_g), kr_g, _swap_halves(kr_g)])),
        "w_out": w_out.astype(BF16), "n2g": row(norm2_g),
        "w_ffn_in": w_ffn_in.astype(BF16), "w_ffn_out": w_ffn_out.astype(BF16),
    }


def kernel(x_prompt, x_sample, cache_ckv, cache_krope, c_prompt, c_sample, w_ada, b_ada, norm1_g, w_in, w_s, b_s, q_norm_g, w_uq, kv_norm_g, w_ukv, qn_g, qr_g, kn_g, kr_g, w_out, norm2_g, w_ffn_in, w_ffn_out):
    depth = w_in.shape[0]
    assert depth == 1, "single trunk layer"
    assert PRE_ROWS == ATTN_TK
    bp, tp, d = x_prompt.shape
    bs, ts, _ = x_sample.shape
    past_len = cache_ckv.shape[2]

    wts = _prep_weights(w_in[0], w_s[0], b_s[0], q_norm_g[0], w_uq[0], kv_norm_g[0], w_ukv[0], qn_g[0], qr_g[0],
                        kn_g[0], kr_g[0], norm1_g[0], w_out[0], norm2_g[0], w_ffn_in[0], w_ffn_out[0])
    mod = _ada(jnp.concatenate([c_prompt, c_sample], axis=0), w_ada[0], b_ada[0]).reshape(bp + bs, 6, d)
    mod_p, mod_s = mod[:bp], mod[bp:]

    tabq_p, tabk_p = _rope_tables(0, tp, mirrored_odd=True)
    ya_p, q_p, k_p, vt_p, ckv_p, krt_p = _pre_pipelined(x_prompt, mod_p, wts, tabq_p, tabk_p,
                                                        rows=PRE_ROWS, mix_len=GMLP_CHUNK)
    gmax = lambda g: jnp.max(jnp.abs(g))
    score_bound = SCALE * (QK_NOPE * gmax(qn_g) * gmax(kn_g) + QK_ROPE * gmax(qr_g) * gmax(kr_g))
    yb_p = _attn_prompt(q_p, k_p, vt_p, score_bound)
    y_p = _post(x_prompt, ya_p, yb_p, mod_p, wts, nb=1, rows=POST_ROWS)

    tabq_s, tabk_s = _rope_tables(past_len, ts, mirrored_odd=False, repeat=bs)
    ya_s, q_s, ckv_s, kr_s, v_s = _pre_single(x_sample, mod_s, wts, tabq_s, tabk_s, mix_len=min(ts, GMLP_CHUNK))
    yb_s = _attn_sample(q_s, cache_ckv[0], cache_krope[0], ckv_s, kr_s, wts["w_ukv"])
    y_s = _post(x_sample, ya_s, yb_s, mod_s, wts, nb=bs, rows=ts)

    return (y_p, y_s, ckv_p[None], jnp.swapaxes(krt_p, 1, 2)[None], ckv_s[None], kr_s[None], v_s[None])
```

```python
import functools
import math

import jax
import jax.numpy as jnp
import numpy as np
from jax import lax
from jax.experimental import pallas as pl
from jax.experimental.pallas import tpu as pltpu

F32 = jnp.float32
BF16 = jnp.bfloat16

CHUNK = 64
GMLP_CHUNK = 128
GMLP_GROUPS = 4
GMLP_HEAD = 128
GMLP_WIDTH = GMLP_GROUPS * GMLP_HEAD
N_HEADS = 8
QK_NOPE = 64
QK_ROPE = 32
QK_HEAD = QK_NOPE + QK_ROPE
V_HEAD = 64
Q_LORA = 384
KV_LORA = 256
MLA_WIDTH = N_HEADS * V_HEAD
ROPE_THETA = 10000.0
EPS = 1e-6
SCALE = QK_HEAD ** -0.5

LANES = 128
HEAD_BLOCK = LANES
KR_BLOCK = LANES
IN_WIDTH_PADDED = 2 * GMLP_WIDTH + Q_LORA + KV_LORA + KR_BLOCK
VMEM_LIMIT_BYTES = 56 * 1024 * 1024

PRE_ROWS = 512
POST_ROWS = 1024
FFN_CHUNK = 256
PROJ_CHUNK = 256
ATTN_TQ = 1024
ATTN_TK = ATTN_TQ // 2
DEN_ROWS = 16
ACC_ROWS = V_HEAD + DEN_ROWS

_NEG = float(jnp.finfo(jnp.float32).min)
_NT = (((1,), (1,)), ((), ()))
LOG2E = math.log2(math.e)
MAX_UNSHIFTED_SCORE = 60.0


def _swap_halves(a, axis=-1):
    n = a.shape[axis] // 2
    lo = lax.slice_in_dim(a, 0, n, axis=axis)
    hi = lax.slice_in_dim(a, n, 2 * n, axis=axis)
    return jnp.concatenate([hi, lo], axis=axis)


def _const_spec(shape):
    nd = len(shape)
    return pl.BlockSpec(shape, lambda *_: (0,) * nd, pipeline_mode=pl.Buffered(1))


def _ada_kernel(c_ref, w_ref, b_ref, o_ref):
    c = c_ref[...]
    a = c * jax.nn.sigmoid(c)
    w = w_ref[...]
    a_hi, w_hi = a.astype(BF16), w.astype(BF16)
    a_lo, w_lo = (a - a_hi.astype(F32)).astype(BF16), (w - w_hi.astype(F32)).astype(BF16)
    n = a.shape[0]
    both = jnp.dot(jnp.concatenate([a_hi, a_lo], axis=0), w_hi, preferred_element_type=F32)
    o_ref[...] = both[:n] + both[n:] + jnp.dot(a_hi, w_lo, preferred_element_type=F32) + b_ref[...]


def _ada(c_all, w_ada, b_ada):
    n, d = c_all.shape
    width = w_ada.shape[1]
    bn = d
    return pl.pallas_call(
        _ada_kernel,
        grid=(width // bn,),
        in_specs=[pl.BlockSpec((n, d), lambda j: (0, 0)),
                  pl.BlockSpec((d, bn), lambda j: (0, j)),
                  pl.BlockSpec((1, bn), lambda j: (0, j))],
        out_specs=pl.BlockSpec((n, bn), lambda j: (0, j)),
        out_shape=jax.ShapeDtypeStruct((n, width), F32),
        compiler_params=pltpu.CompilerParams(vmem_limit_bytes=VMEM_LIMIT_BYTES),
        name="ada",
    )(c_all, w_ada, b_ada.reshape(1, width))


def _rms(x, width):
    return lax.rsqrt(jnp.sum(x * x, axis=-1, keepdims=True) * (1.0 / width) + EPS)


def _pre_norm(x_ref, mod_ref, n1g_ref):
    nb, rows, d = x_ref.shape
    x = x_ref[...]
    mod = mod_ref[...]
    sh1, sc1 = mod[:, 0:1], mod[:, 1:2]
    h = x * _rms(x, d) * (n1g_ref[...] * (1.0 + sc1)) + sh1
    return h.reshape(nb * rows, d).astype(BF16)


def _gelu_tanh(x):
    k = -2.0 * math.sqrt(2.0 / math.pi) * LOG2E
    e = jnp.exp2(x * (x * x * (k * 0.044715) + k))
    return x * pl.reciprocal(1.0 + e, approx=False)


def _half_sums(a, lo):
    sq = a * a
    ss_all = jnp.sum(sq, axis=-1, keepdims=True)
    ss_lo = jnp.sum(jnp.where(lo, sq, 0.0), axis=-1, keepdims=True)
    return jnp.where(lo, ss_lo, ss_all - ss_lo)


def _pre_heads(z_ref, ws_ref, bs_ref, qng_ref, wuq_ref, kvg_ref, qrow_ref, krg_ref, tabq_ref, tabk_ref,
               ya_ref, q_ref, ckv_ref, kr_ref, maybe_kv, maybe_vraw_ref, mix_len, tick=lambda: None):
    nb, tile_rows = ya_ref.shape[0], ya_ref.shape[1]
    rows = z_ref.shape[0]
    per_head = lambda a: a.reshape(nb, tile_rows, HEAD_BLOCK)
    o_v, o_q, o_kv, o_kr = GMLP_WIDTH, 2 * GMLP_WIDTH, 2 * GMLP_WIDTH + Q_LORA, 2 * GMLP_WIDTH + Q_LORA + KV_LORA
    lane = lax.broadcasted_iota(jnp.int32, (1, LANES), 1)
    lo = lane < LANES // 2

    kr = z_ref[:, o_kr:o_kr + KR_BLOCK]
    kr_ss = jnp.sum(jnp.where(lane < QK_ROPE, kr * kr, 0.0), axis=-1, keepdims=True)
    kr_p = kr * lax.rsqrt(kr_ss * (1.0 / QK_ROPE) + EPS) * krg_ref[...] * tabk_ref[...]
    krope = kr_p + pltpu.roll(kr_p, LANES - QK_ROPE, axis=1)
    if kr_ref.shape[-1] == QK_ROPE:
        kr_ref[...] = krope[:, :QK_ROPE].reshape(kr_ref.shape)
    else:
        kr_ref[0] = krope.T[:QK_ROPE, :]

    ckv_raw = z_ref[:, o_kv:o_kr]
    ckv = ckv_raw * _rms(ckv_raw, KV_LORA) * kvg_ref[...]
    ckv_ref[...] = ckv.reshape(ckv_ref.shape)
    cq_raw = z_ref[:, o_q:o_kv]
    cq = cq_raw * _rms(cq_raw, Q_LORA) * qng_ref[...]
    tick()
    q = jnp.dot(cq.astype(BF16), wuq_ref[...], preferred_element_type=F32)
    blocks = [q[:, hd * HEAD_BLOCK:(hd + 1) * HEAD_BLOCK] for hd in range(N_HEADS)]
    if maybe_kv is not None:
        wuk_ref, wuvt_ref, k_ref, vt_ref = maybe_kv
        ckv_b = ckv.astype(BF16)
        kp_all = jnp.dot(ckv_b, wuk_ref[...], preferred_element_type=F32)
        vt_ref[0, 0] = lax.dot_general(wuvt_ref[...], ckv_b, _NT, preferred_element_type=F32).astype(vt_ref.dtype)
        blocks += [kp_all[:, pair * LANES:(pair + 1) * LANES] for pair in range(N_HEADS // 2)]

    tick()
    u = _gelu_tanh(z_ref[:, :o_v])
    tick()
    v = _gelu_tanh(z_ref[:, o_v:o_q])
    if maybe_vraw_ref is not None:
        maybe_vraw_ref[...] = v.reshape(maybe_vraw_ref.shape)
    tick()

    vb = v.astype(BF16)
    pi = lax.broadcasted_iota(jnp.int32, (mix_len, mix_len), 0) // CHUNK
    pj = lax.broadcasted_iota(jnp.int32, (mix_len, mix_len), 1) // CHUNK
    for g in range(GMLP_GROUPS):
        wsg = jnp.where(pj <= pi, ws_ref[g, :mix_len, :mix_len], 0.0).astype(BF16)
        bg = bs_ref[:mix_len, g:g + 1]
        cols = slice(g * GMLP_HEAD, (g + 1) * GMLP_HEAD)
        for c in range(rows // mix_len):
            rws = slice(c * mix_len, (c + 1) * mix_len)
            mixed = jnp.dot(wsg, vb[rws, cols], preferred_element_type=F32) + bg
            bi, off = divmod(c * mix_len, tile_rows)
            ya_ref[bi, off:off + mix_len, cols] = (u[rws, cols] * mixed).astype(ya_ref.dtype)
        tick()

    qfac = tabq_ref[...] * qrow_ref[...]
    scales = []
    for blk in blocks:
        scales.append(lax.rsqrt(_half_sums(blk, lo) + QK_NOPE * EPS))
    for hd in range(N_HEADS):
        fac = qfac[:, (hd % 2) * LANES:(hd % 2 + 1) * LANES]
        q_ref[:, hd] = per_head((blocks[hd] * scales[hd] * fac).astype(q_ref.dtype))
    if maybe_kv is not None:
        krope_b = krope.astype(k_ref.dtype)
        for pair in range(N_HEADS // 2):
            kn = (blocks[N_HEADS + pair] * scales[N_HEADS + pair]).astype(k_ref.dtype)
            k_ref[:, 2 * pair] = per_head(jnp.where(lo, kn, krope_b))
            k_ref[:, 2 * pair + 1] = per_head(jnp.where(lo, krope_b, kn))


_PRE_SINGLE_CONSTS = ("n1g", "w_in", "w_s", "b_s", "qng", "w_uq_flat", "kvg", "qrow_flat", "krg")
_PRE_PIPELINED_CONSTS = ("n1g", "w_in", "w_s", "b_s", "qng", "w_uq", "kvg", "qrow", "krg", "w_uk", "w_uvt")


def _pre_single_kernel(x_ref, mod_ref, n1g_ref, win_ref, ws_ref, bs_ref, qng_ref, wuq_ref, kvg_ref, qrow_ref, krg_ref,
                       tabq_ref, tabk_ref, ya_ref, q_ref, ckv_ref, kr_ref, vraw_ref, z_ref, *, mix_len):
    z_ref[...] = jnp.dot(_pre_norm(x_ref, mod_ref, n1g_ref), win_ref[...], preferred_element_type=F32)
    _pre_heads(z_ref, ws_ref, bs_ref, qng_ref, wuq_ref, kvg_ref, qrow_ref, krg_ref, tabq_ref, tabk_ref,
               ya_ref, q_ref, ckv_ref, kr_ref, None, vraw_ref, mix_len)


def _pre_pipelined_kernel(x_ref, mod_ref, n1g_ref, win_ref, ws_ref, bs_ref, qng_ref, wuq_ref, kvg_ref, qrow_ref,
                          krg_ref, wuk_ref, wuvt_ref, tabq_ref, tabk_ref,
                          ya_ref, q_ref, k_ref, vt_ref, ckv_ref, kr_ref, z_ref, *, mix_len):
    i = pl.program_id(0)

    @pl.when(i == 0)
    def _():
        z_ref[...] = jnp.zeros(z_ref.shape, F32)

    for parity in range(2):
        @pl.when(i % 2 == parity)
        def _():
            normed = []

            def project(c):
                if not normed:
                    normed.append(_pre_norm(x_ref, mod_ref, n1g_ref))
                cols = slice(c * PROJ_CHUNK, (c + 1) * PROJ_CHUNK)
                z_ref[parity, :, cols] = jnp.dot(normed[0], win_ref[:, cols], preferred_element_type=F32)

            chunks = list(range(IN_WIDTH_PADDED // PROJ_CHUNK))
            pending = iter([chunks[:2]] + [[c] for c in chunks[2:]])

            def tick():
                for c in next(pending, []):
                    project(c)

            _pre_heads(z_ref.at[1 - parity], ws_ref, bs_ref, qng_ref, wuq_ref, kvg_ref, qrow_ref, krg_ref,
                       tabq_ref, tabk_ref, ya_ref, q_ref, ckv_ref, kr_ref,
                       (wuk_ref, wuvt_ref, k_ref, vt_ref), None, mix_len, tick)
            for group in pending:
                for c in group:
                    project(c)


def _pre_single(x, mod, wts, tabq, tabk, *, mix_len):
    b, t, d = x.shape
    whole = lambda *shape: pl.BlockSpec(shape, lambda i: (0,) * len(shape))
    consts = [wts[n] for n in _PRE_SINGLE_CONSTS]
    return pl.pallas_call(
        functools.partial(_pre_single_kernel, mix_len=mix_len),
        grid=(1,),
        in_specs=([whole(b, t, d), whole(*mod.shape)] + [_const_spec(a.shape) for a in consts]
                  + [whole(*tabq.shape), whole(*tabk.shape)]),
        out_specs=[whole(b, t, GMLP_WIDTH), whole(b, N_HEADS, t, HEAD_BLOCK), whole(b, t, KV_LORA),
                   whole(b, t, QK_ROPE), whole(b, t, GMLP_WIDTH)],
        out_shape=[jax.ShapeDtypeStruct((b, t, GMLP_WIDTH), BF16),
                   jax.ShapeDtypeStruct((b, N_HEADS, t, HEAD_BLOCK), BF16),
                   jax.ShapeDtypeStruct((b, t, KV_LORA), F32),
                   jax.ShapeDtypeStruct((b, t, QK_ROPE), F32),
                   jax.ShapeDtypeStruct((b, t, GMLP_WIDTH), F32)],
        scratch_shapes=[pltpu.VMEM((b * t, IN_WIDTH_PADDED), F32)],
        compiler_params=pltpu.CompilerParams(
            dimension_semantics=("arbitrary",), vmem_limit_bytes=VMEM_LIMIT_BYTES),
        name="pre_single",
    )(x, mod, *consts, tabq, tabk)


def _pre_pipelined(x, mod, wts, tabq, tabk, *, rows, mix_len):
    b, t, d = x.shape
    per_batch = t // rows
    n_tiles = b * per_batch
    cur = lambda i: jnp.minimum(i, n_tiles - 1)
    prev = lambda i: jnp.maximum(i - 1, 0)
    pb, pj = (lambda i: prev(i) // per_batch), (lambda i: prev(i) % per_batch)
    tok_out = lambda w: pl.BlockSpec((1, rows, w), lambda i: (pb(i), pj(i), 0))
    head = pl.BlockSpec((1, N_HEADS, rows, HEAD_BLOCK), lambda i: (pb(i), 0, pj(i), 0))
    tab = lambda a: pl.BlockSpec((rows, a.shape[1]), lambda i: (pj(i), 0))
    consts = [wts[n] for n in _PRE_PIPELINED_CONSTS]
    return pl.pallas_call(
        functools.partial(_pre_pipelined_kernel, mix_len=mix_len),
        grid=(n_tiles + 1,),
        in_specs=([pl.BlockSpec((1, rows, d), lambda i: (cur(i) // per_batch, cur(i) % per_batch, 0)),
                   pl.BlockSpec((1,) + mod.shape[1:], lambda i: (cur(i) // per_batch, 0, 0))]
                  + [_const_spec(a.shape) for a in consts] + [tab(tabq), tab(tabk)]),
        out_specs=[tok_out(GMLP_WIDTH), head, head,
                   pl.BlockSpec((1, 1, MLA_WIDTH, rows), lambda i: (pb(i), pj(i), 0, 0)),
                   tok_out(KV_LORA),
                   pl.BlockSpec((1, QK_ROPE, rows), lambda i: (pb(i), 0, pj(i)))],
        out_shape=[jax.ShapeDtypeStruct((b, t, GMLP_WIDTH), BF16),
                   jax.ShapeDtypeStruct((b, N_HEADS, t, HEAD_BLOCK), BF16),
                   jax.ShapeDtypeStruct((b, N_HEADS, t, HEAD_BLOCK), BF16),
                   jax.ShapeDtypeStruct((b, per_batch, MLA_WIDTH, rows), BF16),
                   jax.ShapeDtypeStruct((b, t, KV_LORA), F32),
                   jax.ShapeDtypeStruct((b, QK_ROPE, t), F32)],
        scratch_shapes=[pltpu.VMEM((2, rows, IN_WIDTH_PADDED), F32)],
        compiler_params=pltpu.CompilerParams(
            dimension_semantics=("arbitrary",), vmem_limit_bytes=VMEM_LIMIT_BYTES),
        name="pre_pipelined",
    )(x, mod, *consts, tabq, tabk)


def _aligned(index, multiple):
    return index if isinstance(index, int) else pl.multiple_of(index, multiple)


def _attn_kernel(q_ref, k_ref, vt_ref, o_ref, acc_ref, *maybe_m_ref, tq, online):
    t = q_ref.shape[2]
    unrolled = not online
    hb = tq // 2
    mb = hb // 2
    key_chunk = lax.broadcasted_iota(jnp.int32, (mb, mb), 0) // CHUNK
    query_chunk = lax.broadcasted_iota(jnp.int32, (mb, mb), 1) // CHUNK
    visible = key_chunk <= query_chunk
    ones = jnp.ones((DEN_ROWS, hb), BF16)

    def operands(hh, q0, qo, rows, kt, c0, keys):
        q = q_ref[0, hh, pl.ds(_aligned(q0 + qo, mb), rows), :]
        k = k_ref[0, hh, pl.ds(_aligned(kt * hb + c0, mb), keys), :]
        lhs = jnp.concatenate([vt_ref[0, kt, hh * V_HEAD:(hh + 1) * V_HEAD, c0:c0 + keys], ones[:, :keys]], axis=0)
        return lhs, lax.dot_general(k, q, _NT, preferred_element_type=F32)

    def unshifted(hh, q0, qo, rows, kt, c0, keys, masked=False):
        lhs, s = operands(hh, q0, qo, rows, kt, c0, keys)
        p = jnp.exp2(s)
        if masked:
            p = jnp.where(visible, p, 0.0)
        return jnp.dot(lhs, p.astype(BF16), preferred_element_type=F32)

    def online_update(hh, q0, qo, rows, kt, c0, keys, masked=False):
        m_ref = maybe_m_ref[0]
        lhs, s = operands(hh, q0, qo, rows, kt, c0, keys)
        if masked:
            s = jnp.where(visible, s, _NEG)
        cols = slice(qo, qo + rows)
        m_old = m_ref[hh, :, cols]
        m_new = jnp.maximum(m_old, jnp.max(s, axis=0, keepdims=True))
        p = jnp.exp2(s - m_new).astype(BF16)
        acc_ref[hh, :, cols] = (jnp.exp2(m_old - m_new) * acc_ref[hh, :, cols]
                                + jnp.dot(lhs, p, preferred_element_type=F32))
        m_ref[hh, :, cols] = m_new

    diagonal = [(0, mb, 0, 0, mb, True), (mb, mb, 0, 0, mb, False), (mb, mb, 0, mb, mb, True),
                (hb, hb, 0, 0, hb, False),
                (hb, mb, 1, 0, mb, True), (hb + mb, mb, 1, 0, mb, False), (hb + mb, mb, 1, mb, mb, True)]

    def q_tile(qi, carry):
        q0 = qi * tq
        acc_ref[...] = jnp.zeros(acc_ref.shape, F32)
        if online:
            maybe_m_ref[0][...] = jnp.full(maybe_m_ref[0].shape, -jnp.inf, F32)

        def body(j, c):
            tiles = [(hh, 2 * j + half) for half in range(2) for hh in range(2)]
            if online:
                for hh, kt in tiles:
                    online_update(hh, q0, 0, tq, kt, 0, hb)
            else:
                staged = [operands(hh, q0, 0, tq, kt, 0, hb) for hh, kt in tiles]
                probs = [jnp.exp2(s).astype(BF16) for _, s in staged]
                for hh in range(2):
                    acc_ref[hh] += sum(jnp.dot(lhs, p, preferred_element_type=F32)
                                       for (th, _), (lhs, _), p in zip(tiles, staged, probs) if th == hh)
            return c

        if unrolled:
            for j in range(qi):
                body(j, 0)
        else:
            lax.fori_loop(0, qi, body, 0)

        totals = []
        for hh in range(2):
            if online:
                for qo, rows, kt_off, c0, keys, masked in diagonal:
                    online_update(hh, q0, qo, rows, qi * 2 + kt_off, c0, keys, masked)
                totals.append([acc_ref[hh, :, g * mb:(g + 1) * mb] for g in range(tq // mb)])
            else:
                staged = [operands(hh, q0, qo, rows, qi * 2 + kt_off, c0, keys)
                          for qo, rows, kt_off, c0, keys, _ in diagonal]
                probs = [jnp.exp2(s) for _, s in staged]
                probs = [jnp.where(visible, p, 0.0) if blk[5] else p for p, blk in zip(probs, diagonal)]
                parts = [jnp.dot(lhs, p.astype(BF16), preferred_element_type=F32)
                         for (lhs, _), p in zip(staged, probs)]
                groups = [acc_ref[hh, :, g * mb:(g + 1) * mb] for g in range(tq // mb)]
                for (qo, rows, *_), part in zip(diagonal, parts):
                    for g in range(rows // mb):
                        groups[qo // mb + g] = groups[qo // mb + g] + part[:, g * mb:(g + 1) * mb]
                totals.append(groups)
        for g in range(tq // mb):
            o_t = [tot[g][:V_HEAD] * pl.reciprocal(tot[g][V_HEAD:V_HEAD + 1], approx=False) for tot in totals]
            o_ref[0, pl.ds(_aligned(q0 + g * mb, mb), mb), :] = jnp.concatenate(o_t, axis=0).T.astype(o_ref.dtype)
        return carry

    if unrolled:
        for qi in range(t // tq):
            q_tile(qi, 0)
    else:
        lax.fori_loop(0, t // tq, q_tile, 0)


def _attn_prompt(q, k, vt, score_bound):
    b, nh, t, _ = q.shape
    n_kt, _, hb = vt.shape[1:]
    assert hb == ATTN_TK and n_kt * hb == t
    qk = pl.BlockSpec((1, 2, t, HEAD_BLOCK), lambda i, j: (i, j, 0, 0))

    def call(online):
        scratch = [pltpu.VMEM((2, ACC_ROWS, ATTN_TQ), F32)]
        if online:
            scratch.append(pltpu.VMEM((2, 1, ATTN_TQ), F32))
        return pl.pallas_call(
            functools.partial(_attn_kernel, tq=ATTN_TQ, online=online),
            grid=(b, nh // 2),
            in_specs=[qk, qk, pl.BlockSpec((1, n_kt, 2 * V_HEAD, hb), lambda i, j: (i, 0, j, 0))],
            out_specs=pl.BlockSpec((1, t, 2 * V_HEAD), lambda i, j: (i, 0, j)),
            out_shape=jax.ShapeDtypeStruct((b, t, MLA_WIDTH), BF16),
            scratch_shapes=scratch,
            compiler_params=pltpu.CompilerParams(
                dimension_semantics=("parallel", "parallel"), vmem_limit_bytes=VMEM_LIMIT_BYTES),
            name="attn_prompt_online" if online else "attn_prompt_bounded")

    return lax.cond(jnp.logical_not(score_bound <= MAX_UNSHIFTED_SCORE), call(True), call(False), q, k, vt)


def _pair_out(acc0, acc1):
    first = lax.broadcasted_iota(jnp.int32, (1, LANES), 1) < V_HEAD
    num = jnp.where(first, pltpu.roll(acc0, V_HEAD, axis=1), acc1)
    den = jnp.where(first, acc0, pltpu.roll(acc1, V_HEAD, axis=1))
    return num * pl.reciprocal(den, approx=False)


def _attn_sample_kernel(q_ref, past_ckv_ref, past_kr_ref, new_ckv_ref, new_kr_ref, wukv_ref, o_ref):
    lane = lax.broadcasted_iota(jnp.int32, (1, LANES), 1)
    is_nope = lane < QK_NOPE
    spread = (lax.broadcasted_iota(jnp.int32, (QK_ROPE, LANES), 1) % QK_ROPE
              == lax.broadcasted_iota(jnp.int32, (QK_ROPE, LANES), 0)).astype(BF16)
    one_b = jnp.ones((1, LANES), BF16)

    def segment(ckv_ref, kr_ref):
        kv = jnp.dot(ckv_ref[0].astype(BF16), wukv_ref[...], preferred_element_type=F32)
        krope_b = jnp.dot(kr_ref[0].astype(BF16), spread, preferred_element_type=F32).astype(BF16)
        return kv, krope_b

    def head_kv(seg, hd):
        kv, krope_b = seg
        kvh = kv[:, hd * HEAD_BLOCK:(hd + 1) * HEAD_BLOCK]
        ks = jnp.sum(jnp.where(is_nope, kvh * kvh, 0.0), axis=-1, keepdims=True)
        kn = kvh * lax.rsqrt(ks + QK_NOPE * EPS)
        return (jnp.where(is_nope, kn.astype(BF16), krope_b), jnp.where(is_nope, one_b, kvh.astype(BF16)))

    past, new = segment(past_ckv_ref, past_kr_ref), segment(new_ckv_ref, new_kr_ref)
    heads = range(N_HEADS)
    kv = [(head_kv(past, hd), head_kv(new, hd)) for hd in heads]
    scores = [[lax.dot_general(q_ref[0, hd], k, _NT, preferred_element_type=F32) for k, _ in kv[hd]] for hd in heads]
    peaks = [jnp.maximum(*(jnp.max(s, axis=-1, keepdims=True) for s in scores[hd])) for hd in heads]
    probs = [[jnp.exp2(s - peaks[hd]).astype(BF16) for s in scores[hd]] for hd in heads]
    accs = [sum(jnp.dot(p, v, preferred_element_type=F32) for p, (_, v) in zip(probs[hd], kv[hd])) for hd in heads]
    for hd in range(0, N_HEADS, 2):
        o_ref[0, :, hd * V_HEAD:(hd + 2) * V_HEAD] = _pair_out(accs[hd], accs[hd + 1]).astype(o_ref.dtype)


def _attn_sample(q, past_ckv, past_kr, new_ckv, new_kr, w_ukv):
    b, nh, t, _ = q.shape
    per_batch = lambda a: pl.BlockSpec((1,) + a.shape[1:], lambda i: (i,) + (0,) * (a.ndim - 1))
    return pl.pallas_call(
        _attn_sample_kernel,
        grid=(b,),
        in_specs=[per_batch(q), per_batch(past_ckv), per_batch(past_kr), per_batch(new_ckv), per_batch(new_kr),
                  _const_spec(w_ukv.shape)],
        out_specs=pl.BlockSpec((1, t, MLA_WIDTH), lambda i: (i, 0, 0)),
        out_shape=jax.ShapeDtypeStruct((b, t, MLA_WIDTH), BF16),
        compiler_params=pltpu.CompilerParams(
            dimension_semantics=("parallel",), vmem_limit_bytes=VMEM_LIMIT_BYTES),
        name="attn_sample",
    )(q, past_ckv, past_kr, new_ckv, new_kr, w_ukv)


def _post_mixer(x_ref, ya_ref, yb_ref, mod_ref, wout_ref, n2g_ref):
    nb, rows, d = x_ref.shape
    flat = lambda a: a.reshape(nb * rows, a.shape[-1])
    mod = mod_ref[...]
    g1, sh2, sc2 = mod[:, 2:3], mod[:, 3:4], mod[:, 4:5]
    mixer = jnp.concatenate([flat(ya_ref[...]), flat(yb_ref[...])], axis=-1)
    mix = jnp.dot(mixer, wout_ref[...], preferred_element_type=F32)
    x1 = x_ref[...] + g1 * mix.reshape(nb, rows, d)
    h2 = flat(x1 * _rms(x1, d) * (n2g_ref[...] * (1.0 + sc2)) + sh2).astype(BF16)
    return x1, h2


def _post_ffn_chunk(h2, c, wfi_ref, wfo_ref, acc_ref):
    d_ff = wfo_ref.shape[0]
    gate = jnp.dot(h2, wfi_ref[:, c * FFN_CHUNK:(c + 1) * FFN_CHUNK], preferred_element_type=F32)
    up = jnp.dot(h2, wfi_ref[:, d_ff + c * FFN_CHUNK:d_ff + (c + 1) * FFN_CHUNK], preferred_element_type=F32)
    act = (gate * jax.nn.sigmoid(gate) * up).astype(BF16)
    part = jnp.dot(act, wfo_ref[c * FFN_CHUNK:(c + 1) * FFN_CHUNK, :], preferred_element_type=F32)
    if c == 0:
        acc_ref[...] = part
    else:
        acc_ref[...] += part


def _post_kernel(x_ref, ya_ref, yb_ref, mod_ref, wout_ref, n2g_ref, wfi_ref, wfo_ref, o_ref, acc_ref):
    x1, h2 = _post_mixer(x_ref, ya_ref, yb_ref, mod_ref, wout_ref, n2g_ref)
    o_ref[...] = x1
    for c in range(wfo_ref.shape[0] // FFN_CHUNK):
        _post_ffn_chunk(h2, c, wfi_ref, wfo_ref, acc_ref)
    o_ref[...] = o_ref[...] + mod_ref[:, 5:6] * acc_ref[...].reshape(o_ref.shape)


def _post(x, ya, yb, mod, wts, *, nb, rows):
    b, t, d = x.shape
    tok = lambda w: pl.BlockSpec((nb, rows, w), lambda i, j: (i, j, 0))
    consts = [wts[n] for n in ("w_out", "n2g", "w_ffn_in", "w_ffn_out")]
    return pl.pallas_call(
        _post_kernel,
        grid=(b // nb, t // rows),
        in_specs=[tok(d), tok(GMLP_WIDTH), tok(MLA_WIDTH),
                  pl.BlockSpec((nb,) + mod.shape[1:], lambda i, j: (i, 0, 0)),
                  _const_spec(consts[0].shape), _const_spec(consts[1].shape),
                  _const_spec(consts[2].shape), _const_spec(consts[3].shape)],
        out_specs=tok(d),
        out_shape=jax.ShapeDtypeStruct((b, t, d), F32),
        scratch_shapes=[pltpu.VMEM((nb * rows, d), F32)],
        compiler_params=pltpu.CompilerParams(
            dimension_semantics=("parallel", "parallel"), vmem_limit_bytes=VMEM_LIMIT_BYTES),
        name="post",
    )(x, ya, yb, mod, consts[0], consts[1], consts[2], consts[3])


def _rope_tables(first_pos, n, *, mirrored_odd, repeat=1):
    half = QK_ROPE // 2
    inv = ROPE_THETA ** (-np.arange(half, dtype=np.float64) / half)
    ang = (first_pos + np.arange(n, dtype=np.float64))[:, None] * inv[None, :]
    cos, sin = np.cos(ang), np.sin(ang)
    rot = np.concatenate([cos, cos, -sin, sin], axis=-1)
    nope = np.ones((n, QK_NOPE))
    tabq = (SCALE * LOG2E) * np.concatenate([nope, rot] + ([rot, nope] if mirrored_odd else [nope, rot]), axis=-1)
    tabk = np.concatenate([rot, rot], axis=-1)
    return jnp.asarray(np.tile(tabq, (repeat, 1)), F32), jnp.asarray(np.tile(tabk, (repeat, 1)), F32)


def _prep_weights(w_in, w_s, b_s, q_norm_g, w_uq, kv_norm_g, w_ukv, qn_g, qr_g, kn_g, kr_g,
                  norm1_g, w_out, norm2_g, w_ffn_in, w_ffn_out):
    o_kr = 2 * GMLP_WIDTH + Q_LORA + KV_LORA
    w_kr = w_in[:, o_kr:o_kr + QK_ROPE]
    w_kr_sw = _swap_halves(w_kr)
    w_in_p = jnp.concatenate([w_in[:, :o_kr], w_kr, w_kr_sw, w_kr, w_kr_sw], axis=-1)
    d_q = w_uq.shape[0]
    wq = w_uq.reshape(d_q, N_HEADS, QK_HEAD)
    wq_n, wq_r = wq[:, :, :QK_NOPE], wq[:, :, QK_NOPE:]
    wq_plain = jnp.concatenate([wq_n, wq_r, _swap_halves(wq_r)], axis=-1)
    wq_mirror = jnp.concatenate([wq_r, _swap_halves(wq_r), wq_n], axis=-1)
    odd = (jnp.arange(N_HEADS) % 2 == 1)[None, :, None]
    w_uq_flat = wq_plain.reshape(d_q, N_HEADS * HEAD_BLOCK)
    w_uq_pair = jnp.where(odd, wq_mirror, wq_plain).reshape(d_q, N_HEADS * HEAD_BLOCK)
    wkv = w_ukv.reshape(KV_LORA, N_HEADS, QK_NOPE + V_HEAD)
    w_uk = wkv[:, :, :QK_NOPE].reshape(KV_LORA, N_HEADS * QK_NOPE)
    w_uvt = wkv[:, :, QK_NOPE:].reshape(KV_LORA, MLA_WIDTH).T
    root = math.sqrt(QK_NOPE)
    q_nope_row = qn_g * kn_g * (root * root)
    q_rope_row = jnp.concatenate([qr_g, _swap_halves(qr_g)]) * root
    qrow_plain = jnp.concatenate([q_nope_row, q_rope_row])
    qrow_mirror = jnp.concatenate([q_rope_row, q_nope_row])
    row = lambda a: a.reshape(1, -1).astype(F32)
    return {
        "n1g": row(norm1_g), "w_in": w_in_p.astype(BF16), "w_s": w_s.astype(F32), "b_s": b_s.T.astype(F32),
        "qng": row(q_norm_g), "kvg": row(kv_norm_g),
        "w_uq": w_uq_pair.astype(BF16), "w_uq_flat": w_uq_flat.astype(BF16),
        "qrow": row(jnp.concatenate([qrow_plain, qrow_mirror])),
        "qrow_flat": row(jnp.concatenate([qrow_plain, qrow_plain])),
        "w_ukv": w_ukv.astype(BF16), "w_uk": w_uk.astype(BF16), "w_uvt": w_uvt.astype(BF16),
        "krg": row(jnp.concatenate([kr_g, _swap_halves(kr_g), kr_g, _swap_halves(kr_g)])),
        "w_out": w_out.astype(BF16), "n2g": row(norm2_g),
        "w_ffn_in": w_ffn_in.astype(BF16), "w_ffn_out": w_ffn_out.astype(BF16),
    }


def kernel(x_prompt, x_sample, cache_ckv, cache_krope, c_prompt, c_sample, w_ada, b_ada, norm1_g, w_in, w_s, b_s, q_norm_g, w_uq, kv_norm_g, w_ukv, qn_g, qr_g, kn_g, kr_g, w_out, norm2_g, w_ffn_in, w_ffn_out):
    depth = w_in.shape[0]
    assert depth == 1, "single trunk layer"
    assert PRE_ROWS == ATTN_TK
    bp, tp, d = x_prompt.shape
    bs, ts, _ = x_sample.shape
    past_len = cache_ckv.shape[2]

    wts = _prep_weights(w_in[0], w_s[0], b_s[0], q_norm_g[0], w_uq[0], kv_norm_g[0], w_ukv[0], qn_g[0], qr_g[0],
                        kn_g[0], kr_g[0], norm1_g[0], w_out[0], norm2_g[0], w_ffn_in[0], w_ffn_out[0])
    mod = _ada(jnp.concatenate([c_prompt, c_sample], axis=0), w_ada[0], b_ada[0]).reshape(bp + bs, 6, d)
    mod_p, mod_s = mod[:bp], mod[bp:]

    tabq_p, tabk_p = _rope_tables(0, tp, mirrored_odd=True)
    ya_p, q_p, k_p, vt_p, ckv_p, krt_p = _pre_pipelined(x_prompt, mod_p, wts, tabq_p, tabk_p,
                                                        rows=PRE_ROWS, mix_len=GMLP_CHUNK)
    gmax = lambda g: jnp.max(jnp.abs(g))
    score_bound = SCALE * (QK_NOPE * gmax(qn_g) * gmax(kn_g) + QK_ROPE * gmax(qr_g) * gmax(kr_g))
    yb_p = _attn_prompt(q_p, k_p, vt_p, score_bound)
    y_p = _post(x_prompt, ya_p, yb_p, mod_p, wts, nb=1, rows=POST_ROWS)

    tabq_s, tabk_s = _rope_tables(past_len, ts, mirrored_odd=False, repeat=bs)
    ya_s, q_s, ckv_s, kr_s, v_s = _pre_single(x_sample, mod_s, wts, tabq_s, tabk_s, mix_len=min(ts, GMLP_CHUNK))
    yb_s = _attn_sample(q_s, cache_ckv[0], cache_krope[0], ckv_s, kr_s, wts["w_ukv"])
    y_s = _post(x_sample, ya_s, yb_s, mod_s, wts, nb=bs, rows=ts)

    return (y_p, y_s, ckv_p[None], jnp.swapaxes(krt_p, 1, 2)[None], ckv_s[None], kr_s[None], v_s[None])
```

```python
import functools
import math

import jax
import jax.numpy as jnp
import numpy as np
from jax import lax
from jax.experimental import pallas as pl
from jax.experimental.pallas import tpu as pltpu

F32 = jnp.float32
BF16 = jnp.bfloat16

CHUNK = 64
GMLP_CHUNK = 128
GMLP_GROUPS = 4
GMLP_HEAD = 128
GMLP_WIDTH = GMLP_GROUPS * GMLP_HEAD
N_HEADS = 8
QK_NOPE = 64
QK_ROPE = 32
QK_HEAD = QK_NOPE + QK_ROPE
V_HEAD = 64
Q_LORA = 384
KV_LORA = 256
MLA_WIDTH = N_HEADS * V_HEAD
ROPE_THETA = 10000.0
EPS = 1e-6
SCALE = QK_HEAD ** -0.5

LANES = 128
HEAD_BLOCK = LANES
KR_BLOCK = LANES
IN_WIDTH_PADDED = 2 * GMLP_WIDTH + Q_LORA + KV_LORA + KR_BLOCK
VMEM_LIMIT_BYTES = 56 * 1024 * 1024

PRE_ROWS = 512
POST_ROWS = 1024
FFN_CHUNK = 256
PROJ_CHUNK = 256
ATTN_TQ = 1024
ATTN_TK = ATTN_TQ // 2
DEN_ROWS = 16
ACC_ROWS = V_HEAD + DEN_ROWS

_NEG = float(jnp.finfo(jnp.float32).min)
_NT = (((1,), (1,)), ((), ()))
LOG2E = math.log2(math.e)
MAX_UNSHIFTED_SCORE = 60.0


def _swap_halves(a, axis=-1):
    n = a.shape[axis] // 2
    lo = lax.slice_in_dim(a, 0, n, axis=axis)
    hi = lax.slice_in_dim(a, n, 2 * n, axis=axis)
    return jnp.concatenate([hi, lo], axis=axis)


def _const_spec(shape):
    nd = len(shape)
    return pl.BlockSpec(shape, lambda *_: (0,) * nd, pipeline_mode=pl.Buffered(1))


def _ada_kernel(c_ref, w_ref, b_ref, o_ref):
    c = c_ref[...]
    a = c * jax.nn.sigmoid(c)
    w = w_ref[...]
    a_hi, w_hi = a.astype(BF16), w.astype(BF16)
    a_lo, w_lo = (a - a_hi.astype(F32)).astype(BF16), (w - w_hi.astype(F32)).astype(BF16)
    n = a.shape[0]
    both = jnp.dot(jnp.concatenate([a_hi, a_lo], axis=0), w_hi, preferred_element_type=F32)
    o_ref[...] = both[:n] + both[n:] + jnp.dot(a_hi, w_lo, preferred_element_type=F32) + b_ref[...]


def _ada(c_all, w_ada, b_ada):
    n, d = c_all.shape
    width = w_ada.shape[1]
    bn = d
    return pl.pallas_call(
        _ada_kernel,
        grid=(width // bn,),
        in_specs=[pl.BlockSpec((n, d), lambda j: (0, 0)),
                  pl.BlockSpec((d, bn), lambda j: (0, j)),
                  pl.BlockSpec((1, bn), lambda j: (0, j))],
        out_specs=pl.BlockSpec((n, bn), lambda j: (0, j)),
        out_shape=jax.ShapeDtypeStruct((n, width), F32),
        compiler_params=pltpu.CompilerParams(vmem_limit_bytes=VMEM_LIMIT_BYTES),
        name="ada",
    )(c_all, w_ada, b_ada.reshape(1, width))


def _rms(x, width):
    return lax.rsqrt(jnp.sum(x * x, axis=-1, keepdims=True) * (1.0 / width) + EPS)


def _pre_norm(x_ref, mod_ref, n1g_ref):
    nb, rows, d = x_ref.shape
    x = x_ref[...]
    mod = mod_ref[...]
    sh1, sc1 = mod[:, 0:1], mod[:, 1:2]
    h = x * _rms(x, d) * (n1g_ref[...] * (1.0 + sc1)) + sh1
    return h.reshape(nb * rows, d).astype(BF16)


def _gelu_tanh(x):
    k = -2.0 * math.sqrt(2.0 / math.pi) * LOG2E
    e = jnp.exp2(x * (x * x * (k * 0.044715) + k))
    return x * pl.reciprocal(1.0 + e, approx=False)


def _half_sums(a, lo):
    sq = a * a
    ss_all = jnp.sum(sq, axis=-1, keepdims=True)
    ss_lo = jnp.sum(jnp.where(lo, sq, 0.0), axis=-1, keepdims=True)
    return jnp.where(lo, ss_lo, ss_all - ss_lo)


def _pre_heads(z_ref, ws_ref, bs_ref, qng_ref, wuq_ref, kvg_ref, qrow_ref, krg_ref, tabq_ref, tabk_ref,
               ya_ref, q_ref, ckv_ref, kr_ref, maybe_kv, maybe_vraw_ref, mix_len, tick=lambda: None):
    nb, tile_rows = ya_ref.shape[0], ya_ref.shape[1]
    rows = z_ref.shape[0]
    per_head = lambda a: a.reshape(nb, tile_rows, HEAD_BLOCK)
    o_v, o_q, o_kv, o_kr = GMLP_WIDTH, 2 * GMLP_WIDTH, 2 * GMLP_WIDTH + Q_LORA, 2 * GMLP_WIDTH + Q_LORA + KV_LORA
    lane = lax.broadcasted_iota(jnp.int32, (1, LANES), 1)
    lo = lane < LANES // 2

    kr = z_ref[:, o_kr:o_kr + KR_BLOCK]
    kr_ss = jnp.sum(jnp.where(lane < QK_ROPE, kr * kr, 0.0), axis=-1, keepdims=True)
    kr_p = kr * lax.rsqrt(kr_ss * (1.0 / QK_ROPE) + EPS) * krg_ref[...] * tabk_ref[...]
    krope = kr_p + pltpu.roll(kr_p, LANES - QK_ROPE, axis=1)
    if kr_ref.shape[-1] == QK_ROPE:
        kr_ref[...] = krope[:, :QK_ROPE].reshape(kr_ref.shape)
    else:
        kr_ref[0] = krope.T[:QK_ROPE, :]

    ckv_raw = z_ref[:, o_kv:o_kr]
    ckv = ckv_raw * _rms(ckv_raw, KV_LORA) * kvg_ref[...]
    ckv_ref[...] = ckv.reshape(ckv_ref.shape)
    cq_raw = z_ref[:, o_q:o_kv]
    cq = cq_raw * _rms(cq_raw, Q_LORA) * qng_ref[...]
    tick()
    q = jnp.dot(cq.astype(BF16), wuq_ref[...], preferred_element_type=F32)
    blocks = [q[:, hd * HEAD_BLOCK:(hd + 1) * HEAD_BLOCK] for hd in range(N_HEADS)]
    if maybe_kv is not None:
        wuk_ref, wuvt_ref, k_ref, vt_ref = maybe_kv
        ckv_b = ckv.astype(BF16)
        kp_all = jnp.dot(ckv_b, wuk_ref[...], preferred_element_type=F32)
        vt_ref[0, 0] = lax.dot_general(wuvt_ref[...], ckv_b, _NT, preferred_element_type=F32).astype(vt_ref.dtype)
        blocks += [kp_all[:, pair * LANES:(pair + 1) * LANES] for pair in range(N_HEADS // 2)]

    pi = lax.broadcasted_iota(jnp.int32, (mix_len, mix_len), 0) // CHUNK
    pj = lax.broadcasted_iota(jnp.int32, (mix_len, mix_len), 1) // CHUNK
    for g in range(GMLP_GROUPS):
        tick()
        cols = slice(g * GMLP_HEAD, (g + 1) * GMLP_HEAD)
        u = _gelu_tanh(z_ref[:, cols])
        v = _gelu_tanh(z_ref[:, o_v + g * GMLP_HEAD:o_v + (g + 1) * GMLP_HEAD])
        if maybe_vraw_ref is not None:
            maybe_vraw_ref[:, :, cols] = v.reshape(nb, tile_rows, GMLP_HEAD)
        vb = v.astype(BF16)
        wsg = jnp.where(pj <= pi, ws_ref[g, :mix_len, :mix_len], 0.0).astype(BF16)
        bg = bs_ref[:mix_len, g:g + 1]
        for c in range(rows // mix_len):
            rws = slice(c * mix_len, (c + 1) * mix_len)
            mixed = jnp.dot(wsg, vb[rws], preferred_element_type=F32) + bg
            bi, off = divmod(c * mix_len, tile_rows)
            ya_ref[bi, off:off + mix_len, cols] = (u[rws] * mixed).astype(ya_ref.dtype)
        tick()

    qfac = tabq_ref[...] * qrow_ref[...]
    scales = []
    for blk in blocks:
        scales.append(lax.rsqrt(_half_sums(blk, lo) + QK_NOPE * EPS))
    for hd in range(N_HEADS):
        fac = qfac[:, (hd % 2) * LANES:(hd % 2 + 1) * LANES]
        q_ref[:, hd] = per_head((blocks[hd] * scales[hd] * fac).astype(q_ref.dtype))
    if maybe_kv is not None:
        krope_b = krope.astype(k_ref.dtype)
        for pair in range(N_HEADS // 2):
            kn = (blocks[N_HEADS + pair] * scales[N_HEADS + pair]).astype(k_ref.dtype)
            k_ref[:, 2 * pair] = per_head(jnp.where(lo, kn, krope_b))
            k_ref[:, 2 * pair + 1] = per_head(jnp.where(lo, krope_b, kn))


_PRE_SINGLE_CONSTS = ("n1g", "w_in", "w_s", "b_s", "qng", "w_uq_flat", "kvg", "qrow_flat", "krg")
_PRE_PIPELINED_CONSTS = ("n1g", "w_in", "w_s", "b_s", "qng", "w_uq", "kvg", "qrow", "krg", "w_uk", "w_uvt")


def _pre_single_kernel(x_ref, mod_ref, n1g_ref, win_ref, ws_ref, bs_ref, qng_ref, wuq_ref, kvg_ref, qrow_ref, krg_ref,
                       tabq_ref, tabk_ref, ya_ref, q_ref, ckv_ref, kr_ref, vraw_ref, z_ref, *, mix_len):
    z_ref[...] = jnp.dot(_pre_norm(x_ref, mod_ref, n1g_ref), win_ref[...], preferred_element_type=F32)
    _pre_heads(z_ref, ws_ref, bs_ref, qng_ref, wuq_ref, kvg_ref, qrow_ref, krg_ref, tabq_ref, tabk_ref,
               ya_ref, q_ref, ckv_ref, kr_ref, None, vraw_ref, mix_len)


def _pre_pipelined_kernel(x_ref, mod_ref, n1g_ref, win_ref, ws_ref, bs_ref, qng_ref, wuq_ref, kvg_ref, qrow_ref,
                          krg_ref, wuk_ref, wuvt_ref, tabq_ref, tabk_ref,
                          ya_ref, q_ref, k_ref, vt_ref, ckv_ref, kr_ref, z_ref, *, mix_len):
    i = pl.program_id(0)

    @pl.when(i == 0)
    def _():
        z_ref[...] = jnp.zeros(z_ref.shape, F32)

    for parity in range(2):
        @pl.when(i % 2 == parity)
        def _():
            normed = []

            def project(c):
                if not normed:
                    normed.append(_pre_norm(x_ref, mod_ref, n1g_ref))
                cols = slice(c * PROJ_CHUNK, (c + 1) * PROJ_CHUNK)
                z_ref[parity, :, cols] = jnp.dot(normed[0], win_ref[:, cols], preferred_element_type=F32)

            chunks = list(range(IN_WIDTH_PADDED // PROJ_CHUNK))
            pending = iter([chunks[:2]] + [[c] for c in chunks[2:]])

            def tick():
                for c in next(pending, []):
                    project(c)

            _pre_heads(z_ref.at[1 - parity], ws_ref, bs_ref, qng_ref, wuq_ref, kvg_ref, qrow_ref, krg_ref,
                       tabq_ref, tabk_ref, ya_ref, q_ref, ckv_ref, kr_ref,
                       (wuk_ref, wuvt_ref, k_ref, vt_ref), None, mix_len, tick)
            for group in pending:
                for c in group:
                    project(c)


def _pre_single(x, mod, wts, tabq, tabk, *, mix_len):
    b, t, d = x.shape
    whole = lambda *shape: pl.BlockSpec(shape, lambda i: (0,) * len(shape))
    consts = [wts[n] for n in _PRE_SINGLE_CONSTS]
    return pl.pallas_call(
        functools.partial(_pre_single_kernel, mix_len=mix_len),
        grid=(1,),
        in_specs=([whole(b, t, d), whole(*mod.shape)] + [_const_spec(a.shape) for a in consts]
                  + [whole(*tabq.shape), whole(*tabk.shape)]),
        out_specs=[whole(b, t, GMLP_WIDTH), whole(b, N_HEADS, t, HEAD_BLOCK), whole(b, t, KV_LORA),
                   whole(b, t, QK_ROPE), whole(b, t, GMLP_WIDTH)],
        out_shape=[jax.ShapeDtypeStruct((b, t, GMLP_WIDTH), BF16),
                   jax.ShapeDtypeStruct((b, N_HEADS, t, HEAD_BLOCK), BF16),
                   jax.ShapeDtypeStruct((b, t, KV_LORA), F32),
                   jax.ShapeDtypeStruct((b, t, QK_ROPE), F32),
                   jax.ShapeDtypeStruct((b, t, GMLP_WIDTH), F32)],
        scratch_shapes=[pltpu.VMEM((b * t, IN_WIDTH_PADDED), F32)],
        compiler_params=pltpu.CompilerParams(
            dimension_semantics=("arbitrary",), vmem_limit_bytes=VMEM_LIMIT_BYTES),
        name="pre_single",
    )(x, mod, *consts, tabq, tabk)


def _pre_pipelined(x, mod, wts, tabq, tabk, *, rows, mix_len):
    b, t, d = x.shape
    per_batch = t // rows
    n_tiles = b * per_batch
    cur = lambda i: jnp.minimum(i, n_tiles - 1)
    prev = lambda i: jnp.maximum(i - 1, 0)
    pb, pj = (lambda i: prev(i) // per_batch), (lambda i: prev(i) % per_batch)
    tok_out = lambda w: pl.BlockSpec((1, rows, w), lambda i: (pb(i), pj(i), 0))
    head = pl.BlockSpec((1, N_HEADS, rows, HEAD_BLOCK), lambda i: (pb(i), 0, pj(i), 0))
    tab = lambda a: pl.BlockSpec((rows, a.shape[1]), lambda i: (pj(i), 0))
    consts = [wts[n] for n in _PRE_PIPELINED_CONSTS]
    return pl.pallas_call(
        functools.partial(_pre_pipelined_kernel, mix_len=mix_len),
        grid=(n_tiles + 1,),
        in_specs=([pl.BlockSpec((1, rows, d), lambda i: (cur(i) // per_batch, cur(i) % per_batch, 0)),
                   pl.BlockSpec((1,) + mod.shape[1:], lambda i: (cur(i) // per_batch, 0, 0))]
                  + [_const_spec(a.shape) for a in consts] + [tab(tabq), tab(tabk)]),
        out_specs=[tok_out(GMLP_WIDTH), head, head,
                   pl.BlockSpec((1, 1, MLA_WIDTH, rows), lambda i: (pb(i), pj(i), 0, 0)),
                   tok_out(KV_LORA),
                   pl.BlockSpec((1, QK_ROPE, rows), lambda i: (pb(i), 0, pj(i)))],
        out_shape=[jax.ShapeDtypeStruct((b, t, GMLP_WIDTH), BF16),
                   jax.ShapeDtypeStruct((b, N_HEADS, t, HEAD_BLOCK), BF16),
                   jax.ShapeDtypeStruct((b, N_HEADS, t, HEAD_BLOCK), BF16),
                   jax.ShapeDtypeStruct((b, per_batch, MLA_WIDTH, rows), BF16),
                   jax.ShapeDtypeStruct((b, t, KV_LORA), F32),
                   jax.ShapeDtypeStruct((b, QK_ROPE, t), F32)],
        scratch_shapes=[pltpu.VMEM((2, rows, IN_WIDTH_PADDED), F32)],
        compiler_params=pltpu.CompilerParams(
            dimension_semantics=("arbitrary",), vmem_limit_bytes=VMEM_LIMIT_BYTES),
        name="pre_pipelined",
    )(x, mod, *consts, tabq, tabk)


def _aligned(index, multiple):
    return index if isinstance(index, int) else pl.multiple_of(index, multiple)


def _attn_kernel(q_ref, k_ref, vt_ref, o_ref, acc_ref, *maybe_m_ref, tq, online):
    t = q_ref.shape[2]
    unrolled = not online
    hb = tq // 2
    mb = hb // 2
    key_chunk = lax.broadcasted_iota(jnp.int32, (mb, mb), 0) // CHUNK
    query_chunk = lax.broadcasted_iota(jnp.int32, (mb, mb), 1) // CHUNK
    visible = key_chunk <= query_chunk
    ones = jnp.ones((DEN_ROWS, hb), BF16)

    def operands(hh, q0, qo, rows, kt, c0, keys):
        q = q_ref[0, hh, pl.ds(_aligned(q0 + qo, mb), rows), :]
        k = k_ref[0, hh, pl.ds(_aligned(kt * hb + c0, mb), keys), :]
        lhs = jnp.concatenate([vt_ref[0, kt, hh * V_HEAD:(hh + 1) * V_HEAD, c0:c0 + keys], ones[:, :keys]], axis=0)
        return lhs, lax.dot_general(k, q, _NT, preferred_element_type=F32)

    def unshifted(hh, q0, qo, rows, kt, c0, keys, masked=False):
        lhs, s = operands(hh, q0, qo, rows, kt, c0, keys)
        p = jnp.exp2(s)
        if masked:
            p = jnp.where(visible, p, 0.0)
        return jnp.dot(lhs, p.astype(BF16), preferred_element_type=F32)

    def online_update(hh, q0, qo, rows, kt, c0, keys, masked=False):
        m_ref = maybe_m_ref[0]
        lhs, s = operands(hh, q0, qo, rows, kt, c0, keys)
        if masked:
            s = jnp.where(visible, s, _NEG)
        cols = slice(qo, qo + rows)
        m_old = m_ref[hh, :, cols]
        m_new = jnp.maximum(m_old, jnp.max(s, axis=0, keepdims=True))
        p = jnp.exp2(s - m_new).astype(BF16)
        acc_ref[hh, :, cols] = (jnp.exp2(m_old - m_new) * acc_ref[hh, :, cols]
                                + jnp.dot(lhs, p, preferred_element_type=F32))
        m_ref[hh, :, cols] = m_new

    diagonal = [(0, mb, 0, 0, mb, True), (mb, mb, 0, 0, mb, False), (mb, mb, 0, mb, mb, True),
                (hb, hb, 0, 0, hb, False),
                (hb, mb, 1, 0, mb, True), (hb + mb, mb, 1, 0, mb, False), (hb + mb, mb, 1, mb, mb, True)]

    def q_tile(qi, carry):
        q0 = qi * tq
        acc_ref[...] = jnp.zeros(acc_ref.shape, F32)
        if online:
            maybe_m_ref[0][...] = jnp.full(maybe_m_ref[0].shape, -jnp.inf, F32)

        def body(j, c):
            tiles = [(hh, 2 * j + half) for half in range(2) for hh in range(2)]
            if online:
                for hh, kt in tiles:
                    online_update(hh, q0, 0, tq, kt, 0, hb)
            else:
                staged = [operands(hh, q0, 0, tq, kt, 0, hb) for hh, kt in tiles]
                probs = [jnp.exp2(s).astype(BF16) for _, s in staged]
                for hh in range(2):
                    acc_ref[hh] += sum(jnp.dot(lhs, p, preferred_element_type=F32)
                                       for (th, _), (lhs, _), p in zip(tiles, staged, probs) if th == hh)
            return c

        if unrolled:
            for j in range(qi):
                body(j, 0)
        else:
            lax.fori_loop(0, qi, body, 0)

        totals = []
        for hh in range(2):
            if online:
                for qo, rows, kt_off, c0, keys, masked in diagonal:
                    online_update(hh, q0, qo, rows, qi * 2 + kt_off, c0, keys, masked)
                totals.append([acc_ref[hh, :, g * mb:(g + 1) * mb] for g in range(tq // mb)])
            else:
                staged = [operands(hh, q0, qo, rows, qi * 2 + kt_off, c0, keys)
                          for qo, rows, kt_off, c0, keys, _ in diagonal]
                probs = [jnp.exp2(s) for _, s in staged]
                probs = [jnp.where(visible, p, 0.0) if blk[5] else p for p, blk in zip(probs, diagonal)]
                parts = [jnp.dot(lhs, p.astype(BF16), preferred_element_type=F32)
                         for (lhs, _), p in zip(staged, probs)]
                groups = [acc_ref[hh, :, g * mb:(g + 1) * mb] for g in range(tq // mb)]
                for (qo, rows, *_), part in zip(diagonal, parts):
                    for g in range(rows // mb):
                        groups[qo // mb + g] = groups[qo // mb + g] + part[:, g * mb:(g + 1) * mb]
                totals.append(groups)
        for g in range(tq // mb):
            o_t = [tot[g][:V_HEAD] * pl.reciprocal(tot[g][V_HEAD:V_HEAD + 1], approx=False) for tot in totals]
            o_ref[0, pl.ds(_aligned(q0 + g * mb, mb), mb), :] = jnp.concatenate(o_t, axis=0).T.astype(o_ref.dtype)
        return carry

    if unrolled:
        for qi in range(t // tq):
            q_tile(qi, 0)
    else:
        lax.fori_loop(0, t // tq, q_tile, 0)


def _attn_prompt(q, k, vt, score_bound):
    b, nh, t, _ = q.shape
    n_kt, _, hb = vt.shape[1:]
    assert hb == ATTN_TK and n_kt * hb == t
    qk = pl.BlockSpec((1, 2, t, HEAD_BLOCK), lambda i, j: (i, j, 0, 0))

    def call(online):
        scratch = [pltpu.VMEM((2, ACC_ROWS, ATTN_TQ), F32)]
        if online:
            scratch.append(pltpu.VMEM((2, 1, ATTN_TQ), F32))
        return pl.pallas_call(
            functools.partial(_attn_kernel, tq=ATTN_TQ, online=online),
            grid=(b, nh // 2),
            in_specs=[qk, qk, pl.BlockSpec((1, n_kt, 2 * V_HEAD, hb), lambda i, j: (i, 0, j, 0))],
            out_specs=pl.BlockSpec((1, t, 2 * V_HEAD), lambda i, j: (i, 0, j)),
            out_shape=jax.ShapeDtypeStruct((b, t, MLA_WIDTH), BF16),
            scratch_shapes=scratch,
            compiler_params=pltpu.CompilerParams(
                dimension_semantics=("parallel", "parallel"), vmem_limit_bytes=VMEM_LIMIT_BYTES),
            name="attn_prompt_online" if online else "attn_prompt_bounded")

    return lax.cond(jnp.logical_not(score_bound <= MAX_UNSHIFTED_SCORE), call(True), call(False), q, k, vt)


def _pair_out(acc0, acc1):
    first = lax.broadcasted_iota(jnp.int32, (1, LANES), 1) < V_HEAD
    num = jnp.where(first, pltpu.roll(acc0, V_HEAD, axis=1), acc1)
    den = jnp.where(first, acc0, pltpu.roll(acc1, V_HEAD, axis=1))
    return num * pl.reciprocal(den, approx=False)


def _attn_sample_kernel(q_ref, past_ckv_ref, past_kr_ref, new_ckv_ref, new_kr_ref, wukv_ref, o_ref):
    lane = lax.broadcasted_iota(jnp.int32, (1, LANES), 1)
    is_nope = lane < QK_NOPE
    spread = (lax.broadcasted_iota(jnp.int32, (QK_ROPE, LANES), 1) % QK_ROPE
              == lax.broadcasted_iota(jnp.int32, (QK_ROPE, LANES), 0)).astype(BF16)
    one_b = jnp.ones((1, LANES), BF16)

    def segment(ckv_ref, kr_ref):
        kv = jnp.dot(ckv_ref[0].astype(BF16), wukv_ref[...], preferred_element_type=F32)
        krope_b = jnp.dot(kr_ref[0].astype(BF16), spread, preferred_element_type=F32).astype(BF16)
        return kv, krope_b

    def head_kv(seg, hd):
        kv, krope_b = seg
        kvh = kv[:, hd * HEAD_BLOCK:(hd + 1) * HEAD_BLOCK]
        ks = jnp.sum(jnp.where(is_nope, kvh * kvh, 0.0), axis=-1, keepdims=True)
        kn = kvh * lax.rsqrt(ks + QK_NOPE * EPS)
        return (jnp.where(is_nope, kn.astype(BF16), krope_b), jnp.where(is_nope, one_b, kvh.astype(BF16)))

    past, new = segment(past_ckv_ref, past_kr_ref), segment(new_ckv_ref, new_kr_ref)
    heads = range(N_HEADS)
    kv = [(head_kv(past, hd), head_kv(new, hd)) for hd in heads]
    scores = [[lax.dot_general(q_ref[0, hd], k, _NT, preferred_element_type=F32) for k, _ in kv[hd]] for hd in heads]
    peaks = [jnp.maximum(*(jnp.max(s, axis=-1, keepdims=True) for s in scores[hd])) for hd in heads]
    probs = [[jnp.exp2(s - peaks[hd]).astype(BF16) for s in scores[hd]] for hd in heads]
    accs = [sum(jnp.dot(p, v, preferred_element_type=F32) for p, (_, v) in zip(probs[hd], kv[hd])) for hd in heads]
    for hd in range(0, N_HEADS, 2):
        o_ref[0, :, hd * V_HEAD:(hd + 2) * V_HEAD] = _pair_out(accs[hd], accs[hd + 1]).astype(o_ref.dtype)


def _attn_sample(q, past_ckv, past_kr, new_ckv, new_kr, w_ukv):
    b, nh, t, _ = q.shape
    per_batch = lambda a: pl.BlockSpec((1,) + a.shape[1:], lambda i: (i,) + (0,) * (a.ndim - 1))
    return pl.pallas_call(
        _attn_sample_kernel,
        grid=(b,),
        in_specs=[per_batch(q), per_batch(past_ckv), per_batch(past_kr), per_batch(new_ckv), per_batch(new_kr),
                  _const_spec(w_ukv.shape)],
        out_specs=pl.BlockSpec((1, t, MLA_WIDTH), lambda i: (i, 0, 0)),
        out_shape=jax.ShapeDtypeStruct((b, t, MLA_WIDTH), BF16),
        compiler_params=pltpu.CompilerParams(
            dimension_semantics=("parallel",), vmem_limit_bytes=VMEM_LIMIT_BYTES),
        name="attn_sample",
    )(q, past_ckv, past_kr, new_ckv, new_kr, w_ukv)


def _post_mixer(x_ref, ya_ref, yb_ref, mod_ref, wout_ref, n2g_ref):
    nb, rows, d = x_ref.shape
    flat = lambda a: a.reshape(nb * rows, a.shape[-1])
    mod = mod_ref[...]
    g1, sh2, sc2 = mod[:, 2:3], mod[:, 3:4], mod[:, 4:5]
    mixer = jnp.concatenate([flat(ya_ref[...]), flat(yb_ref[...])], axis=-1)
    mix = jnp.dot(mixer, wout_ref[...], preferred_element_type=F32)
    x1 = x_ref[...] + g1 * mix.reshape(nb, rows, d)
    h2 = flat(x1 * _rms(x1, d) * (n2g_ref[...] * (1.0 + sc2)) + sh2).astype(BF16)
    return x1, h2


def _post_ffn_chunk(h2, c, wfi_ref, wfo_ref, acc_ref):
    d_ff = wfo_ref.shape[0]
    gate = jnp.dot(h2, wfi_ref[:, c * FFN_CHUNK:(c + 1) * FFN_CHUNK], preferred_element_type=F32)
    up = jnp.dot(h2, wfi_ref[:, d_ff + c * FFN_CHUNK:d_ff + (c + 1) * FFN_CHUNK], preferred_element_type=F32)
    act = (gate * jax.nn.sigmoid(gate) * up).astype(BF16)
    part = jnp.dot(act, wfo_ref[c * FFN_CHUNK:(c + 1) * FFN_CHUNK, :], preferred_element_type=F32)
    if c == 0:
        acc_ref[...] = part
    else:
        acc_ref[...] += part


def _post_kernel(x_ref, ya_ref, yb_ref, mod_ref, wout_ref, n2g_ref, wfi_ref, wfo_ref, o_ref, acc_ref):
    x1, h2 = _post_mixer(x_ref, ya_ref, yb_ref, mod_ref, wout_ref, n2g_ref)
    o_ref[...] = x1
    for c in range(wfo_ref.shape[0] // FFN_CHUNK):
        _post_ffn_chunk(h2, c, wfi_ref, wfo_ref, acc_ref)
    o_ref[...] = o_ref[...] + mod_ref[:, 5:6] * acc_ref[...].reshape(o_ref.shape)


def _post(x, ya, yb, mod, wts, *, nb, rows):
    b, t, d = x.shape
    tok = lambda w: pl.BlockSpec((nb, rows, w), lambda i, j: (i, j, 0))
    consts = [wts[n] for n in ("w_out", "n2g", "w_ffn_in", "w_ffn_out")]
    return pl.pallas_call(
        _post_kernel,
        grid=(b // nb, t // rows),
        in_specs=[tok(d), tok(GMLP_WIDTH), tok(MLA_WIDTH),
                  pl.BlockSpec((nb,) + mod.shape[1:], lambda i, j: (i, 0, 0)),
                  _const_spec(consts[0].shape), _const_spec(consts[1].shape),
                  _const_spec(consts[2].shape), _const_spec(consts[3].shape)],
        out_specs=tok(d),
        out_shape=jax.ShapeDtypeStruct((b, t, d), F32),
        scratch_shapes=[pltpu.VMEM((nb * rows, d), F32)],
        compiler_params=pltpu.CompilerParams(
            dimension_semantics=("parallel", "parallel"), vmem_limit_bytes=VMEM_LIMIT_BYTES),
        name="post",
    )(x, ya, yb, mod, consts[0], consts[1], consts[2], consts[3])


def _rope_tables(first_pos, n, *, mirrored_odd, repeat=1):
    half = QK_ROPE // 2
    inv = ROPE_THETA ** (-np.arange(half, dtype=np.float64) / half)
    ang = (first_pos + np.arange(n, dtype=np.float64))[:, None] * inv[None, :]
    cos, sin = np.cos(ang), np.sin(ang)
    rot = np.concatenate([cos, cos, -sin, sin], axis=-1)
    nope = np.ones((n, QK_NOPE))
    tabq = (SCALE * LOG2E) * np.concatenate([nope, rot] + ([rot, nope] if mirrored_odd else [nope, rot]), axis=-1)
    tabk = np.concatenate([rot, rot], axis=-1)
    return jnp.asarray(np.tile(tabq, (repeat, 1)), F32), jnp.asarray(np.tile(tabk, (repeat, 1)), F32)


def _prep_weights(w_in, w_s, b_s, q_norm_g, w_uq, kv_norm_g, w_ukv, qn_g, qr_g, kn_g, kr_g,
                  norm1_g, w_out, norm2_g, w_ffn_in, w_ffn_out):
    o_kr = 2 * GMLP_WIDTH + Q_LORA + KV_LORA
    w_kr = w_in[:, o_kr:o_kr + QK_ROPE]
    w_kr_sw = _swap_halves(w_kr)
    w_in_p = jnp.concatenate([w_in[:, :o_kr], w_kr, w_kr_sw, w_kr, w_kr_sw], axis=-1)
    d_q = w_uq.shape[0]
    wq = w_uq.reshape(d_q, N_HEADS, QK_HEAD)
    wq_n, wq_r = wq[:, :, :QK_NOPE], wq[:, :, QK_NOPE:]
    wq_plain = jnp.concatenate([wq_n, wq_r, _swap_halves(wq_r)], axis=-1)
    wq_mirror = jnp.concatenate([wq_r, _swap_halves(wq_r), wq_n], axis=-1)
    odd = (jnp.arange(N_HEADS) % 2 == 1)[None, :, None]
    w_uq_flat = wq_plain.reshape(d_q, N_HEADS * HEAD_BLOCK)
    w_uq_pair = jnp.where(odd, wq_mirror, wq_plain).reshape(d_q, N_HEADS * HEAD_BLOCK)
    wkv = w_ukv.reshape(KV_LORA, N_HEADS, QK_NOPE + V_HEAD)
    w_uk = wkv[:, :, :QK_NOPE].reshape(KV_LORA, N_HEADS * QK_NOPE)
    w_uvt = wkv[:, :, QK_NOPE:].reshape(KV_LORA, MLA_WIDTH).T
    root = math.sqrt(QK_NOPE)
    q_nope_row = qn_g * kn_g * (root * root)
    q_rope_row = jnp.concatenate([qr_g, _swap_halves(qr_g)]) * root
    qrow_plain = jnp.concatenate([q_nope_row, q_rope_row])
    qrow_mirror = jnp.concatenate([q_rope_row, q_nope_row])
    row = lambda a: a.reshape(1, -1).astype(F32)
    return {
        "n1g": row(norm1_g), "w_in": w_in_p.astype(BF16), "w_s": w_s.astype(F32), "b_s": b_s.T.astype(F32),
        "qng": row(q_norm_g), "kvg": row(kv_norm_g),
        "w_uq": w_uq_pair.astype(BF16), "w_uq_flat": w_uq_flat.astype(BF16),
        "qrow": row(jnp.concatenate([qrow_plain, qrow_mirror])),
        "qrow_flat": row(jnp.concatenate([qrow_plain, qrow_plain])),
        "w_ukv": w_ukv.astype(BF16), "w_uk": w_uk.astype(BF16), "w_uvt": w_uvt.astype(BF16),
        "krg": row(jnp.concatenate([kr_g, _swap_halves(kr_g), kr_g, _swap_halves(kr_g)])),
        "w_out": w_out.astype(BF16), "n2g": row(norm2_g),
        "w_ffn_in": w_ffn_in.astype(BF16), "w_ffn_out": w_ffn_out.astype(BF16),
    }


def kernel(x_prompt, x_sample, cache_ckv, cache_krope, c_prompt, c_sample, w_ada, b_ada, norm1_g, w_in, w_s, b_s, q_norm_g, w_uq, kv_norm_g, w_ukv, qn_g, qr_g, kn_g, kr_g, w_out, norm2_g, w_ffn_in, w_ffn_out):
    depth = w_in.shape[0]
    assert depth == 1, "single trunk layer"
    assert PRE_ROWS == ATTN_TK
    bp, tp, d = x_prompt.shape
    bs, ts, _ = x_sample.shape
    past_len = cache_ckv.shape[2]

    wts = _prep_weights(w_in[0], w_s[0], b_s[0], q_norm_g[0], w_uq[0], kv_norm_g[0], w_ukv[0], qn_g[0], qr_g[0],
                        kn_g[0], kr_g[0], norm1_g[0], w_out[0], norm2_g[0], w_ffn_in[0], w_ffn_out[0])
    mod = _ada(jnp.concatenate([c_prompt, c_sample], axis=0), w_ada[0], b_ada[0]).reshape(bp + bs, 6, d)
    mod_p, mod_s = mod[:bp], mod[bp:]

    tabq_p, tabk_p = _rope_tables(0, tp, mirrored_odd=True)
    ya_p, q_p, k_p, vt_p, ckv_p, krt_p = _pre_pipelined(x_prompt, mod_p, wts, tabq_p, tabk_p,
                                                        rows=PRE_ROWS, mix_len=GMLP_CHUNK)
    gmax = lambda g: jnp.max(jnp.abs(g))
    score_bound = SCALE * (QK_NOPE * gmax(qn_g) * gmax(kn_g) + QK_ROPE * gmax(qr_g) * gmax(kr_g))
    yb_p = _attn_prompt(q_p, k_p, vt_p, score_bound)
    y_p = _post(x_prompt, ya_p, yb_p, mod_p, wts, nb=1, rows=POST_ROWS)

    tabq_s, tabk_s = _rope_tables(past_len, ts, mirrored_odd=False, repeat=bs)
    ya_s, q_s, ckv_s, kr_s, v_s = _pre_single(x_sample, mod_s, wts, tabq_s, tabk_s, mix_len=min(ts, GMLP_CHUNK))
    yb_s = _attn_sample(q_s, cache_ckv[0], cache_krope[0], ckv_s, kr_s, wts["w_ukv"])
    y_s = _post(x_sample, ya_s, yb_s, mod_s, wts, nb=bs, rows=ts)

    return (y_p, y_s, ckv_p[None], jnp.swapaxes(krt_p, 1, 2)[None], ckv_s[None], kr_s[None], v_s[None])
```

```python
import functools
import math

import jax
import jax.numpy as jnp
import numpy as np
from jax import lax
from jax.experimental import pallas as pl
from jax.experimental.pallas import tpu as pltpu

F32 = jnp.float32
BF16 = jnp.bfloat16

CHUNK = 64
GMLP_CHUNK = 128
GMLP_GROUPS = 4
GMLP_HEAD = 128
GMLP_WIDTH = GMLP_GROUPS * GMLP_HEAD
N_HEADS = 8
QK_NOPE = 64
QK_ROPE = 32
QK_HEAD = QK_NOPE + QK_ROPE
V_HEAD = 64
Q_LORA = 384
KV_LORA = 256
MLA_WIDTH = N_HEADS * V_HEAD
ROPE_THETA = 10000.0
EPS = 1e-6
SCALE = QK_HEAD ** -0.5

LANES = 128
HEAD_BLOCK = LANES
KR_BLOCK = LANES
IN_WIDTH_PADDED = 2 * GMLP_WIDTH + Q_LORA + KV_LORA + KR_BLOCK
VMEM_LIMIT_BYTES = 56 * 1024 * 1024

PRE_ROWS = 512
POST_ROWS = 1024
FFN_CHUNK = 256
PROJ_CHUNK = 256
ATTN_TQ = 1024
ATTN_TK = ATTN_TQ // 2
DEN_ROWS = 16
ACC_ROWS = V_HEAD + DEN_ROWS

_NEG = float(jnp.finfo(jnp.float32).min)
_NT = (((1,), (1,)), ((), ()))
LOG2E = math.log2(math.e)
MAX_UNSHIFTED_SCORE = 60.0


def _swap_halves(a, axis=-1):
    n = a.shape[axis] // 2
    lo = lax.slice_in_dim(a, 0, n, axis=axis)
    hi = lax.slice_in_dim(a, n, 2 * n, axis=axis)
    return jnp.concatenate([hi, lo], axis=axis)


def _const_spec(shape):
    nd = len(shape)
    return pl.BlockSpec(shape, lambda *_: (0,) * nd, pipeline_mode=pl.Buffered(1))


def _ada_kernel(c_ref, w_ref, b_ref, o_ref):
    c = c_ref[...]
    a = c * jax.nn.sigmoid(c)
    w = w_ref[...]
    a_hi, w_hi = a.astype(BF16), w.astype(BF16)
    a_lo, w_lo = (a - a_hi.astype(F32)).astype(BF16), (w - w_hi.astype(F32)).astype(BF16)
    n = a.shape[0]
    both = jnp.dot(jnp.concatenate([a_hi, a_lo], axis=0), w_hi, preferred_element_type=F32)
    o_ref[...] = both[:n] + both[n:] + jnp.dot(a_hi, w_lo, preferred_element_type=F32) + b_ref[...]


def _ada(c_all, w_ada, b_ada):
    n, d = c_all.shape
    width = w_ada.shape[1]
    bn = d
    return pl.pallas_call(
        _ada_kernel,
        grid=(width // bn,),
        in_specs=[pl.BlockSpec((n, d), lambda j: (0, 0)),
                  pl.BlockSpec((d, bn), lambda j: (0, j)),
                  pl.BlockSpec((1, bn), lambda j: (0, j))],
        out_specs=pl.BlockSpec((n, bn), lambda j: (0, j)),
        out_shape=jax.ShapeDtypeStruct((n, width), F32),
        compiler_params=pltpu.CompilerParams(vmem_limit_bytes=VMEM_LIMIT_BYTES),
        name="ada",
    )(c_all, w_ada, b_ada.reshape(1, width))


def _rms(x, width):
    return lax.rsqrt(jnp.sum(x * x, axis=-1, keepdims=True) * (1.0 / width) + EPS)


def _pre_norm(x_ref, mod_ref, n1g_ref):
    nb, rows, d = x_ref.shape
    x = x_ref[...]
    mod = mod_ref[...]
    sh1, sc1 = mod[:, 0:1], mod[:, 1:2]
    h = x * _rms(x, d) * (n1g_ref[...] * (1.0 + sc1)) + sh1
    return h.reshape(nb * rows, d).astype(BF16)


def _gelu_tanh(x):
    k = -2.0 * math.sqrt(2.0 / math.pi) * LOG2E
    e = jnp.exp2(x * (x * x * (k * 0.044715) + k))
    return x * pl.reciprocal(1.0 + e, approx=False)


def _half_sums(a, lo):
    sq = a * a
    ss_all = jnp.sum(sq, axis=-1, keepdims=True)
    ss_lo = jnp.sum(jnp.where(lo, sq, 0.0), axis=-1, keepdims=True)
    return jnp.where(lo, ss_lo, ss_all - ss_lo)


def _pre_heads(z_ref, ws_ref, bs_ref, qng_ref, wuq_ref, kvg_ref, qrow_ref, krg_ref, tabq_ref, tabk_ref,
               ya_ref, q_ref, ckv_ref, kr_ref, maybe_kv, maybe_vraw_ref, mix_len, tick=lambda: None):
    nb, tile_rows = ya_ref.shape[0], ya_ref.shape[1]
    rows = z_ref.shape[0]
    per_head = lambda a: a.reshape(nb, tile_rows, HEAD_BLOCK)
    o_v, o_q, o_kv, o_kr = GMLP_WIDTH, 2 * GMLP_WIDTH, 2 * GMLP_WIDTH + Q_LORA, 2 * GMLP_WIDTH + Q_LORA + KV_LORA
    lane = lax.broadcasted_iota(jnp.int32, (1, LANES), 1)
    lo = lane < LANES // 2

    kr = z_ref[:, o_kr:o_kr + KR_BLOCK]
    kr_ss = jnp.sum(jnp.where(lane < QK_ROPE, kr * kr, 0.0), axis=-1, keepdims=True)
    kr_p = kr * lax.rsqrt(kr_ss * (1.0 / QK_ROPE) + EPS) * krg_ref[...] * tabk_ref[...]
    krope = kr_p + pltpu.roll(kr_p, LANES - QK_ROPE, axis=1)
    if kr_ref.shape[-1] == QK_ROPE:
        kr_ref[...] = krope[:, :QK_ROPE].reshape(kr_ref.shape)
    else:
        kr_ref[0] = krope.T[:QK_ROPE, :]

    ckv_raw = z_ref[:, o_kv:o_kr]
    ckv = ckv_raw * _rms(ckv_raw, KV_LORA) * kvg_ref[...]
    ckv_ref[...] = ckv.reshape(ckv_ref.shape)
    cq_raw = z_ref[:, o_q:o_kv]
    cq = cq_raw * _rms(cq_raw, Q_LORA) * qng_ref[...]
    tick()
    q = jnp.dot(cq.astype(BF16), wuq_ref[...], preferred_element_type=F32)
    blocks = [q[:, hd * HEAD_BLOCK:(hd + 1) * HEAD_BLOCK] for hd in range(N_HEADS)]
    if maybe_kv is not None:
        wuk_ref, wuvt_ref, k_ref, vt_ref = maybe_kv
        ckv_b = ckv.astype(BF16)
        kp_all = jnp.dot(ckv_b, wuk_ref[...], preferred_element_type=F32)
        vt_ref[0, 0] = lax.dot_general(wuvt_ref[...], ckv_b, _NT, preferred_element_type=F32).astype(vt_ref.dtype)
        blocks += [kp_all[:, pair * LANES:(pair + 1) * LANES] for pair in range(N_HEADS // 2)]

    pi = lax.broadcasted_iota(jnp.int32, (mix_len, mix_len), 0) // CHUNK
    pj = lax.broadcasted_iota(jnp.int32, (mix_len, mix_len), 1) // CHUNK
    for g in range(GMLP_GROUPS):
        tick()
        cols = slice(g * GMLP_HEAD, (g + 1) * GMLP_HEAD)
        u = _gelu_tanh(z_ref[:, cols])
        v = _gelu_tanh(z_ref[:, o_v + g * GMLP_HEAD:o_v + (g + 1) * GMLP_HEAD])
        if maybe_vraw_ref is not None:
            maybe_vraw_ref[:, :, cols] = v.reshape(nb, tile_rows, GMLP_HEAD)
        vb = v.astype(BF16)
        wsg = jnp.where(pj <= pi, ws_ref[g, :mix_len, :mix_len], 0.0).astype(BF16)
        bg = bs_ref[:mix_len, g:g + 1]
        for c in range(rows // mix_len):
            rws = slice(c * mix_len, (c + 1) * mix_len)
            mixed = jnp.dot(wsg, vb[rws], preferred_element_type=F32) + bg
            bi, off = divmod(c * mix_len, tile_rows)
            ya_ref[bi, off:off + mix_len, cols] = (u[rws] * mixed).astype(ya_ref.dtype)
        tick()

    qfac = tabq_ref[...] * qrow_ref[...]
    scales = []
    for blk in blocks:
        scales.append(lax.rsqrt(_half_sums(blk, lo) + QK_NOPE * EPS))
    for hd in range(N_HEADS):
        fac = qfac[:, (hd % 2) * LANES:(hd % 2 + 1) * LANES]
        q_ref[:, hd] = per_head((blocks[hd] * scales[hd] * fac).astype(q_ref.dtype))
    if maybe_kv is not None:
        krope_b = krope.astype(k_ref.dtype)
        for pair in range(N_HEADS // 2):
            kn = (blocks[N_HEADS + pair] * scales[N_HEADS + pair]).astype(k_ref.dtype)
            k_ref[:, 2 * pair] = per_head(jnp.where(lo, kn, krope_b))
            k_ref[:, 2 * pair + 1] = per_head(jnp.where(lo, krope_b, kn))


_PRE_SINGLE_CONSTS = ("n1g", "w_in", "w_s", "b_s", "qng", "w_uq_flat", "kvg", "qrow_flat", "krg")
_PRE_PIPELINED_CONSTS = ("n1g", "w_in", "w_s", "b_s", "qng", "w_uq", "kvg", "qrow", "krg", "w_uk", "w_uvt")


def _pre_single_kernel(x_ref, mod_ref, n1g_ref, win_ref, ws_ref, bs_ref, qng_ref, wuq_ref, kvg_ref, qrow_ref, krg_ref,
                       tabq_ref, tabk_ref, ya_ref, q_ref, ckv_ref, kr_ref, vraw_ref, z_ref, *, mix_len):
    z_ref[...] = jnp.dot(_pre_norm(x_ref, mod_ref, n1g_ref), win_ref[...], preferred_element_type=F32)
    _pre_heads(z_ref, ws_ref, bs_ref, qng_ref, wuq_ref, kvg_ref, qrow_ref, krg_ref, tabq_ref, tabk_ref,
               ya_ref, q_ref, ckv_ref, kr_ref, None, vraw_ref, mix_len)


def _pre_pipelined_kernel(x_ref, mod_ref, n1g_ref, win_ref, ws_ref, bs_ref, qng_ref, wuq_ref, kvg_ref, qrow_ref,
                          krg_ref, wuk_ref, wuvt_ref, tabq_ref, tabk_ref,
                          ya_ref, q_ref, k_ref, vt_ref, ckv_ref, kr_ref, z_ref, *, mix_len):
    i = pl.program_id(0)

    @pl.when(i == 0)
    def _():
        z_ref[...] = jnp.zeros(z_ref.shape, F32)

    for parity in range(2):
        @pl.when(i % 2 == parity)
        def _():
            normed = []

            def project(c):
                if not normed:
                    normed.append(_pre_norm(x_ref, mod_ref, n1g_ref))
                cols = slice(c * PROJ_CHUNK, (c + 1) * PROJ_CHUNK)
                z_ref[parity, :, cols] = jnp.dot(normed[0], win_ref[:, cols], preferred_element_type=F32)

            chunks = list(range(IN_WIDTH_PADDED // PROJ_CHUNK))
            pending = iter([chunks[:2]] + [[c] for c in chunks[2:]])

            def tick():
                for c in next(pending, []):
                    project(c)

            _pre_heads(z_ref.at[1 - parity], ws_ref, bs_ref, qng_ref, wuq_ref, kvg_ref, qrow_ref, krg_ref,
                       tabq_ref, tabk_ref, ya_ref, q_ref, ckv_ref, kr_ref,
                       (wuk_ref, wuvt_ref, k_ref, vt_ref), None, mix_len, tick)
            for group in pending:
                for c in group:
                    project(c)


def _pre_single(x, mod, wts, tabq, tabk, *, mix_len):
    b, t, d = x.shape
    whole = lambda *shape: pl.BlockSpec(shape, lambda i: (0,) * len(shape))
    consts = [wts[n] for n in _PRE_SINGLE_CONSTS]
    return pl.pallas_call(
        functools.partial(_pre_single_kernel, mix_len=mix_len),
        grid=(1,),
        in_specs=([whole(b, t, d), whole(*mod.shape)] + [_const_spec(a.shape) for a in consts]
                  + [whole(*tabq.shape), whole(*tabk.shape)]),
        out_specs=[whole(b, t, GMLP_WIDTH), whole(b, N_HEADS, t, HEAD_BLOCK), whole(b, t, KV_LORA),
                   whole(b, t, QK_ROPE), whole(b, t, GMLP_WIDTH)],
        out_shape=[jax.ShapeDtypeStruct((b, t, GMLP_WIDTH), BF16),
                   jax.ShapeDtypeStruct((b, N_HEADS, t, HEAD_BLOCK), BF16),
                   jax.ShapeDtypeStruct((b, t, KV_LORA), F32),
                   jax.ShapeDtypeStruct((b, t, QK_ROPE), F32),
                   jax.ShapeDtypeStruct((b, t, GMLP_WIDTH), F32)],
        scratch_shapes=[pltpu.VMEM((b * t, IN_WIDTH_PADDED), F32)],
        compiler_params=pltpu.CompilerParams(
            dimension_semantics=("arbitrary",), vmem_limit_bytes=VMEM_LIMIT_BYTES),
        name="pre_single",
    )(x, mod, *consts, tabq, tabk)


def _pre_pipelined(x, mod, wts, tabq, tabk, *, rows, mix_len):
    b, t, d = x.shape
    per_batch = t // rows
    n_tiles = b * per_batch
    cur = lambda i: jnp.minimum(i, n_tiles - 1)
    prev = lambda i: jnp.maximum(i - 1, 0)
    pb, pj = (lambda i: prev(i) // per_batch), (lambda i: prev(i) % per_batch)
    tok_out = lambda w: pl.BlockSpec((1, rows, w), lambda i: (pb(i), pj(i), 0))
    head = pl.BlockSpec((1, N_HEADS, rows, HEAD_BLOCK), lambda i: (pb(i), 0, pj(i), 0))
    tab = lambda a: pl.BlockSpec((rows, a.shape[1]), lambda i: (pj(i), 0))
    consts = [wts[n] for n in _PRE_PIPELINED_CONSTS]
    return pl.pallas_call(
        functools.partial(_pre_pipelined_kernel, mix_len=mix_len),
        grid=(n_tiles + 1,),
        in_specs=([pl.BlockSpec((1, rows, d), lambda i: (cur(i) // per_batch, cur(i) % per_batch, 0)),
                   pl.BlockSpec((1,) + mod.shape[1:], lambda i: (cur(i) // per_batch, 0, 0))]
                  + [_const_spec(a.shape) for a in consts] + [tab(tabq), tab(tabk)]),
        out_specs=[tok_out(GMLP_WIDTH), head, head,
                   pl.BlockSpec((1, 1, MLA_WIDTH, rows), lambda i: (pb(i), pj(i), 0, 0)),
                   tok_out(KV_LORA),
                   pl.BlockSpec((1, QK_ROPE, rows), lambda i: (pb(i), 0, pj(i)))],
        out_shape=[jax.ShapeDtypeStruct((b, t, GMLP_WIDTH), BF16),
                   jax.ShapeDtypeStruct((b, N_HEADS, t, HEAD_BLOCK), BF16),
                   jax.ShapeDtypeStruct((b, N_HEADS, t, HEAD_BLOCK), BF16),
                   jax.ShapeDtypeStruct((b, per_batch, MLA_WIDTH, rows), BF16),
                   jax.ShapeDtypeStruct((b, t, KV_LORA), F32),
                   jax.ShapeDtypeStruct((b, QK_ROPE, t), F32)],
        scratch_shapes=[pltpu.VMEM((2, rows, IN_WIDTH_PADDED), F32)],
        compiler_params=pltpu.CompilerParams(
            dimension_semantics=("arbitrary",), vmem_limit_bytes=VMEM_LIMIT_BYTES),
        name="pre_pipelined",
    )(x, mod, *consts, tabq, tabk)


def _aligned(index, multiple):
    return index if isinstance(index, int) else pl.multiple_of(index, multiple)


def _attn_kernel(q_ref, k_ref, vt_ref, o_ref, acc_ref, *maybe_m_ref, tq, online):
    t = q_ref.shape[2]
    unrolled = not online
    hb = tq // 2
    mb = hb // 2
    key_chunk = lax.broadcasted_iota(jnp.int32, (mb, mb), 0) // CHUNK
    query_chunk = lax.broadcasted_iota(jnp.int32, (mb, mb), 1) // CHUNK
    visible = key_chunk <= query_chunk
    ones = jnp.ones((DEN_ROWS, hb), BF16)

    def operands(hh, q0, qo, rows, kt, c0, keys):
        q = q_ref[0, hh, pl.ds(_aligned(q0 + qo, mb), rows), :]
        k = k_ref[0, hh, pl.ds(_aligned(kt * hb + c0, mb), keys), :]
        lhs = jnp.concatenate([vt_ref[0, kt, hh * V_HEAD:(hh + 1) * V_HEAD, c0:c0 + keys], ones[:, :keys]], axis=0)
        return lhs, lax.dot_general(k, q, _NT, preferred_element_type=F32)

    def unshifted(hh, q0, qo, rows, kt, c0, keys, masked=False):
        lhs, s = operands(hh, q0, qo, rows, kt, c0, keys)
        p = jnp.exp2(s)
        if masked:
            p = jnp.where(visible, p, 0.0)
        return jnp.dot(lhs, p.astype(BF16), preferred_element_type=F32)

    def online_update(hh, q0, qo, rows, kt, c0, keys, masked=False):
        m_ref = maybe_m_ref[0]
        lhs, s = operands(hh, q0, qo, rows, kt, c0, keys)
        if masked:
            s = jnp.where(visible, s, _NEG)
        cols = slice(qo, qo + rows)
        m_old = m_ref[hh, :, cols]
        m_new = jnp.maximum(m_old, jnp.max(s, axis=0, keepdims=True))
        p = jnp.exp2(s - m_new).astype(BF16)
        acc_ref[hh, :, cols] = (jnp.exp2(m_old - m_new) * acc_ref[hh, :, cols]
                                + jnp.dot(lhs, p, preferred_element_type=F32))
        m_ref[hh, :, cols] = m_new

    diagonal = [(0, mb, 0, 0, mb, True), (mb, mb, 0, 0, mb, False), (mb, mb, 0, mb, mb, True),
                (hb, hb, 0, 0, hb, False),
                (hb, mb, 1, 0, mb, True), (hb + mb, mb, 1, 0, mb, False), (hb + mb, mb, 1, mb, mb, True)]

    def q_tile(qi, carry):
        q0 = qi * tq
        acc_ref[...] = jnp.zeros(acc_ref.shape, F32)
        if online:
            maybe_m_ref[0][...] = jnp.full(maybe_m_ref[0].shape, -jnp.inf, F32)

        def body(j, c):
            tiles = [(hh, 2 * j + half) for half in range(2) for hh in range(2)]
            if online:
                for hh, kt in tiles:
                    online_update(hh, q0, 0, tq, kt, 0, hb)
            else:
                staged = [operands(hh, q0, 0, tq, kt, 0, hb) for hh, kt in tiles]
                probs = [jnp.exp2(s).astype(BF16) for _, s in staged]
                for hh in range(2):
                    acc_ref[hh] += sum(jnp.dot(lhs, p, preferred_element_type=F32)
                                       for (th, _), (lhs, _), p in zip(tiles, staged, probs) if th == hh)
            return c

        if unrolled:
            for j in range(qi):
                body(j, 0)
        else:
            lax.fori_loop(0, qi, body, 0)

        totals = []
        for hh in range(2):
            if online:
                for qo, rows, kt_off, c0, keys, masked in diagonal:
                    online_update(hh, q0, qo, rows, qi * 2 + kt_off, c0, keys, masked)
                totals.append([acc_ref[hh, :, g * mb:(g + 1) * mb] for g in range(tq // mb)])
            else:
                staged = [operands(hh, q0, qo, rows, qi * 2 + kt_off, c0, keys)
                          for qo, rows, kt_off, c0, keys, _ in diagonal]
                probs = [jnp.exp2(s) for _, s in staged]
                probs = [jnp.where(visible, p, 0.0) if blk[5] else p for p, blk in zip(probs, diagonal)]
                parts = [jnp.dot(lhs, p.astype(BF16), preferred_element_type=F32)
                         for (lhs, _), p in zip(staged, probs)]
                groups = [acc_ref[hh, :, g * mb:(g + 1) * mb] for g in range(tq // mb)]
                for (qo, rows, *_), part in zip(diagonal, parts):
                    for g in range(rows // mb):
                        groups[qo // mb + g] = groups[qo // mb + g] + part[:, g * mb:(g + 1) * mb]
                totals.append(groups)
        for g in range(tq // mb):
            o_t = [tot[g][:V_HEAD] * pl.reciprocal(tot[g][V_HEAD:V_HEAD + 1], approx=False) for tot in totals]
            o_ref[0, pl.ds(_aligned(q0 + g * mb, mb), mb), :] = jnp.concatenate(o_t, axis=0).T.astype(o_ref.dtype)
        return carry

    if unrolled:
        for qi in range(t // tq):
            q_tile(qi, 0)
    else:
        lax.fori_loop(0, t // tq, q_tile, 0)


def _attn_prompt(q, k, vt, score_bound):
    b, nh, t, _ = q.shape
    n_kt, _, hb = vt.shape[1:]
    assert hb == ATTN_TK and n_kt * hb == t
    qk = pl.BlockSpec((1, 2, t, HEAD_BLOCK), lambda i, j: (i, j, 0, 0))

    def call(online):
        scratch = [pltpu.VMEM((2, ACC_ROWS, ATTN_TQ), F32)]
        if online:
            scratch.append(pltpu.VMEM((2, 1, ATTN_TQ), F32))
        return pl.pallas_call(
            functools.partial(_attn_kernel, tq=ATTN_TQ, online=online),
            grid=(b, nh // 2),
            in_specs=[qk, qk, pl.BlockSpec((1, n_kt, 2 * V_HEAD, hb), lambda i, j: (i, 0, j, 0))],
            out_specs=pl.BlockSpec((1, t, 2 * V_HEAD), lambda i, j: (i, 0, j)),
            out_shape=jax.ShapeDtypeStruct((b, t, MLA_WIDTH), BF16),
            scratch_shapes=scratch,
            compiler_params=pltpu.CompilerParams(
                dimension_semantics=("parallel", "parallel"), vmem_limit_bytes=VMEM_LIMIT_BYTES),
            name="attn_prompt_online" if online else "attn_prompt_bounded")

    return lax.cond(jnp.logical_not(score_bound <= MAX_UNSHIFTED_SCORE), call(True), call(False), q, k, vt)


def _pair_out(acc0, acc1):
    first = lax.broadcasted_iota(jnp.int32, (1, LANES), 1) < V_HEAD
    num = jnp.where(first, pltpu.roll(acc0, V_HEAD, axis=1), acc1)
    den = jnp.where(first, acc0, pltpu.roll(acc1, V_HEAD, axis=1))
    return num * pl.reciprocal(den, approx=False)


def _attn_sample_kernel(q_ref, past_ckv_ref, past_kr_ref, new_ckv_ref, new_kr_ref, wukv_ref, o_ref):
    lane = lax.broadcasted_iota(jnp.int32, (1, LANES), 1)
    is_nope = lane < QK_NOPE
    spread = (lax.broadcasted_iota(jnp.int32, (QK_ROPE, LANES), 1) % QK_ROPE
              == lax.broadcasted_iota(jnp.int32, (QK_ROPE, LANES), 0)).astype(BF16)
    one_b = jnp.ones((1, LANES), BF16)

    segments = [(ckv_ref[0].astype(BF16),
                 jnp.dot(kr_ref[0].astype(BF16), spread, preferred_element_type=F32).astype(BF16))
                for ckv_ref, kr_ref in ((past_ckv_ref, past_kr_ref), (new_ckv_ref, new_kr_ref))]
    pair_cols = 2 * HEAD_BLOCK

    def up_project(pair):
        cols = slice(pair * pair_cols, (pair + 1) * pair_cols)
        return [jnp.dot(ckv_b, wukv_ref[:, cols], preferred_element_type=F32) for ckv_b, _ in segments]

    def keys_values(kv_pair, half):
        out = []
        for kv, (_, krope_b) in zip(kv_pair, segments):
            kvh = kv[:, half * HEAD_BLOCK:(half + 1) * HEAD_BLOCK]
            ks = jnp.sum(jnp.where(is_nope, kvh * kvh, 0.0), axis=-1, keepdims=True)
            kn = kvh * lax.rsqrt(ks + QK_NOPE * EPS)
            out.append((jnp.where(is_nope, kn.astype(BF16), krope_b), jnp.where(is_nope, one_b, kvh.astype(BF16))))
        return out

    n_pairs = N_HEADS // 2
    kv_up, kv, scores, probs = {}, {}, {}, {}
    for step in range(n_pairs + 3):
        p = step
        if p < n_pairs:
            kv_up[p] = up_project(p)
        p = step - 2
        if 0 <= p < n_pairs:
            scores[p] = [[lax.dot_general(q_ref[0, 2 * p + half], k, _NT, preferred_element_type=F32)
                          for k, _ in kv[p][half]] for half in range(2)]
        p = step - 3
        if 0 <= p < n_pairs:
            accs = [sum(jnp.dot(pr, v, preferred_element_type=F32) for pr, (_, v) in zip(probs[p][half], kv[p][half]))
                    for half in range(2)]
            o_ref[0, :, p * 2 * V_HEAD:(p + 1) * 2 * V_HEAD] = _pair_out(accs[0], accs[1]).astype(o_ref.dtype)
        p = step - 1
        if 0 <= p < n_pairs:
            kv[p] = [keys_values(kv_up[p], half) for half in range(2)]
        p = step - 2
        if 0 <= p < n_pairs:
            probs[p] = []
            for half in range(2):
                peak = jnp.maximum(*(jnp.max(s, axis=-1, keepdims=True) for s in scores[p][half]))
                probs[p].append([jnp.exp2(s - peak).astype(BF16) for s in scores[p][half]])


def _attn_sample(q, past_ckv, past_kr, new_ckv, new_kr, w_ukv):
    b, nh, t, _ = q.shape
    per_batch = lambda a: pl.BlockSpec((1,) + a.shape[1:], lambda i: (i,) + (0,) * (a.ndim - 1))
    return pl.pallas_call(
        _attn_sample_kernel,
        grid=(b,),
        in_specs=[per_batch(q), per_batch(past_ckv), per_batch(past_kr), per_batch(new_ckv), per_batch(new_kr),
                  _const_spec(w_ukv.shape)],
        out_specs=pl.BlockSpec((1, t, MLA_WIDTH), lambda i: (i, 0, 0)),
        out_shape=jax.ShapeDtypeStruct((b, t, MLA_WIDTH), BF16),
        compiler_params=pltpu.CompilerParams(
            dimension_semantics=("parallel",), vmem_limit_bytes=VMEM_LIMIT_BYTES),
        name="attn_sample",
    )(q, past_ckv, past_kr, new_ckv, new_kr, w_ukv)


def _post_mixer(x_ref, ya_ref, yb_ref, mod_ref, wout_ref, n2g_ref):
    nb, rows, d = x_ref.shape
    flat = lambda a: a.reshape(nb * rows, a.shape[-1])
    mod = mod_ref[...]
    g1, sh2, sc2 = mod[:, 2:3], mod[:, 3:4], mod[:, 4:5]
    mixer = jnp.concatenate([flat(ya_ref[...]), flat(yb_ref[...])], axis=-1)
    mix = jnp.dot(mixer, wout_ref[...], preferred_element_type=F32)
    x1 = x_ref[...] + g1 * mix.reshape(nb, rows, d)
    h2 = flat(x1 * _rms(x1, d) * (n2g_ref[...] * (1.0 + sc2)) + sh2).astype(BF16)
    return x1, h2


def _post_ffn_chunk(h2, c, wfi_ref, wfo_ref, acc_ref):
    d_ff = wfo_ref.shape[0]
    gate = jnp.dot(h2, wfi_ref[:, c * FFN_CHUNK:(c + 1) * FFN_CHUNK], preferred_element_type=F32)
    up = jnp.dot(h2, wfi_ref[:, d_ff + c * FFN_CHUNK:d_ff + (c + 1) * FFN_CHUNK], preferred_element_type=F32)
    act = (gate * jax.nn.sigmoid(gate) * up).astype(BF16)
    part = jnp.dot(act, wfo_ref[c * FFN_CHUNK:(c + 1) * FFN_CHUNK, :], preferred_element_type=F32)
    if c == 0:
        acc_ref[...] = part
    else:
        acc_ref[...] += part


def _post_kernel(x_ref, ya_ref, yb_ref, mod_ref, wout_ref, n2g_ref, wfi_ref, wfo_ref, o_ref, acc_ref):
    x1, h2 = _post_mixer(x_ref, ya_ref, yb_ref, mod_ref, wout_ref, n2g_ref)
    o_ref[...] = x1
    for c in range(wfo_ref.shape[0] // FFN_CHUNK):
        _post_ffn_chunk(h2, c, wfi_ref, wfo_ref, acc_ref)
    o_ref[...] = o_ref[...] + mod_ref[:, 5:6] * acc_ref[...].reshape(o_ref.shape)


def _post(x, ya, yb, mod, wts, *, nb, rows):
    b, t, d = x.shape
    tok = lambda w: pl.BlockSpec((nb, rows, w), lambda i, j: (i, j, 0))
    consts = [wts[n] for n in ("w_out", "n2g", "w_ffn_in", "w_ffn_out")]
    return pl.pallas_call(
        _post_kernel,
        grid=(b // nb, t // rows),
        in_specs=[tok(d), tok(GMLP_WIDTH), tok(MLA_WIDTH),
                  pl.BlockSpec((nb,) + mod.shape[1:], lambda i, j: (i, 0, 0)),
                  _const_spec(consts[0].shape), _const_spec(consts[1].shape),
                  _const_spec(consts[2].shape), _const_spec(consts[3].shape)],
        out_specs=tok(d),
        out_shape=jax.ShapeDtypeStruct((b, t, d), F32),
        scratch_shapes=[pltpu.VMEM((nb * rows, d), F32)],
        compiler_params=pltpu.CompilerParams(
            dimension_semantics=("parallel", "parallel"), vmem_limit_bytes=VMEM_LIMIT_BYTES),
        name="post",
    )(x, ya, yb, mod, consts[0], consts[1], consts[2], consts[3])


def _rope_tables(first_pos, n, *, mirrored_odd, repeat=1):
    half = QK_ROPE // 2
    inv = ROPE_THETA ** (-np.arange(half, dtype=np.float64) / half)
    ang = (first_pos + np.arange(n, dtype=np.float64))[:, None] * inv[None, :]
    cos, sin = np.cos(ang), np.sin(ang)
    rot = np.concatenate([cos, cos, -sin, sin], axis=-1)
    nope = np.ones((n, QK_NOPE))
    tabq = (SCALE * LOG2E) * np.concatenate([nope, rot] + ([rot, nope] if mirrored_odd else [nope, rot]), axis=-1)
    tabk = np.concatenate([rot, rot], axis=-1)
    return jnp.asarray(np.tile(tabq, (repeat, 1)), F32), jnp.asarray(np.tile(tabk, (repeat, 1)), F32)


def _prep_weights(w_in, w_s, b_s, q_norm_g, w_uq, kv_norm_g, w_ukv, qn_g, qr_g, kn_g, kr_g,
                  norm1_g, w_out, norm2_g, w_ffn_in, w_ffn_out):
    o_kr = 2 * GMLP_WIDTH + Q_LORA + KV_LORA
    w_kr = w_in[:, o_kr:o_kr + QK_ROPE]
    w_kr_sw = _swap_halves(w_kr)
    w_in_p = jnp.concatenate([w_in[:, :o_kr], w_kr, w_kr_sw, w_kr, w_kr_sw], axis=-1)
    d_q = w_uq.shape[0]
    wq = w_uq.reshape(d_q, N_HEADS, QK_HEAD)
    wq_n, wq_r = wq[:, :, :QK_NOPE], wq[:, :, QK_NOPE:]
    wq_plain = jnp.concatenate([wq_n, wq_r, _swap_halves(wq_r)], axis=-1)
    wq_mirror = jnp.concatenate([wq_r, _swap_halves(wq_r), wq_n], axis=-1)
    odd = (jnp.arange(N_HEADS) % 2 == 1)[None, :, None]
    w_uq_flat = wq_plain.reshape(d_q, N_HEADS * HEAD_BLOCK)
    w_uq_pair = jnp.where(odd, wq_mirror, wq_plain).reshape(d_q, N_HEADS * HEAD_BLOCK)
    wkv = w_ukv.reshape(KV_LORA, N_HEADS, QK_NOPE + V_HEAD)
    w_uk = wkv[:, :, :QK_NOPE].reshape(KV_LORA, N_HEADS * QK_NOPE)
    w_uvt = wkv[:, :, QK_NOPE:].reshape(KV_LORA, MLA_WIDTH).T
    root = math.sqrt(QK_NOPE)
    q_nope_row = qn_g * kn_g * (root * root)
    q_rope_row = jnp.concatenate([qr_g, _swap_halves(qr_g)]) * root
    qrow_plain = jnp.concatenate([q_nope_row, q_rope_row])
    qrow_mirror = jnp.concatenate([q_rope_row, q_nope_row])
    row = lambda a: a.reshape(1, -1).astype(F32)
    return {
        "n1g": row(norm1_g), "w_in": w_in_p.astype(BF16), "w_s": w_s.astype(F32), "b_s": b_s.T.astype(F32),
        "qng": row(q_norm_g), "kvg": row(kv_norm_g),
        "w_uq": w_uq_pair.astype(BF16), "w_uq_flat": w_uq_flat.astype(BF16),
        "qrow": row(jnp.concatenate([qrow_plain, qrow_mirror])),
        "qrow_flat": row(jnp.concatenate([qrow_plain, qrow_plain])),
        "w_ukv": w_ukv.astype(BF16), "w_uk": w_uk.astype(BF16), "w_uvt": w_uvt.astype(BF16),
        "krg": row(jnp.concatenate([kr_g, _swap_halves(kr_g), kr_g, _swap_halves(kr_g)])),
        "w_out": w_out.astype(BF16), "n2g": row(norm2_g),
        "w_ffn_in": w_ffn_in.astype(BF16), "w_ffn_out": w_ffn_out.astype(BF16),
    }


def kernel(x_prompt, x_sample, cache_ckv, cache_krope, c_prompt, c_sample, w_ada, b_ada, norm1_g, w_in, w_s, b_s, q_norm_g, w_uq, kv_norm_g, w_ukv, qn_g, qr_g, kn_g, kr_g, w_out, norm2_g, w_ffn_in, w_ffn_out):
    depth = w_in.shape[0]
    assert depth == 1, "single trunk layer"
    assert PRE_ROWS == ATTN_TK
    bp, tp, d = x_prompt.shape
    bs, ts, _ = x_sample.shape
    past_len = cache_ckv.shape[2]

    wts = _prep_weights(w_in[0], w_s[0], b_s[0], q_norm_g[0], w_uq[0], kv_norm_g[0], w_ukv[0], qn_g[0], qr_g[0],
                        kn_g[0], kr_g[0], norm1_g[0], w_out[0], norm2_g[0], w_ffn_in[0], w_ffn_out[0])
    mod = _ada(jnp.concatenate([c_prompt, c_sample], axis=0), w_ada[0], b_ada[0]).reshape(bp + bs, 6, d)
    mod_p, mod_s = mod[:bp], mod[bp:]

    tabq_p, tabk_p = _rope_tables(0, tp, mirrored_odd=True)
    ya_p, q_p, k_p, vt_p, ckv_p, krt_p = _pre_pipelined(x_prompt, mod_p, wts, tabq_p, tabk_p,
                                                        rows=PRE_ROWS, mix_len=GMLP_CHUNK)
    gmax = lambda g: jnp.max(jnp.abs(g))
    score_bound = SCALE * (QK_NOPE * gmax(qn_g) * gmax(kn_g) + QK_ROPE * gmax(qr_g) * gmax(kr_g))
    yb_p = _attn_prompt(q_p, k_p, vt_p, score_bound)
    y_p = _post(x_prompt, ya_p, yb_p, mod_p, wts, nb=1, rows=POST_ROWS)

    tabq_s, tabk_s = _rope_tables(past_len, ts, mirrored_odd=False, repeat=bs)
    ya_s, q_s, ckv_s, kr_s, v_s = _pre_single(x_sample, mod_s, wts, tabq_s, tabk_s, mix_len=min(ts, GMLP_CHUNK))
    yb_s = _attn_sample(q_s, cache_ckv[0], cache_krope[0], ckv_s, kr_s, wts["w_ukv"])
    y_s = _post(x_sample, ya_s, yb_s, mod_s, wts, nb=bs, rows=ts)

    return (y_p, y_s, ckv_p[None], jnp.swapaxes(krt_p, 1, 2)[None], ckv_s[None], kr_s[None], v_s[None])
```
